```python
import jax, jax.numpy as jnp
from jax import lax
import numpy as np

D_MODEL = 1024
BATCH = 8
SEQ = 4096
DEPTH = 4

GRID_W = 64
CTX_LEN = 256

A_HEADS = 12
A_NOPE = 64
A_ROPE = 32
A_QK = A_NOPE + A_ROPE
A_V = 64
A_Q_RANK = 256
A_KV_RANK = 128
F_GROUPS = 4
F_CH = 64
F_WIDTH = F_GROUPS * F_CH
Q_END = A_Q_RANK
KV_END = Q_END + A_KV_RANK
R_END = KV_END + A_ROPE
A_IN = R_END + F_WIDTH
A_MIX = A_HEADS * A_V + F_WIDTH
C_HEADS = 16
C_DH = 64
C_WIDTH = C_HEADS * C_DH
WIN_R = 8
WIN_C = 16
D_FF = 2816
N_EXPERTS = 8
TOP_K = 2
D_FF_EXPERT = 1408

ROPE_THETA = 10000.0
EPS = 1e-6
Q_BLOCK = 128

kernel_name = "hybrid_mla_fourier_natten_moe_dit"


def rms_norm(x, g):
    x32 = x.astype(jnp.float32)
    y = x32 * lax.rsqrt(jnp.mean(x32 * x32, axis=-1, keepdims=True) + EPS)
    return (y * g.astype(jnp.float32)).astype(x.dtype)


def _axial_rope(t, rows, cols):
    n_pairs = A_ROPE // 2
    per_axis = n_pairs // 2
    inv = ROPE_THETA ** (-jnp.arange(per_axis, dtype=jnp.float32) / per_axis)
    ang = jnp.concatenate([rows.astype(jnp.float32)[:, None] * inv, cols.astype(jnp.float32)[:, None] * inv], axis=-1)
    cos = jnp.cos(ang)[:, None, :].astype(t.dtype)
    sin = jnp.sin(ang)[:, None, :].astype(t.dtype)
    tp = t.reshape(*t.shape[:-1], n_pairs, 2)
    t0, t1 = tp[..., 0], tp[..., 1]
    return jnp.stack([t0 * cos - t1 * sin, t0 * sin + t1 * cos], axis=-1).reshape(t.shape)


def _rope_tail(t, rows, cols):
    return jnp.concatenate([t[..., :A_NOPE], _axial_rope(t[..., A_NOPE:], rows, cols)], axis=-1)


def block_attention(q, k, v, scale):
    B, N, H, dq = q.shape
    nb = N // Q_BLOCK
    qb = q.reshape(B, nb, Q_BLOCK, H, dq).transpose(1, 0, 2, 3, 4)

    def one(qi):
        s = jnp.einsum('bqhd,bkhd->bhqk', qi, k).astype(jnp.float32) * scale
        p = jax.nn.softmax(s, axis=-1).astype(v.dtype)
        return jnp.einsum('bhqk,bkhd->bqhd', p, v)

    o = lax.map(one, qb)
    return o.transpose(1, 0, 2, 3, 4).reshape(B, N, H, v.shape[-1])


def _fourier(f):
    B, N, _ = f.shape
    f32 = f.astype(jnp.float32).reshape(B, N, F_GROUPS, F_CH)
    y = jnp.fft.fft2(f32, axes=(1, 3), norm='ortho').real
    return y.reshape(B, N, F_WIDTH).astype(f.dtype)


def _mla_q(q_lat, g_cq, w_uq, g_q):
    B, N, _ = q_lat.shape
    q = (rms_norm(q_lat, g_cq) @ w_uq).reshape(B, N, A_HEADS, A_QK)
    return rms_norm(q, g_q)


def _mla_kv(kv_lat, k_rope, g_ckv, w_ukv, g_k):
    B, N, _ = kv_lat.shape
    kv = (rms_norm(kv_lat, g_ckv) @ w_ukv).reshape(B, N, A_HEADS, A_NOPE + A_V)
    k_nope, v = kv[..., :A_NOPE], kv[..., A_NOPE:]
    k_r = jnp.broadcast_to(k_rope[:, :, None, :], (B, N, A_HEADS, A_ROPE))
    k = rms_norm(jnp.concatenate([k_nope, k_r], axis=-1), g_k)
    return k, v


def _mla_fourier_mixer(xn, cn, rows, cols, w_in, g_cq, g_ckv, w_uq, w_ukv, g_q, g_k, w_out, need_ctx):
    B, S, _ = xn.shape
    L = cn.shape[1]
    scale = A_QK ** -0.5
    px = xn @ w_in
    qx = _rope_tail(_mla_q(px[..., :Q_END], g_cq, w_uq, g_q), rows, cols)
    kx, vx = _mla_kv(px[..., Q_END:KV_END], px[..., KV_END:R_END], g_ckv, w_ukv, g_k)
    kx = _rope_tail(kx, rows, cols)
    if need_ctx:
        pc = cn @ w_in
        kc, vc = _mla_kv(pc[..., Q_END:KV_END], pc[..., KV_END:R_END], g_ckv, w_ukv, g_k)
    else:
        pc = cn @ w_in[:, Q_END:R_END]
        kc, vc = _mla_kv(pc[..., :A_KV_RANK], pc[..., A_KV_RANK:], g_ckv, w_ukv, g_k)
    k_all = jnp.concatenate([kx, kc], axis=1)
    v_all = jnp.concatenate([vx, vc], axis=1)
    ax = block_attention(qx, k_all, v_all, scale).reshape(B, S, A_HEADS * A_V)
    ox = jnp.concatenate([ax, _fourier(px[..., R_END:])], axis=-1) @ w_out
    if need_ctx:
        qc = _mla_q(pc[..., :Q_END], g_cq, w_uq, g_q)
        ac = block_attention(qc, kc, vc, scale).reshape(B, L, A_HEADS * A_V)
        oc = jnp.concatenate([ac, _fourier(pc[..., R_END:])], axis=-1) @ w_out
    else:
        oc = None
    return ox, oc


def _na_mixer(xn, cn, w_in, g_q, g_k, rpb, w_out, need_ctx):
    B, S, _ = xn.shape
    L = cn.shape[1]
    rows_n = S // GRID_W
    wr = min(WIN_R, rows_n)
    wc = min(WIN_C, GRID_W)
    scale = C_DH ** -0.5
    px = (xn @ w_in).reshape(B, S, 3, C_HEADS, C_DH)
    qx = rms_norm(px[:, :, 0], g_q)
    kx = rms_norm(px[:, :, 1], g_k)
    vx = px[:, :, 2]
    if need_ctx:
        pc = (cn @ w_in).reshape(B, L, 3, C_HEADS, C_DH)
        qc = rms_norm(pc[:, :, 0], g_q)
        kc = rms_norm(pc[:, :, 1], g_k)
        vc = pc[:, :, 2]
    else:
        pc = (cn @ w_in[:, C_WIDTH:]).reshape(B, L, 2, C_HEADS, C_DH)
        kc = rms_norm(pc[:, :, 0], g_k)
        vc = pc[:, :, 1]
    row_start = jnp.clip(jnp.arange(rows_n) - wr // 2, 0, rows_n - wr)
    col_pos = jnp.arange(GRID_W)
    col_idx = jnp.clip(col_pos - wc // 2, 0, GRID_W - wc)[:, None] + jnp.arange(wc)[None, :]
    rpb_c = rpb[:, :, col_idx - col_pos[:, None] + (WIN_C - 1)]
    k_grid = kx.reshape(B, rows_n, GRID_W, C_HEADS, C_DH)
    v_grid = vx.reshape(B, rows_n, GRID_W, C_HEADS, C_DH)
    q_rows = qx.reshape(B, rows_n, GRID_W, C_HEADS, C_DH).transpose(1, 0, 2, 3, 4)

    def one_row(args):
        r, qr = args
        start = row_start[r]
        kb = lax.dynamic_slice_in_dim(k_grid, start, wr, axis=1)
        vb = lax.dynamic_slice_in_dim(v_grid, start, wr, axis=1)
        ks = jnp.take(kb, col_idx, axis=2)
        vs = jnp.take(vb, col_idx, axis=2)
        dr = start + jnp.arange(wr) - r + (WIN_R - 1)
        bias = jnp.take(rpb_c, dr, axis=1).transpose(0, 2, 1, 3)
        s_loc = jnp.einsum('bqhd,baqchd->bhqac', qr, ks).astype(jnp.float32) * scale + bias[None].astype(jnp.float32)
        s_ctx = jnp.einsum('bqhd,blhd->bhql', qr, kc).astype(jnp.float32) * scale
        s = jnp.concatenate([s_loc.reshape(B, C_HEADS, GRID_W, wr * wc), s_ctx], axis=-1)
        p = jax.nn.softmax(s, axis=-1).astype(vs.dtype)
        p_loc = p[..., :wr * wc].reshape(B, C_HEADS, GRID_W, wr, wc)
        p_ctx = p[..., wr * wc:]
        return jnp.einsum('bhqac,baqchd->bqhd', p_loc, vs) + jnp.einsum('bhql,blhd->bqhd', p_ctx, vc)

    o = lax.map(one_row, (jnp.arange(rows_n), q_rows))
    ox = o.transpose(1, 0, 2, 3, 4).reshape(B, S, C_WIDTH) @ w_out
    if need_ctx:
        oc = block_attention(qc, kc, vc, scale).reshape(B, L, C_WIDTH) @ w_out
    else:
        oc = None
    return ox, oc


def _swiglu(h, wg, wu, wd):
    return (jax.nn.silu(h @ wg) * (h @ wu)) @ wd


def _moe(h, w_router, wg, wu, wd):
    logits = (h @ w_router).astype(jnp.float32)
    top_v, top_i = lax.top_k(logits, TOP_K)
    gates = jax.nn.softmax(top_v, axis=-1)
    combine = jnp.sum(jax.nn.one_hot(top_i, N_EXPERTS, dtype=jnp.float32) * gates[..., None], axis=-2).astype(h.dtype)
    out = jnp.zeros_like(h)
    for e in range(N_EXPERTS):
        out = out + combine[..., e, None] * _swiglu(h, wg[e], wu[e], wd[e])
    return out


def setup_inputs(seed: int = 0) -> dict:
    key = jax.random.key(seed)
    ks = iter(jax.random.split(key, 32))
    ne, no = (DEPTH + 1) // 2, DEPTH // 2

    def nrm(shape, scale):
        return jax.random.normal(next(ks), shape, jnp.float32) * scale

    def gain(shape):
        return 1.0 + nrm(shape, 0.05)

    D = D_MODEL
    return {
        "x": nrm((BATCH, SEQ, D), 1.0),
        "c": nrm((BATCH, D), 1.0),
        "ctx": nrm((BATCH, CTX_LEN, D), 1.0),
        "c_ctx": nrm((D,), 1.0),
        "w_mod": nrm((DEPTH, D, 6 * D), D ** -0.5),
        "b_mod": nrm((DEPTH, 6 * D), 0.01),
        "norm_g": gain((DEPTH, 2, D)),
        "a_w_in": nrm((ne, D, A_IN), D ** -0.5),
        "a_g_cq": gain((ne, A_Q_RANK)),
        "a_g_ckv": gain((ne, A_KV_RANK)),
        "a_w_uq": nrm((ne, A_Q_RANK, A_HEADS * A_QK), A_Q_RANK ** -0.5),
        "a_w_ukv": nrm((ne, A_KV_RANK, A_HEADS * (A_NOPE + A_V)), A_KV_RANK ** -0.5),
        "a_g_q": gain((ne, A_QK)),
        "a_g_k": gain((ne, A_QK)),
        "a_w_out": nrm((ne, A_MIX, D), A_MIX ** -0.5),
        "f_w_gate": nrm((ne, D, D_FF), D ** -0.5),
        "f_w_up": nrm((ne, D, D_FF), D ** -0.5),
        "f_w_down": nrm((ne, D_FF, D), D_FF ** -0.5),
        "c_w_in": nrm((no, D, 3 * C_WIDTH), D ** -0.5),
        "c_g_q": gain((no, C_DH)),
        "c_g_k": gain((no, C_DH)),
        "c_rpb": nrm((no, C_HEADS, 2 * WIN_R - 1, 2 * WIN_C - 1), 0.2),
        "c_w_out": nrm((no, C_WIDTH, D), C_WIDTH ** -0.5),
        "m_w_router": nrm((no, D, N_EXPERTS), D ** -0.5),
        "m_w_gate": nrm((no, N_EXPERTS, D, D_FF_EXPERT), D ** -0.5),
        "m_w_up": nrm((no, N_EXPERTS, D, D_FF_EXPERT), D ** -0.5),
        "m_w_down": nrm((no, N_EXPERTS, D_FF_EXPERT, D), D_FF_EXPERT ** -0.5),
    }


def reference(x, c, ctx, c_ctx, w_mod, b_mod, norm_g, a_w_in, a_g_cq, a_g_ckv, a_w_uq, a_w_ukv, a_g_q, a_g_k, a_w_out,
              f_w_gate, f_w_up, f_w_down, c_w_in, c_g_q, c_g_k, c_rpb, c_w_out, m_w_router, m_w_gate, m_w_up, m_w_down):
    B, S, D = x.shape
    L = ctx.shape[1]
    t = jnp.arange(S)
    rows = t // GRID_W
    cols = t % GRID_W
    sc = jax.nn.silu(c)
    scc = jax.nn.silu(c_ctx)[None]
    hx, hc = x, ctx
    for i in range(DEPTH):
        last = i == DEPTH - 1
        j = i // 2
        mx = (sc @ w_mod[i] + b_mod[i]).reshape(B, 6, D)
        mc = (scc @ w_mod[i] + b_mod[i]).reshape(1, 6, D)
        shx1, scx1, gx1, shx2, scx2, gx2 = [mx[:, k, None, :] for k in range(6)]
        shc1, scc1, gc1, shc2, scc2, gc2 = [mc[:, k, None, :] for k in range(6)]
        xn = rms_norm(hx, norm_g[i, 0]) * (1 + scx1) + shx1
        cn = rms_norm(hc, norm_g[i, 0]) * (1 + scc1) + shc1
        if i % 2 == 0:
            ox, oc = _mla_fourier_mixer(xn, cn, rows, cols, a_w_in[j], a_g_cq[j], a_g_ckv[j], a_w_uq[j], a_w_ukv[j],
                                        a_g_q[j], a_g_k[j], a_w_out[j], not last)
        else:
            ox, oc = _na_mixer(xn, cn, c_w_in[j], c_g_q[j], c_g_k[j], c_rpb[j], c_w_out[j], not last)
        hx = hx + gx1 * ox
        xn2 = rms_norm(hx, norm_g[i, 1]) * (1 + scx2) + shx2
        if last:
            h_all = xn2
        else:
            hc = hc + gc1 * oc
            cn2 = rms_norm(hc, norm_g[i, 1]) * (1 + scc2) + shc2
            h_all = jnp.concatenate([cn2, xn2], axis=1)
        if i % 2 == 0:
            y = _swiglu(h_all, f_w_gate[j], f_w_up[j], f_w_down[j])
        else:
            y = _moe(h_all, m_w_router[j], m_w_gate[j], m_w_up[j], m_w_down[j])
        if last:
            hx = hx + gx2 * y
        else:
            hc = hc + gc2 * y[:, :L]
            hx = hx + gx2 * y[:, L:]
    return hx
```

```python
import functools
import math

import jax
import jax.numpy as jnp
import numpy as np
from jax import lax
from jax.experimental import pallas as pl
from jax.experimental.pallas import tpu as pltpu

F32 = jnp.float32
BF16 = jnp.bfloat16

GRID_W = 64
A_HEADS = 12
A_NOPE = 64
A_ROPE = 32
A_QK = A_NOPE + A_ROPE
A_V = 64
A_Q_RANK = 256
A_KV_RANK = 128
F_GROUPS = 4
F_CH = 64
F_WIDTH = F_GROUPS * F_CH
C_HEADS = 16
C_DH = 64
WIN_R = 8
WIN_C = 16
N_EXPERTS = 8
ROPE_THETA = 10000.0
EPS = 1e-6

LANES = 128
HEAD_PAD = 128
TM = 256
MOE_TM = 1024
VMEM_LIMIT = 52 * 1024 * 1024
NEG_BIG = -1e30


def _params(sem, vmem=VMEM_LIMIT):
    return pltpu.CompilerParams(dimension_semantics=sem, vmem_limit_bytes=vmem)


def _rms(x, g, n=None):
    n = x.shape[-1] if n is None else n
    ss = jnp.sum(x * x, axis=-1, keepdims=True)
    return x * lax.rsqrt(ss * (1.0 / n) + EPS) * g


def _silu(x):
    return x * (1.0 / (1.0 + jnp.exp(-x)))


def _norm_mod(h_ref, g_ref, mod_ref, row):
    x = h_ref[...]
    xn = _rms(x, g_ref[...])
    return xn * (1.0 + mod_ref[row + 1:row + 2, :]) + mod_ref[row:row + 1, :]


def _mod_kernel(c_ref, w_ref, b_ref, o_ref):
    s = _silu(c_ref[...])
    o_ref[...] = jnp.dot(s, w_ref[...], precision=lax.Precision.HIGHEST,
                         preferred_element_type=F32) + b_ref[...]


def _modulation(cin, w_mod, b_mod):
    depth, d, n = w_mod.shape
    tn = 1536
    rows = cin.shape[0]
    return pl.pallas_call(
        _mod_kernel,
        grid=(depth, n // tn),
        in_specs=[
            pl.BlockSpec((rows, d), lambda l, j: (0, 0)),
            pl.BlockSpec((None, d, tn), lambda l, j: (l, 0, j)),
            pl.BlockSpec((None, 1, tn), lambda l, j: (l, 0, j)),
        ],
        out_specs=pl.BlockSpec((None, rows, tn), lambda l, j: (l, 0, j)),
        out_shape=jax.ShapeDtypeStruct((depth, rows, n), F32),
        compiler_params=_params(("arbitrary", "arbitrary")),
        name="modulation",
    )(cin, w_mod, b_mod.reshape(depth, 1, n))


def _mla_front_kernel(h_ref, mod_ref, g_ref, win_ref, gcq_ref, gckv_ref, wq_ref, wkv_ref,
                      t1_ref, t2_ref, gq_ref, gk_ref, cb_ref,
                      q_ref, k_ref, v_ref, fcs_ref):
    xn = _norm_mod(h_ref, g_ref, mod_ref, 0)
    px = jnp.dot(xn.astype(BF16), win_ref[...], preferred_element_type=F32)
    t1 = t1_ref[...]
    t2 = t2_ref[...]
    hw = A_HEADS * HEAD_PAD

    qn = _rms(px[:, 0:A_Q_RANK], gcq_ref[...]).astype(BF16)
    qq = jnp.dot(qn, wq_ref[...], preferred_element_type=F32)
    aq = t1 * gq_ref[0:1, :]
    bq = t2 * gq_ref[1:2, :]
    scale = A_QK ** -0.5
    for h in range(A_HEADS):
        qm = qq[:, h * HEAD_PAD:(h + 1) * HEAD_PAD]
        qs = qq[:, hw + h * HEAD_PAD:hw + (h + 1) * HEAD_PAD]
        ss = jnp.sum(qm * qm, axis=-1, keepdims=True)
        inv = lax.rsqrt(ss * (1.0 / A_QK) + EPS) * scale
        q_ref[:, h * HEAD_PAD:(h + 1) * HEAD_PAD] = ((qm * aq + qs * bq) * inv).astype(BF16)

    kvn = _rms(px[:, A_Q_RANK:A_Q_RANK + A_KV_RANK], gckv_ref[...]).astype(BF16)
    kv = jnp.dot(kvn, wkv_ref[...], preferred_element_type=F32)
    krm = px[:, 384:512]
    krs = px[:, 512:640]
    tail = krm * (t1 * gk_ref[1:2, :]) + krs * (t2 * gk_ref[2:3, :])
    ssr = jnp.sum(krm * krm, axis=-1, keepdims=True)
    gkn = gk_ref[0:1, :]
    for h in range(A_HEADS):
        km = kv[:, h * HEAD_PAD:(h + 1) * HEAD_PAD]
        ss = jnp.sum(km * km, axis=-1, keepdims=True) + ssr
        inv = lax.rsqrt(ss * (1.0 / A_QK) + EPS)
        k_ref[:, h * HEAD_PAD:(h + 1) * HEAD_PAD] = ((km * gkn + tail) * inv).astype(BF16)
    v_ref[...] = kv[:, hw:].astype(BF16)

    f = px[:, 640:896].astype(BF16)
    fcs_ref[...] = jnp.dot(f, cb_ref[...], preferred_element_type=F32).astype(BF16)


def _mla_front(h, mod, g, win, gcq, gckv, wq, wkv, t1, t2, gq, gk, cb):
    b, n, d = h.shape
    nt = n // TM
    hw = A_HEADS * HEAD_PAD
    tile = lambda w: pl.BlockSpec((None, TM, w), lambda i, t: (i, t, 0))
    full = lambda a: pl.BlockSpec(a.shape, lambda i, t: (0,) * a.ndim)
    return pl.pallas_call(
        _mla_front_kernel,
        grid=(b, nt),
        in_specs=[
            tile(d),
            pl.BlockSpec((None, 6, d), lambda i, t: (jnp.where(t == 0, b, i), 0, 0)),
            full(g), full(win), full(gcq), full(gckv), full(wq), full(wkv),
            pl.BlockSpec((TM, LANES), lambda i, t: (t, 0)),
            pl.BlockSpec((TM, LANES), lambda i, t: (t, 0)),
            full(gq), full(gk), full(cb),
        ],
        out_specs=[tile(hw), tile(hw), tile(A_HEADS * A_V), tile(2 * F_WIDTH)],
        out_shape=[
            jax.ShapeDtypeStruct((b, n, hw), BF16),
            jax.ShapeDtypeStruct((b, n, hw), BF16),
            jax.ShapeDtypeStruct((b, n, A_HEADS * A_V), BF16),
            jax.ShapeDtypeStruct((b, n, 2 * F_WIDTH), BF16),
        ],
        compiler_params=_params(("arbitrary", "arbitrary")),
        name="mla_front",
    )(h, mod, g, win, gcq, gckv, wq, wkv, t1, t2, gq, gk, cb)


def _softmax_pv(q, k, v):
    s = lax.dot_general(q, k, (((1,), (1,)), ((), ())), preferred_element_type=F32)
    m = jnp.max(s, axis=-1, keepdims=True)
    p = jnp.exp(s - m)
    l = jnp.sum(p, axis=-1, keepdims=True)
    o = jnp.dot(p.astype(BF16), v, preferred_element_type=F32)
    return o * (1.0 / l)


def _mla_attn_kernel(q_ref, k_ref, v_ref, o_ref, *, ctx_len):
    t = pl.program_id(2)
    lane = lax.broadcasted_iota(jnp.int32, o_ref.shape, 1)

    def run(kl):
        outs = []
        for hh in range(2):
            q = q_ref[:, hh * HEAD_PAD:(hh + 1) * HEAD_PAD]
            k = k_ref[0:kl, hh * HEAD_PAD:(hh + 1) * HEAD_PAD]
            outs.append(_softmax_pv(q, k, v_ref[0:kl, :]))
        o_ref[...] = jnp.where(lane < A_V, outs[0], outs[1]).astype(BF16)

    @pl.when(t == 0)
    def _():
        run(ctx_len)

    @pl.when(t > 0)
    def _():
        run(k_ref.shape[0])


def _mla_attn(q, k, v, ctx_len):
    b, n, _ = q.shape
    hp = A_HEADS // 2
    return pl.pallas_call(
        functools.partial(_mla_attn_kernel, ctx_len=ctx_len),
        grid=(b, hp, n // TM),
        in_specs=[
            pl.BlockSpec((None, TM, 2 * HEAD_PAD), lambda i, p, t: (i, t, p)),
            pl.BlockSpec((None, n, 2 * HEAD_PAD), lambda i, p, t: (i, 0, p)),
            pl.BlockSpec((None, n, 2 * A_V), lambda i, p, t: (i, 0, p)),
        ],
        out_specs=pl.BlockSpec((None, TM, 2 * A_V), lambda i, p, t: (i, t, p)),
        out_shape=jax.ShapeDtypeStruct((b, n, A_HEADS * A_V), BF16),
        compiler_params=_params(("arbitrary", "arbitrary", "arbitrary")),
        name="mla_attn",
    )(q, k, v)


def _dft_kernel(cn_ref, sn_ref, cc_ref, sc_ref, fcs_ref, y_ref, *, ctx_len):
    t = pl.program_id(1)

    @pl.when(t == 0)
    def _():
        fc = fcs_ref[0:ctx_len, 0:F_WIDTH]
        fs = fcs_ref[0:ctx_len, F_WIDTH:2 * F_WIDTH]
        y = (jnp.dot(cc_ref[...], fc, preferred_element_type=F32)
             - jnp.dot(sc_ref[...], fs, preferred_element_type=F32))
        y_ref[...] = y.astype(BF16)

    @pl.when(t > 0)
    def _():
        n = fcs_ref.shape[0]
        fc = fcs_ref[ctx_len:n, 0:F_WIDTH]
        fs = fcs_ref[ctx_len:n, F_WIDTH:2 * F_WIDTH]
        y = (jnp.dot(cn_ref[...], fc, preferred_element_type=F32)
             - jnp.dot(sn_ref[...], fs, preferred_element_type=F32))
        y_ref[...] = y.astype(BF16)


def _dft(fcs, cn, sn, cc, sc, ctx_len):
    b, n, _ = fcs.shape
    s = n - ctx_len
    assert ctx_len == TM
    return pl.pallas_call(
        functools.partial(_dft_kernel, ctx_len=ctx_len),
        grid=(b, n // TM),
        in_specs=[
            pl.BlockSpec((TM, s), lambda i, t: (jnp.maximum(t - 1, 0), 0)),
            pl.BlockSpec((TM, s), lambda i, t: (jnp.maximum(t - 1, 0), 0)),
            pl.BlockSpec((ctx_len, ctx_len), lambda i, t: (0, 0)),
            pl.BlockSpec((ctx_len, ctx_len), lambda i, t: (0, 0)),
            pl.BlockSpec((None, n, 2 * F_WIDTH), lambda i, t: (i, 0, 0)),
        ],
        out_specs=pl.BlockSpec((None, TM, F_WIDTH), lambda i, t: (i, t, 0)),
        out_shape=jax.ShapeDtypeStruct((b, n, F_WIDTH), BF16),
        compiler_params=_params(("arbitrary", "arbitrary")),
        name="fourier_dft",
    )(cn, sn, cc, sc, fcs)


def _out_proj_kernel(*refs, widths):
    mix_refs = refs[:len(widths)]
    w_ref, h_ref, mod_ref, o_ref = refs[len(widths):]
    acc = None
    off = 0
    for m_ref, wd in zip(mix_refs, widths):
        part = jnp.dot(m_ref[...], w_ref[off:off + wd, :], preferred_element_type=F32)
        acc = part if acc is None else acc + part
        off += wd
    o_ref[...] = h_ref[...] + mod_ref[2:3, :] * acc


def _out_proj(mixes, w, h, mod):
    b, n, d = h.shape
    widths = tuple(m.shape[-1] for m in mixes)
    tile = lambda wd: pl.BlockSpec((None, TM, wd), lambda i, t: (i, t, 0))
    return pl.pallas_call(
        functools.partial(_out_proj_kernel, widths=widths),
        grid=(b, n // TM),
        in_specs=[tile(wd) for wd in widths] + [
            pl.BlockSpec(w.shape, lambda i, t: (0, 0)),
            tile(d),
            pl.BlockSpec((None, 6, d), lambda i, t: (jnp.where(t == 0, b, i), 0, 0)),
        ],
        out_specs=tile(d),
        out_shape=jax.ShapeDtypeStruct((b, n, d), F32),
        compiler_params=_params(("arbitrary", "arbitrary")),
        name="out_proj",
    )(*mixes, w, h, mod)


def _ffn_kernel(h_ref, mod_ref, g_ref, wg_ref, wu_ref, wd_ref, o_ref):
    xn = _norm_mod(h_ref, g_ref, mod_ref, 3).astype(BF16)
    gt = jnp.dot(xn, wg_ref[...], preferred_element_type=F32)
    up = jnp.dot(xn, wu_ref[...], preferred_element_type=F32)
    a = (_silu(gt) * up).astype(BF16)
    y = jnp.dot(a, wd_ref[...], preferred_element_type=F32)
    o_ref[...] = h_ref[...] + mod_ref[5:6, :] * y


def _ffn(h, mod, g, wg, wu, wd):
    b, n, d = h.shape
    tile = pl.BlockSpec((None, TM, d), lambda i, t: (i, t, 0))
    full = lambda a: pl.BlockSpec(a.shape, lambda i, t: (0,) * a.ndim)
    return pl.pallas_call(
        _ffn_kernel,
        grid=(b, n // TM),
        in_specs=[
            tile,
            pl.BlockSpec((None, 6, d), lambda i, t: (jnp.where(t == 0, b, i), 0, 0)),
            full(g), full(wg), full(wu), full(wd),
        ],
        out_specs=tile,
        out_shape=jax.ShapeDtypeStruct((b, n, d), F32),
        compiler_params=_params(("arbitrary", "arbitrary")),
        name="ffn_dense",
    )(h, mod, g, wg, wu, wd)


def _na_front_kernel(h_ref, mod_ref, g_ref, win_ref, gq_ref, gk_ref, q_ref, k_ref, v_ref):
    xn = _norm_mod(h_ref, g_ref, mod_ref, 0)
    px = jnp.dot(xn.astype(BF16), win_ref[...], preferred_element_type=F32)
    width = C_HEADS * C_DH
    lane = lax.broadcasted_iota(jnp.int32, (px.shape[0], LANES), 1)
    low = lane < C_DH
    scale = C_DH ** -0.5

    def norm_pairs(base, g_ref_, out_ref, mult):
        gg = g_ref_[...]
        for j in range(width // LANES):
            x = px[:, base + j * LANES:base + (j + 1) * LANES]
            x2 = x * x
            s_all = jnp.sum(x2, axis=-1, keepdims=True)
            s_lo = jnp.sum(jnp.where(low, x2, 0.0), axis=-1, keepdims=True)
            ss = jnp.where(low, s_lo, s_all - s_lo)
            inv = lax.rsqrt(ss * (1.0 / C_DH) + EPS)
            if mult != 1.0:
                inv = inv * mult
            out_ref[:, j * LANES:(j + 1) * LANES] = (x * inv * gg).astype(BF16)

    norm_pairs(0, gq_ref, q_ref, scale)
    norm_pairs(width, gk_ref, k_ref, 1.0)
    v_ref[...] = px[:, 2 * width:].astype(BF16)


def _na_front(h, mod, g, win, gq, gk):
    b, n, d = h.shape
    width = C_HEADS * C_DH
    tile = lambda w: pl.BlockSpec((None, TM, w), lambda i, t: (i, t, 0))
    full = lambda a: pl.BlockSpec(a.shape, lambda i, t: (0,) * a.ndim)
    return pl.pallas_call(
        _na_front_kernel,
        grid=(b, n // TM),
        in_specs=[
            tile(d),
            pl.BlockSpec((None, 6, d), lambda i, t: (jnp.where(t == 0, b, i), 0, 0)),
            full(g), full(win), full(gq), full(gk),
        ],
        out_specs=[tile(width)] * 3,
        out_shape=[jax.ShapeDtypeStruct((b, n, width), BF16)] * 3,
        compiler_params=_params(("arbitrary", "arbitrary")),
        name="na_front",
    )(h, mod, g, win, gq, gk)


def _na_attn_kernel(q_ref, k_ref, v_ref, bias_ref, o_ref, *, ctx_len, rows_n, wr):
    wlen = wr * GRID_W
    lane_q = lax.broadcasted_iota(jnp.int32, (GRID_W, LANES), 1)
    kc = k_ref[0:ctx_len, :]
    vc = v_ref[0:ctx_len, :]

    qc = q_ref[0:ctx_len, :]
    lane_c = lax.broadcasted_iota(jnp.int32, (ctx_len, LANES), 1)
    outs = []
    for hh in range(2):
        sel = (lane_c < C_DH) if hh == 0 else (lane_c >= C_DH)
        outs.append(_softmax_pv(jnp.where(sel, qc, jnp.zeros_like(qc)), kc, vc))
    o_ref[0:ctx_len, :] = jnp.where(lane_c < C_DH, outs[0], outs[1]).astype(BF16)

    def row_body(r, carry):
        start = jnp.clip(r - wr // 2, 0, rows_n - wr)
        pat = r - start
        q0 = pl.multiple_of(ctx_len + r * GRID_W, GRID_W)
        k0 = pl.multiple_of(ctx_len + start * GRID_W, GRID_W)
        qr = q_ref[pl.ds(q0, GRID_W), :]
        kw = k_ref[pl.ds(k0, wlen), :]
        vw = v_ref[pl.ds(k0, wlen), :]
        res = []
        for hh in range(2):
            sel = (lane_q < C_DH) if hh == 0 else (lane_q >= C_DH)
            qh = jnp.where(sel, qr, jnp.zeros_like(qr))
            s_loc = lax.dot_general(qh, kw, (((1,), (1,)), ((), ())),
                                    preferred_element_type=F32) + bias_ref[pat, hh]
            s_ctx = lax.dot_general(qh, kc, (((1,), (1,)), ((), ())), preferred_element_type=F32)
            m = jnp.maximum(jnp.max(s_loc, axis=-1, keepdims=True),
                            jnp.max(s_ctx, axis=-1, keepdims=True))
            p_loc = jnp.exp(s_loc - m)
            p_ctx = jnp.exp(s_ctx - m)
            l = jnp.sum(p_loc, axis=-1, keepdims=True) + jnp.sum(p_ctx, axis=-1, keepdims=True)
            o = (jnp.dot(p_loc.astype(BF16), vw, preferred_element_type=F32)
                 + jnp.dot(p_ctx.astype(BF16), vc, preferred_element_type=F32))
            res.append(o * (1.0 / l))
        o_ref[pl.ds(q0, GRID_W), :] = jnp.where(lane_q < C_DH, res[0], res[1]).astype(BF16)
        return carry

    lax.fori_loop(0, rows_n, row_body, 0)


def _na_attn(q, k, v, bias, ctx_len):
    b, n, width = q.shape
    rows_n = (n - ctx_len) // GRID_W
    wr = min(WIN_R, rows_n)
    hp = C_HEADS // 2
    blk = pl.BlockSpec((None, n, LANES), lambda i, p: (i, 0, p))
    return pl.pallas_call(
        functools.partial(_na_attn_kernel, ctx_len=ctx_len, rows_n=rows_n, wr=wr),
        grid=(b, hp),
        in_specs=[blk, blk, blk,
                  pl.BlockSpec((bias.shape[0], 2, GRID_W, wr * GRID_W), lambda i, p: (0, p, 0, 0))],
        out_specs=blk,
        out_shape=jax.ShapeDtypeStruct((b, n, width), BF16),
        compiler_params=_params(("arbitrary", "arbitrary")),
        name="na_attn",
    )(q, k, v, bias)


def _router_kernel(h_ref, mod_ref, g_ref, wr_ref, xn_ref, comb_ref):
    xn = _norm_mod(h_ref, g_ref, mod_ref, 3)
    xn_ref[...] = xn.astype(BF16)
    logits = jnp.dot(xn, wr_ref[...], precision=lax.Precision.HIGHEST, preferred_element_type=F32)
    lane = lax.broadcasted_iota(jnp.int32, logits.shape, 1).astype(F32)
    lg = jnp.where(lane < N_EXPERTS, logits, -jnp.inf)
    m1 = jnp.max(lg, axis=-1, keepdims=True)
    i1 = jnp.min(jnp.where(lg == m1, lane, float(LANES)), axis=-1, keepdims=True)
    lg2 = jnp.where(lane == i1, -jnp.inf, lg)
    m2 = jnp.max(lg2, axis=-1, keepdims=True)
    i2 = jnp.min(jnp.where(lg2 == m2, lane, float(LANES)), axis=-1, keepdims=True)
    e2 = jnp.exp(m2 - m1)
    den = 1.0 / (1.0 + e2)
    comb_ref[...] = jnp.where(lane == i1, den, 0.0) + jnp.where(lane == i2, e2 * den, 0.0)


def _router(h, mod, g, wr):
    b, n, d = h.shape
    tile = lambda w, dt=None: pl.BlockSpec((None, TM, w), lambda i, t: (i, t, 0))
    full = lambda a: pl.BlockSpec(a.shape, lambda i, t: (0,) * a.ndim)
    return pl.pallas_call(
        _router_kernel,
        grid=(b, n // TM),
        in_specs=[
            tile(d),
            pl.BlockSpec((None, 6, d), lambda i, t: (jnp.where(t == 0, b, i), 0, 0)),
            full(g), full(wr),
        ],
        out_specs=[tile(d), tile(LANES)],
        out_shape=[jax.ShapeDtypeStruct((b, n, d), BF16), jax.ShapeDtypeStruct((b, n, LANES), F32)],
        compiler_params=_params(("arbitrary", "arbitrary")),
        name="moe_router",
    )(h, mod, g, wr)


def _moe_kernel(x_ref, comb_ref, wg_ref, wu_ref, wd_ref, y_ref, acc_ref):
    e = pl.program_id(1)

    @pl.when(e == 0)
    def _():
        acc_ref[...] = jnp.zeros_like(acc_ref)

    x = x_ref[...]
    gt = jnp.dot(x, wg_ref[...], preferred_element_type=F32)
    up = jnp.dot(x, wu_ref[...], preferred_element_type=F32)
    a = (_silu(gt) * up).astype(BF16)
    comb = comb_ref[...]
    lane = lax.broadcasted_iota(jnp.int32, comb.shape, 1)
    ce = jnp.sum(jnp.where(lane == e, comb, 0.0), axis=-1, keepdims=True)
    acc_ref[...] += ce * jnp.dot(a, wd_ref[...], preferred_element_type=F32)

    @pl.when(e == pl.num_programs(1) - 1)
    def _():
        y_ref[...] = acc_ref[...]


def _moe(x, comb, wg, wu, wd):
    t, d = x.shape
    ne, _, f = wg.shape
    tm = MOE_TM if t % MOE_TM == 0 else TM
    return pl.pallas_call(
        _moe_kernel,
        grid=(t // tm, ne),
        in_specs=[
            pl.BlockSpec((tm, d), lambda i, e: (i, 0)),
            pl.BlockSpec((tm, LANES), lambda i, e: (i, 0)),
            pl.BlockSpec((None, d, f), lambda i, e: (e, 0, 0)),
            pl.BlockSpec((None, d, f), lambda i, e: (e, 0, 0)),
            pl.BlockSpec((None, f, d), lambda i, e: (e, 0, 0)),
        ],
        out_specs=pl.BlockSpec((tm, d), lambda i, e: (i, 0)),
        out_shape=jax.ShapeDtypeStruct((t, d), F32),
        scratch_shapes=[pltpu.VMEM((tm, d), F32)],
        compiler_params=_params(("arbitrary", "arbitrary")),
        name="moe_experts",
    )(x, comb, wg, wu, wd)


def _resid_kernel(h_ref, y_ref, mod_ref, o_ref):
    o_ref[...] = h_ref[...] + mod_ref[5:6, :] * y_ref[...]


def _resid(h, y, mod, latent_only):
    b, n, d = h.shape
    skip = 1 if latent_only else 0
    nt = n // TM - skip
    tile = pl.BlockSpec((None, TM, d), lambda i, t: (i, t + skip, 0))
    if latent_only:
        mod_spec = pl.BlockSpec((None, 6, d), lambda i, t: (i, 0, 0))
    else:
        mod_spec = pl.BlockSpec((None, 6, d), lambda i, t: (jnp.where(t == 0, b, i), 0, 0))
    return pl.pallas_call(
        _resid_kernel,
        grid=(b, nt),
        in_specs=[tile, tile, mod_spec],
        out_specs=pl.BlockSpec((None, TM, d), lambda i, t: (i, t, 0)),
        out_shape=jax.ShapeDtypeStruct((b, nt * TM, d), F32),
        compiler_params=_params(("arbitrary", "arbitrary")),
        name="moe_residual",
    )(h, y, mod)


def _rope_tables(s, ctx_len):
    t = jnp.arange(s)
    rows = (t // GRID_W).astype(F32)
    cols = (t % GRID_W).astype(F32)
    n_pairs = A_ROPE // 2
    per_axis = n_pairs // 2
    inv = ROPE_THETA ** (-jnp.arange(per_axis, dtype=F32) / per_axis)
    ang = jnp.concatenate([rows[:, None] * inv, cols[:, None] * inv], axis=-1)
    cos = jnp.repeat(jnp.cos(ang), 2, axis=-1)
    sin = jnp.repeat(jnp.sin(ang), 2, axis=-1)
    sign = jnp.tile(jnp.array([-1.0, 1.0], F32), n_pairs)
    pad = HEAD_PAD - A_QK
    t1 = jnp.concatenate([jnp.ones((s, A_NOPE), F32), cos, jnp.zeros((s, pad), F32)], axis=-1)
    t2 = jnp.concatenate([jnp.zeros((s, A_NOPE), F32), sin * sign, jnp.zeros((s, pad), F32)], axis=-1)
    c1 = jnp.concatenate([jnp.ones((ctx_len, A_QK), F32), jnp.zeros((ctx_len, pad), F32)], axis=-1)
    c2 = jnp.zeros((ctx_len, HEAD_PAD), F32)
    return jnp.concatenate([c1, t1], axis=0), jnp.concatenate([c2, t2], axis=0)


_PAIR_SWAP = np.arange(A_ROPE) ^ 1


def _rope_lane_vec(g_tail, swapped):
    gt = g_tail[_PAIR_SWAP] if swapped else g_tail
    return jnp.concatenate([jnp.zeros((A_NOPE,), F32), gt, jnp.zeros((HEAD_PAD - A_QK,), F32)])[None]


def _mla_weights(w_in, w_uq, w_ukv, g_q, g_k):
    d = w_in.shape[0]
    q_end = A_Q_RANK
    kv_end = q_end + A_KV_RANK
    r_end = kv_end + A_ROPE
    z = lambda n: jnp.zeros((d, n), w_in.dtype)
    kr = w_in[:, kv_end:r_end]
    pad = HEAD_PAD - A_QK
    win = jnp.concatenate([w_in[:, :kv_end], z(A_NOPE), kr, z(pad), z(A_NOPE), kr[:, _PAIR_SWAP], z(pad),
                           w_in[:, r_end:]], axis=-1).astype(BF16)
    wq = w_uq.reshape(A_Q_RANK, A_HEADS, A_QK)
    zq = lambda n: jnp.zeros((A_Q_RANK, A_HEADS, n), wq.dtype)
    wq_main = jnp.concatenate([wq, zq(pad)], axis=-1)
    wq_swap = jnp.concatenate([zq(A_NOPE), wq[:, :, A_NOPE:][:, :, _PAIR_SWAP], zq(pad)], axis=-1)
    wq_ext = jnp.concatenate([wq_main.reshape(A_Q_RANK, -1), wq_swap.reshape(A_Q_RANK, -1)], axis=-1).astype(BF16)
    wkv = w_ukv.reshape(A_KV_RANK, A_HEADS, A_NOPE + A_V)
    wk = jnp.concatenate([wkv[:, :, :A_NOPE], jnp.zeros((A_KV_RANK, A_HEADS, HEAD_PAD - A_NOPE), wkv.dtype)], axis=-1)
    wkv_ext = jnp.concatenate([wk.reshape(A_KV_RANK, -1), wkv[:, :, A_NOPE:].reshape(A_KV_RANK, -1)],
                              axis=-1).astype(BF16)
    gq = jnp.concatenate([jnp.concatenate([g_q, jnp.zeros((pad,), F32)])[None], _rope_lane_vec(g_q[A_NOPE:], True)], axis=0)
    gk = jnp.concatenate([jnp.concatenate([g_k[:A_NOPE], jnp.zeros((HEAD_PAD - A_NOPE,), F32)])[None],
                          _rope_lane_vec(g_k[A_NOPE:], False), _rope_lane_vec(g_k[A_NOPE:], True)], axis=0)
    return win, wq_ext, wkv_ext, gq, gk


def _dft_mats(n, norm):
    k = jnp.arange(n, dtype=jnp.int32)
    ang = ((k[:, None] * k[None, :]) % n).astype(F32) * (2.0 * math.pi / n)
    return (jnp.cos(ang) * norm).astype(BF16), (jnp.sin(ang) * norm).astype(BF16)


def _channel_dft():
    c = np.arange(F_CH)
    ang = 2.0 * np.pi * ((c[:, None] * c[None, :]) % F_CH) / F_CH
    eye = np.eye(F_GROUPS)
    cb = np.kron(eye, np.cos(ang)) / math.sqrt(F_CH)
    sb = np.kron(eye, np.sin(ang)) / math.sqrt(F_CH)
    return jnp.asarray(np.concatenate([cb, sb], axis=1), BF16)


def _na_bias(rpb, rows_n):
    wr = min(WIN_R, rows_n)
    wc = min(WIN_C, GRID_W)
    p = np.arange(wr)
    a = np.arange(wr)
    dr = a[None, :] - p[:, None] + (WIN_R - 1)
    qc = np.arange(GRID_W)
    cs = np.clip(qc - wc // 2, 0, GRID_W - wc)
    kcol = np.arange(GRID_W)
    valid = (kcol[None, :] >= cs[:, None]) & (kcol[None, :] < cs[:, None] + wc)
    dc = np.clip(kcol[None, :] - qc[:, None] + (WIN_C - 1), 0, 2 * WIN_C - 2)
    tab = rpb[:, dr[:, None, :, None], dc[None, :, None, :]]
    tab = jnp.where(valid[None, None, :, None, :], tab.astype(F32), NEG_BIG)
    h = rpb.shape[0]
    return tab.transpose(1, 0, 2, 3, 4).reshape(wr, h, GRID_W, wr * GRID_W)


def kernel(x, c, ctx, c_ctx, w_mod, b_mod, norm_g, a_w_in, a_g_cq, a_g_ckv, a_w_uq, a_w_ukv, a_g_q, a_g_k, a_w_out,
           f_w_gate, f_w_up, f_w_down, c_w_in, c_g_q, c_g_k, c_rpb, c_w_out, m_w_router, m_w_gate, m_w_up, m_w_down):
    b, s, d = x.shape
    l = ctx.shape[1]
    depth = w_mod.shape[0]
    n = l + s
    assert l == TM and s % TM == 0 and s % GRID_W == 0

    mod_rows = 16
    cin = jnp.concatenate([c, c_ctx[None], jnp.zeros((mod_rows - b - 1, d), F32)], axis=0)
    mod_all = _modulation(cin, w_mod, b_mod).reshape(depth, mod_rows, 6, d)

    t1, t2 = _rope_tables(s, l)
    cn, sn = _dft_mats(s, 1.0 / math.sqrt(s))
    cc, sc = _dft_mats(l, 1.0 / math.sqrt(l))
    cb = _channel_dft()

    h = jnp.concatenate([ctx, x], axis=1)
    for i in range(depth):
        j = i // 2
        last = i == depth - 1
        mod = mod_all[i]
        g1 = norm_g[i, 0][None]
        g2 = norm_g[i, 1][None]
        if i % 2 == 0:
            win, wq_ext, wkv_ext, gq, gk = _mla_weights(a_w_in[j], a_w_uq[j], a_w_ukv[j], a_g_q[j], a_g_k[j])
            q, k, v, fcs = _mla_front(h, mod, g1, win, a_g_cq[j][None], a_g_ckv[j][None], wq_ext, wkv_ext,
                                      t1, t2, gq, gk, cb)
            att = _mla_attn(q, k, v, l)
            yf = _dft(fcs, cn, sn, cc, sc, l)
            h = _out_proj([att, yf], a_w_out[j].astype(BF16), h, mod)
            h = _ffn(h, mod, g2, f_w_gate[j].astype(BF16), f_w_up[j].astype(BF16), f_w_down[j].astype(BF16))
        else:
            gq2 = jnp.tile(c_g_q[j], 2)[None]
            gk2 = jnp.tile(c_g_k[j], 2)[None]
            q, k, v = _na_front(h, mod, g1, c_w_in[j].astype(BF16), gq2, gk2)
            bias = _na_bias(c_rpb[j], s // GRID_W)
            att = _na_attn(q, k, v, bias, l)
            h = _out_proj([att], c_w_out[j].astype(BF16), h, mod)
            wr = jnp.concatenate([m_w_router[j], jnp.zeros((d, LANES - N_EXPERTS), F32)], axis=-1)
            xn2, comb = _router(h, mod, g2, wr)
            y = _moe(xn2.reshape(b * n, d), comb.reshape(b * n, LANES),
                     m_w_gate[j].astype(BF16), m_w_up[j].astype(BF16), m_w_down[j].astype(BF16))
            h = _resid(h, y.reshape(b, n, d), mod, last)
    if h.shape[1] != s:
        h = h[:, l:]
    return h
```

```python
import functools
import math

import jax
import jax.numpy as jnp
import numpy as np
from jax import lax
from jax.experimental import pallas as pl
from jax.experimental.pallas import tpu as pltpu

F32 = jnp.float32
BF16 = jnp.bfloat16

GRID_W = 64
A_HEADS = 12
A_NOPE = 64
A_ROPE = 32
A_QK = A_NOPE + A_ROPE
A_V = 64
A_Q_RANK = 256
A_KV_RANK = 128
F_GROUPS = 4
F_CH = 64
F_WIDTH = F_GROUPS * F_CH
C_HEADS = 16
C_DH = 64
WIN_R = 8
WIN_C = 16
N_EXPERTS = 8
ROPE_THETA = 10000.0
EPS = 1e-6

LANES = 128
HEAD_PAD = 128
TM = 256
MOE_TM = 1024
NA_GROUP = 4
NA_WINDOW = NA_GROUP + WIN_R
VMEM_LIMIT = 52 * 1024 * 1024
NEG_BIG = -1e30


def _params(sem, vmem=VMEM_LIMIT):
    return pltpu.CompilerParams(dimension_semantics=sem, vmem_limit_bytes=vmem)


def _rms(x, g, n=None):
    n = x.shape[-1] if n is None else n
    ss = jnp.sum(x * x, axis=-1, keepdims=True)
    return x * lax.rsqrt(ss * (1.0 / n) + EPS) * g


def _silu(x):
    return x * (1.0 / (1.0 + jnp.exp(-x)))


def _norm_mod(h_ref, g_ref, mod_ref, row):
    x = h_ref[...]
    xn = _rms(x, g_ref[...])
    return xn * (1.0 + mod_ref[row + 1:row + 2, :]) + mod_ref[row:row + 1, :]


def _mod_kernel(c_ref, w_ref, b_ref, o_ref):
    s = _silu(c_ref[...])
    o_ref[...] = jnp.dot(s, w_ref[...], precision=lax.Precision.HIGHEST,
                         preferred_element_type=F32) + b_ref[...]


def _modulation(cin, w_mod, b_mod):
    depth, d, n = w_mod.shape
    tn = 1536
    rows = cin.shape[0]
    return pl.pallas_call(
        _mod_kernel,
        grid=(depth, n // tn),
        in_specs=[
            pl.BlockSpec((rows, d), lambda l, j: (0, 0)),
            pl.BlockSpec((None, d, tn), lambda l, j: (l, 0, j)),
            pl.BlockSpec((None, 1, tn), lambda l, j: (l, 0, j)),
        ],
        out_specs=pl.BlockSpec((None, rows, tn), lambda l, j: (l, 0, j)),
        out_shape=jax.ShapeDtypeStruct((depth, rows, n), F32),
        compiler_params=_params(("arbitrary", "arbitrary")),
        name="modulation",
    )(cin, w_mod, b_mod.reshape(depth, 1, n))


def _mla_front_kernel(h_ref, mod_ref, g_ref, win_ref, gcq_ref, gckv_ref, wq_ref, wkv_ref,
                      t1_ref, t2_ref, gq_ref, gk_ref, cb_ref,
                      q_ref, k_ref, v_ref, fcs_ref):
    xn = _norm_mod(h_ref, g_ref, mod_ref, 0)
    px = jnp.dot(xn.astype(BF16), win_ref[...], preferred_element_type=F32)
    t1 = t1_ref[...]
    t2 = t2_ref[...]
    hw = A_HEADS * HEAD_PAD

    qn = _rms(px[:, 0:A_Q_RANK], gcq_ref[...]).astype(BF16)
    qq = jnp.dot(qn, wq_ref[...], preferred_element_type=F32)
    aq = t1 * gq_ref[0:1, :]
    bq = t2 * gq_ref[1:2, :]
    scale = A_QK ** -0.5
    for h in range(A_HEADS):
        qm = qq[:, h * HEAD_PAD:(h + 1) * HEAD_PAD]
        qs = qq[:, hw + h * HEAD_PAD:hw + (h + 1) * HEAD_PAD]
        ss = jnp.sum(qm * qm, axis=-1, keepdims=True)
        inv = lax.rsqrt(ss * (1.0 / A_QK) + EPS) * scale
        q_ref[:, h * HEAD_PAD:(h + 1) * HEAD_PAD] = ((qm * aq + qs * bq) * inv).astype(BF16)

    kvn = _rms(px[:, A_Q_RANK:A_Q_RANK + A_KV_RANK], gckv_ref[...]).astype(BF16)
    kv = jnp.dot(kvn, wkv_ref[...], preferred_element_type=F32)
    krm = px[:, 384:512]
    krs = px[:, 512:640]
    tail = krm * (t1 * gk_ref[1:2, :]) + krs * (t2 * gk_ref[2:3, :])
    ssr = jnp.sum(krm * krm, axis=-1, keepdims=True)
    gkn = gk_ref[0:1, :]
    for h in range(A_HEADS):
        km = kv[:, h * HEAD_PAD:(h + 1) * HEAD_PAD]
        ss = jnp.sum(km * km, axis=-1, keepdims=True) + ssr
        inv = lax.rsqrt(ss * (1.0 / A_QK) + EPS)
        k_ref[:, h * HEAD_PAD:(h + 1) * HEAD_PAD] = ((km * gkn + tail) * inv).astype(BF16)
    v_ref[...] = kv[:, hw:].astype(BF16)

    f = px[:, 640:896].astype(BF16)
    fcs_ref[...] = jnp.dot(f, cb_ref[...], preferred_element_type=F32).astype(BF16)


def _mla_front(h, mod, g, win, gcq, gckv, wq, wkv, t1, t2, gq, gk, cb):
    b, n, d = h.shape
    nt = n // TM
    hw = A_HEADS * HEAD_PAD
    tile = lambda w: pl.BlockSpec((None, TM, w), lambda i, t: (i, t, 0))
    full = lambda a: pl.BlockSpec(a.shape, lambda i, t: (0,) * a.ndim)
    return pl.pallas_call(
        _mla_front_kernel,
        grid=(b, nt),
        in_specs=[
            tile(d),
            pl.BlockSpec((None, 6, d), lambda i, t: (jnp.where(t == 0, b, i), 0, 0)),
            full(g), full(win), full(gcq), full(gckv), full(wq), full(wkv),
            pl.BlockSpec((TM, LANES), lambda i, t: (t, 0)),
            pl.BlockSpec((TM, LANES), lambda i, t: (t, 0)),
            full(gq), full(gk), full(cb),
        ],
        out_specs=[tile(hw), tile(hw), tile(A_HEADS * A_V), tile(2 * F_WIDTH)],
        out_shape=[
            jax.ShapeDtypeStruct((b, n, hw), BF16),
            jax.ShapeDtypeStruct((b, n, hw), BF16),
            jax.ShapeDtypeStruct((b, n, A_HEADS * A_V), BF16),
            jax.ShapeDtypeStruct((b, n, 2 * F_WIDTH), BF16),
        ],
        compiler_params=_params(("arbitrary", "arbitrary")),
        name="mla_front",
    )(h, mod, g, win, gcq, gckv, wq, wkv, t1, t2, gq, gk, cb)


def _softmax_pv(q, k, v):
    s = lax.dot_general(q, k, (((1,), (1,)), ((), ())), preferred_element_type=F32)
    m = jnp.max(s, axis=-1, keepdims=True)
    p = jnp.exp(s - m)
    l = jnp.sum(p, axis=-1, keepdims=True)
    o = jnp.dot(p.astype(BF16), v, preferred_element_type=F32)
    return o * (1.0 / l)


def _mla_attn_kernel(q_ref, k_ref, v_ref, o_ref, *, ctx_len):
    t = pl.program_id(2)
    lane = lax.broadcasted_iota(jnp.int32, o_ref.shape, 1)

    def run(kl):
        outs = []
        for hh in range(2):
            q = q_ref[:, hh * HEAD_PAD:(hh + 1) * HEAD_PAD]
            k = k_ref[0:kl, hh * HEAD_PAD:(hh + 1) * HEAD_PAD]
            outs.append(_softmax_pv(q, k, v_ref[0:kl, :]))
        o_ref[...] = jnp.where(lane < A_V, outs[0], outs[1]).astype(BF16)

    @pl.when(t == 0)
    def _():
        run(ctx_len)

    @pl.when(t > 0)
    def _():
        run(k_ref.shape[0])


def _mla_attn(q, k, v, ctx_len):
    b, n, _ = q.shape
    hp = A_HEADS // 2
    return pl.pallas_call(
        functools.partial(_mla_attn_kernel, ctx_len=ctx_len),
        grid=(b, hp, n // TM),
        in_specs=[
            pl.BlockSpec((None, TM, 2 * HEAD_PAD), lambda i, p, t: (i, t, p)),
            pl.BlockSpec((None, n, 2 * HEAD_PAD), lambda i, p, t: (i, 0, p)),
            pl.BlockSpec((None, n, 2 * A_V), lambda i, p, t: (i, 0, p)),
        ],
        out_specs=pl.BlockSpec((None, TM, 2 * A_V), lambda i, p, t: (i, t, p)),
        out_shape=jax.ShapeDtypeStruct((b, n, A_HEADS * A_V), BF16),
        compiler_params=_params(("arbitrary", "arbitrary", "arbitrary")),
        name="mla_attn",
    )(q, k, v)


def _dft_kernel(cn_ref, sn_ref, cc_ref, sc_ref, fcs_ref, y_ref, *, ctx_len):
    t = pl.program_id(1)

    @pl.when(t == 0)
    def _():
        fc = fcs_ref[0:ctx_len, 0:F_WIDTH]
        fs = fcs_ref[0:ctx_len, F_WIDTH:2 * F_WIDTH]
        y = (jnp.dot(cc_ref[...], fc, preferred_element_type=F32)
             - jnp.dot(sc_ref[...], fs, preferred_element_type=F32))
        y_ref[...] = y.astype(BF16)

    @pl.when(t > 0)
    def _():
        n = fcs_ref.shape[0]
        fc = fcs_ref[ctx_len:n, 0:F_WIDTH]
        fs = fcs_ref[ctx_len:n, F_WIDTH:2 * F_WIDTH]
        y = (jnp.dot(cn_ref[...], fc, preferred_element_type=F32)
             - jnp.dot(sn_ref[...], fs, preferred_element_type=F32))
        y_ref[...] = y.astype(BF16)


def _dft(fcs, cn, sn, cc, sc, ctx_len):
    b, n, _ = fcs.shape
    s = n - ctx_len
    assert ctx_len == TM
    return pl.pallas_call(
        functools.partial(_dft_kernel, ctx_len=ctx_len),
        grid=(b, n // TM),
        in_specs=[
            pl.BlockSpec((TM, s), lambda i, t: (jnp.maximum(t - 1, 0), 0)),
            pl.BlockSpec((TM, s), lambda i, t: (jnp.maximum(t - 1, 0), 0)),
            pl.BlockSpec((ctx_len, ctx_len), lambda i, t: (0, 0)),
            pl.BlockSpec((ctx_len, ctx_len), lambda i, t: (0, 0)),
            pl.BlockSpec((None, n, 2 * F_WIDTH), lambda i, t: (i, 0, 0)),
        ],
        out_specs=pl.BlockSpec((None, TM, F_WIDTH), lambda i, t: (i, t, 0)),
        out_shape=jax.ShapeDtypeStruct((b, n, F_WIDTH), BF16),
        compiler_params=_params(("arbitrary", "arbitrary")),
        name="fourier_dft",
    )(cn, sn, cc, sc, fcs)


def _out_proj_kernel(*refs, widths):
    mix_refs = refs[:len(widths)]
    w_ref, h_ref, mod_ref, o_ref = refs[len(widths):]
    acc = None
    off = 0
    for m_ref, wd in zip(mix_refs, widths):
        part = jnp.dot(m_ref[...], w_ref[off:off + wd, :], preferred_element_type=F32)
        acc = part if acc is None else acc + part
        off += wd
    o_ref[...] = h_ref[...] + mod_ref[2:3, :] * acc


def _out_proj(mixes, w, h, mod):
    b, n, d = h.shape
    widths = tuple(m.shape[-1] for m in mixes)
    tile = lambda wd: pl.BlockSpec((None, TM, wd), lambda i, t: (i, t, 0))
    return pl.pallas_call(
        functools.partial(_out_proj_kernel, widths=widths),
        grid=(b, n // TM),
        in_specs=[tile(wd) for wd in widths] + [
            pl.BlockSpec(w.shape, lambda i, t: (0, 0)),
            tile(d),
            pl.BlockSpec((None, 6, d), lambda i, t: (jnp.where(t == 0, b, i), 0, 0)),
        ],
        out_specs=tile(d),
        out_shape=jax.ShapeDtypeStruct((b, n, d), F32),
        compiler_params=_params(("arbitrary", "arbitrary")),
        name="out_proj",
    )(*mixes, w, h, mod)


def _ffn_kernel(h_ref, mod_ref, g_ref, wg_ref, wu_ref, wd_ref, o_ref):
    xn = _norm_mod(h_ref, g_ref, mod_ref, 3).astype(BF16)
    gt = jnp.dot(xn, wg_ref[...], preferred_element_type=F32)
    up = jnp.dot(xn, wu_ref[...], preferred_element_type=F32)
    a = (_silu(gt) * up).astype(BF16)
    y = jnp.dot(a, wd_ref[...], preferred_element_type=F32)
    o_ref[...] = h_ref[...] + mod_ref[5:6, :] * y


def _ffn(h, mod, g, wg, wu, wd):
    b, n, d = h.shape
    tile = pl.BlockSpec((None, TM, d), lambda i, t: (i, t, 0))
    full = lambda a: pl.BlockSpec(a.shape, lambda i, t: (0,) * a.ndim)
    return pl.pallas_call(
        _ffn_kernel,
        grid=(b, n // TM),
        in_specs=[
            tile,
            pl.BlockSpec((None, 6, d), lambda i, t: (jnp.where(t == 0, b, i), 0, 0)),
            full(g), full(wg), full(wu), full(wd),
        ],
        out_specs=tile,
        out_shape=jax.ShapeDtypeStruct((b, n, d), F32),
        compiler_params=_params(("arbitrary", "arbitrary")),
        name="ffn_dense",
    )(h, mod, g, wg, wu, wd)


def _na_front_kernel(h_ref, mod_ref, g_ref, win_ref, gq_ref, gk_ref, q_ref, k_ref, v_ref):
    xn = _norm_mod(h_ref, g_ref, mod_ref, 0)
    px = jnp.dot(xn.astype(BF16), win_ref[...], preferred_element_type=F32)
    width = C_HEADS * C_DH
    lane = lax.broadcasted_iota(jnp.int32, (px.shape[0], LANES), 1)
    low = lane < C_DH
    scale = C_DH ** -0.5

    def norm_pairs(base, g_ref_, out_ref, mult):
        gg = g_ref_[...]
        for j in range(width // LANES):
            x = px[:, base + j * LANES:base + (j + 1) * LANES]
            x2 = x * x
            s_all = jnp.sum(x2, axis=-1, keepdims=True)
            s_lo = jnp.sum(jnp.where(low, x2, 0.0), axis=-1, keepdims=True)
            ss = jnp.where(low, s_lo, s_all - s_lo)
            inv = lax.rsqrt(ss * (1.0 / C_DH) + EPS)
            if mult != 1.0:
                inv = inv * mult
            out_ref[:, j * LANES:(j + 1) * LANES] = (x * inv * gg).astype(BF16)

    norm_pairs(0, gq_ref, q_ref, scale)
    norm_pairs(width, gk_ref, k_ref, 1.0)
    v_ref[...] = px[:, 2 * width:].astype(BF16)


def _na_front(h, mod, g, win, gq, gk):
    b, n, d = h.shape
    width = C_HEADS * C_DH
    tile = lambda w: pl.BlockSpec((None, TM, w), lambda i, t: (i, t, 0))
    full = lambda a: pl.BlockSpec(a.shape, lambda i, t: (0,) * a.ndim)
    return pl.pallas_call(
        _na_front_kernel,
        grid=(b, n // TM),
        in_specs=[
            tile(d),
            pl.BlockSpec((None, 6, d), lambda i, t: (jnp.where(t == 0, b, i), 0, 0)),
            full(g), full(win), full(gq), full(gk),
        ],
        out_specs=[tile(width)] * 3,
        out_shape=[jax.ShapeDtypeStruct((b, n, width), BF16)] * 3,
        compiler_params=_params(("arbitrary", "arbitrary")),
        name="na_front",
    )(h, mod, g, win, gq, gk)


def _na_attn_kernel(q_ref, k_ref, v_ref, bias_ref, o_ref, *, ctx_len, rows_n):
    qlen = NA_GROUP * GRID_W
    wlen = NA_WINDOW * GRID_W
    n_groups = rows_n // NA_GROUP
    lane_q = lax.broadcasted_iota(jnp.int32, (qlen, LANES), 1)
    kc = k_ref[0:ctx_len, :]
    vc = v_ref[0:ctx_len, :]

    qc = q_ref[0:ctx_len, :]
    lane_c = lax.broadcasted_iota(jnp.int32, (ctx_len, LANES), 1)
    outs = []
    for hh in range(2):
        sel = (lane_c < C_DH) if hh == 0 else (lane_c >= C_DH)
        outs.append(_softmax_pv(jnp.where(sel, qc, jnp.zeros_like(qc)), kc, vc))
    o_ref[0:ctx_len, :] = jnp.where(lane_c < C_DH, outs[0], outs[1]).astype(BF16)

    def group_body(g, carry):
        start = jnp.clip(g * NA_GROUP - WIN_R // 2, 0, rows_n - NA_WINDOW)
        pat = jnp.where(g == 0, 0, jnp.where(g == n_groups - 1, 2, 1))
        q0 = pl.multiple_of(ctx_len + g * qlen, GRID_W)
        k0 = pl.multiple_of(ctx_len + start * GRID_W, GRID_W)
        qr = q_ref[pl.ds(q0, qlen), :]
        kw = k_ref[pl.ds(k0, wlen), :]
        vw = v_ref[pl.ds(k0, wlen), :]
        res = []
        for hh in range(2):
            sel = (lane_q < C_DH) if hh == 0 else (lane_q >= C_DH)
            qh = jnp.where(sel, qr, jnp.zeros_like(qr))
            s_loc = lax.dot_general(qh, kw, (((1,), (1,)), ((), ())),
                                    preferred_element_type=F32) + bias_ref[pat, hh]
            s_ctx = lax.dot_general(qh, kc, (((1,), (1,)), ((), ())), preferred_element_type=F32)
            m = jnp.maximum(jnp.max(s_loc, axis=-1, keepdims=True),
                            jnp.max(s_ctx, axis=-1, keepdims=True))
            p_loc = jnp.exp(s_loc - m)
            p_ctx = jnp.exp(s_ctx - m)
            l = jnp.sum(p_loc, axis=-1, keepdims=True) + jnp.sum(p_ctx, axis=-1, keepdims=True)
            o = (jnp.dot(p_loc.astype(BF16), vw, preferred_element_type=F32)
                 + jnp.dot(p_ctx.astype(BF16), vc, preferred_element_type=F32))
            res.append(o * (1.0 / l))
        o_ref[pl.ds(q0, qlen), :] = jnp.where(lane_q < C_DH, res[0], res[1]).astype(BF16)
        return carry

    lax.fori_loop(0, n_groups, group_body, 0, unroll=2)


def _na_attn(q, k, v, bias, ctx_len):
    b, n, width = q.shape
    rows_n = (n - ctx_len) // GRID_W
    hp = C_HEADS // 2
    blk = pl.BlockSpec((None, n, LANES), lambda i, p: (i, 0, p))
    return pl.pallas_call(
        functools.partial(_na_attn_kernel, ctx_len=ctx_len, rows_n=rows_n),
        grid=(b, hp),
        in_specs=[blk, blk, blk,
                  pl.BlockSpec((3, 2) + bias.shape[2:], lambda i, p: (0, p, 0, 0))],
        out_specs=blk,
        out_shape=jax.ShapeDtypeStruct((b, n, width), BF16),
        compiler_params=_params(("arbitrary", "arbitrary")),
        name="na_attn",
    )(q, k, v, bias)


def _router_kernel(h_ref, mod_ref, g_ref, wr_ref, xn_ref, comb_ref):
    xn = _norm_mod(h_ref, g_ref, mod_ref, 3)
    xn_ref[...] = xn.astype(BF16)
    logits = jnp.dot(xn, wr_ref[...], precision=lax.Precision.HIGHEST, preferred_element_type=F32)
    lane = lax.broadcasted_iota(jnp.int32, logits.shape, 1).astype(F32)
    lg = jnp.where(lane < N_EXPERTS, logits, -jnp.inf)
    m1 = jnp.max(lg, axis=-1, keepdims=True)
    i1 = jnp.min(jnp.where(lg == m1, lane, float(LANES)), axis=-1, keepdims=True)
    lg2 = jnp.where(lane == i1, -jnp.inf, lg)
    m2 = jnp.max(lg2, axis=-1, keepdims=True)
    i2 = jnp.min(jnp.where(lg2 == m2, lane, float(LANES)), axis=-1, keepdims=True)
    e2 = jnp.exp(m2 - m1)
    den = 1.0 / (1.0 + e2)
    comb_ref[...] = jnp.where(lane == i1, den, 0.0) + jnp.where(lane == i2, e2 * den, 0.0)


def _router(h, mod, g, wr):
    b, n, d = h.shape
    tile = lambda w, dt=None: pl.BlockSpec((None, TM, w), lambda i, t: (i, t, 0))
    full = lambda a: pl.BlockSpec(a.shape, lambda i, t: (0,) * a.ndim)
    return pl.pallas_call(
        _router_kernel,
        grid=(b, n // TM),
        in_specs=[
            tile(d),
            pl.BlockSpec((None, 6, d), lambda i, t: (jnp.where(t == 0, b, i), 0, 0)),
            full(g), full(wr),
        ],
        out_specs=[tile(d), tile(LANES)],
        out_shape=[jax.ShapeDtypeStruct((b, n, d), BF16), jax.ShapeDtypeStruct((b, n, LANES), F32)],
        compiler_params=_params(("arbitrary", "arbitrary")),
        name="moe_router",
    )(h, mod, g, wr)


def _moe_kernel(x_ref, comb_ref, wg_ref, wu_ref, wd_ref, y_ref, acc_ref):
    e = pl.program_id(1)

    @pl.when(e == 0)
    def _():
        acc_ref[...] = jnp.zeros_like(acc_ref)

    x = x_ref[...]
    gt = jnp.dot(x, wg_ref[...], preferred_element_type=F32)
    up = jnp.dot(x, wu_ref[...], preferred_element_type=F32)
    a = (_silu(gt) * up).astype(BF16)
    comb = comb_ref[...]
    lane = lax.broadcasted_iota(jnp.int32, comb.shape, 1)
    ce = jnp.sum(jnp.where(lane == e, comb, 0.0), axis=-1, keepdims=True)
    acc_ref[...] += ce * jnp.dot(a, wd_ref[...], preferred_element_type=F32)

    @pl.when(e == pl.num_programs(1) - 1)
    def _():
        y_ref[...] = acc_ref[...]


def _moe(x, comb, wg, wu, wd):
    t, d = x.shape
    ne, _, f = wg.shape
    tm = MOE_TM if t % MOE_TM == 0 else TM
    return pl.pallas_call(
        _moe_kernel,
        grid=(t // tm, ne),
        in_specs=[
            pl.BlockSpec((tm, d), lambda i, e: (i, 0)),
            pl.BlockSpec((tm, LANES), lambda i, e: (i, 0)),
            pl.BlockSpec((None, d, f), lambda i, e: (e, 0, 0)),
            pl.BlockSpec((None, d, f), lambda i, e: (e, 0, 0)),
            pl.BlockSpec((None, f, d), lambda i, e: (e, 0, 0)),
        ],
        out_specs=pl.BlockSpec((tm, d), lambda i, e: (i, 0)),
        out_shape=jax.ShapeDtypeStruct((t, d), F32),
        scratch_shapes=[pltpu.VMEM((tm, d), F32)],
        compiler_params=_params(("arbitrary", "arbitrary")),
        name="moe_experts",
    )(x, comb, wg, wu, wd)


def _resid_kernel(h_ref, y_ref, mod_ref, o_ref):
    o_ref[...] = h_ref[...] + mod_ref[5:6, :] * y_ref[...]


def _resid(h, y, mod, latent_only):
    b, n, d = h.shape
    skip = 1 if latent_only else 0
    nt = n // TM - skip
    tile = pl.BlockSpec((None, TM, d), lambda i, t: (i, t + skip, 0))
    if latent_only:
        mod_spec = pl.BlockSpec((None, 6, d), lambda i, t: (i, 0, 0))
    else:
        mod_spec = pl.BlockSpec((None, 6, d), lambda i, t: (jnp.where(t == 0, b, i), 0, 0))
    return pl.pallas_call(
        _resid_kernel,
        grid=(b, nt),
        in_specs=[tile, tile, mod_spec],
        out_specs=pl.BlockSpec((None, TM, d), lambda i, t: (i, t, 0)),
        out_shape=jax.ShapeDtypeStruct((b, nt * TM, d), F32),
        compiler_params=_params(("arbitrary", "arbitrary")),
        name="moe_residual",
    )(h, y, mod)


def _rope_tables(s, ctx_len):
    t = jnp.arange(s)
    rows = (t // GRID_W).astype(F32)
    cols = (t % GRID_W).astype(F32)
    n_pairs = A_ROPE // 2
    per_axis = n_pairs // 2
    inv = ROPE_THETA ** (-jnp.arange(per_axis, dtype=F32) / per_axis)
    ang = jnp.concatenate([rows[:, None] * inv, cols[:, None] * inv], axis=-1)
    cos = jnp.repeat(jnp.cos(ang), 2, axis=-1)
    sin = jnp.repeat(jnp.sin(ang), 2, axis=-1)
    sign = jnp.tile(jnp.array([-1.0, 1.0], F32), n_pairs)
    pad = HEAD_PAD - A_QK
    t1 = jnp.concatenate([jnp.ones((s, A_NOPE), F32), cos, jnp.zeros((s, pad), F32)], axis=-1)
    t2 = jnp.concatenate([jnp.zeros((s, A_NOPE), F32), sin * sign, jnp.zeros((s, pad), F32)], axis=-1)
    c1 = jnp.concatenate([jnp.ones((ctx_len, A_QK), F32), jnp.zeros((ctx_len, pad), F32)], axis=-1)
    c2 = jnp.zeros((ctx_len, HEAD_PAD), F32)
    return jnp.concatenate([c1, t1], axis=0), jnp.concatenate([c2, t2], axis=0)


_PAIR_SWAP = np.arange(A_ROPE) ^ 1


def _rope_lane_vec(g_tail, swapped):
    gt = g_tail[_PAIR_SWAP] if swapped else g_tail
    return jnp.concatenate([jnp.zeros((A_NOPE,), F32), gt, jnp.zeros((HEAD_PAD - A_QK,), F32)])[None]


def _mla_weights(w_in, w_uq, w_ukv, g_q, g_k):
    d = w_in.shape[0]
    q_end = A_Q_RANK
    kv_end = q_end + A_KV_RANK
    r_end = kv_end + A_ROPE
    z = lambda n: jnp.zeros((d, n), w_in.dtype)
    kr = w_in[:, kv_end:r_end]
    pad = HEAD_PAD - A_QK
    win = jnp.concatenate([w_in[:, :kv_end], z(A_NOPE), kr, z(pad), z(A_NOPE), kr[:, _PAIR_SWAP], z(pad),
                           w_in[:, r_end:]], axis=-1).astype(BF16)
    wq = w_uq.reshape(A_Q_RANK, A_HEADS, A_QK)
    zq = lambda n: jnp.zeros((A_Q_RANK, A_HEADS, n), wq.dtype)
    wq_main = jnp.concatenate([wq, zq(pad)], axis=-1)
    wq_swap = jnp.concatenate([zq(A_NOPE), wq[:, :, A_NOPE:][:, :, _PAIR_SWAP], zq(pad)], axis=-1)
    wq_ext = jnp.concatenate([wq_main.reshape(A_Q_RANK, -1), wq_swap.reshape(A_Q_RANK, -1)], axis=-1).astype(BF16)
    wkv = w_ukv.reshape(A_KV_RANK, A_HEADS, A_NOPE + A_V)
    wk = jnp.concatenate([wkv[:, :, :A_NOPE], jnp.zeros((A_KV_RANK, A_HEADS, HEAD_PAD - A_NOPE), wkv.dtype)], axis=-1)
    wkv_ext = jnp.concatenate([wk.reshape(A_KV_RANK, -1), wkv[:, :, A_NOPE:].reshape(A_KV_RANK, -1)],
                              axis=-1).astype(BF16)
    gq = jnp.concatenate([jnp.concatenate([g_q, jnp.zeros((pad,), F32)])[None], _rope_lane_vec(g_q[A_NOPE:], True)], axis=0)
    gk = jnp.concatenate([jnp.concatenate([g_k[:A_NOPE], jnp.zeros((HEAD_PAD - A_NOPE,), F32)])[None],
                          _rope_lane_vec(g_k[A_NOPE:], False), _rope_lane_vec(g_k[A_NOPE:], True)], axis=0)
    return win, wq_ext, wkv_ext, gq, gk


def _dft_mats(n, norm):
    k = jnp.arange(n, dtype=jnp.int32)
    ang = ((k[:, None] * k[None, :]) % n).astype(F32) * (2.0 * math.pi / n)
    return (jnp.cos(ang) * norm).astype(BF16), (jnp.sin(ang) * norm).astype(BF16)


def _channel_dft():
    c = np.arange(F_CH)
    ang = 2.0 * np.pi * ((c[:, None] * c[None, :]) % F_CH) / F_CH
    eye = np.eye(F_GROUPS)
    cb = np.kron(eye, np.cos(ang)) / math.sqrt(F_CH)
    sb = np.kron(eye, np.sin(ang)) / math.sqrt(F_CH)
    return jnp.asarray(np.concatenate([cb, sb], axis=1), BF16)


def _na_group_rows(rows_n):
    assert rows_n % NA_GROUP == 0 and rows_n >= NA_WINDOW + NA_GROUP
    n_groups = rows_n // NA_GROUP
    n_dr = 2 * WIN_R - 1
    idx = np.full((3, NA_GROUP, NA_WINDOW), n_dr, np.int32)
    seen = {}
    for g in range(n_groups):
        gs = int(np.clip(g * NA_GROUP - WIN_R // 2, 0, rows_n - NA_WINDOW))
        pat = 0 if g == 0 else (2 if g == n_groups - 1 else 1)
        cur = np.full((NA_GROUP, NA_WINDOW), n_dr, np.int32)
        for qi in range(NA_GROUP):
            r = g * NA_GROUP + qi
            start = int(np.clip(r - WIN_R // 2, 0, rows_n - WIN_R))
            assert gs <= start and start + WIN_R <= gs + NA_WINDOW
            for a in range(start, start + WIN_R):
                cur[qi, a - gs] = a - r + (WIN_R - 1)
        assert pat not in seen or np.array_equal(seen[pat], cur)
        seen[pat] = cur
        idx[pat] = cur
    return idx


def _na_bias(rpb, rows_n):
    h, n_dr, n_dc = rpb.shape
    qc = np.arange(GRID_W)
    cs = np.clip(qc - WIN_C // 2, 0, GRID_W - WIN_C)
    kcol = np.arange(GRID_W)
    valid = (kcol[None, :] >= cs[:, None]) & (kcol[None, :] < cs[:, None] + WIN_C)
    dc = kcol[None, :] - qc[:, None] + (WIN_C - 1)
    onehot = (valid[:, :, None] & (dc[:, :, None] == np.arange(n_dc)[None, None, :])).astype(np.float32)
    col = jnp.einsum('hdc,qkc->hdqk', rpb.astype(F32), jnp.asarray(onehot), precision=lax.Precision.HIGHEST)
    col = jnp.where(jnp.asarray(valid)[None, None], col, NEG_BIG)
    col = jnp.concatenate([col, jnp.full((h, 1, GRID_W, GRID_W), NEG_BIG, F32)], axis=1)
    idx = _na_group_rows(rows_n)
    blocks = [[jnp.stack([col[:, int(idx[p, qi, a])] for a in range(NA_WINDOW)], axis=2)
               for qi in range(NA_GROUP)] for p in range(3)]
    tab = jnp.stack([jnp.stack(bq, axis=1) for bq in blocks], axis=0)
    return tab.reshape(3, h, NA_GROUP * GRID_W, NA_WINDOW * GRID_W)


def kernel(x, c, ctx, c_ctx, w_mod, b_mod, norm_g, a_w_in, a_g_cq, a_g_ckv, a_w_uq, a_w_ukv, a_g_q, a_g_k, a_w_out,
           f_w_gate, f_w_up, f_w_down, c_w_in, c_g_q, c_g_k, c_rpb, c_w_out, m_w_router, m_w_gate, m_w_up, m_w_down):
    b, s, d = x.shape
    l = ctx.shape[1]
    depth = w_mod.shape[0]
    n = l + s
    assert l == TM and s % TM == 0 and s % GRID_W == 0

    mod_rows = 16
    cin = jnp.concatenate([c, c_ctx[None], jnp.zeros((mod_rows - b - 1, d), F32)], axis=0)
    mod_all = _modulation(cin, w_mod, b_mod).reshape(depth, mod_rows, 6, d)

    t1, t2 = _rope_tables(s, l)
    cn, sn = _dft_mats(s, 1.0 / math.sqrt(s))
    cc, sc = _dft_mats(l, 1.0 / math.sqrt(l))
    cb = _channel_dft()

    h = jnp.concatenate([ctx, x], axis=1)
    for i in range(depth):
        j = i // 2
        last = i == depth - 1
        mod = mod_all[i]
        g1 = norm_g[i, 0][None]
        g2 = norm_g[i, 1][None]
        if i % 2 == 0:
            win, wq_ext, wkv_ext, gq, gk = _mla_weights(a_w_in[j], a_w_uq[j], a_w_ukv[j], a_g_q[j], a_g_k[j])
            q, k, v, fcs = _mla_front(h, mod, g1, win, a_g_cq[j][None], a_g_ckv[j][None], wq_ext, wkv_ext,
                                      t1, t2, gq, gk, cb)
            att = _mla_attn(q, k, v, l)
            yf = _dft(fcs, cn, sn, cc, sc, l)
            h = _out_proj([att, yf], a_w_out[j].astype(BF16), h, mod)
            h = _ffn(h, mod, g2, f_w_gate[j].astype(BF16), f_w_up[j].astype(BF16), f_w_down[j].astype(BF16))
        else:
            gq2 = jnp.tile(c_g_q[j], 2)[None]
            gk2 = jnp.tile(c_g_k[j], 2)[None]
            q, k, v = _na_front(h, mod, g1, c_w_in[j].astype(BF16), gq2, gk2)
            bias = _na_bias(c_rpb[j], s // GRID_W)
            att = _na_attn(q, k, v, bias, l)
            h = _out_proj([att], c_w_out[j].astype(BF16), h, mod)
            wr = jnp.concatenate([m_w_router[j], jnp.zeros((d, LANES - N_EXPERTS), F32)], axis=-1)
            xn2, comb = _router(h, mod, g2, wr)
            y = _moe(xn2.reshape(b * n, d), comb.reshape(b * n, LANES),
                     m_w_gate[j].astype(BF16), m_w_up[j].astype(BF16), m_w_down[j].astype(BF16))
            h = _resid(h, y.reshape(b, n, d), mod, last)
    if h.shape[1] != s:
        h = h[:, l:]
    return h
```

```python
import functools
import math

import jax
import jax.numpy as jnp
import numpy as np
from jax import lax
from jax.experimental import pallas as pl
from jax.experimental.pallas import tpu as pltpu

F32 = jnp.float32
BF16 = jnp.bfloat16

GRID_W = 64
A_HEADS = 12
A_NOPE = 64
A_ROPE = 32
A_QK = A_NOPE + A_ROPE
A_V = 64
A_Q_RANK = 256
A_KV_RANK = 128
F_GROUPS = 4
F_CH = 64
F_WIDTH = F_GROUPS * F_CH
C_HEADS = 16
C_DH = 64
WIN_R = 8
WIN_C = 16
N_EXPERTS = 8
ROPE_THETA = 10000.0
EPS = 1e-6

LANES = 128
HEAD_PAD = 128
TM = 256
MOE_TM = 1024
LOG2E = math.log2(math.e)
MLA_SAFE_BOUND = 40.0
MLA_KCHUNK = 512
MLA_HPS = 2
NA_GROUP = 4
NA_WINDOW = NA_GROUP + WIN_R
VMEM_LIMIT = 52 * 1024 * 1024
NEG_BIG = -1e30


def _params(sem, vmem=VMEM_LIMIT):
    return pltpu.CompilerParams(dimension_semantics=sem, vmem_limit_bytes=vmem)


def _rms(x, g, n=None):
    n = x.shape[-1] if n is None else n
    ss = jnp.sum(x * x, axis=-1, keepdims=True)
    return x * lax.rsqrt(ss * (1.0 / n) + EPS) * g


def _silu(x):
    return x * (1.0 / (1.0 + jnp.exp(-x)))


def _norm_mod(h_ref, g_ref, mod_ref, row):
    x = h_ref[...]
    xn = _rms(x, g_ref[...])
    return xn * (1.0 + mod_ref[row + 1:row + 2, :]) + mod_ref[row:row + 1, :]


def _mod_kernel(c_ref, w_ref, b_ref, o_ref):
    s = _silu(c_ref[...])
    o_ref[...] = jnp.dot(s, w_ref[...], precision=lax.Precision.HIGHEST,
                         preferred_element_type=F32) + b_ref[...]


def _modulation(cin, w_mod, b_mod):
    depth, d, n = w_mod.shape
    tn = 1536
    rows = cin.shape[0]
    return pl.pallas_call(
        _mod_kernel,
        grid=(depth, n // tn),
        in_specs=[
            pl.BlockSpec((rows, d), lambda l, j: (0, 0)),
            pl.BlockSpec((None, d, tn), lambda l, j: (l, 0, j)),
            pl.BlockSpec((None, 1, tn), lambda l, j: (l, 0, j)),
        ],
        out_specs=pl.BlockSpec((None, rows, tn), lambda l, j: (l, 0, j)),
        out_shape=jax.ShapeDtypeStruct((depth, rows, n), F32),
        compiler_params=_params(("arbitrary", "arbitrary")),
        name="modulation",
    )(cin, w_mod, b_mod.reshape(depth, 1, n))


def _mla_front_kernel(h_ref, mod_ref, g_ref, win_ref, gcq_ref, gckv_ref, wq_ref, wkv_ref,
                      t1_ref, t2_ref, gq_ref, gk_ref, cb_ref,
                      q_ref, k_ref, v_ref, fcs_ref):
    xn = _norm_mod(h_ref, g_ref, mod_ref, 0)
    px = jnp.dot(xn.astype(BF16), win_ref[...], preferred_element_type=F32)
    t1 = t1_ref[...]
    t2 = t2_ref[...]
    hw = A_HEADS * HEAD_PAD

    qn = _rms(px[:, 0:A_Q_RANK], gcq_ref[...]).astype(BF16)
    qq = jnp.dot(qn, wq_ref[...], preferred_element_type=F32)
    aq = t1 * gq_ref[0:1, :]
    bq = t2 * gq_ref[1:2, :]
    q_shift = gq_ref[2:3, :]
    scale = A_QK ** -0.5 * LOG2E
    for h in range(A_HEADS):
        qm = qq[:, h * HEAD_PAD:(h + 1) * HEAD_PAD]
        qs = qq[:, hw + h * HEAD_PAD:hw + (h + 1) * HEAD_PAD]
        ss = jnp.sum(qm * qm, axis=-1, keepdims=True)
        inv = lax.rsqrt(ss * (1.0 / A_QK) + EPS) * scale
        q_ref[:, h * HEAD_PAD:(h + 1) * HEAD_PAD] = ((qm * aq + qs * bq) * inv + q_shift).astype(BF16)

    kvn = _rms(px[:, A_Q_RANK:A_Q_RANK + A_KV_RANK], gckv_ref[...]).astype(BF16)
    kv = jnp.dot(kvn, wkv_ref[...], preferred_element_type=F32)
    krm = px[:, 384:512]
    krs = px[:, 512:640]
    tail = krm * (t1 * gk_ref[1:2, :]) + krs * (t2 * gk_ref[2:3, :])
    ssr = jnp.sum(krm * krm, axis=-1, keepdims=True)
    gkn = gk_ref[0:1, :]
    k_one = gk_ref[3:4, :]
    v_one = gk_ref[4:5, :]
    for h in range(A_HEADS):
        km = kv[:, h * HEAD_PAD:(h + 1) * HEAD_PAD]
        ss = jnp.sum(km * km, axis=-1, keepdims=True) + ssr
        inv = lax.rsqrt(ss * (1.0 / A_QK) + EPS)
        k_ref[:, h * HEAD_PAD:(h + 1) * HEAD_PAD] = ((km * gkn + tail) * inv + k_one).astype(BF16)
        v_ref[:, h * HEAD_PAD:(h + 1) * HEAD_PAD] = (kv[:, hw + h * HEAD_PAD:hw + (h + 1) * HEAD_PAD]
                                                     + v_one).astype(BF16)

    f = px[:, 640:896].astype(BF16)
    fcs_ref[...] = jnp.dot(f, cb_ref[...], preferred_element_type=F32).astype(BF16)


def _mla_front(h, mod, g, win, gcq, gckv, wq, wkv, t1, t2, gq, gk, cb):
    b, n, d = h.shape
    nt = n // TM
    hw = A_HEADS * HEAD_PAD
    tile = lambda w: pl.BlockSpec((None, TM, w), lambda i, t: (i, t, 0))
    full = lambda a: pl.BlockSpec(a.shape, lambda i, t: (0,) * a.ndim)
    return pl.pallas_call(
        _mla_front_kernel,
        grid=(b, nt),
        in_specs=[
            tile(d),
            pl.BlockSpec((None, 6, d), lambda i, t: (jnp.where(t == 0, b, i), 0, 0)),
            full(g), full(win), full(gcq), full(gckv), full(wq), full(wkv),
            pl.BlockSpec((TM, LANES), lambda i, t: (t, 0)),
            pl.BlockSpec((TM, LANES), lambda i, t: (t, 0)),
            full(gq), full(gk), full(cb),
        ],
        out_specs=[tile(hw), tile(hw), tile(hw), tile(2 * F_WIDTH)],
        out_shape=[
            jax.ShapeDtypeStruct((b, n, hw), BF16),
            jax.ShapeDtypeStruct((b, n, hw), BF16),
            jax.ShapeDtypeStruct((b, n, hw), BF16),
            jax.ShapeDtypeStruct((b, n, 2 * F_WIDTH), BF16),
        ],
        compiler_params=_params(("arbitrary", "arbitrary")),
        name="mla_front",
    )(h, mod, g, win, gcq, gckv, wq, wkv, t1, t2, gq, gk, cb)


def _softmax_pv(q, k, v):
    s = lax.dot_general(q, k, (((1,), (1,)), ((), ())), preferred_element_type=F32)
    m = jnp.max(s, axis=-1, keepdims=True)
    p = jnp.exp(s - m)
    l = jnp.sum(p, axis=-1, keepdims=True)
    o = jnp.dot(p.astype(BF16), v, preferred_element_type=F32)
    return o * (1.0 / l)


def _mla_attn_kernel(bounded_ref, q_ref, k_ref, v_ref, o_ref, m_ref, acc_ref, *, ctx_len):
    t = pl.program_id(2)
    n = k_ref.shape[0]
    ck = MLA_KCHUNK

    def scores(hh, k0, klen):
        q = q_ref[:, hh * HEAD_PAD:(hh + 1) * HEAD_PAD]
        k = k_ref[pl.ds(k0, klen), hh * HEAD_PAD:(hh + 1) * HEAD_PAD]
        return lax.dot_general(q, k, (((1,), (1,)), ((), ())), preferred_element_type=F32)

    def vblk(hh, k0, klen):
        return v_ref[pl.ds(k0, klen), hh * HEAD_PAD:(hh + 1) * HEAD_PAD]

    def bounded(klen):
        for hh in range(MLA_HPS):
            p = jnp.exp2(scores(hh, 0, klen)).astype(BF16)
            acc_ref[hh] = jnp.dot(p, vblk(hh, 0, klen), preferred_element_type=F32)

    def online(n_chunks):
        for hh in range(MLA_HPS):
            s = scores(hh, 0, ctx_len)
            m = jnp.max(s, axis=-1, keepdims=True)
            m_ref[hh] = jnp.broadcast_to(m, m_ref.shape[1:])
            acc_ref[hh] = jnp.dot(jnp.exp2(s - m).astype(BF16), vblk(hh, 0, ctx_len),
                                  preferred_element_type=F32)

        def body(c, carry):
            k0 = pl.multiple_of(ctx_len + c * ck, math.gcd(ctx_len, ck))
            for hh in range(MLA_HPS):
                s = scores(hh, k0, ck)
                m_old = m_ref[hh]
                m_new = jnp.maximum(m_old, jnp.max(s, axis=-1, keepdims=True))
                p = jnp.exp2(s - jnp.tile(m_new, (1, ck // LANES))).astype(BF16)
                acc_ref[hh] = (jnp.exp2(m_old - m_new) * acc_ref[hh]
                               + jnp.dot(p, vblk(hh, k0, ck), preferred_element_type=F32))
                m_ref[hh] = m_new
            return carry

        lax.fori_loop(0, n_chunks, body, 0)

    is_ctx = t == 0
    fast = bounded_ref[0] != 0

    @pl.when(fast & is_ctx)
    def _():
        bounded(ctx_len)

    @pl.when(fast & jnp.logical_not(is_ctx))
    def _():
        bounded(n)

    @pl.when(jnp.logical_not(fast) & is_ctx)
    def _():
        online(0)

    @pl.when(jnp.logical_not(fast) & jnp.logical_not(is_ctx))
    def _():
        online((n - ctx_len) // ck)

    lane = lax.broadcasted_iota(jnp.int32, (o_ref.shape[0], LANES), 1)
    for hp in range(MLA_HPS // 2):
        outs = []
        for hh in (2 * hp, 2 * hp + 1):
            acc = acc_ref[hh]
            outs.append(acc * (1.0 / acc[:, A_V:A_V + 1]))
        o_ref[:, hp * LANES:(hp + 1) * LANES] = jnp.where(
            lane < A_V, outs[0], pltpu.roll(outs[1], A_V, axis=1)).astype(BF16)


def _mla_attn(bounded, q, k, v, ctx_len):
    b, n, _ = q.shape
    hp = A_HEADS // MLA_HPS
    assert (n - ctx_len) % MLA_KCHUNK == 0
    return pl.pallas_call(
        functools.partial(_mla_attn_kernel, ctx_len=ctx_len),
        grid_spec=pltpu.PrefetchScalarGridSpec(
            num_scalar_prefetch=1,
            grid=(b, hp, n // TM),
            in_specs=[
                pl.BlockSpec((None, TM, MLA_HPS * HEAD_PAD), lambda i, p, t, f: (i, t, p)),
                pl.BlockSpec((None, n, MLA_HPS * HEAD_PAD), lambda i, p, t, f: (i, 0, p)),
                pl.BlockSpec((None, n, MLA_HPS * HEAD_PAD), lambda i, p, t, f: (i, 0, p)),
            ],
            out_specs=pl.BlockSpec((None, TM, MLA_HPS * A_V), lambda i, p, t, f: (i, t, p)),
            scratch_shapes=[
                pltpu.VMEM((MLA_HPS, TM, LANES), F32),
                pltpu.VMEM((MLA_HPS, TM, HEAD_PAD), F32),
            ],
        ),
        out_shape=jax.ShapeDtypeStruct((b, n, A_HEADS * A_V), BF16),
        compiler_params=_params(("arbitrary", "arbitrary", "arbitrary")),
        name="mla_attn",
    )(bounded, q, k, v)


def _dft_kernel(cn_ref, sn_ref, cc_ref, sc_ref, fcs_ref, y_ref, *, ctx_len):
    t = pl.program_id(1)

    @pl.when(t == 0)
    def _():
        fc = fcs_ref[0:ctx_len, 0:F_WIDTH]
        fs = fcs_ref[0:ctx_len, F_WIDTH:2 * F_WIDTH]
        y = (jnp.dot(cc_ref[...], fc, preferred_element_type=F32)
             - jnp.dot(sc_ref[...], fs, preferred_element_type=F32))
        y_ref[...] = y.astype(BF16)

    @pl.when(t > 0)
    def _():
        n = fcs_ref.shape[0]
        fc = fcs_ref[ctx_len:n, 0:F_WIDTH]
        fs = fcs_ref[ctx_len:n, F_WIDTH:2 * F_WIDTH]
        y = (jnp.dot(cn_ref[...], fc, preferred_element_type=F32)
             - jnp.dot(sn_ref[...], fs, preferred_element_type=F32))
        y_ref[...] = y.astype(BF16)


def _dft(fcs, cn, sn, cc, sc, ctx_len):
    b, n, _ = fcs.shape
    s = n - ctx_len
    assert ctx_len == TM
    return pl.pallas_call(
        functools.partial(_dft_kernel, ctx_len=ctx_len),
        grid=(b, n // TM),
        in_specs=[
            pl.BlockSpec((TM, s), lambda i, t: (jnp.maximum(t - 1, 0), 0)),
            pl.BlockSpec((TM, s), lambda i, t: (jnp.maximum(t - 1, 0), 0)),
            pl.BlockSpec((ctx_len, ctx_len), lambda i, t: (0, 0)),
            pl.BlockSpec((ctx_len, ctx_len), lambda i, t: (0, 0)),
            pl.BlockSpec((None, n, 2 * F_WIDTH), lambda i, t: (i, 0, 0)),
        ],
        out_specs=pl.BlockSpec((None, TM, F_WIDTH), lambda i, t: (i, t, 0)),
        out_shape=jax.ShapeDtypeStruct((b, n, F_WIDTH), BF16),
        compiler_params=_params(("arbitrary", "arbitrary")),
        name="fourier_dft",
    )(cn, sn, cc, sc, fcs)


def _out_proj_kernel(*refs, widths):
    mix_refs = refs[:len(widths)]
    w_ref, h_ref, mod_ref, o_ref = refs[len(widths):]
    acc = None
    off = 0
    for m_ref, wd in zip(mix_refs, widths):
        part = jnp.dot(m_ref[...], w_ref[off:off + wd, :], preferred_element_type=F32)
        acc = part if acc is None else acc + part
        off += wd
    o_ref[...] = h_ref[...] + mod_ref[2:3, :] * acc


def _out_proj(mixes, w, h, mod):
    b, n, d = h.shape
    widths = tuple(m.shape[-1] for m in mixes)
    tile = lambda wd: pl.BlockSpec((None, TM, wd), lambda i, t: (i, t, 0))
    return pl.pallas_call(
        functools.partial(_out_proj_kernel, widths=widths),
        grid=(b, n // TM),
        in_specs=[tile(wd) for wd in widths] + [
            pl.BlockSpec(w.shape, lambda i, t: (0, 0)),
            tile(d),
            pl.BlockSpec((None, 6, d), lambda i, t: (jnp.where(t == 0, b, i), 0, 0)),
        ],
        out_specs=tile(d),
        out_shape=jax.ShapeDtypeStruct((b, n, d), F32),
        compiler_params=_params(("arbitrary", "arbitrary")),
        name="out_proj",
    )(*mixes, w, h, mod)


def _ffn_kernel(h_ref, mod_ref, g_ref, wg_ref, wu_ref, wd_ref, o_ref):
    xn = _norm_mod(h_ref, g_ref, mod_ref, 3).astype(BF16)
    gt = jnp.dot(xn, wg_ref[...], preferred_element_type=F32)
    up = jnp.dot(xn, wu_ref[...], preferred_element_type=F32)
    a = (_silu(gt) * up).astype(BF16)
    y = jnp.dot(a, wd_ref[...], preferred_element_type=F32)
    o_ref[...] = h_ref[...] + mod_ref[5:6, :] * y


def _ffn(h, mod, g, wg, wu, wd):
    b, n, d = h.shape
    tile = pl.BlockSpec((None, TM, d), lambda i, t: (i, t, 0))
    full = lambda a: pl.BlockSpec(a.shape, lambda i, t: (0,) * a.ndim)
    return pl.pallas_call(
        _ffn_kernel,
        grid=(b, n // TM),
        in_specs=[
            tile,
            pl.BlockSpec((None, 6, d), lambda i, t: (jnp.where(t == 0, b, i), 0, 0)),
            full(g), full(wg), full(wu), full(wd),
        ],
        out_specs=tile,
        out_shape=jax.ShapeDtypeStruct((b, n, d), F32),
        compiler_params=_params(("arbitrary", "arbitrary")),
        name="ffn_dense",
    )(h, mod, g, wg, wu, wd)


def _na_front_kernel(h_ref, mod_ref, g_ref, win_ref, gq_ref, gk_ref, q_ref, k_ref, v_ref):
    xn = _norm_mod(h_ref, g_ref, mod_ref, 0)
    px = jnp.dot(xn.astype(BF16), win_ref[...], preferred_element_type=F32)
    width = C_HEADS * C_DH
    lane = lax.broadcasted_iota(jnp.int32, (px.shape[0], LANES), 1)
    low = lane < C_DH
    scale = C_DH ** -0.5

    def norm_pairs(base, g_ref_, out_ref, mult):
        gg = g_ref_[...]
        for j in range(width // LANES):
            x = px[:, base + j * LANES:base + (j + 1) * LANES]
            x2 = x * x
            s_all = jnp.sum(x2, axis=-1, keepdims=True)
            s_lo = jnp.sum(jnp.where(low, x2, 0.0), axis=-1, keepdims=True)
            ss = jnp.where(low, s_lo, s_all - s_lo)
            inv = lax.rsqrt(ss * (1.0 / C_DH) + EPS)
            if mult != 1.0:
                inv = inv * mult
            out_ref[:, j * LANES:(j + 1) * LANES] = (x * inv * gg).astype(BF16)

    norm_pairs(0, gq_ref, q_ref, scale)
    norm_pairs(width, gk_ref, k_ref, 1.0)
    v_ref[...] = px[:, 2 * width:].astype(BF16)


def _na_front(h, mod, g, win, gq, gk):
    b, n, d = h.shape
    width = C_HEADS * C_DH
    tile = lambda w: pl.BlockSpec((None, TM, w), lambda i, t: (i, t, 0))
    full = lambda a: pl.BlockSpec(a.shape, lambda i, t: (0,) * a.ndim)
    return pl.pallas_call(
        _na_front_kernel,
        grid=(b, n // TM),
        in_specs=[
            tile(d),
            pl.BlockSpec((None, 6, d), lambda i, t: (jnp.where(t == 0, b, i), 0, 0)),
            full(g), full(win), full(gq), full(gk),
        ],
        out_specs=[tile(width)] * 3,
        out_shape=[jax.ShapeDtypeStruct((b, n, width), BF16)] * 3,
        compiler_params=_params(("arbitrary", "arbitrary")),
        name="na_front",
    )(h, mod, g, win, gq, gk)


def _na_attn_kernel(q_ref, k_ref, v_ref, bias_ref, o_ref, *, ctx_len, rows_n):
    qlen = NA_GROUP * GRID_W
    wlen = NA_WINDOW * GRID_W
    n_groups = rows_n // NA_GROUP
    lane_q = lax.broadcasted_iota(jnp.int32, (qlen, LANES), 1)
    kc = k_ref[0:ctx_len, :]
    vc = v_ref[0:ctx_len, :]

    qc = q_ref[0:ctx_len, :]
    lane_c = lax.broadcasted_iota(jnp.int32, (ctx_len, LANES), 1)
    outs = []
    for hh in range(2):
        sel = (lane_c < C_DH) if hh == 0 else (lane_c >= C_DH)
        outs.append(_softmax_pv(jnp.where(sel, qc, jnp.zeros_like(qc)), kc, vc))
    o_ref[0:ctx_len, :] = jnp.where(lane_c < C_DH, outs[0], outs[1]).astype(BF16)

    def group_body(g, carry):
        start = jnp.clip(g * NA_GROUP - WIN_R // 2, 0, rows_n - NA_WINDOW)
        pat = jnp.where(g == 0, 0, jnp.where(g == n_groups - 1, 2, 1))
        q0 = pl.multiple_of(ctx_len + g * qlen, GRID_W)
        k0 = pl.multiple_of(ctx_len + start * GRID_W, GRID_W)
        qr = q_ref[pl.ds(q0, qlen), :]
        kw = k_ref[pl.ds(k0, wlen), :]
        vw = v_ref[pl.ds(k0, wlen), :]
        res = []
        for hh in range(2):
            sel = (lane_q < C_DH) if hh == 0 else (lane_q >= C_DH)
            qh = jnp.where(sel, qr, jnp.zeros_like(qr))
            s_loc = lax.dot_general(qh, kw, (((1,), (1,)), ((), ())),
                                    preferred_element_type=F32) + bias_ref[pat, hh]
            s_ctx = lax.dot_general(qh, kc, (((1,), (1,)), ((), ())), preferred_element_type=F32)
            m = jnp.maximum(jnp.max(s_loc, axis=-1, keepdims=True),
                            jnp.max(s_ctx, axis=-1, keepdims=True))
            p_loc = jnp.exp(s_loc - m)
            p_ctx = jnp.exp(s_ctx - m)
            l = jnp.sum(p_loc, axis=-1, keepdims=True) + jnp.sum(p_ctx, axis=-1, keepdims=True)
            o = (jnp.dot(p_loc.astype(BF16), vw, preferred_element_type=F32)
                 + jnp.dot(p_ctx.astype(BF16), vc, preferred_element_type=F32))
            res.append(o * (1.0 / l))
        o_ref[pl.ds(q0, qlen), :] = jnp.where(lane_q < C_DH, res[0], res[1]).astype(BF16)
        return carry

    lax.fori_loop(0, n_groups, group_body, 0, unroll=2)


def _na_attn(q, k, v, bias, ctx_len):
    b, n, width = q.shape
    rows_n = (n - ctx_len) // GRID_W
    hp = C_HEADS // 2
    blk = pl.BlockSpec((None, n, LANES), lambda i, p: (i, 0, p))
    return pl.pallas_call(
        functools.partial(_na_attn_kernel, ctx_len=ctx_len, rows_n=rows_n),
        grid=(b, hp),
        in_specs=[blk, blk, blk,
                  pl.BlockSpec((3, 2) + bias.shape[2:], lambda i, p: (0, p, 0, 0))],
        out_specs=blk,
        out_shape=jax.ShapeDtypeStruct((b, n, width), BF16),
        compiler_params=_params(("arbitrary", "arbitrary")),
        name="na_attn",
    )(q, k, v, bias)


def _router_kernel(h_ref, mod_ref, g_ref, wr_ref, xn_ref, comb_ref):
    xn = _norm_mod(h_ref, g_ref, mod_ref, 3)
    xn_ref[...] = xn.astype(BF16)
    logits = jnp.dot(xn, wr_ref[...], precision=lax.Precision.HIGHEST, preferred_element_type=F32)
    lane = lax.broadcasted_iota(jnp.int32, logits.shape, 1).astype(F32)
    lg = jnp.where(lane < N_EXPERTS, logits, -jnp.inf)
    m1 = jnp.max(lg, axis=-1, keepdims=True)
    i1 = jnp.min(jnp.where(lg == m1, lane, float(LANES)), axis=-1, keepdims=True)
    lg2 = jnp.where(lane == i1, -jnp.inf, lg)
    m2 = jnp.max(lg2, axis=-1, keepdims=True)
    i2 = jnp.min(jnp.where(lg2 == m2, lane, float(LANES)), axis=-1, keepdims=True)
    e2 = jnp.exp(m2 - m1)
    den = 1.0 / (1.0 + e2)
    comb_ref[...] = jnp.where(lane == i1, den, 0.0) + jnp.where(lane == i2, e2 * den, 0.0)


def _router(h, mod, g, wr):
    b, n, d = h.shape
    tile = lambda w, dt=None: pl.BlockSpec((None, TM, w), lambda i, t: (i, t, 0))
    full = lambda a: pl.BlockSpec(a.shape, lambda i, t: (0,) * a.ndim)
    return pl.pallas_call(
        _router_kernel,
        grid=(b, n // TM),
        in_specs=[
            tile(d),
            pl.BlockSpec((None, 6, d), lambda i, t: (jnp.where(t == 0, b, i), 0, 0)),
            full(g), full(wr),
        ],
        out_specs=[tile(d), tile(LANES)],
        out_shape=[jax.ShapeDtypeStruct((b, n, d), BF16), jax.ShapeDtypeStruct((b, n, LANES), F32)],
        compiler_params=_params(("arbitrary", "arbitrary")),
        name="moe_router",
    )(h, mod, g, wr)


def _moe_kernel(x_ref, comb_ref, wg_ref, wu_ref, wd_ref, y_ref, acc_ref):
    e = pl.program_id(1)

    @pl.when(e == 0)
    def _():
        acc_ref[...] = jnp.zeros_like(acc_ref)

    x = x_ref[...]
    gt = jnp.dot(x, wg_ref[...], preferred_element_type=F32)
    up = jnp.dot(x, wu_ref[...], preferred_element_type=F32)
    a = (_silu(gt) * up).astype(BF16)
    comb = comb_ref[...]
    lane = lax.broadcasted_iota(jnp.int32, comb.shape, 1)
    ce = jnp.sum(jnp.where(lane == e, comb, 0.0), axis=-1, keepdims=True)
    acc_ref[...] += ce * jnp.dot(a, wd_ref[...], preferred_element_type=F32)

    @pl.when(e == pl.num_programs(1) - 1)
    def _():
        y_ref[...] = acc_ref[...]


def _moe(x, comb, wg, wu, wd):
    t, d = x.shape
    ne, _, f = wg.shape
    tm = MOE_TM if t % MOE_TM == 0 else TM
    return pl.pallas_call(
        _moe_kernel,
        grid=(t // tm, ne),
        in_specs=[
            pl.BlockSpec((tm, d), lambda i, e: (i, 0)),
            pl.BlockSpec((tm, LANES), lambda i, e: (i, 0)),
            pl.BlockSpec((None, d, f), lambda i, e: (e, 0, 0)),
            pl.BlockSpec((None, d, f), lambda i, e: (e, 0, 0)),
            pl.BlockSpec((None, f, d), lambda i, e: (e, 0, 0)),
        ],
        out_specs=pl.BlockSpec((tm, d), lambda i, e: (i, 0)),
        out_shape=jax.ShapeDtypeStruct((t, d), F32),
        scratch_shapes=[pltpu.VMEM((tm, d), F32)],
        compiler_params=_params(("arbitrary", "arbitrary")),
        name="moe_experts",
    )(x, comb, wg, wu, wd)


def _resid_kernel(h_ref, y_ref, mod_ref, o_ref):
    o_ref[...] = h_ref[...] + mod_ref[5:6, :] * y_ref[...]


def _resid(h, y, mod, latent_only):
    b, n, d = h.shape
    skip = 1 if latent_only else 0
    nt = n // TM - skip
    tile = pl.BlockSpec((None, TM, d), lambda i, t: (i, t + skip, 0))
    if latent_only:
        mod_spec = pl.BlockSpec((None, 6, d), lambda i, t: (i, 0, 0))
    else:
        mod_spec = pl.BlockSpec((None, 6, d), lambda i, t: (jnp.where(t == 0, b, i), 0, 0))
    return pl.pallas_call(
        _resid_kernel,
        grid=(b, nt),
        in_specs=[tile, tile, mod_spec],
        out_specs=pl.BlockSpec((None, TM, d), lambda i, t: (i, t, 0)),
        out_shape=jax.ShapeDtypeStruct((b, nt * TM, d), F32),
        compiler_params=_params(("arbitrary", "arbitrary")),
        name="moe_residual",
    )(h, y, mod)


def _rope_tables(s, ctx_len):
    t = jnp.arange(s)
    rows = (t // GRID_W).astype(F32)
    cols = (t % GRID_W).astype(F32)
    n_pairs = A_ROPE // 2
    per_axis = n_pairs // 2
    inv = ROPE_THETA ** (-jnp.arange(per_axis, dtype=F32) / per_axis)
    ang = jnp.concatenate([rows[:, None] * inv, cols[:, None] * inv], axis=-1)
    cos = jnp.repeat(jnp.cos(ang), 2, axis=-1)
    sin = jnp.repeat(jnp.sin(ang), 2, axis=-1)
    sign = jnp.tile(jnp.array([-1.0, 1.0], F32), n_pairs)
    pad = HEAD_PAD - A_QK
    t1 = jnp.concatenate([jnp.ones((s, A_NOPE), F32), cos, jnp.zeros((s, pad), F32)], axis=-1)
    t2 = jnp.concatenate([jnp.zeros((s, A_NOPE), F32), sin * sign, jnp.zeros((s, pad), F32)], axis=-1)
    c1 = jnp.concatenate([jnp.ones((ctx_len, A_QK), F32), jnp.zeros((ctx_len, pad), F32)], axis=-1)
    c2 = jnp.zeros((ctx_len, HEAD_PAD), F32)
    return jnp.concatenate([c1, t1], axis=0), jnp.concatenate([c2, t2], axis=0)


_PAIR_SWAP = np.arange(A_ROPE) ^ 1


def _rope_lane_vec(g_tail, swapped):
    gt = g_tail[_PAIR_SWAP] if swapped else g_tail
    return jnp.concatenate([jnp.zeros((A_NOPE,), F32), gt, jnp.zeros((HEAD_PAD - A_QK,), F32)])[None]


def _mla_weights(w_in, w_uq, w_ukv, g_q, g_k):
    d = w_in.shape[0]
    q_end = A_Q_RANK
    kv_end = q_end + A_KV_RANK
    r_end = kv_end + A_ROPE
    z = lambda n: jnp.zeros((d, n), w_in.dtype)
    kr = w_in[:, kv_end:r_end]
    pad = HEAD_PAD - A_QK
    win = jnp.concatenate([w_in[:, :kv_end], z(A_NOPE), kr, z(pad), z(A_NOPE), kr[:, _PAIR_SWAP], z(pad),
                           w_in[:, r_end:]], axis=-1).astype(BF16)
    wq = w_uq.reshape(A_Q_RANK, A_HEADS, A_QK)
    zq = lambda n: jnp.zeros((A_Q_RANK, A_HEADS, n), wq.dtype)
    wq_main = jnp.concatenate([wq, zq(pad)], axis=-1)
    wq_swap = jnp.concatenate([zq(A_NOPE), wq[:, :, A_NOPE:][:, :, _PAIR_SWAP], zq(pad)], axis=-1)
    wq_ext = jnp.concatenate([wq_main.reshape(A_Q_RANK, -1), wq_swap.reshape(A_Q_RANK, -1)], axis=-1).astype(BF16)
    wkv = w_ukv.reshape(A_KV_RANK, A_HEADS, A_NOPE + A_V)
    zkv = jnp.zeros((A_KV_RANK, A_HEADS, HEAD_PAD - A_NOPE), wkv.dtype)
    wk = jnp.concatenate([wkv[:, :, :A_NOPE], zkv], axis=-1)
    wv = jnp.concatenate([wkv[:, :, A_NOPE:], zkv], axis=-1)
    wkv_ext = jnp.concatenate([wk.reshape(A_KV_RANK, -1), wv.reshape(A_KV_RANK, -1)], axis=-1).astype(BF16)

    bound = math.sqrt(A_QK) * jnp.max(jnp.abs(g_q)) * jnp.max(jnp.abs(g_k))
    bounded = bound <= MLA_SAFE_BOUND
    shift = jnp.where(bounded, -bound * LOG2E, 0.0)
    unit = lambda lane_idx: jnp.zeros((1, HEAD_PAD), F32).at[0, lane_idx].set(1.0)
    gq = jnp.concatenate([jnp.concatenate([g_q, jnp.zeros((pad,), F32)])[None], _rope_lane_vec(g_q[A_NOPE:], True),
                          unit(A_QK) * shift], axis=0)
    gk = jnp.concatenate([jnp.concatenate([g_k[:A_NOPE], jnp.zeros((HEAD_PAD - A_NOPE,), F32)])[None],
                          _rope_lane_vec(g_k[A_NOPE:], False), _rope_lane_vec(g_k[A_NOPE:], True),
                          unit(A_QK), unit(A_V)], axis=0)
    return win, wq_ext, wkv_ext, gq, gk, bounded.astype(jnp.int32).reshape(1)


def _dft_mats(n, norm):
    k = jnp.arange(n, dtype=jnp.int32)
    ang = ((k[:, None] * k[None, :]) % n).astype(F32) * (2.0 * math.pi / n)
    return (jnp.cos(ang) * norm).astype(BF16), (jnp.sin(ang) * norm).astype(BF16)


def _channel_dft():
    c = np.arange(F_CH)
    ang = 2.0 * np.pi * ((c[:, None] * c[None, :]) % F_CH) / F_CH
    eye = np.eye(F_GROUPS)
    cb = np.kron(eye, np.cos(ang)) / math.sqrt(F_CH)
    sb = np.kron(eye, np.sin(ang)) / math.sqrt(F_CH)
    return jnp.asarray(np.concatenate([cb, sb], axis=1), BF16)


def _na_group_rows(rows_n):
    assert rows_n % NA_GROUP == 0 and rows_n >= NA_WINDOW + NA_GROUP
    n_groups = rows_n // NA_GROUP
    n_dr = 2 * WIN_R - 1
    idx = np.full((3, NA_GROUP, NA_WINDOW), n_dr, np.int32)
    seen = {}
    for g in range(n_groups):
        gs = int(np.clip(g * NA_GROUP - WIN_R // 2, 0, rows_n - NA_WINDOW))
        pat = 0 if g == 0 else (2 if g == n_groups - 1 else 1)
        cur = np.full((NA_GROUP, NA_WINDOW), n_dr, np.int32)
        for qi in range(NA_GROUP):
            r = g * NA_GROUP + qi
            start = int(np.clip(r - WIN_R // 2, 0, rows_n - WIN_R))
            assert gs <= start and start + WIN_R <= gs + NA_WINDOW
            for a in range(start, start + WIN_R):
                cur[qi, a - gs] = a - r + (WIN_R - 1)
        assert pat not in seen or np.array_equal(seen[pat], cur)
        seen[pat] = cur
        idx[pat] = cur
    return idx


def _na_bias(rpb, rows_n):
    h, n_dr, n_dc = rpb.shape
    qc = np.arange(GRID_W)
    cs = np.clip(qc - WIN_C // 2, 0, GRID_W - WIN_C)
    kcol = np.arange(GRID_W)
    valid = (kcol[None, :] >= cs[:, None]) & (kcol[None, :] < cs[:, None] + WIN_C)
    dc = kcol[None, :] - qc[:, None] + (WIN_C - 1)
    onehot = (valid[:, :, None] & (dc[:, :, None] == np.arange(n_dc)[None, None, :])).astype(np.float32)
    col = jnp.einsum('hdc,qkc->hdqk', rpb.astype(F32), jnp.asarray(onehot), precision=lax.Precision.HIGHEST)
    col = jnp.where(jnp.asarray(valid)[None, None], col, NEG_BIG)
    col = jnp.concatenate([col, jnp.full((h, 1, GRID_W, GRID_W), NEG_BIG, F32)], axis=1)
    idx = _na_group_rows(rows_n)
    blocks = [[jnp.stack([col[:, int(idx[p, qi, a])] for a in range(NA_WINDOW)], axis=2)
               for qi in range(NA_GROUP)] for p in range(3)]
    tab = jnp.stack([jnp.stack(bq, axis=1) for bq in blocks], axis=0)
    return tab.reshape(3, h, NA_GROUP * GRID_W, NA_WINDOW * GRID_W)


def kernel(x, c, ctx, c_ctx, w_mod, b_mod, norm_g, a_w_in, a_g_cq, a_g_ckv, a_w_uq, a_w_ukv, a_g_q, a_g_k, a_w_out,
           f_w_gate, f_w_up, f_w_down, c_w_in, c_g_q, c_g_k, c_rpb, c_w_out, m_w_router, m_w_gate, m_w_up, m_w_down):
    b, s, d = x.shape
    l = ctx.shape[1]
    depth = w_mod.shape[0]
    n = l + s
    assert l == TM and s % TM == 0 and s % GRID_W == 0

    mod_rows = 16
    cin = jnp.concatenate([c, c_ctx[None], jnp.zeros((mod_rows - b - 1, d), F32)], axis=0)
    mod_all = _modulation(cin, w_mod, b_mod).reshape(depth, mod_rows, 6, d)

    t1, t2 = _rope_tables(s, l)
    cn, sn = _dft_mats(s, 1.0 / math.sqrt(s))
    cc, sc = _dft_mats(l, 1.0 / math.sqrt(l))
    cb = _channel_dft()

    h = jnp.concatenate([ctx, x], axis=1)
    for i in range(depth):
        j = i // 2
        last = i == depth - 1
        mod = mod_all[i]
        g1 = norm_g[i, 0][None]
        g2 = norm_g[i, 1][None]
        if i % 2 == 0:
            win, wq_ext, wkv_ext, gq, gk, bounded = _mla_weights(a_w_in[j], a_w_uq[j], a_w_ukv[j],
                                                                 a_g_q[j], a_g_k[j])
            q, k, v, fcs = _mla_front(h, mod, g1, win, a_g_cq[j][None], a_g_ckv[j][None], wq_ext, wkv_ext,
                                      t1, t2, gq, gk, cb)
            att = _mla_attn(bounded, q, k, v, l)
            yf = _dft(fcs, cn, sn, cc, sc, l)
            h = _out_proj([att, yf], a_w_out[j].astype(BF16), h, mod)
            h = _ffn(h, mod, g2, f_w_gate[j].astype(BF16), f_w_up[j].astype(BF16), f_w_down[j].astype(BF16))
        else:
            gq2 = jnp.tile(c_g_q[j], 2)[None]
            gk2 = jnp.tile(c_g_k[j], 2)[None]
            q, k, v = _na_front(h, mod, g1, c_w_in[j].astype(BF16), gq2, gk2)
            bias = _na_bias(c_rpb[j], s // GRID_W)
            att = _na_attn(q, k, v, bias, l)
            h = _out_proj([att], c_w_out[j].astype(BF16), h, mod)
            wr = jnp.concatenate([m_w_router[j], jnp.zeros((d, LANES - N_EXPERTS), F32)], axis=-1)
            xn2, comb = _router(h, mod, g2, wr)
            y = _moe(xn2.reshape(b * n, d), comb.reshape(b * n, LANES),
                     m_w_gate[j].astype(BF16), m_w_up[j].astype(BF16), m_w_down[j].astype(BF16))
            h = _resid(h, y.reshape(b, n, d), mod, last)
    if h.shape[1] != s:
        h = h[:, l:]
    return h
```

```python
import functools
import math

import jax
import jax.numpy as jnp
import numpy as np
from jax import lax
from jax.experimental import pallas as pl
from jax.experimental.pallas import tpu as pltpu

F32 = jnp.float32
BF16 = jnp.bfloat16

GRID_W = 64
A_HEADS = 12
A_NOPE = 64
A_ROPE = 32
A_QK = A_NOPE + A_ROPE
A_V = 64
A_Q_RANK = 256
A_KV_RANK = 128
F_GROUPS = 4
F_CH = 64
F_WIDTH = F_GROUPS * F_CH
C_HEADS = 16
C_DH = 64
WIN_R = 8
WIN_C = 16
N_EXPERTS = 8
ROPE_THETA = 10000.0
EPS = 1e-6

LANES = 128
HEAD_PAD = 128
TM = 256
MOE_TM = 256
_PAIR_LO = tuple(lo for lo in range(N_EXPERTS) for hi in range(lo + 1, N_EXPERTS))
_PAIR_HI = tuple(hi for lo in range(N_EXPERTS) for hi in range(lo + 1, N_EXPERTS))
LOG2E = math.log2(math.e)
MLA_SAFE_BOUND = 40.0
MLA_KCHUNK = 512
MLA_HPS = 2
NA_GROUP = 4
NA_WINDOW = NA_GROUP + WIN_R
VMEM_LIMIT = 52 * 1024 * 1024
NEG_BIG = -1e30


def _params(sem, vmem=VMEM_LIMIT):
    return pltpu.CompilerParams(dimension_semantics=sem, vmem_limit_bytes=vmem)


def _rms(x, g, n=None):
    n = x.shape[-1] if n is None else n
    ss = jnp.sum(x * x, axis=-1, keepdims=True)
    return x * lax.rsqrt(ss * (1.0 / n) + EPS) * g


def _silu(x):
    return x * (1.0 / (1.0 + jnp.exp(-x)))


def _norm_mod(h_ref, g_ref, mod_ref, row):
    x = h_ref[...]
    xn = _rms(x, g_ref[...])
    return xn * (1.0 + mod_ref[row + 1:row + 2, :]) + mod_ref[row:row + 1, :]


def _mod_kernel(c_ref, w_ref, b_ref, o_ref):
    s = _silu(c_ref[...])
    o_ref[...] = jnp.dot(s, w_ref[...], precision=lax.Precision.HIGHEST,
                         preferred_element_type=F32) + b_ref[...]


def _modulation(cin, w_mod, b_mod):
    depth, d, n = w_mod.shape
    tn = 1536
    rows = cin.shape[0]
    return pl.pallas_call(
        _mod_kernel,
        grid=(depth, n // tn),
        in_specs=[
            pl.BlockSpec((rows, d), lambda l, j: (0, 0)),
            pl.BlockSpec((None, d, tn), lambda l, j: (l, 0, j)),
            pl.BlockSpec((None, 1, tn), lambda l, j: (l, 0, j)),
        ],
        out_specs=pl.BlockSpec((None, rows, tn), lambda l, j: (l, 0, j)),
        out_shape=jax.ShapeDtypeStruct((depth, rows, n), F32),
        compiler_params=_params(("arbitrary", "arbitrary")),
        name="modulation",
    )(cin, w_mod, b_mod.reshape(depth, 1, n))


def _mla_front_kernel(h_ref, mod_ref, g_ref, win_ref, gcq_ref, gckv_ref, wq_ref, wkv_ref,
                      t1_ref, t2_ref, gq_ref, gk_ref, cb_ref,
                      q_ref, k_ref, v_ref, fcs_ref):
    xn = _norm_mod(h_ref, g_ref, mod_ref, 0)
    px = jnp.dot(xn.astype(BF16), win_ref[...], preferred_element_type=F32)
    t1 = t1_ref[...]
    t2 = t2_ref[...]
    hw = A_HEADS * HEAD_PAD

    qn = _rms(px[:, 0:A_Q_RANK], gcq_ref[...]).astype(BF16)
    qq = jnp.dot(qn, wq_ref[...], preferred_element_type=F32)
    aq = t1 * gq_ref[0:1, :]
    bq = t2 * gq_ref[1:2, :]
    q_shift = gq_ref[2:3, :]
    scale = A_QK ** -0.5 * LOG2E
    for h in range(A_HEADS):
        qm = qq[:, h * HEAD_PAD:(h + 1) * HEAD_PAD]
        qs = qq[:, hw + h * HEAD_PAD:hw + (h + 1) * HEAD_PAD]
        ss = jnp.sum(qm * qm, axis=-1, keepdims=True)
        inv = lax.rsqrt(ss * (1.0 / A_QK) + EPS) * scale
        q_ref[:, h * HEAD_PAD:(h + 1) * HEAD_PAD] = ((qm * aq + qs * bq) * inv + q_shift).astype(BF16)

    kvn = _rms(px[:, A_Q_RANK:A_Q_RANK + A_KV_RANK], gckv_ref[...]).astype(BF16)
    kv = jnp.dot(kvn, wkv_ref[...], preferred_element_type=F32)
    krm = px[:, 384:512]
    krs = px[:, 512:640]
    tail = krm * (t1 * gk_ref[1:2, :]) + krs * (t2 * gk_ref[2:3, :])
    ssr = jnp.sum(krm * krm, axis=-1, keepdims=True)
    gkn = gk_ref[0:1, :]
    k_one = gk_ref[3:4, :]
    v_one = gk_ref[4:5, :]
    for h in range(A_HEADS):
        km = kv[:, h * HEAD_PAD:(h + 1) * HEAD_PAD]
        ss = jnp.sum(km * km, axis=-1, keepdims=True) + ssr
        inv = lax.rsqrt(ss * (1.0 / A_QK) + EPS)
        k_ref[:, h * HEAD_PAD:(h + 1) * HEAD_PAD] = ((km * gkn + tail) * inv + k_one).astype(BF16)
        v_ref[:, h * HEAD_PAD:(h + 1) * HEAD_PAD] = (kv[:, hw + h * HEAD_PAD:hw + (h + 1) * HEAD_PAD]
                                                     + v_one).astype(BF16)

    f = px[:, 640:896].astype(BF16)
    fcs_ref[...] = jnp.dot(f, cb_ref[...], preferred_element_type=F32).astype(BF16)


def _mla_front(h, mod, g, win, gcq, gckv, wq, wkv, t1, t2, gq, gk, cb):
    b, n, d = h.shape
    nt = n // TM
    hw = A_HEADS * HEAD_PAD
    tile = lambda w: pl.BlockSpec((None, TM, w), lambda i, t: (i, t, 0))
    full = lambda a: pl.BlockSpec(a.shape, lambda i, t: (0,) * a.ndim)
    return pl.pallas_call(
        _mla_front_kernel,
        grid=(b, nt),
        in_specs=[
            tile(d),
            pl.BlockSpec((None, 6, d), lambda i, t: (jnp.where(t == 0, b, i), 0, 0)),
            full(g), full(win), full(gcq), full(gckv), full(wq), full(wkv),
            pl.BlockSpec((TM, LANES), lambda i, t: (t, 0)),
            pl.BlockSpec((TM, LANES), lambda i, t: (t, 0)),
            full(gq), full(gk), full(cb),
        ],
        out_specs=[tile(hw), tile(hw), tile(hw), tile(2 * F_WIDTH)],
        out_shape=[
            jax.ShapeDtypeStruct((b, n, hw), BF16),
            jax.ShapeDtypeStruct((b, n, hw), BF16),
            jax.ShapeDtypeStruct((b, n, hw), BF16),
            jax.ShapeDtypeStruct((b, n, 2 * F_WIDTH), BF16),
        ],
        compiler_params=_params(("arbitrary", "arbitrary")),
        name="mla_front",
    )(h, mod, g, win, gcq, gckv, wq, wkv, t1, t2, gq, gk, cb)


def _softmax_pv(q, k, v):
    s = lax.dot_general(q, k, (((1,), (1,)), ((), ())), preferred_element_type=F32)
    m = jnp.max(s, axis=-1, keepdims=True)
    p = jnp.exp(s - m)
    l = jnp.sum(p, axis=-1, keepdims=True)
    o = jnp.dot(p.astype(BF16), v, preferred_element_type=F32)
    return o * (1.0 / l)


def _mla_attn_kernel(bounded_ref, q_ref, k_ref, v_ref, o_ref, m_ref, acc_ref, *, ctx_len):
    t = pl.program_id(2)
    n = k_ref.shape[0]
    ck = MLA_KCHUNK

    def scores(hh, k0, klen):
        q = q_ref[:, hh * HEAD_PAD:(hh + 1) * HEAD_PAD]
        k = k_ref[pl.ds(k0, klen), hh * HEAD_PAD:(hh + 1) * HEAD_PAD]
        return lax.dot_general(q, k, (((1,), (1,)), ((), ())), preferred_element_type=F32)

    def vblk(hh, k0, klen):
        return v_ref[pl.ds(k0, klen), hh * HEAD_PAD:(hh + 1) * HEAD_PAD]

    def bounded(klen):
        for hh in range(MLA_HPS):
            p = jnp.exp2(scores(hh, 0, klen)).astype(BF16)
            acc_ref[hh] = jnp.dot(p, vblk(hh, 0, klen), preferred_element_type=F32)

    def online(n_chunks):
        for hh in range(MLA_HPS):
            s = scores(hh, 0, ctx_len)
            m = jnp.max(s, axis=-1, keepdims=True)
            m_ref[hh] = jnp.broadcast_to(m, m_ref.shape[1:])
            acc_ref[hh] = jnp.dot(jnp.exp2(s - m).astype(BF16), vblk(hh, 0, ctx_len),
                                  preferred_element_type=F32)

        def body(c, carry):
            k0 = pl.multiple_of(ctx_len + c * ck, math.gcd(ctx_len, ck))
            for hh in range(MLA_HPS):
                s = scores(hh, k0, ck)
                m_old = m_ref[hh]
                m_new = jnp.maximum(m_old, jnp.max(s, axis=-1, keepdims=True))
                p = jnp.exp2(s - jnp.tile(m_new, (1, ck // LANES))).astype(BF16)
                acc_ref[hh] = (jnp.exp2(m_old - m_new) * acc_ref[hh]
                               + jnp.dot(p, vblk(hh, k0, ck), preferred_element_type=F32))
                m_ref[hh] = m_new
            return carry

        lax.fori_loop(0, n_chunks, body, 0)

    is_ctx = t == 0
    fast = bounded_ref[0] != 0

    @pl.when(fast & is_ctx)
    def _():
        bounded(ctx_len)

    @pl.when(fast & jnp.logical_not(is_ctx))
    def _():
        bounded(n)

    @pl.when(jnp.logical_not(fast) & is_ctx)
    def _():
        online(0)

    @pl.when(jnp.logical_not(fast) & jnp.logical_not(is_ctx))
    def _():
        online((n - ctx_len) // ck)

    lane = lax.broadcasted_iota(jnp.int32, (o_ref.shape[0], LANES), 1)
    for hp in range(MLA_HPS // 2):
        outs = []
        for hh in (2 * hp, 2 * hp + 1):
            acc = acc_ref[hh]
            outs.append(acc * (1.0 / acc[:, A_V:A_V + 1]))
        o_ref[:, hp * LANES:(hp + 1) * LANES] = jnp.where(
            lane < A_V, outs[0], pltpu.roll(outs[1], A_V, axis=1)).astype(BF16)


def _mla_attn(bounded, q, k, v, ctx_len):
    b, n, _ = q.shape
    hp = A_HEADS // MLA_HPS
    assert (n - ctx_len) % MLA_KCHUNK == 0
    return pl.pallas_call(
        functools.partial(_mla_attn_kernel, ctx_len=ctx_len),
        grid_spec=pltpu.PrefetchScalarGridSpec(
            num_scalar_prefetch=1,
            grid=(b, hp, n // TM),
            in_specs=[
                pl.BlockSpec((None, TM, MLA_HPS * HEAD_PAD), lambda i, p, t, f: (i, t, p)),
                pl.BlockSpec((None, n, MLA_HPS * HEAD_PAD), lambda i, p, t, f: (i, 0, p)),
                pl.BlockSpec((None, n, MLA_HPS * HEAD_PAD), lambda i, p, t, f: (i, 0, p)),
            ],
            out_specs=pl.BlockSpec((None, TM, MLA_HPS * A_V), lambda i, p, t, f: (i, t, p)),
            scratch_shapes=[
                pltpu.VMEM((MLA_HPS, TM, LANES), F32),
                pltpu.VMEM((MLA_HPS, TM, HEAD_PAD), F32),
            ],
        ),
        out_shape=jax.ShapeDtypeStruct((b, n, A_HEADS * A_V), BF16),
        compiler_params=_params(("arbitrary", "arbitrary", "arbitrary")),
        name="mla_attn",
    )(bounded, q, k, v)


def _dft_kernel(cn_ref, sn_ref, cc_ref, sc_ref, fcs_ref, y_ref, *, ctx_len):
    t = pl.program_id(1)

    @pl.when(t == 0)
    def _():
        fc = fcs_ref[0:ctx_len, 0:F_WIDTH]
        fs = fcs_ref[0:ctx_len, F_WIDTH:2 * F_WIDTH]
        y = (jnp.dot(cc_ref[...], fc, preferred_element_type=F32)
             - jnp.dot(sc_ref[...], fs, preferred_element_type=F32))
        y_ref[...] = y.astype(BF16)

    @pl.when(t > 0)
    def _():
        n = fcs_ref.shape[0]
        fc = fcs_ref[ctx_len:n, 0:F_WIDTH]
        fs = fcs_ref[ctx_len:n, F_WIDTH:2 * F_WIDTH]
        y = (jnp.dot(cn_ref[...], fc, preferred_element_type=F32)
             - jnp.dot(sn_ref[...], fs, preferred_element_type=F32))
        y_ref[...] = y.astype(BF16)


def _dft(fcs, cn, sn, cc, sc, ctx_len):
    b, n, _ = fcs.shape
    s = n - ctx_len
    assert ctx_len == TM
    return pl.pallas_call(
        functools.partial(_dft_kernel, ctx_len=ctx_len),
        grid=(b, n // TM),
        in_specs=[
            pl.BlockSpec((TM, s), lambda i, t: (jnp.maximum(t - 1, 0), 0)),
            pl.BlockSpec((TM, s), lambda i, t: (jnp.maximum(t - 1, 0), 0)),
            pl.BlockSpec((ctx_len, ctx_len), lambda i, t: (0, 0)),
            pl.BlockSpec((ctx_len, ctx_len), lambda i, t: (0, 0)),
            pl.BlockSpec((None, n, 2 * F_WIDTH), lambda i, t: (i, 0, 0)),
        ],
        out_specs=pl.BlockSpec((None, TM, F_WIDTH), lambda i, t: (i, t, 0)),
        out_shape=jax.ShapeDtypeStruct((b, n, F_WIDTH), BF16),
        compiler_params=_params(("arbitrary", "arbitrary")),
        name="fourier_dft",
    )(cn, sn, cc, sc, fcs)


def _out_proj_kernel(*refs, widths):
    mix_refs = refs[:len(widths)]
    w_ref, h_ref, mod_ref, o_ref = refs[len(widths):]
    acc = None
    off = 0
    for m_ref, wd in zip(mix_refs, widths):
        part = jnp.dot(m_ref[...], w_ref[off:off + wd, :], preferred_element_type=F32)
        acc = part if acc is None else acc + part
        off += wd
    o_ref[...] = h_ref[...] + mod_ref[2:3, :] * acc


def _out_proj(mixes, w, h, mod):
    b, n, d = h.shape
    widths = tuple(m.shape[-1] for m in mixes)
    tile = lambda wd: pl.BlockSpec((None, TM, wd), lambda i, t: (i, t, 0))
    return pl.pallas_call(
        functools.partial(_out_proj_kernel, widths=widths),
        grid=(b, n // TM),
        in_specs=[tile(wd) for wd in widths] + [
            pl.BlockSpec(w.shape, lambda i, t: (0, 0)),
            tile(d),
            pl.BlockSpec((None, 6, d), lambda i, t: (jnp.where(t == 0, b, i), 0, 0)),
        ],
        out_specs=tile(d),
        out_shape=jax.ShapeDtypeStruct((b, n, d), F32),
        compiler_params=_params(("arbitrary", "arbitrary")),
        name="out_proj",
    )(*mixes, w, h, mod)


def _ffn_kernel(h_ref, mod_ref, g_ref, wg_ref, wu_ref, wd_ref, o_ref):
    xn = _norm_mod(h_ref, g_ref, mod_ref, 3).astype(BF16)
    gt = jnp.dot(xn, wg_ref[...], preferred_element_type=F32)
    up = jnp.dot(xn, wu_ref[...], preferred_element_type=F32)
    a = (_silu(gt) * up).astype(BF16)
    y = jnp.dot(a, wd_ref[...], preferred_element_type=F32)
    o_ref[...] = h_ref[...] + mod_ref[5:6, :] * y


def _ffn(h, mod, g, wg, wu, wd):
    b, n, d = h.shape
    tile = pl.BlockSpec((None, TM, d), lambda i, t: (i, t, 0))
    full = lambda a: pl.BlockSpec(a.shape, lambda i, t: (0,) * a.ndim)
    return pl.pallas_call(
        _ffn_kernel,
        grid=(b, n // TM),
        in_specs=[
            tile,
            pl.BlockSpec((None, 6, d), lambda i, t: (jnp.where(t == 0, b, i), 0, 0)),
            full(g), full(wg), full(wu), full(wd),
        ],
        out_specs=tile,
        out_shape=jax.ShapeDtypeStruct((b, n, d), F32),
        compiler_params=_params(("arbitrary", "arbitrary")),
        name="ffn_dense",
    )(h, mod, g, wg, wu, wd)


def _na_front_kernel(h_ref, mod_ref, g_ref, win_ref, gq_ref, gk_ref, q_ref, k_ref, v_ref):
    xn = _norm_mod(h_ref, g_ref, mod_ref, 0)
    px = jnp.dot(xn.astype(BF16), win_ref[...], preferred_element_type=F32)
    width = C_HEADS * C_DH
    lane = lax.broadcasted_iota(jnp.int32, (px.shape[0], LANES), 1)
    low = lane < C_DH
    scale = C_DH ** -0.5

    def norm_pairs(base, g_ref_, out_ref, mult):
        gg = g_ref_[...]
        for j in range(width // LANES):
            x = px[:, base + j * LANES:base + (j + 1) * LANES]
            x2 = x * x
            s_all = jnp.sum(x2, axis=-1, keepdims=True)
            s_lo = jnp.sum(jnp.where(low, x2, 0.0), axis=-1, keepdims=True)
            ss = jnp.where(low, s_lo, s_all - s_lo)
            inv = lax.rsqrt(ss * (1.0 / C_DH) + EPS)
            if mult != 1.0:
                inv = inv * mult
            out_ref[:, j * LANES:(j + 1) * LANES] = (x * inv * gg).astype(BF16)

    norm_pairs(0, gq_ref, q_ref, scale)
    norm_pairs(width, gk_ref, k_ref, 1.0)
    v_ref[...] = px[:, 2 * width:].astype(BF16)


def _na_front(h, mod, g, win, gq, gk):
    b, n, d = h.shape
    width = C_HEADS * C_DH
    tile = lambda w: pl.BlockSpec((None, TM, w), lambda i, t: (i, t, 0))
    full = lambda a: pl.BlockSpec(a.shape, lambda i, t: (0,) * a.ndim)
    return pl.pallas_call(
        _na_front_kernel,
        grid=(b, n // TM),
        in_specs=[
            tile(d),
            pl.BlockSpec((None, 6, d), lambda i, t: (jnp.where(t == 0, b, i), 0, 0)),
            full(g), full(win), full(gq), full(gk),
        ],
        out_specs=[tile(width)] * 3,
        out_shape=[jax.ShapeDtypeStruct((b, n, width), BF16)] * 3,
        compiler_params=_params(("arbitrary", "arbitrary")),
        name="na_front",
    )(h, mod, g, win, gq, gk)


def _na_attn_kernel(q_ref, k_ref, v_ref, bias_ref, o_ref, *, ctx_len, rows_n):
    qlen = NA_GROUP * GRID_W
    wlen = NA_WINDOW * GRID_W
    n_groups = rows_n // NA_GROUP
    lane_q = lax.broadcasted_iota(jnp.int32, (qlen, LANES), 1)
    kc = k_ref[0:ctx_len, :]
    vc = v_ref[0:ctx_len, :]

    qc = q_ref[0:ctx_len, :]
    lane_c = lax.broadcasted_iota(jnp.int32, (ctx_len, LANES), 1)
    outs = []
    for hh in range(2):
        sel = (lane_c < C_DH) if hh == 0 else (lane_c >= C_DH)
        outs.append(_softmax_pv(jnp.where(sel, qc, jnp.zeros_like(qc)), kc, vc))
    o_ref[0:ctx_len, :] = jnp.where(lane_c < C_DH, outs[0], outs[1]).astype(BF16)

    def group_body(g, carry):
        start = jnp.clip(g * NA_GROUP - WIN_R // 2, 0, rows_n - NA_WINDOW)
        pat = jnp.where(g == 0, 0, jnp.where(g == n_groups - 1, 2, 1))
        q0 = pl.multiple_of(ctx_len + g * qlen, GRID_W)
        k0 = pl.multiple_of(ctx_len + start * GRID_W, GRID_W)
        qr = q_ref[pl.ds(q0, qlen), :]
        kw = k_ref[pl.ds(k0, wlen), :]
        vw = v_ref[pl.ds(k0, wlen), :]
        res = []
        for hh in range(2):
            sel = (lane_q < C_DH) if hh == 0 else (lane_q >= C_DH)
            qh = jnp.where(sel, qr, jnp.zeros_like(qr))
            s_loc = lax.dot_general(qh, kw, (((1,), (1,)), ((), ())),
                                    preferred_element_type=F32) + bias_ref[pat, hh]
            s_ctx = lax.dot_general(qh, kc, (((1,), (1,)), ((), ())), preferred_element_type=F32)
            m = jnp.maximum(jnp.max(s_loc, axis=-1, keepdims=True),
                            jnp.max(s_ctx, axis=-1, keepdims=True))
            p_loc = jnp.exp(s_loc - m)
            p_ctx = jnp.exp(s_ctx - m)
            l = jnp.sum(p_loc, axis=-1, keepdims=True) + jnp.sum(p_ctx, axis=-1, keepdims=True)
            o = (jnp.dot(p_loc.astype(BF16), vw, preferred_element_type=F32)
                 + jnp.dot(p_ctx.astype(BF16), vc, preferred_element_type=F32))
            res.append(o * (1.0 / l))
        o_ref[pl.ds(q0, qlen), :] = jnp.where(lane_q < C_DH, res[0], res[1]).astype(BF16)
        return carry

    lax.fori_loop(0, n_groups, group_body, 0, unroll=2)


def _na_attn(q, k, v, bias, ctx_len):
    b, n, width = q.shape
    rows_n = (n - ctx_len) // GRID_W
    hp = C_HEADS // 2
    blk = pl.BlockSpec((None, n, LANES), lambda i, p: (i, 0, p))
    return pl.pallas_call(
        functools.partial(_na_attn_kernel, ctx_len=ctx_len, rows_n=rows_n),
        grid=(b, hp),
        in_specs=[blk, blk, blk,
                  pl.BlockSpec((3, 2) + bias.shape[2:], lambda i, p: (0, p, 0, 0))],
        out_specs=blk,
        out_shape=jax.ShapeDtypeStruct((b, n, width), BF16),
        compiler_params=_params(("arbitrary", "arbitrary")),
        name="na_attn",
    )(q, k, v, bias)


def _router_kernel(h_ref, mod_ref, g_ref, wr_ref, xn_ref, route_ref):
    xn = _norm_mod(h_ref, g_ref, mod_ref, 3)
    xn_ref[...] = xn
    logits = jnp.dot(xn, wr_ref[...], precision=lax.Precision.HIGHEST, preferred_element_type=F32)
    lane = lax.broadcasted_iota(jnp.int32, logits.shape, 1).astype(F32)
    lg = jnp.where(lane < N_EXPERTS, logits, -jnp.inf)
    m1 = jnp.max(lg, axis=-1, keepdims=True)
    i1 = jnp.min(jnp.where(lg == m1, lane, float(LANES)), axis=-1, keepdims=True)
    lg2 = jnp.where(lane == i1, -jnp.inf, lg)
    m2 = jnp.max(lg2, axis=-1, keepdims=True)
    i2 = jnp.min(jnp.where(lg2 == m2, lane, float(LANES)), axis=-1, keepdims=True)
    e2 = jnp.exp(m2 - m1)
    g1 = 1.0 / (1.0 + e2)
    g2 = e2 * g1
    first_low = i1 < i2
    vals = (jnp.minimum(i1, i2), jnp.maximum(i1, i2), jnp.where(first_low, g1, g2), jnp.where(first_low, g2, g1))
    route = jnp.zeros_like(lane)
    for idx, val in enumerate(vals):
        route = jnp.where(lane == float(idx), val, route)
    route_ref[...] = route


def _router(h, mod, g, wr):
    b, n, d = h.shape
    tile = lambda w, dt=None: pl.BlockSpec((None, TM, w), lambda i, t: (i, t, 0))
    full = lambda a: pl.BlockSpec(a.shape, lambda i, t: (0,) * a.ndim)
    return pl.pallas_call(
        _router_kernel,
        grid=(b, n // TM),
        in_specs=[
            tile(d),
            pl.BlockSpec((None, 6, d), lambda i, t: (jnp.where(t == 0, b, i), 0, 0)),
            full(g), full(wr),
        ],
        out_specs=[tile(d), tile(LANES)],
        out_shape=[jax.ShapeDtypeStruct((b, n, d), F32), jax.ShapeDtypeStruct((b, n, LANES), F32)],
        compiler_params=_params(("arbitrary", "arbitrary")),
        name="moe_router",
    )(h, mod, g, wr)


def _moe_plan(route, n_tok):
    n_pairs = len(_PAIR_LO)
    n_tiles = n_tok // MOE_TM + n_pairs
    rows = n_tiles * MOE_TM
    lo = route[:, 0].astype(jnp.int32)
    hi = route[:, 1].astype(jnp.int32)
    pid = (lo * (2 * N_EXPERTS - 1 - lo)) // 2 + (hi - lo - 1)
    onehot = (pid[:, None] == jnp.arange(n_pairs, dtype=jnp.int32)[None, :]).astype(jnp.int32)
    csum = jnp.cumsum(onehot, axis=0)
    rank = jnp.sum(csum * onehot, axis=1) - 1
    counts = csum[-1]
    padded = ((counts + MOE_TM - 1) // MOE_TM) * MOE_TM
    gend = jnp.cumsum(padded)
    gstart = gend - padded
    dest = jnp.sum(gstart[None, :] * onehot, axis=1) + rank
    src = jnp.full((rows,), -1, jnp.int32).at[dest].set(jnp.arange(n_tok, dtype=jnp.int32))
    valid = src >= 0
    src_tok = jnp.where(valid, src, 0)
    n_valid = jnp.sum(valid.reshape(n_tiles, MOE_TM).astype(jnp.int32), axis=1)
    gates = jnp.where(valid[:, None], route[src_tok, 2:4], 0.0)
    gates = jnp.concatenate([gates, jnp.zeros((rows, LANES - 2), F32)], axis=1)
    n_used = gend[-1] // MOE_TM
    tile_row = jnp.minimum(jnp.arange(n_tiles, dtype=jnp.int32), n_used - 1) * MOE_TM
    group = jnp.minimum(jnp.sum((gend[None, :] <= tile_row[:, None]).astype(jnp.int32), axis=1), n_pairs - 1)
    ea = jnp.asarray(_PAIR_LO, jnp.int32)[group]
    eb = jnp.asarray(_PAIR_HI, jnp.int32)[group]
    return ea, eb, src_tok, n_valid, n_used.astype(jnp.int32).reshape(1), gates


def _moe_kernel(ea_ref, eb_ref, src_ref, nvalid_ref, nused_ref,
                x_hbm, gates_ref, wga_ref, wua_ref, wda_ref, wgb_ref, wub_ref, wdb_ref,
                y_hbm, xbuf, ybuf, gsem, ssem):
    i = pl.program_id(0)
    n_steps = pl.num_programs(0)
    n_used = nused_ref[0]
    slot = i % 2

    def gather_copy(tile, slot_, r):
        return pltpu.make_async_copy(x_hbm.at[pl.ds(src_ref[tile * MOE_TM + r], 1)],
                                     xbuf.at[slot_, pl.ds(r, 1)], gsem.at[slot_])

    def scatter_copy(tile, slot_, r):
        return pltpu.make_async_copy(ybuf.at[slot_, pl.ds(r, 1)],
                                     y_hbm.at[pl.ds(src_ref[tile * MOE_TM + r], 1)], ssem.at[slot_])

    def start_gather(tile, slot_):
        for r in range(MOE_TM):
            gather_copy(tile, slot_, r).start()

    def wait_gather(slot_):
        pltpu.make_async_copy(x_hbm.at[pl.ds(0, MOE_TM)], xbuf.at[slot_], gsem.at[slot_]).wait()

    def start_scatter(tile, slot_):
        full = nvalid_ref[tile] == MOE_TM

        @pl.when(full)
        def _():
            for r in range(MOE_TM):
                scatter_copy(tile, slot_, r).start()

        @pl.when(jnp.logical_not(full))
        def _():
            def body(r, carry):
                scatter_copy(tile, slot_, r).start()
                return carry
            lax.fori_loop(0, nvalid_ref[tile], body, 0)

    def wait_scatter(tile, slot_):
        full = nvalid_ref[tile] == MOE_TM

        @pl.when(full)
        def _():
            pltpu.make_async_copy(ybuf.at[slot_], y_hbm.at[pl.ds(0, MOE_TM)], ssem.at[slot_]).wait()

        @pl.when(jnp.logical_not(full))
        def _():
            def body(r, carry):
                scatter_copy(tile, slot_, r).wait()
                return carry
            lax.fori_loop(0, nvalid_ref[tile], body, 0)

    @pl.when((i >= 2) & (i - 2 < n_used))
    def _():
        wait_scatter(i - 2, slot)

    @pl.when(i < n_used)
    def _():
        @pl.when(i == 0)
        def _():
            start_gather(0, 0)

        @pl.when(i + 1 < n_used)
        def _():
            start_gather(i + 1, 1 - slot)

        wait_gather(slot)
        x = xbuf[slot].astype(BF16)

        def expert(wg_ref, wu_ref, wd_ref):
            gt = jnp.dot(x, wg_ref[...], preferred_element_type=F32)
            up = jnp.dot(x, wu_ref[...], preferred_element_type=F32)
            a = (_silu(gt) * up).astype(BF16)
            return jnp.dot(a, wd_ref[...], preferred_element_type=F32)

        gates = gates_ref[...]
        ybuf[slot] = (gates[:, 0:1] * expert(wga_ref, wua_ref, wda_ref)
                      + gates[:, 1:2] * expert(wgb_ref, wub_ref, wdb_ref))
        start_scatter(i, slot)

    @pl.when(i == n_steps - 1)
    def _():
        for back in (2, 1):
            @pl.when(n_steps - back < n_used)
            def _():
                wait_scatter(n_steps - back, (n_steps - back) % 2)


def _moe(xn, route, wg, wu, wd):
    n_tok, d = xn.shape
    _, _, f = wg.shape
    ea, eb, src_tok, n_valid, n_used, gates = _moe_plan(route, n_tok)
    n_tiles = ea.shape[0]
    wspec = lambda shape, which: pl.BlockSpec(
        (None,) + shape, lambda i, ea_, eb_, s_, d_, u_: ((ea_, eb_)[which][i], 0, 0))
    return pl.pallas_call(
        _moe_kernel,
        grid_spec=pltpu.PrefetchScalarGridSpec(
            num_scalar_prefetch=5,
            grid=(n_tiles,),
            in_specs=[
                pl.BlockSpec(memory_space=pl.ANY),
                pl.BlockSpec((MOE_TM, LANES), lambda i, *_: (i, 0)),
                wspec((d, f), 0), wspec((d, f), 0), wspec((f, d), 0),
                wspec((d, f), 1), wspec((d, f), 1), wspec((f, d), 1),
            ],
            out_specs=pl.BlockSpec(memory_space=pl.ANY),
            scratch_shapes=[
                pltpu.VMEM((2, MOE_TM, d), F32),
                pltpu.VMEM((2, MOE_TM, d), F32),
                pltpu.SemaphoreType.DMA((2,)),
                pltpu.SemaphoreType.DMA((2,)),
            ],
        ),
        out_shape=jax.ShapeDtypeStruct((n_tok, d), F32),
        compiler_params=_params(("arbitrary",)),
        name="moe_experts",
    )(ea, eb, src_tok, n_valid, n_used, xn, gates, wg, wu, wd, wg, wu, wd)


def _resid_kernel(h_ref, y_ref, mod_ref, o_ref):
    o_ref[...] = h_ref[...] + mod_ref[5:6, :] * y_ref[...]


def _resid(h, y, mod, latent_only):
    b, n, d = h.shape
    skip = 1 if latent_only else 0
    nt_all = n // TM
    nt = nt_all - skip
    if latent_only:
        mod_spec = pl.BlockSpec((None, 6, d), lambda i, t: (i, 0, 0))
    else:
        mod_spec = pl.BlockSpec((None, 6, d), lambda i, t: (jnp.where(t == 0, b, i), 0, 0))
    return pl.pallas_call(
        _resid_kernel,
        grid=(b, nt),
        in_specs=[
            pl.BlockSpec((None, TM, d), lambda i, t: (i, t + skip, 0)),
            pl.BlockSpec((TM, d), lambda i, t: (i * nt_all + t + skip, 0)),
            mod_spec,
        ],
        out_specs=pl.BlockSpec((None, TM, d), lambda i, t: (i, t, 0)),
        out_shape=jax.ShapeDtypeStruct((b, nt * TM, d), F32),
        compiler_params=_params(("arbitrary", "arbitrary")),
        name="moe_residual",
    )(h, y, mod)


def _rope_tables(s, ctx_len):
    t = jnp.arange(s)
    rows = (t // GRID_W).astype(F32)
    cols = (t % GRID_W).astype(F32)
    n_pairs = A_ROPE // 2
    per_axis = n_pairs // 2
    inv = ROPE_THETA ** (-jnp.arange(per_axis, dtype=F32) / per_axis)
    ang = jnp.concatenate([rows[:, None] * inv, cols[:, None] * inv], axis=-1)
    cos = jnp.repeat(jnp.cos(ang), 2, axis=-1)
    sin = jnp.repeat(jnp.sin(ang), 2, axis=-1)
    sign = jnp.tile(jnp.array([-1.0, 1.0], F32), n_pairs)
    pad = HEAD_PAD - A_QK
    t1 = jnp.concatenate([jnp.ones((s, A_NOPE), F32), cos, jnp.zeros((s, pad), F32)], axis=-1)
    t2 = jnp.concatenate([jnp.zeros((s, A_NOPE), F32), sin * sign, jnp.zeros((s, pad), F32)], axis=-1)
    c1 = jnp.concatenate([jnp.ones((ctx_len, A_QK), F32), jnp.zeros((ctx_len, pad), F32)], axis=-1)
    c2 = jnp.zeros((ctx_len, HEAD_PAD), F32)
    return jnp.concatenate([c1, t1], axis=0), jnp.concatenate([c2, t2], axis=0)


_PAIR_SWAP = np.arange(A_ROPE) ^ 1


def _rope_lane_vec(g_tail, swapped):
    gt = g_tail[_PAIR_SWAP] if swapped else g_tail
    return jnp.concatenate([jnp.zeros((A_NOPE,), F32), gt, jnp.zeros((HEAD_PAD - A_QK,), F32)])[None]


def _mla_weights(w_in, w_uq, w_ukv, g_q, g_k):
    d = w_in.shape[0]
    q_end = A_Q_RANK
    kv_end = q_end + A_KV_RANK
    r_end = kv_end + A_ROPE
    z = lambda n: jnp.zeros((d, n), w_in.dtype)
    kr = w_in[:, kv_end:r_end]
    pad = HEAD_PAD - A_QK
    win = jnp.concatenate([w_in[:, :kv_end], z(A_NOPE), kr, z(pad), z(A_NOPE), kr[:, _PAIR_SWAP], z(pad),
                           w_in[:, r_end:]], axis=-1).astype(BF16)
    wq = w_uq.reshape(A_Q_RANK, A_HEADS, A_QK)
    zq = lambda n: jnp.zeros((A_Q_RANK, A_HEADS, n), wq.dtype)
    wq_main = jnp.concatenate([wq, zq(pad)], axis=-1)
    wq_swap = jnp.concatenate([zq(A_NOPE), wq[:, :, A_NOPE:][:, :, _PAIR_SWAP], zq(pad)], axis=-1)
    wq_ext = jnp.concatenate([wq_main.reshape(A_Q_RANK, -1), wq_swap.reshape(A_Q_RANK, -1)], axis=-1).astype(BF16)
    wkv = w_ukv.reshape(A_KV_RANK, A_HEADS, A_NOPE + A_V)
    zkv = jnp.zeros((A_KV_RANK, A_HEADS, HEAD_PAD - A_NOPE), wkv.dtype)
    wk = jnp.concatenate([wkv[:, :, :A_NOPE], zkv], axis=-1)
    wv = jnp.concatenate([wkv[:, :, A_NOPE:], zkv], axis=-1)
    wkv_ext = jnp.concatenate([wk.reshape(A_KV_RANK, -1), wv.reshape(A_KV_RANK, -1)], axis=-1).astype(BF16)

    bound = math.sqrt(A_QK) * jnp.max(jnp.abs(g_q)) * jnp.max(jnp.abs(g_k))
    bounded = bound <= MLA_SAFE_BOUND
    shift = jnp.where(bounded, -bound * LOG2E, 0.0)
    unit = lambda lane_idx: jnp.zeros((1, HEAD_PAD), F32).at[0, lane_idx].set(1.0)
    gq = jnp.concatenate([jnp.concatenate([g_q, jnp.zeros((pad,), F32)])[None], _rope_lane_vec(g_q[A_NOPE:], True),
                          unit(A_QK) * shift], axis=0)
    gk = jnp.concatenate([jnp.concatenate([g_k[:A_NOPE], jnp.zeros((HEAD_PAD - A_NOPE,), F32)])[None],
                          _rope_lane_vec(g_k[A_NOPE:], False), _rope_lane_vec(g_k[A_NOPE:], True),
                          unit(A_QK), unit(A_V)], axis=0)
    return win, wq_ext, wkv_ext, gq, gk, bounded.astype(jnp.int32).reshape(1)


def _dft_mats(n, norm):
    k = jnp.arange(n, dtype=jnp.int32)
    ang = ((k[:, None] * k[None, :]) % n).astype(F32) * (2.0 * math.pi / n)
    return (jnp.cos(ang) * norm).astype(BF16), (jnp.sin(ang) * norm).astype(BF16)


def _channel_dft():
    c = np.arange(F_CH)
    ang = 2.0 * np.pi * ((c[:, None] * c[None, :]) % F_CH) / F_CH
    eye = np.eye(F_GROUPS)
    cb = np.kron(eye, np.cos(ang)) / math.sqrt(F_CH)
    sb = np.kron(eye, np.sin(ang)) / math.sqrt(F_CH)
    return jnp.asarray(np.concatenate([cb, sb], axis=1), BF16)


def _na_group_rows(rows_n):
    assert rows_n % NA_GROUP == 0 and rows_n >= NA_WINDOW + NA_GROUP
    n_groups = rows_n // NA_GROUP
    n_dr = 2 * WIN_R - 1
    idx = np.full((3, NA_GROUP, NA_WINDOW), n_dr, np.int32)
    seen = {}
    for g in range(n_groups):
        gs = int(np.clip(g * NA_GROUP - WIN_R // 2, 0, rows_n - NA_WINDOW))
        pat = 0 if g == 0 else (2 if g == n_groups - 1 else 1)
        cur = np.full((NA_GROUP, NA_WINDOW), n_dr, np.int32)
        for qi in range(NA_GROUP):
            r = g * NA_GROUP + qi
            start = int(np.clip(r - WIN_R // 2, 0, rows_n - WIN_R))
            assert gs <= start and start + WIN_R <= gs + NA_WINDOW
            for a in range(start, start + WIN_R):
                cur[qi, a - gs] = a - r + (WIN_R - 1)
        assert pat not in seen or np.array_equal(seen[pat], cur)
        seen[pat] = cur
        idx[pat] = cur
    return idx


def _na_bias(rpb, rows_n):
    h, n_dr, n_dc = rpb.shape
    qc = np.arange(GRID_W)
    cs = np.clip(qc - WIN_C // 2, 0, GRID_W - WIN_C)
    kcol = np.arange(GRID_W)
    valid = (kcol[None, :] >= cs[:, None]) & (kcol[None, :] < cs[:, None] + WIN_C)
    dc = kcol[None, :] - qc[:, None] + (WIN_C - 1)
    onehot = (valid[:, :, None] & (dc[:, :, None] == np.arange(n_dc)[None, None, :])).astype(np.float32)
    col = jnp.einsum('hdc,qkc->hdqk', rpb.astype(F32), jnp.asarray(onehot), precision=lax.Precision.HIGHEST)
    col = jnp.where(jnp.asarray(valid)[None, None], col, NEG_BIG)
    col = jnp.concatenate([col, jnp.full((h, 1, GRID_W, GRID_W), NEG_BIG, F32)], axis=1)
    idx = _na_group_rows(rows_n)
    blocks = [[jnp.stack([col[:, int(idx[p, qi, a])] for a in range(NA_WINDOW)], axis=2)
               for qi in range(NA_GROUP)] for p in range(3)]
    tab = jnp.stack([jnp.stack(bq, axis=1) for bq in blocks], axis=0)
    return tab.reshape(3, h, NA_GROUP * GRID_W, NA_WINDOW * GRID_W)


def kernel(x, c, ctx, c_ctx, w_mod, b_mod, norm_g, a_w_in, a_g_cq, a_g_ckv, a_w_uq, a_w_ukv, a_g_q, a_g_k, a_w_out,
           f_w_gate, f_w_up, f_w_down, c_w_in, c_g_q, c_g_k, c_rpb, c_w_out, m_w_router, m_w_gate, m_w_up, m_w_down):
    b, s, d = x.shape
    l = ctx.shape[1]
    depth = w_mod.shape[0]
    n = l + s
    assert l == TM and s % TM == 0 and s % GRID_W == 0

    mod_rows = 16
    cin = jnp.concatenate([c, c_ctx[None], jnp.zeros((mod_rows - b - 1, d), F32)], axis=0)
    mod_all = _modulation(cin, w_mod, b_mod).reshape(depth, mod_rows, 6, d)

    t1, t2 = _rope_tables(s, l)
    cn, sn = _dft_mats(s, 1.0 / math.sqrt(s))
    cc, sc = _dft_mats(l, 1.0 / math.sqrt(l))
    cb = _channel_dft()

    h = jnp.concatenate([ctx, x], axis=1)
    for i in range(depth):
        j = i // 2
        last = i == depth - 1
        mod = mod_all[i]
        g1 = norm_g[i, 0][None]
        g2 = norm_g[i, 1][None]
        if i % 2 == 0:
            win, wq_ext, wkv_ext, gq, gk, bounded = _mla_weights(a_w_in[j], a_w_uq[j], a_w_ukv[j],
                                                                 a_g_q[j], a_g_k[j])
            q, k, v, fcs = _mla_front(h, mod, g1, win, a_g_cq[j][None], a_g_ckv[j][None], wq_ext, wkv_ext,
                                      t1, t2, gq, gk, cb)
            att = _mla_attn(bounded, q, k, v, l)
            yf = _dft(fcs, cn, sn, cc, sc, l)
            h = _out_proj([att, yf], a_w_out[j].astype(BF16), h, mod)
            h = _ffn(h, mod, g2, f_w_gate[j].astype(BF16), f_w_up[j].astype(BF16), f_w_down[j].astype(BF16))
        else:
            gq2 = jnp.tile(c_g_q[j], 2)[None]
            gk2 = jnp.tile(c_g_k[j], 2)[None]
            q, k, v = _na_front(h, mod, g1, c_w_in[j].astype(BF16), gq2, gk2)
            bias = _na_bias(c_rpb[j], s // GRID_W)
            att = _na_attn(q, k, v, bias, l)
            h = _out_proj([att], c_w_out[j].astype(BF16), h, mod)
            wr = jnp.concatenate([m_w_router[j], jnp.zeros((d, LANES - N_EXPERTS), F32)], axis=-1)
            xn2, route = _router(h, mod, g2, wr)
            y = _moe(xn2.reshape(b * n, d), route.reshape(b * n, LANES),
                     m_w_gate[j].astype(BF16), m_w_up[j].astype(BF16), m_w_down[j].astype(BF16))
            h = _resid(h, y, mod, last)
    if h.shape[1] != s:
        h = h[:, l:]
    return h
```

```python
import functools
import math

import jax
import jax.numpy as jnp
import numpy as np
from jax import lax
from jax.experimental import pallas as pl
from jax.experimental.pallas import tpu as pltpu

F32 = jnp.float32
BF16 = jnp.bfloat16

GRID_W = 64
A_HEADS = 12
A_NOPE = 64
A_ROPE = 32
A_QK = A_NOPE + A_ROPE
A_V = 64
A_Q_RANK = 256
A_KV_RANK = 128
F_GROUPS = 4
F_CH = 64
F_WIDTH = F_GROUPS * F_CH
C_HEADS = 16
C_DH = 64
WIN_R = 8
WIN_C = 16
N_EXPERTS = 8
ROPE_THETA = 10000.0
EPS = 1e-6

LANES = 128
HEAD_PAD = 128
TM = 256
MOE_TM = 256
_PAIR_LO = tuple(lo for lo in range(N_EXPERTS) for hi in range(lo + 1, N_EXPERTS))
_PAIR_HI = tuple(hi for lo in range(N_EXPERTS) for hi in range(lo + 1, N_EXPERTS))
LOG2E = math.log2(math.e)
MLA_SAFE_BOUND = 40.0
MLA_KCHUNK = 512
MLA_HPS = 2
NA_GROUP = 4
NA_WINDOW = NA_GROUP + WIN_R
VMEM_LIMIT = 52 * 1024 * 1024
NEG_BIG = -1e30


def _params(sem, vmem=VMEM_LIMIT):
    return pltpu.CompilerParams(dimension_semantics=sem, vmem_limit_bytes=vmem)


def _rms(x, g, n=None):
    n = x.shape[-1] if n is None else n
    ss = jnp.sum(x * x, axis=-1, keepdims=True)
    return x * lax.rsqrt(ss * (1.0 / n) + EPS) * g


def _silu(x):
    return x * (1.0 / (1.0 + jnp.exp(-x)))


def _norm_mod(h_ref, g_ref, mod_ref, row):
    x = h_ref[...]
    xn = _rms(x, g_ref[...])
    return xn * (1.0 + mod_ref[row + 1:row + 2, :]) + mod_ref[row:row + 1, :]


def _mod_kernel(c_ref, w_ref, b_ref, o_ref):
    s = _silu(c_ref[...])
    o_ref[...] = jnp.dot(s, w_ref[...], precision=lax.Precision.HIGHEST,
                         preferred_element_type=F32) + b_ref[...]


def _modulation(cin, w_mod, b_mod):
    depth, d, n = w_mod.shape
    tn = 1536
    rows = cin.shape[0]
    return pl.pallas_call(
        _mod_kernel,
        grid=(depth, n // tn),
        in_specs=[
            pl.BlockSpec((rows, d), lambda l, j: (0, 0)),
            pl.BlockSpec((None, d, tn), lambda l, j: (l, 0, j)),
            pl.BlockSpec((None, 1, tn), lambda l, j: (l, 0, j)),
        ],
        out_specs=pl.BlockSpec((None, rows, tn), lambda l, j: (l, 0, j)),
        out_shape=jax.ShapeDtypeStruct((depth, rows, n), F32),
        compiler_params=_params(("arbitrary", "arbitrary")),
        name="modulation",
    )(cin, w_mod, b_mod.reshape(depth, 1, n))


def _mla_front_kernel(h_ref, mod_ref, g_ref, win_ref, gcq_ref, gckv_ref, wq_ref, wkv_ref,
                      t1_ref, t2_ref, gq_ref, gk_ref, cb_ref,
                      q_ref, k_ref, v_ref, fcs_ref):
    xn = _norm_mod(h_ref, g_ref, mod_ref, 0)
    px = jnp.dot(xn.astype(BF16), win_ref[...], preferred_element_type=F32)
    t1 = t1_ref[...]
    t2 = t2_ref[...]
    hw = A_HEADS * HEAD_PAD

    qn = _rms(px[:, 0:A_Q_RANK], gcq_ref[...]).astype(BF16)
    qq = jnp.dot(qn, wq_ref[...], preferred_element_type=F32)
    aq = t1 * gq_ref[0:1, :]
    bq = t2 * gq_ref[1:2, :]
    q_shift = gq_ref[2:3, :]
    scale = A_QK ** -0.5 * LOG2E
    for h in range(A_HEADS):
        qm = qq[:, h * HEAD_PAD:(h + 1) * HEAD_PAD]
        qs = qq[:, hw + h * HEAD_PAD:hw + (h + 1) * HEAD_PAD]
        ss = jnp.sum(qm * qm, axis=-1, keepdims=True)
        inv = lax.rsqrt(ss * (1.0 / A_QK) + EPS) * scale
        q_ref[:, h * HEAD_PAD:(h + 1) * HEAD_PAD] = ((qm * aq + qs * bq) * inv + q_shift).astype(BF16)

    kvn = _rms(px[:, A_Q_RANK:A_Q_RANK + A_KV_RANK], gckv_ref[...]).astype(BF16)
    kv = jnp.dot(kvn, wkv_ref[...], preferred_element_type=F32)
    krm = px[:, 384:512]
    krs = px[:, 512:640]
    tail = krm * (t1 * gk_ref[1:2, :]) + krs * (t2 * gk_ref[2:3, :])
    ssr = jnp.sum(krm * krm, axis=-1, keepdims=True)
    gkn = gk_ref[0:1, :]
    k_one = gk_ref[3:4, :]
    v_one = gk_ref[4:5, :]
    for h in range(A_HEADS):
        km = kv[:, h * HEAD_PAD:(h + 1) * HEAD_PAD]
        ss = jnp.sum(km * km, axis=-1, keepdims=True) + ssr
        inv = lax.rsqrt(ss * (1.0 / A_QK) + EPS)
        k_ref[:, h * HEAD_PAD:(h + 1) * HEAD_PAD] = ((km * gkn + tail) * inv + k_one).astype(BF16)
        v_ref[:, h * HEAD_PAD:(h + 1) * HEAD_PAD] = (kv[:, hw + h * HEAD_PAD:hw + (h + 1) * HEAD_PAD]
                                                     + v_one).astype(BF16)

    f = px[:, 640:896].astype(BF16)
    fcs_ref[...] = jnp.dot(f, cb_ref[...], preferred_element_type=F32).astype(BF16)


def _mla_front(h, mod, g, win, gcq, gckv, wq, wkv, t1, t2, gq, gk, cb):
    b, n, d = h.shape
    nt = n // TM
    hw = A_HEADS * HEAD_PAD
    tile = lambda w: pl.BlockSpec((None, TM, w), lambda i, t: (i, t, 0))
    full = lambda a: pl.BlockSpec(a.shape, lambda i, t: (0,) * a.ndim)
    return pl.pallas_call(
        _mla_front_kernel,
        grid=(b, nt),
        in_specs=[
            tile(d),
            pl.BlockSpec((None, 6, d), lambda i, t: (jnp.where(t == 0, b, i), 0, 0)),
            full(g), full(win), full(gcq), full(gckv), full(wq), full(wkv),
            pl.BlockSpec((TM, LANES), lambda i, t: (t, 0)),
            pl.BlockSpec((TM, LANES), lambda i, t: (t, 0)),
            full(gq), full(gk), full(cb),
        ],
        out_specs=[tile(hw), tile(hw), tile(hw), tile(2 * F_WIDTH)],
        out_shape=[
            jax.ShapeDtypeStruct((b, n, hw), BF16),
            jax.ShapeDtypeStruct((b, n, hw), BF16),
            jax.ShapeDtypeStruct((b, n, hw), BF16),
            jax.ShapeDtypeStruct((b, n, 2 * F_WIDTH), BF16),
        ],
        compiler_params=_params(("arbitrary", "arbitrary")),
        name="mla_front",
    )(h, mod, g, win, gcq, gckv, wq, wkv, t1, t2, gq, gk, cb)


def _softmax_pv(q, k, v):
    s = lax.dot_general(q, k, (((1,), (1,)), ((), ())), preferred_element_type=F32)
    m = jnp.max(s, axis=-1, keepdims=True)
    p = jnp.exp2(s - m)
    l = jnp.sum(p, axis=-1, keepdims=True)
    o = jnp.dot(p.astype(BF16), v, preferred_element_type=F32)
    return o * (1.0 / l)


def _mla_attn_kernel(bounded_ref, q_ref, k_ref, v_ref, o_ref, m_ref, acc_ref, *, ctx_len):
    t = pl.program_id(2)
    n = k_ref.shape[0]
    ck = MLA_KCHUNK

    def scores(hh, k0, klen):
        q = q_ref[:, hh * HEAD_PAD:(hh + 1) * HEAD_PAD]
        k = k_ref[pl.ds(k0, klen), hh * HEAD_PAD:(hh + 1) * HEAD_PAD]
        return lax.dot_general(q, k, (((1,), (1,)), ((), ())), preferred_element_type=F32)

    def vblk(hh, k0, klen):
        return v_ref[pl.ds(k0, klen), hh * HEAD_PAD:(hh + 1) * HEAD_PAD]

    def bounded(klen):
        for hh in range(MLA_HPS):
            p = jnp.exp2(scores(hh, 0, klen)).astype(BF16)
            acc_ref[hh] = jnp.dot(p, vblk(hh, 0, klen), preferred_element_type=F32)

    def online(n_chunks):
        for hh in range(MLA_HPS):
            s = scores(hh, 0, ctx_len)
            m = jnp.max(s, axis=-1, keepdims=True)
            m_ref[hh] = jnp.broadcast_to(m, m_ref.shape[1:])
            acc_ref[hh] = jnp.dot(jnp.exp2(s - m).astype(BF16), vblk(hh, 0, ctx_len),
                                  preferred_element_type=F32)

        def body(c, carry):
            k0 = pl.multiple_of(ctx_len + c * ck, math.gcd(ctx_len, ck))
            for hh in range(MLA_HPS):
                s = scores(hh, k0, ck)
                m_old = m_ref[hh]
                m_new = jnp.maximum(m_old, jnp.max(s, axis=-1, keepdims=True))
                p = jnp.exp2(s - jnp.tile(m_new, (1, ck // LANES))).astype(BF16)
                acc_ref[hh] = (jnp.exp2(m_old - m_new) * acc_ref[hh]
                               + jnp.dot(p, vblk(hh, k0, ck), preferred_element_type=F32))
                m_ref[hh] = m_new
            return carry

        lax.fori_loop(0, n_chunks, body, 0)

    is_ctx = t == 0
    fast = bounded_ref[0] != 0

    @pl.when(fast & is_ctx)
    def _():
        bounded(ctx_len)

    @pl.when(fast & jnp.logical_not(is_ctx))
    def _():
        bounded(n)

    @pl.when(jnp.logical_not(fast) & is_ctx)
    def _():
        online(0)

    @pl.when(jnp.logical_not(fast) & jnp.logical_not(is_ctx))
    def _():
        online((n - ctx_len) // ck)

    lane = lax.broadcasted_iota(jnp.int32, (o_ref.shape[0], LANES), 1)
    for hp in range(MLA_HPS // 2):
        outs = []
        for hh in (2 * hp, 2 * hp + 1):
            acc = acc_ref[hh]
            outs.append(acc * (1.0 / acc[:, A_V:A_V + 1]))
        o_ref[:, hp * LANES:(hp + 1) * LANES] = jnp.where(
            lane < A_V, outs[0], pltpu.roll(outs[1], A_V, axis=1)).astype(BF16)


def _mla_attn(bounded, q, k, v, ctx_len):
    b, n, _ = q.shape
    hp = A_HEADS // MLA_HPS
    assert (n - ctx_len) % MLA_KCHUNK == 0
    return pl.pallas_call(
        functools.partial(_mla_attn_kernel, ctx_len=ctx_len),
        grid_spec=pltpu.PrefetchScalarGridSpec(
            num_scalar_prefetch=1,
            grid=(b, hp, n // TM),
            in_specs=[
                pl.BlockSpec((None, TM, MLA_HPS * HEAD_PAD), lambda i, p, t, f: (i, t, p)),
                pl.BlockSpec((None, n, MLA_HPS * HEAD_PAD), lambda i, p, t, f: (i, 0, p)),
                pl.BlockSpec((None, n, MLA_HPS * HEAD_PAD), lambda i, p, t, f: (i, 0, p)),
            ],
            out_specs=pl.BlockSpec((None, TM, MLA_HPS * A_V), lambda i, p, t, f: (i, t, p)),
            scratch_shapes=[
                pltpu.VMEM((MLA_HPS, TM, LANES), F32),
                pltpu.VMEM((MLA_HPS, TM, HEAD_PAD), F32),
            ],
        ),
        out_shape=jax.ShapeDtypeStruct((b, n, A_HEADS * A_V), BF16),
        compiler_params=_params(("arbitrary", "arbitrary", "arbitrary")),
        name="mla_attn",
    )(bounded, q, k, v)


def _dft_kernel(cn_ref, sn_ref, cc_ref, sc_ref, fcs_ref, y_ref, *, ctx_len):
    t = pl.program_id(1)

    @pl.when(t == 0)
    def _():
        fc = fcs_ref[0:ctx_len, 0:F_WIDTH]
        fs = fcs_ref[0:ctx_len, F_WIDTH:2 * F_WIDTH]
        y = (jnp.dot(cc_ref[...], fc, preferred_element_type=F32)
             - jnp.dot(sc_ref[...], fs, preferred_element_type=F32))
        y_ref[...] = y.astype(BF16)

    @pl.when(t > 0)
    def _():
        n = fcs_ref.shape[0]
        fc = fcs_ref[ctx_len:n, 0:F_WIDTH]
        fs = fcs_ref[ctx_len:n, F_WIDTH:2 * F_WIDTH]
        y = (jnp.dot(cn_ref[...], fc, preferred_element_type=F32)
             - jnp.dot(sn_ref[...], fs, preferred_element_type=F32))
        y_ref[...] = y.astype(BF16)


def _dft(fcs, cn, sn, cc, sc, ctx_len):
    b, n, _ = fcs.shape
    s = n - ctx_len
    assert ctx_len == TM
    return pl.pallas_call(
        functools.partial(_dft_kernel, ctx_len=ctx_len),
        grid=(b, n // TM),
        in_specs=[
            pl.BlockSpec((TM, s), lambda i, t: (jnp.maximum(t - 1, 0), 0)),
            pl.BlockSpec((TM, s), lambda i, t: (jnp.maximum(t - 1, 0), 0)),
            pl.BlockSpec((ctx_len, ctx_len), lambda i, t: (0, 0)),
            pl.BlockSpec((ctx_len, ctx_len), lambda i, t: (0, 0)),
            pl.BlockSpec((None, n, 2 * F_WIDTH), lambda i, t: (i, 0, 0)),
        ],
        out_specs=pl.BlockSpec((None, TM, F_WIDTH), lambda i, t: (i, t, 0)),
        out_shape=jax.ShapeDtypeStruct((b, n, F_WIDTH), BF16),
        compiler_params=_params(("arbitrary", "arbitrary")),
        name="fourier_dft",
    )(cn, sn, cc, sc, fcs)


def _out_proj_kernel(*refs, widths):
    mix_refs = refs[:len(widths)]
    w_ref, h_ref, mod_ref, o_ref = refs[len(widths):]
    acc = None
    off = 0
    for m_ref, wd in zip(mix_refs, widths):
        part = jnp.dot(m_ref[...], w_ref[off:off + wd, :], preferred_element_type=F32)
        acc = part if acc is None else acc + part
        off += wd
    o_ref[...] = h_ref[...] + mod_ref[2:3, :] * acc


def _out_proj(mixes, w, h, mod):
    b, n, d = h.shape
    widths = tuple(m.shape[-1] for m in mixes)
    tile = lambda wd: pl.BlockSpec((None, TM, wd), lambda i, t: (i, t, 0))
    return pl.pallas_call(
        functools.partial(_out_proj_kernel, widths=widths),
        grid=(b, n // TM),
        in_specs=[tile(wd) for wd in widths] + [
            pl.BlockSpec(w.shape, lambda i, t: (0, 0)),
            tile(d),
            pl.BlockSpec((None, 6, d), lambda i, t: (jnp.where(t == 0, b, i), 0, 0)),
        ],
        out_specs=tile(d),
        out_shape=jax.ShapeDtypeStruct((b, n, d), F32),
        compiler_params=_params(("arbitrary", "arbitrary")),
        name="out_proj",
    )(*mixes, w, h, mod)


def _ffn_kernel(h_ref, mod_ref, g_ref, wg_ref, wu_ref, wd_ref, o_ref):
    xn = _norm_mod(h_ref, g_ref, mod_ref, 3).astype(BF16)
    gt = jnp.dot(xn, wg_ref[...], preferred_element_type=F32)
    up = jnp.dot(xn, wu_ref[...], preferred_element_type=F32)
    a = (_silu(gt) * up).astype(BF16)
    y = jnp.dot(a, wd_ref[...], preferred_element_type=F32)
    o_ref[...] = h_ref[...] + mod_ref[5:6, :] * y


def _ffn(h, mod, g, wg, wu, wd):
    b, n, d = h.shape
    tile = pl.BlockSpec((None, TM, d), lambda i, t: (i, t, 0))
    full = lambda a: pl.BlockSpec(a.shape, lambda i, t: (0,) * a.ndim)
    return pl.pallas_call(
        _ffn_kernel,
        grid=(b, n // TM),
        in_specs=[
            tile,
            pl.BlockSpec((None, 6, d), lambda i, t: (jnp.where(t == 0, b, i), 0, 0)),
            full(g), full(wg), full(wu), full(wd),
        ],
        out_specs=tile,
        out_shape=jax.ShapeDtypeStruct((b, n, d), F32),
        compiler_params=_params(("arbitrary", "arbitrary")),
        name="ffn_dense",
    )(h, mod, g, wg, wu, wd)


def _na_front_kernel(h_ref, mod_ref, g_ref, win_ref, gq_ref, gk_ref, q_ref, k_ref, v_ref):
    xn = _norm_mod(h_ref, g_ref, mod_ref, 0)
    px = jnp.dot(xn.astype(BF16), win_ref[...], preferred_element_type=F32)
    width = C_HEADS * C_DH
    lane = lax.broadcasted_iota(jnp.int32, (px.shape[0], LANES), 1)
    low = lane < C_DH
    scale = C_DH ** -0.5 * LOG2E

    def norm_pairs(base, g_ref_, out_ref, mult):
        gg = g_ref_[...]
        for j in range(width // LANES):
            x = px[:, base + j * LANES:base + (j + 1) * LANES]
            x2 = x * x
            s_all = jnp.sum(x2, axis=-1, keepdims=True)
            s_lo = jnp.sum(jnp.where(low, x2, 0.0), axis=-1, keepdims=True)
            ss = jnp.where(low, s_lo, s_all - s_lo)
            inv = lax.rsqrt(ss * (1.0 / C_DH) + EPS)
            if mult != 1.0:
                inv = inv * mult
            out_ref[:, j * LANES:(j + 1) * LANES] = (x * inv * gg).astype(BF16)

    norm_pairs(0, gq_ref, q_ref, scale)
    norm_pairs(width, gk_ref, k_ref, 1.0)
    v_ref[...] = px[:, 2 * width:].astype(BF16)


def _na_front(h, mod, g, win, gq, gk):
    b, n, d = h.shape
    width = C_HEADS * C_DH
    tile = lambda w: pl.BlockSpec((None, TM, w), lambda i, t: (i, t, 0))
    full = lambda a: pl.BlockSpec(a.shape, lambda i, t: (0,) * a.ndim)
    return pl.pallas_call(
        _na_front_kernel,
        grid=(b, n // TM),
        in_specs=[
            tile(d),
            pl.BlockSpec((None, 6, d), lambda i, t: (jnp.where(t == 0, b, i), 0, 0)),
            full(g), full(win), full(gq), full(gk),
        ],
        out_specs=[tile(width)] * 3,
        out_shape=[jax.ShapeDtypeStruct((b, n, width), BF16)] * 3,
        compiler_params=_params(("arbitrary", "arbitrary")),
        name="na_front",
    )(h, mod, g, win, gq, gk)


def _na_attn_kernel(bounded_ref, q_ref, k_ref, v_ref, bias_ref, shift_ref, o_ref, *, ctx_len, rows_n):
    qlen = NA_GROUP * GRID_W
    wlen = NA_WINDOW * GRID_W
    n_groups = rows_n // NA_GROUP
    lane_q = lax.broadcasted_iota(jnp.int32, (qlen, LANES), 1)
    kc = k_ref[0:ctx_len, :]
    vc = v_ref[0:ctx_len, :]

    qc = q_ref[0:ctx_len, :]
    lane_c = lax.broadcasted_iota(jnp.int32, (ctx_len, LANES), 1)
    outs = []
    for hh in range(2):
        sel = (lane_c < C_DH) if hh == 0 else (lane_c >= C_DH)
        outs.append(_softmax_pv(jnp.where(sel, qc, jnp.zeros_like(qc)), kc, vc))
    o_ref[0:ctx_len, :] = jnp.where(lane_c < C_DH, outs[0], outs[1]).astype(BF16)

    def group_body(g, carry, bounded):
        start = jnp.clip(g * NA_GROUP - WIN_R // 2, 0, rows_n - NA_WINDOW)
        pat = jnp.where(g == 0, 0, jnp.where(g == n_groups - 1, 2, 1))
        q0 = pl.multiple_of(ctx_len + g * qlen, GRID_W)
        k0 = pl.multiple_of(ctx_len + start * GRID_W, GRID_W)
        qr = q_ref[pl.ds(q0, qlen), :]
        kw = k_ref[pl.ds(k0, wlen), :]
        vw = v_ref[pl.ds(k0, wlen), :]
        res = []
        for hh in range(2):
            sel = (lane_q < C_DH) if hh == 0 else (lane_q >= C_DH)
            qh = jnp.where(sel, qr, jnp.zeros_like(qr))
            s_loc = lax.dot_general(qh, kw, (((1,), (1,)), ((), ())),
                                    preferred_element_type=F32) + bias_ref[pat, hh]
            s_ctx = lax.dot_general(qh, kc, (((1,), (1,)), ((), ())), preferred_element_type=F32)
            if bounded:
                p_loc = jnp.exp2(s_loc)
                p_ctx = jnp.exp2(s_ctx - shift_ref[:, 0:1])
            else:
                m = jnp.maximum(jnp.max(s_loc, axis=-1, keepdims=True),
                                jnp.max(s_ctx, axis=-1, keepdims=True))
                p_loc = jnp.exp2(s_loc - m)
                p_ctx = jnp.exp2(s_ctx - m)
            l = jnp.sum(p_loc, axis=-1, keepdims=True) + jnp.sum(p_ctx, axis=-1, keepdims=True)
            o = (jnp.dot(p_loc.astype(BF16), vw, preferred_element_type=F32)
                 + jnp.dot(p_ctx.astype(BF16), vc, preferred_element_type=F32))
            res.append(o * (1.0 / l))
        o_ref[pl.ds(q0, qlen), :] = jnp.where(lane_q < C_DH, res[0], res[1]).astype(BF16)
        return carry

    @pl.when(bounded_ref[0] != 0)
    def _():
        lax.fori_loop(0, n_groups, functools.partial(group_body, bounded=True), 0, unroll=2)

    @pl.when(bounded_ref[0] == 0)
    def _():
        lax.fori_loop(0, n_groups, functools.partial(group_body, bounded=False), 0)


def _na_attn(bounded, q, k, v, bias, shift, ctx_len):
    b, n, width = q.shape
    rows_n = (n - ctx_len) // GRID_W
    hp = C_HEADS // 2
    blk = pl.BlockSpec((None, n, LANES), lambda i, p, f: (i, 0, p))
    return pl.pallas_call(
        functools.partial(_na_attn_kernel, ctx_len=ctx_len, rows_n=rows_n),
        grid_spec=pltpu.PrefetchScalarGridSpec(
            num_scalar_prefetch=1,
            grid=(b, hp),
            in_specs=[blk, blk, blk,
                      pl.BlockSpec((3, 2) + bias.shape[2:], lambda i, p, f: (0, p, 0, 0)),
                      pl.BlockSpec(shift.shape, lambda i, p, f: (0, 0))],
            out_specs=blk,
        ),
        out_shape=jax.ShapeDtypeStruct((b, n, width), BF16),
        compiler_params=_params(("arbitrary", "arbitrary")),
        name="na_attn",
    )(bounded, q, k, v, bias, shift)


def _router_kernel(h_ref, mod_ref, g_ref, wr_ref, xn_ref, route_ref):
    xn = _norm_mod(h_ref, g_ref, mod_ref, 3)
    xn_ref[...] = xn
    logits = jnp.dot(xn, wr_ref[...], precision=lax.Precision.HIGHEST, preferred_element_type=F32)
    lane = lax.broadcasted_iota(jnp.int32, logits.shape, 1).astype(F32)
    lg = jnp.where(lane < N_EXPERTS, logits, -jnp.inf)
    m1 = jnp.max(lg, axis=-1, keepdims=True)
    i1 = jnp.min(jnp.where(lg == m1, lane, float(LANES)), axis=-1, keepdims=True)
    lg2 = jnp.where(lane == i1, -jnp.inf, lg)
    m2 = jnp.max(lg2, axis=-1, keepdims=True)
    i2 = jnp.min(jnp.where(lg2 == m2, lane, float(LANES)), axis=-1, keepdims=True)
    e2 = jnp.exp(m2 - m1)
    g1 = 1.0 / (1.0 + e2)
    g2 = e2 * g1
    first_low = i1 < i2
    vals = (jnp.minimum(i1, i2), jnp.maximum(i1, i2), jnp.where(first_low, g1, g2), jnp.where(first_low, g2, g1))
    route = jnp.zeros_like(lane)
    for idx, val in enumerate(vals):
        route = jnp.where(lane == float(idx), val, route)
    route_ref[...] = route


def _router(h, mod, g, wr):
    b, n, d = h.shape
    tile = lambda w, dt=None: pl.BlockSpec((None, TM, w), lambda i, t: (i, t, 0))
    full = lambda a: pl.BlockSpec(a.shape, lambda i, t: (0,) * a.ndim)
    return pl.pallas_call(
        _router_kernel,
        grid=(b, n // TM),
        in_specs=[
            tile(d),
            pl.BlockSpec((None, 6, d), lambda i, t: (jnp.where(t == 0, b, i), 0, 0)),
            full(g), full(wr),
        ],
        out_specs=[tile(d), tile(LANES)],
        out_shape=[jax.ShapeDtypeStruct((b, n, d), F32), jax.ShapeDtypeStruct((b, n, LANES), F32)],
        compiler_params=_params(("arbitrary", "arbitrary")),
        name="moe_router",
    )(h, mod, g, wr)


def _moe_plan(route, n_tok):
    n_pairs = len(_PAIR_LO)
    n_tiles = n_tok // MOE_TM + n_pairs
    rows = n_tiles * MOE_TM
    lo = route[:, 0].astype(jnp.int32)
    hi = route[:, 1].astype(jnp.int32)
    pid = (lo * (2 * N_EXPERTS - 1 - lo)) // 2 + (hi - lo - 1)
    onehot = (pid[:, None] == jnp.arange(n_pairs, dtype=jnp.int32)[None, :]).astype(jnp.int32)
    csum = jnp.cumsum(onehot, axis=0)
    rank = jnp.sum(csum * onehot, axis=1) - 1
    counts = csum[-1]
    padded = ((counts + MOE_TM - 1) // MOE_TM) * MOE_TM
    gend = jnp.cumsum(padded)
    gstart = gend - padded
    dest = jnp.sum(gstart[None, :] * onehot, axis=1) + rank
    src = jnp.full((rows,), -1, jnp.int32).at[dest].set(jnp.arange(n_tok, dtype=jnp.int32))
    valid = src >= 0
    src_tok = jnp.where(valid, src, 0)
    n_valid = jnp.sum(valid.reshape(n_tiles, MOE_TM).astype(jnp.int32), axis=1)
    gates = jnp.where(valid[:, None], route[src_tok, 2:4], 0.0)
    gates = jnp.concatenate([gates, jnp.zeros((rows, LANES - 2), F32)], axis=1)
    n_used = gend[-1] // MOE_TM
    tile_row = jnp.minimum(jnp.arange(n_tiles, dtype=jnp.int32), n_used - 1) * MOE_TM
    group = jnp.minimum(jnp.sum((gend[None, :] <= tile_row[:, None]).astype(jnp.int32), axis=1), n_pairs - 1)
    ea = jnp.asarray(_PAIR_LO, jnp.int32)[group]
    eb = jnp.asarray(_PAIR_HI, jnp.int32)[group]
    return ea, eb, src_tok, n_valid, n_used.astype(jnp.int32).reshape(1), gates


def _moe_kernel(ea_ref, eb_ref, src_ref, nvalid_ref, nused_ref,
                x_hbm, gates_ref, wga_ref, wua_ref, wda_ref, wgb_ref, wub_ref, wdb_ref,
                y_hbm, xbuf, ybuf, gsem, ssem):
    i = pl.program_id(0)
    n_steps = pl.num_programs(0)
    n_used = nused_ref[0]
    slot = i % 2

    def gather_copy(tile, slot_, r):
        return pltpu.make_async_copy(x_hbm.at[pl.ds(src_ref[tile * MOE_TM + r], 1)],
                                     xbuf.at[slot_, pl.ds(r, 1)], gsem.at[slot_])

    def scatter_copy(tile, slot_, r):
        return pltpu.make_async_copy(ybuf.at[slot_, pl.ds(r, 1)],
                                     y_hbm.at[pl.ds(src_ref[tile * MOE_TM + r], 1)], ssem.at[slot_])

    def start_gather(tile, slot_):
        for r in range(MOE_TM):
            gather_copy(tile, slot_, r).start()

    def wait_gather(slot_):
        pltpu.make_async_copy(x_hbm.at[pl.ds(0, MOE_TM)], xbuf.at[slot_], gsem.at[slot_]).wait()

    def start_scatter(tile, slot_):
        full = nvalid_ref[tile] == MOE_TM

        @pl.when(full)
        def _():
            for r in range(MOE_TM):
                scatter_copy(tile, slot_, r).start()

        @pl.when(jnp.logical_not(full))
        def _():
            def body(r, carry):
                scatter_copy(tile, slot_, r).start()
                return carry
            lax.fori_loop(0, nvalid_ref[tile], body, 0)

    def wait_scatter(tile, slot_):
        full = nvalid_ref[tile] == MOE_TM

        @pl.when(full)
        def _():
            pltpu.make_async_copy(ybuf.at[slot_], y_hbm.at[pl.ds(0, MOE_TM)], ssem.at[slot_]).wait()

        @pl.when(jnp.logical_not(full))
        def _():
            def body(r, carry):
                scatter_copy(tile, slot_, r).wait()
                return carry
            lax.fori_loop(0, nvalid_ref[tile], body, 0)

    @pl.when((i >= 2) & (i - 2 < n_used))
    def _():
        wait_scatter(i - 2, slot)

    @pl.when(i < n_used)
    def _():
        @pl.when(i == 0)
        def _():
            start_gather(0, 0)

        @pl.when(i + 1 < n_used)
        def _():
            start_gather(i + 1, 1 - slot)

        wait_gather(slot)
        x = xbuf[slot].astype(BF16)

        def expert(wg_ref, wu_ref, wd_ref):
            gt = jnp.dot(x, wg_ref[...], preferred_element_type=F32)
            up = jnp.dot(x, wu_ref[...], preferred_element_type=F32)
            a = (_silu(gt) * up).astype(BF16)
            return jnp.dot(a, wd_ref[...], preferred_element_type=F32)

        gates = gates_ref[...]
        ybuf[slot] = (gates[:, 0:1] * expert(wga_ref, wua_ref, wda_ref)
                      + gates[:, 1:2] * expert(wgb_ref, wub_ref, wdb_ref))
        start_scatter(i, slot)

    @pl.when(i == n_steps - 1)
    def _():
        for back in (2, 1):
            @pl.when(n_steps - back < n_used)
            def _():
                wait_scatter(n_steps - back, (n_steps - back) % 2)


def _moe(xn, route, wg, wu, wd):
    n_tok, d = xn.shape
    _, _, f = wg.shape
    ea, eb, src_tok, n_valid, n_used, gates = _moe_plan(route, n_tok)
    n_tiles = ea.shape[0]
    wspec = lambda shape, which: pl.BlockSpec(
        (None,) + shape, lambda i, ea_, eb_, s_, d_, u_: ((ea_, eb_)[which][i], 0, 0))
    return pl.pallas_call(
        _moe_kernel,
        grid_spec=pltpu.PrefetchScalarGridSpec(
            num_scalar_prefetch=5,
            grid=(n_tiles,),
            in_specs=[
                pl.BlockSpec(memory_space=pl.ANY),
                pl.BlockSpec((MOE_TM, LANES), lambda i, *_: (i, 0)),
                wspec((d, f), 0), wspec((d, f), 0), wspec((f, d), 0),
                wspec((d, f), 1), wspec((d, f), 1), wspec((f, d), 1),
            ],
            out_specs=pl.BlockSpec(memory_space=pl.ANY),
            scratch_shapes=[
                pltpu.VMEM((2, MOE_TM, d), F32),
                pltpu.VMEM((2, MOE_TM, d), F32),
                pltpu.SemaphoreType.DMA((2,)),
                pltpu.SemaphoreType.DMA((2,)),
            ],
        ),
        out_shape=jax.ShapeDtypeStruct((n_tok, d), F32),
        compiler_params=_params(("arbitrary",)),
        name="moe_experts",
    )(ea, eb, src_tok, n_valid, n_used, xn, gates, wg, wu, wd, wg, wu, wd)


def _resid_kernel(h_ref, y_ref, mod_ref, o_ref):
    o_ref[...] = h_ref[...] + mod_ref[5:6, :] * y_ref[...]


def _resid(h, y, mod, latent_only):
    b, n, d = h.shape
    skip = 1 if latent_only else 0
    nt_all = n // TM
    nt = nt_all - skip
    if latent_only:
        mod_spec = pl.BlockSpec((None, 6, d), lambda i, t: (i, 0, 0))
    else:
        mod_spec = pl.BlockSpec((None, 6, d), lambda i, t: (jnp.where(t == 0, b, i), 0, 0))
    return pl.pallas_call(
        _resid_kernel,
        grid=(b, nt),
        in_specs=[
            pl.BlockSpec((None, TM, d), lambda i, t: (i, t + skip, 0)),
            pl.BlockSpec((TM, d), lambda i, t: (i * nt_all + t + skip, 0)),
            mod_spec,
        ],
        out_specs=pl.BlockSpec((None, TM, d), lambda i, t: (i, t, 0)),
        out_shape=jax.ShapeDtypeStruct((b, nt * TM, d), F32),
        compiler_params=_params(("arbitrary", "arbitrary")),
        name="moe_residual",
    )(h, y, mod)


def _rope_tables(s, ctx_len):
    t = jnp.arange(s)
    rows = (t // GRID_W).astype(F32)
    cols = (t % GRID_W).astype(F32)
    n_pairs = A_ROPE // 2
    per_axis = n_pairs // 2
    inv = ROPE_THETA ** (-jnp.arange(per_axis, dtype=F32) / per_axis)
    ang = jnp.concatenate([rows[:, None] * inv, cols[:, None] * inv], axis=-1)
    cos = jnp.repeat(jnp.cos(ang), 2, axis=-1)
    sin = jnp.repeat(jnp.sin(ang), 2, axis=-1)
    sign = jnp.tile(jnp.array([-1.0, 1.0], F32), n_pairs)
    pad = HEAD_PAD - A_QK
    t1 = jnp.concatenate([jnp.ones((s, A_NOPE), F32), cos, jnp.zeros((s, pad), F32)], axis=-1)
    t2 = jnp.concatenate([jnp.zeros((s, A_NOPE), F32), sin * sign, jnp.zeros((s, pad), F32)], axis=-1)
    c1 = jnp.concatenate([jnp.ones((ctx_len, A_QK), F32), jnp.zeros((ctx_len, pad), F32)], axis=-1)
    c2 = jnp.zeros((ctx_len, HEAD_PAD), F32)
    return jnp.concatenate([c1, t1], axis=0), jnp.concatenate([c2, t2], axis=0)


_PAIR_SWAP = np.arange(A_ROPE) ^ 1


def _rope_lane_vec(g_tail, swapped):
    gt = g_tail[_PAIR_SWAP] if swapped else g_tail
    return jnp.concatenate([jnp.zeros((A_NOPE,), F32), gt, jnp.zeros((HEAD_PAD - A_QK,), F32)])[None]


def _mla_weights(w_in, w_uq, w_ukv, g_q, g_k):
    d = w_in.shape[0]
    q_end = A_Q_RANK
    kv_end = q_end + A_KV_RANK
    r_end = kv_end + A_ROPE
    z = lambda n: jnp.zeros((d, n), w_in.dtype)
    kr = w_in[:, kv_end:r_end]
    pad = HEAD_PAD - A_QK
    win = jnp.concatenate([w_in[:, :kv_end], z(A_NOPE), kr, z(pad), z(A_NOPE), kr[:, _PAIR_SWAP], z(pad),
                           w_in[:, r_end:]], axis=-1).astype(BF16)
    wq = w_uq.reshape(A_Q_RANK, A_HEADS, A_QK)
    zq = lambda n: jnp.zeros((A_Q_RANK, A_HEADS, n), wq.dtype)
    wq_main = jnp.concatenate([wq, zq(pad)], axis=-1)
    wq_swap = jnp.concatenate([zq(A_NOPE), wq[:, :, A_NOPE:][:, :, _PAIR_SWAP], zq(pad)], axis=-1)
    wq_ext = jnp.concatenate([wq_main.reshape(A_Q_RANK, -1), wq_swap.reshape(A_Q_RANK, -1)], axis=-1).astype(BF16)
    wkv = w_ukv.reshape(A_KV_RANK, A_HEADS, A_NOPE + A_V)
    zkv = jnp.zeros((A_KV_RANK, A_HEADS, HEAD_PAD - A_NOPE), wkv.dtype)
    wk = jnp.concatenate([wkv[:, :, :A_NOPE], zkv], axis=-1)
    wv = jnp.concatenate([wkv[:, :, A_NOPE:], zkv], axis=-1)
    wkv_ext = jnp.concatenate([wk.reshape(A_KV_RANK, -1), wv.reshape(A_KV_RANK, -1)], axis=-1).astype(BF16)

    bound = math.sqrt(A_QK) * jnp.max(jnp.abs(g_q)) * jnp.max(jnp.abs(g_k))
    bounded = bound <= MLA_SAFE_BOUND
    shift = jnp.where(bounded, -bound * LOG2E, 0.0)
    unit = lambda lane_idx: jnp.zeros((1, HEAD_PAD), F32).at[0, lane_idx].set(1.0)
    gq = jnp.concatenate([jnp.concatenate([g_q, jnp.zeros((pad,), F32)])[None], _rope_lane_vec(g_q[A_NOPE:], True),
                          unit(A_QK) * shift], axis=0)
    gk = jnp.concatenate([jnp.concatenate([g_k[:A_NOPE], jnp.zeros((HEAD_PAD - A_NOPE,), F32)])[None],
                          _rope_lane_vec(g_k[A_NOPE:], False), _rope_lane_vec(g_k[A_NOPE:], True),
                          unit(A_QK), unit(A_V)], axis=0)
    return win, wq_ext, wkv_ext, gq, gk, bounded.astype(jnp.int32).reshape(1)


def _dft_mats(n, norm):
    k = jnp.arange(n, dtype=jnp.int32)
    w = GRID_W if n % GRID_W == 0 and n > GRID_W else 1
    hi = jnp.arange(n // w, dtype=jnp.int32) * w
    lo = jnp.arange(w, dtype=jnp.int32)
    ang_hi = ((k[:, None] * hi[None, :]) % n).astype(F32) * (2.0 * math.pi / n)
    ang_lo = ((k[:, None] * lo[None, :]) % n).astype(F32) * (2.0 * math.pi / n)
    ch, sh = jnp.cos(ang_hi)[:, :, None], jnp.sin(ang_hi)[:, :, None]
    cl, sl = jnp.cos(ang_lo)[:, None, :], jnp.sin(ang_lo)[:, None, :]
    cos = ((ch * cl - sh * sl) * norm).astype(BF16).reshape(n, n)
    sin = ((sh * cl + ch * sl) * norm).astype(BF16).reshape(n, n)
    return cos, sin


def _channel_dft():
    c = np.arange(F_CH)
    ang = 2.0 * np.pi * ((c[:, None] * c[None, :]) % F_CH) / F_CH
    eye = np.eye(F_GROUPS)
    cb = np.kron(eye, np.cos(ang)) / math.sqrt(F_CH)
    sb = np.kron(eye, np.sin(ang)) / math.sqrt(F_CH)
    return jnp.asarray(np.concatenate([cb, sb], axis=1), BF16)


def _na_group_rows(rows_n):
    assert rows_n % NA_GROUP == 0 and rows_n >= NA_WINDOW + NA_GROUP
    n_groups = rows_n // NA_GROUP
    n_dr = 2 * WIN_R - 1
    idx = np.full((3, NA_GROUP, NA_WINDOW), n_dr, np.int32)
    seen = {}
    for g in range(n_groups):
        gs = int(np.clip(g * NA_GROUP - WIN_R // 2, 0, rows_n - NA_WINDOW))
        pat = 0 if g == 0 else (2 if g == n_groups - 1 else 1)
        cur = np.full((NA_GROUP, NA_WINDOW), n_dr, np.int32)
        for qi in range(NA_GROUP):
            r = g * NA_GROUP + qi
            start = int(np.clip(r - WIN_R // 2, 0, rows_n - WIN_R))
            assert gs <= start and start + WIN_R <= gs + NA_WINDOW
            for a in range(start, start + WIN_R):
                cur[qi, a - gs] = a - r + (WIN_R - 1)
        assert pat not in seen or np.array_equal(seen[pat], cur)
        seen[pat] = cur
        idx[pat] = cur
    return idx


def _na_score_bound(g_q, g_k, rpb):
    qk = math.sqrt(C_DH) * jnp.max(jnp.abs(g_q)) * jnp.max(jnp.abs(g_k))
    bound = qk + jnp.maximum(jnp.max(rpb), 0.0)
    lowest = -qk + jnp.minimum(jnp.min(rpb), 0.0)
    bounded = (bound - lowest) <= 2.0 * MLA_SAFE_BOUND
    shift = jnp.where(bounded, bound, 0.0).astype(F32)
    return bounded.astype(jnp.int32).reshape(1), shift


def _na_bias(rpb, rows_n, shift):
    h, n_dr, n_dc = rpb.shape
    qc = np.arange(GRID_W)
    cs = np.clip(qc - WIN_C // 2, 0, GRID_W - WIN_C)
    kcol = np.arange(GRID_W)
    valid = (kcol[None, :] >= cs[:, None]) & (kcol[None, :] < cs[:, None] + WIN_C)
    dc = kcol[None, :] - qc[:, None] + (WIN_C - 1)
    onehot = (valid[:, :, None] & (dc[:, :, None] == np.arange(n_dc)[None, None, :])).astype(np.float32)
    col = jnp.einsum('hdc,qkc->hdqk', rpb.astype(F32), jnp.asarray(onehot), precision=lax.Precision.HIGHEST)
    col = jnp.where(jnp.asarray(valid)[None, None], (col - shift) * LOG2E, NEG_BIG)
    col = jnp.concatenate([col, jnp.full((h, 1, GRID_W, GRID_W), NEG_BIG, F32)], axis=1)
    idx = _na_group_rows(rows_n)
    blocks = [[jnp.stack([col[:, int(idx[p, qi, a])] for a in range(NA_WINDOW)], axis=2)
               for qi in range(NA_GROUP)] for p in range(3)]
    tab = jnp.stack([jnp.stack(bq, axis=1) for bq in blocks], axis=0)
    return tab.reshape(3, h, NA_GROUP * GRID_W, NA_WINDOW * GRID_W)


def kernel(x, c, ctx, c_ctx, w_mod, b_mod, norm_g, a_w_in, a_g_cq, a_g_ckv, a_w_uq, a_w_ukv, a_g_q, a_g_k, a_w_out,
           f_w_gate, f_w_up, f_w_down, c_w_in, c_g_q, c_g_k, c_rpb, c_w_out, m_w_router, m_w_gate, m_w_up, m_w_down):
    b, s, d = x.shape
    l = ctx.shape[1]
    depth = w_mod.shape[0]
    n = l + s
    assert l == TM and s % TM == 0 and s % GRID_W == 0

    mod_rows = 16
    cin = jnp.concatenate([c, c_ctx[None], jnp.zeros((mod_rows - b - 1, d), F32)], axis=0)
    mod_all = _modulation(cin, w_mod, b_mod).reshape(depth, mod_rows, 6, d)

    t1, t2 = _rope_tables(s, l)
    cn, sn = _dft_mats(s, 1.0 / math.sqrt(s))
    cc, sc = _dft_mats(l, 1.0 / math.sqrt(l))
    cb = _channel_dft()

    h = jnp.concatenate([ctx, x], axis=1)
    for i in range(depth):
        j = i // 2
        last = i == depth - 1
        mod = mod_all[i]
        g1 = norm_g[i, 0][None]
        g2 = norm_g[i, 1][None]
        if i % 2 == 0:
            win, wq_ext, wkv_ext, gq, gk, bounded = _mla_weights(a_w_in[j], a_w_uq[j], a_w_ukv[j],
                                                                 a_g_q[j], a_g_k[j])
            q, k, v, fcs = _mla_front(h, mod, g1, win, a_g_cq[j][None], a_g_ckv[j][None], wq_ext, wkv_ext,
                                      t1, t2, gq, gk, cb)
            att = _mla_attn(bounded, q, k, v, l)
            yf = _dft(fcs, cn, sn, cc, sc, l)
            h = _out_proj([att, yf], a_w_out[j].astype(BF16), h, mod)
            h = _ffn(h, mod, g2, f_w_gate[j].astype(BF16), f_w_up[j].astype(BF16), f_w_down[j].astype(BF16))
        else:
            gq2 = jnp.tile(c_g_q[j], 2)[None]
            gk2 = jnp.tile(c_g_k[j], 2)[None]
            q, k, v = _na_front(h, mod, g1, c_w_in[j].astype(BF16), gq2, gk2)
            na_bounded, na_shift = _na_score_bound(c_g_q[j], c_g_k[j], c_rpb[j])
            bias = _na_bias(c_rpb[j], s // GRID_W, na_shift)
            att = _na_attn(na_bounded, q, k, v, bias, jnp.full((1, LANES), na_shift * LOG2E, F32), l)
            h = _out_proj([att], c_w_out[j].astype(BF16), h, mod)
            wr = jnp.concatenate([m_w_router[j], jnp.zeros((d, LANES - N_EXPERTS), F32)], axis=-1)
            xn2, route = _router(h, mod, g2, wr)
            y = _moe(xn2.reshape(b * n, d), route.reshape(b * n, LANES),
                     m_w_gate[j].astype(BF16), m_w_up[j].astype(BF16), m_w_down[j].astype(BF16))
            h = _resid(h, y, mod, last)
    if h.shape[1] != s:
        h = h[:, l:]
    return h
```

```python
import functools
import math

import jax
import jax.numpy as jnp
import numpy as np
from jax import lax
from jax.experimental import pallas as pl
from jax.experimental.pallas import tpu as pltpu

F32 = jnp.float32
BF16 = jnp.bfloat16

GRID_W = 64
A_HEADS = 12
A_NOPE = 64
A_ROPE = 32
A_QK = A_NOPE + A_ROPE
A_V = 64
A_Q_RANK = 256
A_KV_RANK = 128
F_GROUPS = 4
F_CH = 64
F_WIDTH = F_GROUPS * F_CH
C_HEADS = 16
C_DH = 64
WIN_R = 8
WIN_C = 16
N_EXPERTS = 8
ROPE_THETA = 10000.0
EPS = 1e-6

LANES = 128
HEAD_PAD = 128
TM = 256
MOE_TM = 256
_PAIR_LO = tuple(lo for lo in range(N_EXPERTS) for hi in range(lo + 1, N_EXPERTS))
_PAIR_HI = tuple(hi for lo in range(N_EXPERTS) for hi in range(lo + 1, N_EXPERTS))
LOG2E = math.log2(math.e)
MLA_SAFE_BOUND = 40.0
MLA_KCHUNK = 512
MLA_HPS = 6
NA_GROUP = 4
NA_WINDOW = NA_GROUP + WIN_R
VMEM_LIMIT = 52 * 1024 * 1024
NEG_BIG = -1e30


def _params(sem, vmem=VMEM_LIMIT):
    return pltpu.CompilerParams(dimension_semantics=sem, vmem_limit_bytes=vmem)


def _rms(x, g, n=None):
    n = x.shape[-1] if n is None else n
    ss = jnp.sum(x * x, axis=-1, keepdims=True)
    return x * lax.rsqrt(ss * (1.0 / n) + EPS) * g


def _silu(x):
    return x * (1.0 / (1.0 + jnp.exp(-x)))


def _norm_mod(x, g_ref, mod_ref, row):
    xn = _rms(x, g_ref[...])
    return xn * (1.0 + mod_ref[row + 1:row + 2, :]) + mod_ref[row:row + 1, :]


def _mod_kernel(c_ref, w_ref, b_ref, o_ref):
    s = _silu(c_ref[...])
    o_ref[...] = jnp.dot(s, w_ref[...], precision=lax.Precision.HIGHEST,
                         preferred_element_type=F32) + b_ref[...]


def _modulation(cin, w_mod, b_mod):
    depth, d, n = w_mod.shape
    tn = 1536
    rows = cin.shape[0]
    return pl.pallas_call(
        _mod_kernel,
        grid=(depth, n // tn),
        in_specs=[
            pl.BlockSpec((rows, d), lambda l, j: (0, 0)),
            pl.BlockSpec((None, d, tn), lambda l, j: (l, 0, j)),
            pl.BlockSpec((None, 1, tn), lambda l, j: (l, 0, j)),
        ],
        out_specs=pl.BlockSpec((None, rows, tn), lambda l, j: (l, 0, j)),
        out_shape=jax.ShapeDtypeStruct((depth, rows, n), F32),
        compiler_params=_params(("arbitrary", "arbitrary")),
        name="modulation",
    )(cin, w_mod, b_mod.reshape(depth, 1, n))


def _mla_front_kernel(h_ref, mod_ref, g_ref, win_ref, gcq_ref, gckv_ref, wq_ref, wkv_ref,
                      t1_ref, t2_ref, gq_ref, gk_ref, cb_ref,
                      q_ref, k_ref, v_ref, fcs_ref):
    xn = _norm_mod(h_ref[...], g_ref, mod_ref, 0)
    px = jnp.dot(xn.astype(BF16), win_ref[...], preferred_element_type=F32)
    t1 = t1_ref[...]
    t2 = t2_ref[...]
    hw = A_HEADS * HEAD_PAD

    qn = _rms(px[:, 0:A_Q_RANK], gcq_ref[...]).astype(BF16)
    qq = jnp.dot(qn, wq_ref[...], preferred_element_type=F32)
    aq = t1 * gq_ref[0:1, :]
    bq = t2 * gq_ref[1:2, :]
    q_shift = gq_ref[2:3, :]
    scale = A_QK ** -0.5 * LOG2E
    for h in range(A_HEADS):
        qm = qq[:, h * HEAD_PAD:(h + 1) * HEAD_PAD]
        qs = qq[:, hw + h * HEAD_PAD:hw + (h + 1) * HEAD_PAD]
        ss = jnp.sum(qm * qm, axis=-1, keepdims=True)
        inv = lax.rsqrt(ss * (1.0 / A_QK) + EPS) * scale
        q_ref[:, h * HEAD_PAD:(h + 1) * HEAD_PAD] = ((qm * aq + qs * bq) * inv + q_shift).astype(BF16)

    kvn = _rms(px[:, A_Q_RANK:A_Q_RANK + A_KV_RANK], gckv_ref[...]).astype(BF16)
    kv = jnp.dot(kvn, wkv_ref[...], preferred_element_type=F32)
    krm = px[:, 384:512]
    krs = px[:, 512:640]
    tail = krm * (t1 * gk_ref[1:2, :]) + krs * (t2 * gk_ref[2:3, :])
    ssr = jnp.sum(krm * krm, axis=-1, keepdims=True)
    gkn = gk_ref[0:1, :]
    k_one = gk_ref[3:4, :]
    v_one = gk_ref[4:5, :]
    for h in range(A_HEADS):
        km = kv[:, h * HEAD_PAD:(h + 1) * HEAD_PAD]
        ss = jnp.sum(km * km, axis=-1, keepdims=True) + ssr
        inv = lax.rsqrt(ss * (1.0 / A_QK) + EPS)
        k_ref[:, h * HEAD_PAD:(h + 1) * HEAD_PAD] = ((km * gkn + tail) * inv + k_one).astype(BF16)
        v_ref[:, h * HEAD_PAD:(h + 1) * HEAD_PAD] = (kv[:, hw + h * HEAD_PAD:hw + (h + 1) * HEAD_PAD]
                                                     + v_one).astype(BF16)

    f = px[:, 640:896].astype(BF16)
    fcs_ref[...] = jnp.dot(f, cb_ref[...], preferred_element_type=F32).astype(BF16)


def _mla_front(h, mod, g, win, gcq, gckv, wq, wkv, t1, t2, gq, gk, cb):
    b, n, d = h.shape
    nt = n // TM
    hw = A_HEADS * HEAD_PAD
    tile = lambda w: pl.BlockSpec((None, TM, w), lambda i, t: (i, t, 0))
    full = lambda a: pl.BlockSpec(a.shape, lambda i, t: (0,) * a.ndim)
    return pl.pallas_call(
        _mla_front_kernel,
        grid=(b, nt),
        in_specs=[
            tile(d),
            pl.BlockSpec((None, 6, d), lambda i, t: (jnp.where(t == 0, b, i), 0, 0)),
            full(g), full(win), full(gcq), full(gckv), full(wq), full(wkv),
            pl.BlockSpec((TM, LANES), lambda i, t: (t, 0)),
            pl.BlockSpec((TM, LANES), lambda i, t: (t, 0)),
            full(gq), full(gk), full(cb),
        ],
        out_specs=[tile(hw), tile(hw), tile(hw), tile(2 * F_WIDTH)],
        out_shape=[
            jax.ShapeDtypeStruct((b, n, hw), BF16),
            jax.ShapeDtypeStruct((b, n, hw), BF16),
            jax.ShapeDtypeStruct((b, n, hw), BF16),
            jax.ShapeDtypeStruct((b, n, 2 * F_WIDTH), BF16),
        ],
        compiler_params=_params(("arbitrary", "arbitrary")),
        name="mla_front",
    )(h, mod, g, win, gcq, gckv, wq, wkv, t1, t2, gq, gk, cb)


def _softmax_pv(q, k, v):
    s = lax.dot_general(q, k, (((1,), (1,)), ((), ())), preferred_element_type=F32)
    m = jnp.max(s, axis=-1, keepdims=True)
    p = jnp.exp2(s - m)
    l = jnp.sum(p, axis=-1, keepdims=True)
    o = jnp.dot(p.astype(BF16), v, preferred_element_type=F32)
    return o * (1.0 / l)


def _mla_attn_kernel(bounded_ref, q_ref, k_ref, v_ref, o_ref, m_ref, acc_ref, *, ctx_len):
    t = pl.program_id(2)
    n = k_ref.shape[0]
    ck = MLA_KCHUNK

    def scores(hh, k0, klen):
        q = q_ref[:, hh * HEAD_PAD:(hh + 1) * HEAD_PAD]
        k = k_ref[pl.ds(k0, klen), hh * HEAD_PAD:(hh + 1) * HEAD_PAD]
        return lax.dot_general(q, k, (((1,), (1,)), ((), ())), preferred_element_type=F32)

    def vblk(hh, k0, klen):
        return v_ref[pl.ds(k0, klen), hh * HEAD_PAD:(hh + 1) * HEAD_PAD]

    def bounded(klen):
        for hh in range(MLA_HPS):
            p = jnp.exp2(scores(hh, 0, klen)).astype(BF16)
            acc_ref[hh] = jnp.dot(p, vblk(hh, 0, klen), preferred_element_type=F32)

    def online(n_chunks):
        for hh in range(MLA_HPS):
            s = scores(hh, 0, ctx_len)
            m = jnp.max(s, axis=-1, keepdims=True)
            m_ref[hh] = jnp.broadcast_to(m, m_ref.shape[1:])
            acc_ref[hh] = jnp.dot(jnp.exp2(s - m).astype(BF16), vblk(hh, 0, ctx_len),
                                  preferred_element_type=F32)

        def body(c, carry):
            k0 = pl.multiple_of(ctx_len + c * ck, math.gcd(ctx_len, ck))
            for hh in range(MLA_HPS):
                s = scores(hh, k0, ck)
                m_old = m_ref[hh]
                m_new = jnp.maximum(m_old, jnp.max(s, axis=-1, keepdims=True))
                p = jnp.exp2(s - jnp.tile(m_new, (1, ck // LANES))).astype(BF16)
                acc_ref[hh] = (jnp.exp2(m_old - m_new) * acc_ref[hh]
                               + jnp.dot(p, vblk(hh, k0, ck), preferred_element_type=F32))
                m_ref[hh] = m_new
            return carry

        lax.fori_loop(0, n_chunks, body, 0)

    is_ctx = t == 0
    fast = bounded_ref[0] != 0

    @pl.when(fast & is_ctx)
    def _():
        bounded(ctx_len)

    @pl.when(fast & jnp.logical_not(is_ctx))
    def _():
        bounded(n)

    @pl.when(jnp.logical_not(fast) & is_ctx)
    def _():
        online(0)

    @pl.when(jnp.logical_not(fast) & jnp.logical_not(is_ctx))
    def _():
        online((n - ctx_len) // ck)

    lane = lax.broadcasted_iota(jnp.int32, (o_ref.shape[0], LANES), 1)
    for hp in range(MLA_HPS // 2):
        outs = []
        for hh in (2 * hp, 2 * hp + 1):
            acc = acc_ref[hh]
            outs.append(acc * (1.0 / acc[:, A_V:A_V + 1]))
        o_ref[:, hp * LANES:(hp + 1) * LANES] = jnp.where(
            lane < A_V, outs[0], pltpu.roll(outs[1], A_V, axis=1)).astype(BF16)


def _mla_attn(bounded, q, k, v, ctx_len):
    b, n, _ = q.shape
    hp = A_HEADS // MLA_HPS
    assert (n - ctx_len) % MLA_KCHUNK == 0
    return pl.pallas_call(
        functools.partial(_mla_attn_kernel, ctx_len=ctx_len),
        grid_spec=pltpu.PrefetchScalarGridSpec(
            num_scalar_prefetch=1,
            grid=(b, hp, n // TM),
            in_specs=[
                pl.BlockSpec((None, TM, MLA_HPS * HEAD_PAD), lambda i, p, t, f: (i, t, p)),
                pl.BlockSpec((None, n, MLA_HPS * HEAD_PAD), lambda i, p, t, f: (i, 0, p)),
                pl.BlockSpec((None, n, MLA_HPS * HEAD_PAD), lambda i, p, t, f: (i, 0, p)),
            ],
            out_specs=pl.BlockSpec((None, TM, MLA_HPS * A_V), lambda i, p, t, f: (i, t, p)),
            scratch_shapes=[
                pltpu.VMEM((MLA_HPS, TM, LANES), F32),
                pltpu.VMEM((MLA_HPS, TM, HEAD_PAD), F32),
            ],
        ),
        out_shape=jax.ShapeDtypeStruct((b, n, A_HEADS * A_V), BF16),
        compiler_params=_params(("arbitrary", "arbitrary", "arbitrary")),
        name="mla_attn",
    )(bounded, q, k, v)


def _dft_kernel(cn_ref, sn_ref, cc_ref, sc_ref, fcs_ref, y_ref, *, ctx_len):
    t = pl.program_id(1)

    @pl.when(t == 0)
    def _():
        fc = fcs_ref[0:ctx_len, 0:F_WIDTH]
        fs = fcs_ref[0:ctx_len, F_WIDTH:2 * F_WIDTH]
        y = (jnp.dot(cc_ref[...], fc, preferred_element_type=F32)
             - jnp.dot(sc_ref[...], fs, preferred_element_type=F32))
        y_ref[...] = y.astype(BF16)

    @pl.when(t > 0)
    def _():
        n = fcs_ref.shape[0]
        fc = fcs_ref[ctx_len:n, 0:F_WIDTH]
        fs = fcs_ref[ctx_len:n, F_WIDTH:2 * F_WIDTH]
        y = (jnp.dot(cn_ref[...], fc, preferred_element_type=F32)
             - jnp.dot(sn_ref[...], fs, preferred_element_type=F32))
        y_ref[...] = y.astype(BF16)


def _dft(fcs, cn, sn, cc, sc, ctx_len):
    b, n, _ = fcs.shape
    s = n - ctx_len
    assert ctx_len == TM
    return pl.pallas_call(
        functools.partial(_dft_kernel, ctx_len=ctx_len),
        grid=(b, n // TM),
        in_specs=[
            pl.BlockSpec((TM, s), lambda i, t: (jnp.maximum(t - 1, 0), 0)),
            pl.BlockSpec((TM, s), lambda i, t: (jnp.maximum(t - 1, 0), 0)),
            pl.BlockSpec((ctx_len, ctx_len), lambda i, t: (0, 0)),
            pl.BlockSpec((ctx_len, ctx_len), lambda i, t: (0, 0)),
            pl.BlockSpec((None, n, 2 * F_WIDTH), lambda i, t: (i, 0, 0)),
        ],
        out_specs=pl.BlockSpec((None, TM, F_WIDTH), lambda i, t: (i, t, 0)),
        out_shape=jax.ShapeDtypeStruct((b, n, F_WIDTH), BF16),
        compiler_params=_params(("arbitrary", "arbitrary")),
        name="fourier_dft",
    )(cn, sn, cc, sc, fcs)


def _mixed_residual(mix_refs, widths, w_ref, h_ref, mod_ref):
    acc = None
    off = 0
    for m_ref, wd in zip(mix_refs, widths):
        part = jnp.dot(m_ref[...], w_ref[off:off + wd, :], preferred_element_type=F32)
        acc = part if acc is None else acc + part
        off += wd
    return h_ref[...] + mod_ref[2:3, :] * acc


def _mix_specs(mixes, w, h, b):
    d = h.shape[-1]
    tile = lambda wd: pl.BlockSpec((None, TM, wd), lambda i, t: (i, t, 0))
    return [tile(m.shape[-1]) for m in mixes] + [
        pl.BlockSpec(w.shape, lambda i, t: (0, 0)),
        tile(d),
        pl.BlockSpec((None, 6, d), lambda i, t: (jnp.where(t == 0, b, i), 0, 0)),
    ]


def _proj_ffn_kernel(*refs, widths):
    nm = len(widths)
    w_ref, h_ref, mod_ref, g_ref, wg_ref, wu_ref, wd_ref, o_ref = refs[nm:]
    h1 = _mixed_residual(refs[:nm], widths, w_ref, h_ref, mod_ref)
    xn = _norm_mod(h1, g_ref, mod_ref, 3).astype(BF16)
    gt = jnp.dot(xn, wg_ref[...], preferred_element_type=F32)
    up = jnp.dot(xn, wu_ref[...], preferred_element_type=F32)
    a = (_silu(gt) * up).astype(BF16)
    y = jnp.dot(a, wd_ref[...], preferred_element_type=F32)
    o_ref[...] = h1 + mod_ref[5:6, :] * y


def _proj_ffn(mixes, w_out, h, mod, g, wg, wu, wd):
    b, n, d = h.shape
    widths = tuple(m.shape[-1] for m in mixes)
    full = lambda a: pl.BlockSpec(a.shape, lambda i, t: (0,) * a.ndim)
    return pl.pallas_call(
        functools.partial(_proj_ffn_kernel, widths=widths),
        grid=(b, n // TM),
        in_specs=_mix_specs(mixes, w_out, h, b) + [full(g), full(wg), full(wu), full(wd)],
        out_specs=pl.BlockSpec((None, TM, d), lambda i, t: (i, t, 0)),
        out_shape=jax.ShapeDtypeStruct((b, n, d), F32),
        compiler_params=_params(("arbitrary", "arbitrary")),
        name="proj_ffn",
    )(*mixes, w_out, h, mod, g, wg, wu, wd)


def _na_front_kernel(h_ref, mod_ref, g_ref, win_ref, gq_ref, gk_ref, q_ref, k_ref, v_ref):
    xn = _norm_mod(h_ref[...], g_ref, mod_ref, 0)
    px = jnp.dot(xn.astype(BF16), win_ref[...], preferred_element_type=F32)
    width = C_HEADS * C_DH
    lane = lax.broadcasted_iota(jnp.int32, (px.shape[0], LANES), 1)
    low = lane < C_DH
    scale = C_DH ** -0.5 * LOG2E

    def norm_pairs(base, g_ref_, out_ref, mult):
        gg = g_ref_[...]
        for j in range(width // LANES):
            x = px[:, base + j * LANES:base + (j + 1) * LANES]
            x2 = x * x
            s_all = jnp.sum(x2, axis=-1, keepdims=True)
            s_lo = jnp.sum(jnp.where(low, x2, 0.0), axis=-1, keepdims=True)
            ss = jnp.where(low, s_lo, s_all - s_lo)
            inv = lax.rsqrt(ss * (1.0 / C_DH) + EPS)
            if mult != 1.0:
                inv = inv * mult
            out_ref[:, j * LANES:(j + 1) * LANES] = (x * inv * gg).astype(BF16)

    norm_pairs(0, gq_ref, q_ref, scale)
    norm_pairs(width, gk_ref, k_ref, 1.0)
    v_ref[...] = px[:, 2 * width:].astype(BF16)


def _na_front(h, mod, g, win, gq, gk):
    b, n, d = h.shape
    width = C_HEADS * C_DH
    tile = lambda w: pl.BlockSpec((None, TM, w), lambda i, t: (i, t, 0))
    full = lambda a: pl.BlockSpec(a.shape, lambda i, t: (0,) * a.ndim)
    return pl.pallas_call(
        _na_front_kernel,
        grid=(b, n // TM),
        in_specs=[
            tile(d),
            pl.BlockSpec((None, 6, d), lambda i, t: (jnp.where(t == 0, b, i), 0, 0)),
            full(g), full(win), full(gq), full(gk),
        ],
        out_specs=[tile(width)] * 3,
        out_shape=[jax.ShapeDtypeStruct((b, n, width), BF16)] * 3,
        compiler_params=_params(("arbitrary", "arbitrary")),
        name="na_front",
    )(h, mod, g, win, gq, gk)


def _na_attn_kernel(bounded_ref, q_ref, k_ref, v_ref, bias_ref, shift_ref, o_ref, *, ctx_len, rows_n):
    qlen = NA_GROUP * GRID_W
    wlen = NA_WINDOW * GRID_W
    n_groups = rows_n // NA_GROUP
    lane_q = lax.broadcasted_iota(jnp.int32, (qlen, LANES), 1)
    kc = k_ref[0:ctx_len, :]
    vc = v_ref[0:ctx_len, :]

    qc = q_ref[0:ctx_len, :]
    lane_c = lax.broadcasted_iota(jnp.int32, (ctx_len, LANES), 1)
    outs = []
    for hh in range(2):
        sel = (lane_c < C_DH) if hh == 0 else (lane_c >= C_DH)
        outs.append(_softmax_pv(jnp.where(sel, qc, jnp.zeros_like(qc)), kc, vc))
    o_ref[0:ctx_len, :] = jnp.where(lane_c < C_DH, outs[0], outs[1]).astype(BF16)

    def group_body(g, carry, bounded):
        start = jnp.clip(g * NA_GROUP - WIN_R // 2, 0, rows_n - NA_WINDOW)
        pat = jnp.where(g == 0, 0, jnp.where(g == n_groups - 1, 2, 1))
        q0 = pl.multiple_of(ctx_len + g * qlen, GRID_W)
        k0 = pl.multiple_of(ctx_len + start * GRID_W, GRID_W)
        qr = q_ref[pl.ds(q0, qlen), :]
        kw = k_ref[pl.ds(k0, wlen), :]
        vw = v_ref[pl.ds(k0, wlen), :]
        res = []
        for hh in range(2):
            sel = (lane_q < C_DH) if hh == 0 else (lane_q >= C_DH)
            qh = jnp.where(sel, qr, jnp.zeros_like(qr))
            s_loc = lax.dot_general(qh, kw, (((1,), (1,)), ((), ())),
                                    preferred_element_type=F32) + bias_ref[pat, hh]
            s_ctx = lax.dot_general(qh, kc, (((1,), (1,)), ((), ())), preferred_element_type=F32)
            if bounded:
                p_loc = jnp.exp2(s_loc)
                p_ctx = jnp.exp2(s_ctx - shift_ref[:, 0:1])
            else:
                m = jnp.maximum(jnp.max(s_loc, axis=-1, keepdims=True),
                                jnp.max(s_ctx, axis=-1, keepdims=True))
                p_loc = jnp.exp2(s_loc - m)
                p_ctx = jnp.exp2(s_ctx - m)
            l = jnp.sum(p_loc, axis=-1, keepdims=True) + jnp.sum(p_ctx, axis=-1, keepdims=True)
            o = (jnp.dot(p_loc.astype(BF16), vw, preferred_element_type=F32)
                 + jnp.dot(p_ctx.astype(BF16), vc, preferred_element_type=F32))
            res.append(o * (1.0 / l))
        o_ref[pl.ds(q0, qlen), :] = jnp.where(lane_q < C_DH, res[0], res[1]).astype(BF16)
        return carry

    @pl.when(bounded_ref[0] != 0)
    def _():
        lax.fori_loop(0, n_groups, functools.partial(group_body, bounded=True), 0, unroll=2)

    @pl.when(bounded_ref[0] == 0)
    def _():
        lax.fori_loop(0, n_groups, functools.partial(group_body, bounded=False), 0)


def _na_attn(bounded, q, k, v, bias, shift, ctx_len):
    b, n, width = q.shape
    rows_n = (n - ctx_len) // GRID_W
    hp = C_HEADS // 2
    blk = pl.BlockSpec((None, n, LANES), lambda i, p, f: (i, 0, p))
    return pl.pallas_call(
        functools.partial(_na_attn_kernel, ctx_len=ctx_len, rows_n=rows_n),
        grid_spec=pltpu.PrefetchScalarGridSpec(
            num_scalar_prefetch=1,
            grid=(b, hp),
            in_specs=[blk, blk, blk,
                      pl.BlockSpec((3, 2) + bias.shape[2:], lambda i, p, f: (0, p, 0, 0)),
                      pl.BlockSpec(shift.shape, lambda i, p, f: (0, 0))],
            out_specs=blk,
        ),
        out_shape=jax.ShapeDtypeStruct((b, n, width), BF16),
        compiler_params=_params(("arbitrary", "arbitrary")),
        name="na_attn",
    )(bounded, q, k, v, bias, shift)


def _proj_router_kernel(*refs, widths):
    nm = len(widths)
    w_ref, h_ref, mod_ref, g_ref, wr_ref, h1_ref, xn_ref, route_ref = refs[nm:]
    h1 = _mixed_residual(refs[:nm], widths, w_ref, h_ref, mod_ref)
    h1_ref[...] = h1
    xn = _norm_mod(h1, g_ref, mod_ref, 3)
    xn_ref[...] = xn
    logits = jnp.dot(xn, wr_ref[...], precision=lax.Precision.HIGHEST, preferred_element_type=F32)
    lane = lax.broadcasted_iota(jnp.int32, logits.shape, 1).astype(F32)
    lg = jnp.where(lane < N_EXPERTS, logits, -jnp.inf)
    m1 = jnp.max(lg, axis=-1, keepdims=True)
    i1 = jnp.min(jnp.where(lg == m1, lane, float(LANES)), axis=-1, keepdims=True)
    lg2 = jnp.where(lane == i1, -jnp.inf, lg)
    m2 = jnp.max(lg2, axis=-1, keepdims=True)
    i2 = jnp.min(jnp.where(lg2 == m2, lane, float(LANES)), axis=-1, keepdims=True)
    e2 = jnp.exp(m2 - m1)
    g1 = 1.0 / (1.0 + e2)
    g2 = e2 * g1
    first_low = i1 < i2
    vals = (jnp.minimum(i1, i2), jnp.maximum(i1, i2), jnp.where(first_low, g1, g2), jnp.where(first_low, g2, g1))
    route = jnp.zeros_like(lane)
    for idx, val in enumerate(vals):
        route = jnp.where(lane == float(idx), val, route)
    route_ref[...] = route


def _proj_router(mixes, w_out, h, mod, g, wr):
    b, n, d = h.shape
    widths = tuple(m.shape[-1] for m in mixes)
    tile = lambda w: pl.BlockSpec((None, TM, w), lambda i, t: (i, t, 0))
    full = lambda a: pl.BlockSpec(a.shape, lambda i, t: (0,) * a.ndim)
    return pl.pallas_call(
        functools.partial(_proj_router_kernel, widths=widths),
        grid=(b, n // TM),
        in_specs=_mix_specs(mixes, w_out, h, b) + [full(g), full(wr)],
        out_specs=[tile(d), tile(d), tile(LANES)],
        out_shape=[jax.ShapeDtypeStruct((b, n, d), F32), jax.ShapeDtypeStruct((b, n, d), F32),
                   jax.ShapeDtypeStruct((b, n, LANES), F32)],
        compiler_params=_params(("arbitrary", "arbitrary")),
        name="proj_router",
    )(*mixes, w_out, h, mod, g, wr)


def _moe_plan(route, n_tok):
    n_pairs = len(_PAIR_LO)
    n_tiles = n_tok // MOE_TM + n_pairs
    rows = n_tiles * MOE_TM
    lo = route[:, 0].astype(jnp.int32)
    hi = route[:, 1].astype(jnp.int32)
    pid = (lo * (2 * N_EXPERTS - 1 - lo)) // 2 + (hi - lo - 1)
    onehot = (pid[:, None] == jnp.arange(n_pairs, dtype=jnp.int32)[None, :]).astype(jnp.int32)
    csum = jnp.cumsum(onehot, axis=0)
    rank = jnp.sum(csum * onehot, axis=1) - 1
    counts = csum[-1]
    padded = ((counts + MOE_TM - 1) // MOE_TM) * MOE_TM
    gend = jnp.cumsum(padded)
    gstart = gend - padded
    dest = jnp.sum(gstart[None, :] * onehot, axis=1) + rank
    src = jnp.full((rows,), -1, jnp.int32).at[dest].set(jnp.arange(n_tok, dtype=jnp.int32))
    valid = src >= 0
    src_tok = jnp.where(valid, src, 0)
    n_valid = jnp.sum(valid.reshape(n_tiles, MOE_TM).astype(jnp.int32), axis=1)
    gates = jnp.where(valid[:, None], route[src_tok, 2:4], 0.0)
    gates = jnp.concatenate([gates, jnp.zeros((rows, LANES - 2), F32)], axis=1)
    n_used = gend[-1] // MOE_TM
    tile_row = jnp.minimum(jnp.arange(n_tiles, dtype=jnp.int32), n_used - 1) * MOE_TM
    group = jnp.minimum(jnp.sum((gend[None, :] <= tile_row[:, None]).astype(jnp.int32), axis=1), n_pairs - 1)
    ea = jnp.asarray(_PAIR_LO, jnp.int32)[group]
    eb = jnp.asarray(_PAIR_HI, jnp.int32)[group]
    return ea, eb, src_tok, n_valid, n_used.astype(jnp.int32).reshape(1), gates


def _moe_kernel(ea_ref, eb_ref, src_ref, nvalid_ref, nused_ref,
                x_hbm, gates_ref, wga_ref, wua_ref, wda_ref, wgb_ref, wub_ref, wdb_ref,
                y_hbm, xbuf, ybuf, gsem, ssem):
    i = pl.program_id(0)
    n_steps = pl.num_programs(0)
    n_used = nused_ref[0]
    slot = i % 2

    def gather_copy(tile, slot_, r):
        return pltpu.make_async_copy(x_hbm.at[pl.ds(src_ref[tile * MOE_TM + r], 1)],
                                     xbuf.at[slot_, pl.ds(r, 1)], gsem.at[slot_])

    def scatter_copy(tile, slot_, r):
        return pltpu.make_async_copy(ybuf.at[slot_, pl.ds(r, 1)],
                                     y_hbm.at[pl.ds(src_ref[tile * MOE_TM + r], 1)], ssem.at[slot_])

    def start_gather(tile, slot_):
        for r in range(MOE_TM):
            gather_copy(tile, slot_, r).start()

    def wait_gather(slot_):
        pltpu.make_async_copy(x_hbm.at[pl.ds(0, MOE_TM)], xbuf.at[slot_], gsem.at[slot_]).wait()

    def start_scatter(tile, slot_):
        full = nvalid_ref[tile] == MOE_TM

        @pl.when(full)
        def _():
            for r in range(MOE_TM):
                scatter_copy(tile, slot_, r).start()

        @pl.when(jnp.logical_not(full))
        def _():
            def body(r, carry):
                scatter_copy(tile, slot_, r).start()
                return carry
            lax.fori_loop(0, nvalid_ref[tile], body, 0)

    def wait_scatter(tile, slot_):
        full = nvalid_ref[tile] == MOE_TM

        @pl.when(full)
        def _():
            pltpu.make_async_copy(ybuf.at[slot_], y_hbm.at[pl.ds(0, MOE_TM)], ssem.at[slot_]).wait()

        @pl.when(jnp.logical_not(full))
        def _():
            def body(r, carry):
                scatter_copy(tile, slot_, r).wait()
                return carry
            lax.fori_loop(0, nvalid_ref[tile], body, 0)

    @pl.when((i >= 2) & (i - 2 < n_used))
    def _():
        wait_scatter(i - 2, slot)

    @pl.when(i < n_used)
    def _():
        @pl.when(i == 0)
        def _():
            start_gather(0, 0)

        @pl.when(i + 1 < n_used)
        def _():
            start_gather(i + 1, 1 - slot)

        wait_gather(slot)
        x = xbuf[slot].astype(BF16)

        def expert(wg_ref, wu_ref, wd_ref):
            gt = jnp.dot(x, wg_ref[...], preferred_element_type=F32)
            up = jnp.dot(x, wu_ref[...], preferred_element_type=F32)
            a = (_silu(gt) * up).astype(BF16)
            return jnp.dot(a, wd_ref[...], preferred_element_type=F32)

        gates = gates_ref[...]
        ybuf[slot] = (gates[:, 0:1] * expert(wga_ref, wua_ref, wda_ref)
                      + gates[:, 1:2] * expert(wgb_ref, wub_ref, wdb_ref))
        start_scatter(i, slot)

    @pl.when(i == n_steps - 1)
    def _():
        for back in (2, 1):
            @pl.when(n_steps - back < n_used)
            def _():
                wait_scatter(n_steps - back, (n_steps - back) % 2)


def _moe(xn, route, wg, wu, wd):
    n_tok, d = xn.shape
    _, _, f = wg.shape
    ea, eb, src_tok, n_valid, n_used, gates = _moe_plan(route, n_tok)
    n_tiles = ea.shape[0]
    wspec = lambda shape, which: pl.BlockSpec(
        (None,) + shape, lambda i, ea_, eb_, s_, d_, u_: ((ea_, eb_)[which][i], 0, 0))
    return pl.pallas_call(
        _moe_kernel,
        grid_spec=pltpu.PrefetchScalarGridSpec(
            num_scalar_prefetch=5,
            grid=(n_tiles,),
            in_specs=[
                pl.BlockSpec(memory_space=pl.ANY),
                pl.BlockSpec((MOE_TM, LANES), lambda i, *_: (i, 0)),
                wspec((d, f), 0), wspec((d, f), 0), wspec((f, d), 0),
                wspec((d, f), 1), wspec((d, f), 1), wspec((f, d), 1),
            ],
            out_specs=pl.BlockSpec(memory_space=pl.ANY),
            scratch_shapes=[
                pltpu.VMEM((2, MOE_TM, d), F32),
                pltpu.VMEM((2, MOE_TM, d), F32),
                pltpu.SemaphoreType.DMA((2,)),
                pltpu.SemaphoreType.DMA((2,)),
            ],
        ),
        out_shape=jax.ShapeDtypeStruct((n_tok, d), F32),
        compiler_params=_params(("arbitrary",)),
        name="moe_experts",
    )(ea, eb, src_tok, n_valid, n_used, xn, gates, wg, wu, wd, wg, wu, wd)


def _resid_kernel(h_ref, y_ref, mod_ref, o_ref):
    o_ref[...] = h_ref[...] + mod_ref[5:6, :] * y_ref[...]


def _resid(h, y, mod, latent_only):
    b, n, d = h.shape
    skip = 1 if latent_only else 0
    nt_all = n // TM
    nt = nt_all - skip
    if latent_only:
        mod_spec = pl.BlockSpec((None, 6, d), lambda i, t: (i, 0, 0))
    else:
        mod_spec = pl.BlockSpec((None, 6, d), lambda i, t: (jnp.where(t == 0, b, i), 0, 0))
    return pl.pallas_call(
        _resid_kernel,
        grid=(b, nt),
        in_specs=[
            pl.BlockSpec((None, TM, d), lambda i, t: (i, t + skip, 0)),
            pl.BlockSpec((TM, d), lambda i, t: (i * nt_all + t + skip, 0)),
            mod_spec,
        ],
        out_specs=pl.BlockSpec((None, TM, d), lambda i, t: (i, t, 0)),
        out_shape=jax.ShapeDtypeStruct((b, nt * TM, d), F32),
        compiler_params=_params(("arbitrary", "arbitrary")),
        name="moe_residual",
    )(h, y, mod)


def _rope_tables(s, ctx_len):
    t = jnp.arange(s)
    rows = (t // GRID_W).astype(F32)
    cols = (t % GRID_W).astype(F32)
    n_pairs = A_ROPE // 2
    per_axis = n_pairs // 2
    inv = ROPE_THETA ** (-jnp.arange(per_axis, dtype=F32) / per_axis)
    ang = jnp.concatenate([rows[:, None] * inv, cols[:, None] * inv], axis=-1)
    cos = jnp.repeat(jnp.cos(ang), 2, axis=-1)
    sin = jnp.repeat(jnp.sin(ang), 2, axis=-1)
    sign = jnp.tile(jnp.array([-1.0, 1.0], F32), n_pairs)
    pad = HEAD_PAD - A_QK
    t1 = jnp.concatenate([jnp.ones((s, A_NOPE), F32), cos, jnp.zeros((s, pad), F32)], axis=-1)
    t2 = jnp.concatenate([jnp.zeros((s, A_NOPE), F32), sin * sign, jnp.zeros((s, pad), F32)], axis=-1)
    c1 = jnp.concatenate([jnp.ones((ctx_len, A_QK), F32), jnp.zeros((ctx_len, pad), F32)], axis=-1)
    c2 = jnp.zeros((ctx_len, HEAD_PAD), F32)
    return jnp.concatenate([c1, t1], axis=0), jnp.concatenate([c2, t2], axis=0)


_PAIR_SWAP = np.arange(A_ROPE) ^ 1


def _rope_lane_vec(g_tail, swapped):
    gt = g_tail[_PAIR_SWAP] if swapped else g_tail
    return jnp.concatenate([jnp.zeros((A_NOPE,), F32), gt, jnp.zeros((HEAD_PAD - A_QK,), F32)])[None]


def _mla_weights(w_in, w_uq, w_ukv, g_q, g_k):
    d = w_in.shape[0]
    q_end = A_Q_RANK
    kv_end = q_end + A_KV_RANK
    r_end = kv_end + A_ROPE
    z = lambda n: jnp.zeros((d, n), w_in.dtype)
    kr = w_in[:, kv_end:r_end]
    pad = HEAD_PAD - A_QK
    win = jnp.concatenate([w_in[:, :kv_end], z(A_NOPE), kr, z(pad), z(A_NOPE), kr[:, _PAIR_SWAP], z(pad),
                           w_in[:, r_end:]], axis=-1).astype(BF16)
    wq = w_uq.reshape(A_Q_RANK, A_HEADS, A_QK)
    zq = lambda n: jnp.zeros((A_Q_RANK, A_HEADS, n), wq.dtype)
    wq_main = jnp.concatenate([wq, zq(pad)], axis=-1)
    wq_swap = jnp.concatenate([zq(A_NOPE), wq[:, :, A_NOPE:][:, :, _PAIR_SWAP], zq(pad)], axis=-1)
    wq_ext = jnp.concatenate([wq_main.reshape(A_Q_RANK, -1), wq_swap.reshape(A_Q_RANK, -1)], axis=-1).astype(BF16)
    wkv = w_ukv.reshape(A_KV_RANK, A_HEADS, A_NOPE + A_V)
    zkv = jnp.zeros((A_KV_RANK, A_HEADS, HEAD_PAD - A_NOPE), wkv.dtype)
    wk = jnp.concatenate([wkv[:, :, :A_NOPE], zkv], axis=-1)
    wv = jnp.concatenate([wkv[:, :, A_NOPE:], zkv], axis=-1)
    wkv_ext = jnp.concatenate([wk.reshape(A_KV_RANK, -1), wv.reshape(A_KV_RANK, -1)], axis=-1).astype(BF16)

    bound = math.sqrt(A_QK) * jnp.max(jnp.abs(g_q)) * jnp.max(jnp.abs(g_k))
    bounded = bound <= MLA_SAFE_BOUND
    shift = jnp.where(bounded, -bound * LOG2E, 0.0)
    unit = lambda lane_idx: jnp.zeros((1, HEAD_PAD), F32).at[0, lane_idx].set(1.0)
    gq = jnp.concatenate([jnp.concatenate([g_q, jnp.zeros((pad,), F32)])[None], _rope_lane_vec(g_q[A_NOPE:], True),
                          unit(A_QK) * shift], axis=0)
    gk = jnp.concatenate([jnp.concatenate([g_k[:A_NOPE], jnp.zeros((HEAD_PAD - A_NOPE,), F32)])[None],
                          _rope_lane_vec(g_k[A_NOPE:], False), _rope_lane_vec(g_k[A_NOPE:], True),
                          unit(A_QK), unit(A_V)], axis=0)
    return win, wq_ext, wkv_ext, gq, gk, bounded.astype(jnp.int32).reshape(1)


def _dft_mats(n, norm):
    k = jnp.arange(n, dtype=jnp.int32)
    w = GRID_W if n % GRID_W == 0 and n > GRID_W else 1
    hi = jnp.arange(n // w, dtype=jnp.int32) * w
    lo = jnp.arange(w, dtype=jnp.int32)
    ang_hi = ((k[:, None] * hi[None, :]) % n).astype(F32) * (2.0 * math.pi / n)
    ang_lo = ((k[:, None] * lo[None, :]) % n).astype(F32) * (2.0 * math.pi / n)
    ch, sh = jnp.cos(ang_hi)[:, :, None], jnp.sin(ang_hi)[:, :, None]
    cl, sl = jnp.cos(ang_lo)[:, None, :], jnp.sin(ang_lo)[:, None, :]
    cos = ((ch * cl - sh * sl) * norm).astype(BF16).reshape(n, n)
    sin = ((sh * cl + ch * sl) * norm).astype(BF16).reshape(n, n)
    return cos, sin


def _channel_dft():
    c = np.arange(F_CH)
    ang = 2.0 * np.pi * ((c[:, None] * c[None, :]) % F_CH) / F_CH
    eye = np.eye(F_GROUPS)
    cb = np.kron(eye, np.cos(ang)) / math.sqrt(F_CH)
    sb = np.kron(eye, np.sin(ang)) / math.sqrt(F_CH)
    return jnp.asarray(np.concatenate([cb, sb], axis=1), BF16)


def _na_group_rows(rows_n):
    assert rows_n % NA_GROUP == 0 and rows_n >= NA_WINDOW + NA_GROUP
    n_groups = rows_n // NA_GROUP
    n_dr = 2 * WIN_R - 1
    idx = np.full((3, NA_GROUP, NA_WINDOW), n_dr, np.int32)
    seen = {}
    for g in range(n_groups):
        gs = int(np.clip(g * NA_GROUP - WIN_R // 2, 0, rows_n - NA_WINDOW))
        pat = 0 if g == 0 else (2 if g == n_groups - 1 else 1)
        cur = np.full((NA_GROUP, NA_WINDOW), n_dr, np.int32)
        for qi in range(NA_GROUP):
            r = g * NA_GROUP + qi
            start = int(np.clip(r - WIN_R // 2, 0, rows_n - WIN_R))
            assert gs <= start and start + WIN_R <= gs + NA_WINDOW
            for a in range(start, start + WIN_R):
                cur[qi, a - gs] = a - r + (WIN_R - 1)
        assert pat not in seen or np.array_equal(seen[pat], cur)
        seen[pat] = cur
        idx[pat] = cur
    return idx


def _na_score_bound(g_q, g_k, rpb):
    qk = math.sqrt(C_DH) * jnp.max(jnp.abs(g_q)) * jnp.max(jnp.abs(g_k))
    bound = qk + jnp.maximum(jnp.max(rpb), 0.0)
    lowest = -qk + jnp.minimum(jnp.min(rpb), 0.0)
    bounded = (bound - lowest) <= 2.0 * MLA_SAFE_BOUND
    shift = jnp.where(bounded, bound, 0.0).astype(F32)
    return bounded.astype(jnp.int32).reshape(1), shift


def _na_bias(rpb, rows_n, shift):
    h, n_dr, n_dc = rpb.shape
    qc = np.arange(GRID_W)
    cs = np.clip(qc - WIN_C // 2, 0, GRID_W - WIN_C)
    kcol = np.arange(GRID_W)
    valid = (kcol[None, :] >= cs[:, None]) & (kcol[None, :] < cs[:, None] + WIN_C)
    dc = kcol[None, :] - qc[:, None] + (WIN_C - 1)
    onehot = (valid[:, :, None] & (dc[:, :, None] == np.arange(n_dc)[None, None, :])).astype(np.float32)
    col = jnp.einsum('hdc,qkc->hdqk', rpb.astype(F32), jnp.asarray(onehot), precision=lax.Precision.HIGHEST)
    col = jnp.where(jnp.asarray(valid)[None, None], (col - shift) * LOG2E, NEG_BIG)
    col = jnp.concatenate([col, jnp.full((h, 1, GRID_W, GRID_W), NEG_BIG, F32)], axis=1)
    idx = _na_group_rows(rows_n)
    blocks = [[jnp.stack([col[:, int(idx[p, qi, a])] for a in range(NA_WINDOW)], axis=2)
               for qi in range(NA_GROUP)] for p in range(3)]
    tab = jnp.stack([jnp.stack(bq, axis=1) for bq in blocks], axis=0)
    return tab.reshape(3, h, NA_GROUP * GRID_W, NA_WINDOW * GRID_W)


def kernel(x, c, ctx, c_ctx, w_mod, b_mod, norm_g, a_w_in, a_g_cq, a_g_ckv, a_w_uq, a_w_ukv, a_g_q, a_g_k, a_w_out,
           f_w_gate, f_w_up, f_w_down, c_w_in, c_g_q, c_g_k, c_rpb, c_w_out, m_w_router, m_w_gate, m_w_up, m_w_down):
    b, s, d = x.shape
    l = ctx.shape[1]
    depth = w_mod.shape[0]
    n = l + s
    assert l == TM and s % TM == 0 and s % GRID_W == 0

    mod_rows = 16
    cin = jnp.concatenate([c, c_ctx[None], jnp.zeros((mod_rows - b - 1, d), F32)], axis=0)
    mod_all = _modulation(cin, w_mod, b_mod).reshape(depth, mod_rows, 6, d)

    t1, t2 = _rope_tables(s, l)
    cn, sn = _dft_mats(s, 1.0 / math.sqrt(s))
    cc, sc = _dft_mats(l, 1.0 / math.sqrt(l))
    cb = _channel_dft()

    h = jnp.concatenate([ctx, x], axis=1)
    for i in range(depth):
        j = i // 2
        last = i == depth - 1
        mod = mod_all[i]
        g1 = norm_g[i, 0][None]
        g2 = norm_g[i, 1][None]
        if i % 2 == 0:
            win, wq_ext, wkv_ext, gq, gk, bounded = _mla_weights(a_w_in[j], a_w_uq[j], a_w_ukv[j],
                                                                 a_g_q[j], a_g_k[j])
            q, k, v, fcs = _mla_front(h, mod, g1, win, a_g_cq[j][None], a_g_ckv[j][None], wq_ext, wkv_ext,
                                      t1, t2, gq, gk, cb)
            att = _mla_attn(bounded, q, k, v, l)
            yf = _dft(fcs, cn, sn, cc, sc, l)
            h = _proj_ffn([att, yf], a_w_out[j].astype(BF16), h, mod, g2,
                          f_w_gate[j].astype(BF16), f_w_up[j].astype(BF16), f_w_down[j].astype(BF16))
        else:
            gq2 = jnp.tile(c_g_q[j], 2)[None]
            gk2 = jnp.tile(c_g_k[j], 2)[None]
            q, k, v = _na_front(h, mod, g1, c_w_in[j].astype(BF16), gq2, gk2)
            na_bounded, na_shift = _na_score_bound(c_g_q[j], c_g_k[j], c_rpb[j])
            bias = _na_bias(c_rpb[j], s // GRID_W, na_shift)
            att = _na_attn(na_bounded, q, k, v, bias, jnp.full((1, LANES), na_shift * LOG2E, F32), l)
            wr = jnp.concatenate([m_w_router[j], jnp.zeros((d, LANES - N_EXPERTS), F32)], axis=-1)
            h, xn2, route = _proj_router([att], c_w_out[j].astype(BF16), h, mod, g2, wr)
            y = _moe(xn2.reshape(b * n, d), route.reshape(b * n, LANES),
                     m_w_gate[j].astype(BF16), m_w_up[j].astype(BF16), m_w_down[j].astype(BF16))
            h = _resid(h, y, mod, last)
    if h.shape[1] != s:
        h = h[:, l:]
    return h
```

```python
import functools
import math
from typing import NamedTuple

import jax
import jax.numpy as jnp
import numpy as np
from jax import lax
from jax.experimental import pallas as pl
from jax.experimental.pallas import tpu as pltpu

F32 = jnp.float32
BF16 = jnp.bfloat16

GRID_W = 64
A_HEADS = 12
A_NOPE = 64
A_ROPE = 32
A_QK = A_NOPE + A_ROPE
A_V = 64
A_Q_RANK = 256
A_KV_RANK = 128
F_GROUPS = 4
F_CH = 64
F_WIDTH = F_GROUPS * F_CH
C_HEADS = 16
C_DH = 64
WIN_R = 8
WIN_C = 16
N_EXPERTS = 8
ROPE_THETA = 10000.0
EPS = 1e-6

LANES = 128
HEAD_PAD = 128
TM = 512
TQ = 256
FFN_CHUNK = 1408
MOE_TM = 256
_PAIR_LO = tuple(lo for lo in range(N_EXPERTS) for hi in range(lo + 1, N_EXPERTS))
_PAIR_HI = tuple(hi for lo in range(N_EXPERTS) for hi in range(lo + 1, N_EXPERTS))
LOG2E = math.log2(math.e)
MLA_SAFE_BOUND = 40.0
MLA_KCHUNK = 512
MLA_HPS = 6
NA_GROUP = 4
NA_WINDOW = NA_GROUP + WIN_R
VMEM_LIMIT = 52 * 1024 * 1024
NEG_BIG = -1e30


def _params(sem, vmem=VMEM_LIMIT):
    return pltpu.CompilerParams(dimension_semantics=sem, vmem_limit_bytes=vmem)


def _rms(x, g, n=None):
    n = x.shape[-1] if n is None else n
    ss = jnp.sum(x * x, axis=-1, keepdims=True)
    return x * lax.rsqrt(ss * (1.0 / n) + EPS) * g


def _silu(x):
    return x * (1.0 / (1.0 + jnp.exp(-x)))


class _Layout(NamedTuple):
    b: int
    s: int
    l: int
    tm: int

    @property
    def n_lat(self):
        return self.b * self.s

    @property
    def n_tok(self):
        return self.b * (self.s + self.l)

    @property
    def lat_tiles(self):
        return self.n_lat // self.tm

    @property
    def all_tiles(self):
        return self.n_tok // self.tm

    def tile(self, width):
        return pl.BlockSpec((self.tm, width), lambda t: (t, 0))

    def mod_spec(self, d):
        per_batch = self.s // self.tm
        return pl.BlockSpec((None, 6, d), lambda t: (jnp.where(t < self.lat_tiles, t // per_batch, self.b), 0, 0))

    def ctx_block(self, i):
        return self.n_lat // self.l + i


def _make_layout(b, s, l):
    tm = TM if (s % TM == 0 and (b * l) % TM == 0) else l
    assert l == TQ and s % tm == 0 and (b * l) % tm == 0 and s % GRID_W == 0
    return _Layout(b, s, l, tm)


def _resident(a):
    return pl.BlockSpec(a.shape, lambda *_: (0,) * a.ndim, pipeline_mode=pl.Buffered(1))


def _norm_mod(x, g_ref, mod_ref, row):
    xn = _rms(x, g_ref[...])
    return xn * (1.0 + mod_ref[row + 1:row + 2, :]) + mod_ref[row:row + 1, :]


def _mod_kernel(c_ref, w_ref, b_ref, o_ref):
    s = _silu(c_ref[...])
    o_ref[...] = jnp.dot(s, w_ref[...], precision=lax.Precision.HIGHEST,
                         preferred_element_type=F32) + b_ref[...]


def _modulation(cin, w_mod, b_mod):
    depth, d, n = w_mod.shape
    tn = 1536
    rows = cin.shape[0]
    return pl.pallas_call(
        _mod_kernel,
        grid=(depth, n // tn),
        in_specs=[
            pl.BlockSpec((rows, d), lambda l, j: (0, 0)),
            pl.BlockSpec((None, d, tn), lambda l, j: (l, 0, j)),
            pl.BlockSpec((None, 1, tn), lambda l, j: (l, 0, j)),
        ],
        out_specs=pl.BlockSpec((None, rows, tn), lambda l, j: (l, 0, j)),
        out_shape=jax.ShapeDtypeStruct((depth, rows, n), F32),
        compiler_params=_params(("arbitrary", "arbitrary")),
        name="modulation",
    )(cin, w_mod, b_mod.reshape(depth, 1, n))


def _mla_front_kernel(h_ref, mod_ref, g_ref, win_ref, gcq_ref, gckv_ref, wq_ref, wkv_ref,
                      t1_ref, t2_ref, gq_ref, gk_ref, cb_ref,
                      q_ref, k_ref, v_ref, fcs_ref):
    xn = _norm_mod(h_ref[...], g_ref, mod_ref, 0)
    px = jnp.dot(xn.astype(BF16), win_ref[...], preferred_element_type=F32)
    t1 = t1_ref[...]
    t2 = t2_ref[...]
    hw = A_HEADS * HEAD_PAD

    qn = _rms(px[:, 0:A_Q_RANK], gcq_ref[...]).astype(BF16)
    qq = jnp.dot(qn, wq_ref[...], preferred_element_type=F32)
    aq = t1 * gq_ref[0:1, :]
    bq = t2 * gq_ref[1:2, :]
    q_shift = gq_ref[2:3, :]
    scale = A_QK ** -0.5 * LOG2E
    for h in range(A_HEADS):
        qm = qq[:, h * HEAD_PAD:(h + 1) * HEAD_PAD]
        qs = qq[:, hw + h * HEAD_PAD:hw + (h + 1) * HEAD_PAD]
        ss = jnp.sum(qm * qm, axis=-1, keepdims=True)
        inv = lax.rsqrt(ss * (1.0 / A_QK) + EPS) * scale
        q_ref[:, h * HEAD_PAD:(h + 1) * HEAD_PAD] = ((qm * aq + qs * bq) * inv + q_shift).astype(BF16)

    kvn = _rms(px[:, A_Q_RANK:A_Q_RANK + A_KV_RANK], gckv_ref[...]).astype(BF16)
    kv = jnp.dot(kvn, wkv_ref[...], preferred_element_type=F32)
    krm = px[:, 384:512]
    krs = px[:, 512:640]
    tail = krm * (t1 * gk_ref[1:2, :]) + krs * (t2 * gk_ref[2:3, :])
    ssr = jnp.sum(krm * krm, axis=-1, keepdims=True)
    gkn = gk_ref[0:1, :]
    k_one = gk_ref[3:4, :]
    v_one = gk_ref[4:5, :]
    for h in range(A_HEADS):
        km = kv[:, h * HEAD_PAD:(h + 1) * HEAD_PAD]
        ss = jnp.sum(km * km, axis=-1, keepdims=True) + ssr
        inv = lax.rsqrt(ss * (1.0 / A_QK) + EPS)
        k_ref[:, h * HEAD_PAD:(h + 1) * HEAD_PAD] = ((km * gkn + tail) * inv + k_one).astype(BF16)
        v_ref[:, h * HEAD_PAD:(h + 1) * HEAD_PAD] = (kv[:, hw + h * HEAD_PAD:hw + (h + 1) * HEAD_PAD]
                                                     + v_one).astype(BF16)

    f = px[:, 640:896].astype(BF16)
    fcs_ref[...] = jnp.dot(f, cb_ref[...], preferred_element_type=F32).astype(BF16)


def _mla_front(lay, h, mod, g, win, gcq, gckv, wq, wkv, t1, t2, gq, gk, cb):
    n, d = h.shape
    hw = A_HEADS * HEAD_PAD
    per_batch = lay.s // lay.tm
    rope = pl.BlockSpec((lay.tm, LANES), lambda t: (jnp.where(t < lay.lat_tiles, t % per_batch, per_batch), 0))
    return pl.pallas_call(
        _mla_front_kernel,
        grid=(lay.all_tiles,),
        in_specs=[
            lay.tile(d), lay.mod_spec(d),
            _resident(g), _resident(win), _resident(gcq), _resident(gckv), _resident(wq), _resident(wkv),
            rope, rope,
            _resident(gq), _resident(gk), _resident(cb),
        ],
        out_specs=[lay.tile(hw), lay.tile(hw), lay.tile(hw), lay.tile(2 * F_WIDTH)],
        out_shape=[
            jax.ShapeDtypeStruct((n, hw), BF16),
            jax.ShapeDtypeStruct((n, hw), BF16),
            jax.ShapeDtypeStruct((n, hw), BF16),
            jax.ShapeDtypeStruct((n, 2 * F_WIDTH), BF16),
        ],
        compiler_params=_params(("arbitrary",)),
        name="mla_front",
    )(h, mod, g, win, gcq, gckv, wq, wkv, t1, t2, gq, gk, cb)


def _softmax_pv(q, k, v):
    s = lax.dot_general(q, k, (((1,), (1,)), ((), ())), preferred_element_type=F32)
    m = jnp.max(s, axis=-1, keepdims=True)
    p = jnp.exp2(s - m)
    l = jnp.sum(p, axis=-1, keepdims=True)
    o = jnp.dot(p.astype(BF16), v, preferred_element_type=F32)
    return o * (1.0 / l)


def _mla_attn_kernel(bounded_ref, q_ref, kx_ref, kc_ref, vx_ref, vc_ref, o_ref, m_ref, acc_ref):
    t = pl.program_id(2)
    ck = MLA_KCHUNK
    n_chunks = kx_ref.shape[0] // ck

    def head(ref, rows=slice(None)):
        return lambda hh: ref[rows, hh * HEAD_PAD:(hh + 1) * HEAD_PAD]

    def scores(hh, k):
        return lax.dot_general(head(q_ref)(hh), k, (((1,), (1,)), ((), ())), preferred_element_type=F32)

    def bounded(with_latent):
        for hh in range(MLA_HPS):
            p = jnp.exp2(scores(hh, head(kc_ref)(hh))).astype(BF16)
            acc = jnp.dot(p, head(vc_ref)(hh), preferred_element_type=F32)
            if with_latent:
                p = jnp.exp2(scores(hh, head(kx_ref)(hh))).astype(BF16)
                acc = acc + jnp.dot(p, head(vx_ref)(hh), preferred_element_type=F32)
            acc_ref[hh] = acc

    def online(with_latent):
        for hh in range(MLA_HPS):
            s = scores(hh, head(kc_ref)(hh))
            m = jnp.max(s, axis=-1, keepdims=True)
            m_ref[hh] = jnp.broadcast_to(m, m_ref.shape[1:])
            acc_ref[hh] = jnp.dot(jnp.exp2(s - m).astype(BF16), head(vc_ref)(hh), preferred_element_type=F32)

        def body(c, carry):
            rows = pl.ds(pl.multiple_of(c * ck, ck), ck)
            for hh in range(MLA_HPS):
                s = scores(hh, head(kx_ref, rows)(hh))
                m_old = m_ref[hh]
                m_new = jnp.maximum(m_old, jnp.max(s, axis=-1, keepdims=True))
                p = jnp.exp2(s - jnp.tile(m_new, (1, ck // LANES))).astype(BF16)
                acc_ref[hh] = (jnp.exp2(m_old - m_new) * acc_ref[hh]
                               + jnp.dot(p, head(vx_ref, rows)(hh), preferred_element_type=F32))
                m_ref[hh] = m_new
            return carry

        if with_latent:
            lax.fori_loop(0, n_chunks, body, 0)

    is_ctx = t == pl.num_programs(2) - 1
    fast = bounded_ref[0] != 0

    @pl.when(fast & is_ctx)
    def _():
        bounded(False)

    @pl.when(fast & jnp.logical_not(is_ctx))
    def _():
        bounded(True)

    @pl.when(jnp.logical_not(fast) & is_ctx)
    def _():
        online(False)

    @pl.when(jnp.logical_not(fast) & jnp.logical_not(is_ctx))
    def _():
        online(True)

    lane = lax.broadcasted_iota(jnp.int32, (o_ref.shape[0], LANES), 1)
    for hp in range(MLA_HPS // 2):
        outs = []
        for hh in (2 * hp, 2 * hp + 1):
            acc = acc_ref[hh]
            outs.append(acc * (1.0 / acc[:, A_V:A_V + 1]))
        o_ref[:, hp * LANES:(hp + 1) * LANES] = jnp.where(
            lane < A_V, outs[0], pltpu.roll(outs[1], A_V, axis=1)).astype(BF16)


def _query_block(lay, i, t):
    nq = lay.s // TQ
    return jnp.where(t < nq, i * nq + t, lay.ctx_block(i))


def _mla_attn(lay, bounded, q, k, v):
    hp = A_HEADS // MLA_HPS
    assert lay.s % MLA_KCHUNK == 0
    wq = MLA_HPS * HEAD_PAD
    lat = pl.BlockSpec((lay.s, wq), lambda i, p, t, f: (i, p))
    ctx = pl.BlockSpec((lay.l, wq), lambda i, p, t, f: (lay.ctx_block(i), p))
    return pl.pallas_call(
        _mla_attn_kernel,
        grid_spec=pltpu.PrefetchScalarGridSpec(
            num_scalar_prefetch=1,
            grid=(lay.b, hp, lay.s // TQ + 1),
            in_specs=[
                pl.BlockSpec((TQ, wq), lambda i, p, t, f: (_query_block(lay, i, t), p)),
                lat, ctx, lat, ctx,
            ],
            out_specs=pl.BlockSpec((TQ, MLA_HPS * A_V), lambda i, p, t, f: (_query_block(lay, i, t), p)),
            scratch_shapes=[
                pltpu.VMEM((MLA_HPS, TQ, LANES), F32),
                pltpu.VMEM((MLA_HPS, TQ, HEAD_PAD), F32),
            ],
        ),
        out_shape=jax.ShapeDtypeStruct((lay.n_tok, A_HEADS * A_V), BF16),
        compiler_params=_params(("arbitrary", "arbitrary", "arbitrary")),
        name="mla_attn",
    )(bounded, q, k, k, v, v)


def _dft_kernel(cn_ref, sn_ref, cc_ref, sc_ref, fx_ref, fc_ref, y_ref):
    is_ctx = pl.program_id(1) == pl.num_programs(1) - 1

    def run(c_ref, s_ref, f_ref):
        y = (jnp.dot(c_ref[...], f_ref[:, 0:F_WIDTH], preferred_element_type=F32)
             - jnp.dot(s_ref[...], f_ref[:, F_WIDTH:2 * F_WIDTH], preferred_element_type=F32))
        y_ref[...] = y.astype(BF16)

    @pl.when(is_ctx)
    def _():
        run(cc_ref, sc_ref, fc_ref)

    @pl.when(jnp.logical_not(is_ctx))
    def _():
        run(cn_ref, sn_ref, fx_ref)


def _dft(lay, fcs, cn, sn, cc, sc):
    nq = lay.s // TQ
    table = pl.BlockSpec((TQ, lay.s), lambda i, t: (jnp.minimum(t, nq - 1), 0))
    return pl.pallas_call(
        _dft_kernel,
        grid=(lay.b, nq + 1),
        in_specs=[
            table, table, _resident(cc), _resident(sc),
            pl.BlockSpec((lay.s, 2 * F_WIDTH), lambda i, t: (i, 0)),
            pl.BlockSpec((lay.l, 2 * F_WIDTH), lambda i, t: (lay.ctx_block(i), 0)),
        ],
        out_specs=pl.BlockSpec((TQ, F_WIDTH), lambda i, t: (_query_block(lay, i, t), 0)),
        out_shape=jax.ShapeDtypeStruct((lay.n_tok, F_WIDTH), BF16),
        compiler_params=_params(("arbitrary", "arbitrary")),
        name="fourier_dft",
    )(cn, sn, cc, sc, fcs, fcs)


def _mixed_residual(mix_refs, widths, w_ref, h_ref, mod_ref):
    acc = None
    off = 0
    for m_ref, wd in zip(mix_refs, widths):
        part = jnp.dot(m_ref[...], w_ref[off:off + wd, :], preferred_element_type=F32)
        acc = part if acc is None else acc + part
        off += wd
    return h_ref[...] + mod_ref[2:3, :] * acc


def _mix_specs(lay, mixes, w, d):
    return [lay.tile(m.shape[-1]) for m in mixes] + [_resident(w), lay.tile(d), lay.mod_spec(d)]


def _proj_ffn_kernel(*refs, widths):
    nm = len(widths)
    w_ref, h_ref, mod_ref, g_ref, wg_ref, wu_ref, wd_ref, o_ref = refs[nm:]
    h1 = _mixed_residual(refs[:nm], widths, w_ref, h_ref, mod_ref)
    xn = _norm_mod(h1, g_ref, mod_ref, 3).astype(BF16)
    ff = wg_ref.shape[1]
    y = None
    for c0 in range(0, ff, FFN_CHUNK):
        c1 = min(c0 + FFN_CHUNK, ff)
        gt = jnp.dot(xn, wg_ref[:, c0:c1], preferred_element_type=F32)
        up = jnp.dot(xn, wu_ref[:, c0:c1], preferred_element_type=F32)
        a = (_silu(gt) * up).astype(BF16)
        part = jnp.dot(a, wd_ref[c0:c1, :], preferred_element_type=F32)
        y = part if y is None else y + part
    o_ref[...] = h1 + mod_ref[5:6, :] * y


def _proj_ffn(lay, mixes, w_out, h, mod, g, wg, wu, wd):
    n, d = h.shape
    widths = tuple(m.shape[-1] for m in mixes)
    return pl.pallas_call(
        functools.partial(_proj_ffn_kernel, widths=widths),
        grid=(lay.all_tiles,),
        in_specs=_mix_specs(lay, mixes, w_out, d) + [_resident(g), _resident(wg), _resident(wu), _resident(wd)],
        out_specs=lay.tile(d),
        out_shape=jax.ShapeDtypeStruct((n, d), F32),
        compiler_params=_params(("arbitrary",)),
        name="proj_ffn",
    )(*mixes, w_out, h, mod, g, wg, wu, wd)


def _na_front_kernel(h_ref, mod_ref, g_ref, win_ref, gq_ref, gk_ref, q_ref, k_ref, v_ref):
    xn = _norm_mod(h_ref[...], g_ref, mod_ref, 0)
    px = jnp.dot(xn.astype(BF16), win_ref[...], preferred_element_type=F32)
    width = C_HEADS * C_DH
    lane = lax.broadcasted_iota(jnp.int32, (px.shape[0], LANES), 1)
    low = lane < C_DH
    scale = C_DH ** -0.5 * LOG2E

    def norm_pairs(base, g_ref_, out_ref, mult):
        gg = g_ref_[...]
        for j in range(width // LANES):
            x = px[:, base + j * LANES:base + (j + 1) * LANES]
            x2 = x * x
            s_all = jnp.sum(x2, axis=-1, keepdims=True)
            s_lo = jnp.sum(jnp.where(low, x2, 0.0), axis=-1, keepdims=True)
            ss = jnp.where(low, s_lo, s_all - s_lo)
            inv = lax.rsqrt(ss * (1.0 / C_DH) + EPS)
            if mult != 1.0:
                inv = inv * mult
            out_ref[:, j * LANES:(j + 1) * LANES] = (x * inv * gg).astype(BF16)

    norm_pairs(0, gq_ref, q_ref, scale)
    norm_pairs(width, gk_ref, k_ref, 1.0)
    v_ref[...] = px[:, 2 * width:].astype(BF16)


def _na_front(lay, h, mod, g, win, gq, gk):
    n, d = h.shape
    width = C_HEADS * C_DH
    return pl.pallas_call(
        _na_front_kernel,
        grid=(lay.all_tiles,),
        in_specs=[lay.tile(d), lay.mod_spec(d), _resident(g), _resident(win), _resident(gq), _resident(gk)],
        out_specs=[lay.tile(width)] * 3,
        out_shape=[jax.ShapeDtypeStruct((n, width), BF16)] * 3,
        compiler_params=_params(("arbitrary",)),
        name="na_front",
    )(h, mod, g, win, gq, gk)


def _na_attn_kernel(bounded_ref, qx_ref, qc_ref, kx_ref, kc_ref, vx_ref, vc_ref, bias_ref, shift_ref,
                    *out_refs, rows_n):
    ox_ref = out_refs[0]
    qlen = NA_GROUP * GRID_W
    wlen = NA_WINDOW * GRID_W
    n_groups = rows_n // NA_GROUP
    lane_q = lax.broadcasted_iota(jnp.int32, (qlen, LANES), 1)
    kc = kc_ref[...]
    vc = vc_ref[...]

    if len(out_refs) > 1:
        qc = qc_ref[...]
        lane_c = lax.broadcasted_iota(jnp.int32, qc.shape, 1)
        outs = []
        for hh in range(2):
            sel = (lane_c < C_DH) if hh == 0 else (lane_c >= C_DH)
            outs.append(_softmax_pv(jnp.where(sel, qc, jnp.zeros_like(qc)), kc, vc))
        out_refs[1][...] = jnp.where(lane_c < C_DH, outs[0], outs[1]).astype(BF16)

    def group_body(g, carry, bounded):
        start = jnp.clip(g * NA_GROUP - WIN_R // 2, 0, rows_n - NA_WINDOW)
        pat = jnp.where(g == 0, 0, jnp.where(g == n_groups - 1, 2, 1))
        q0 = pl.multiple_of(g * qlen, qlen)
        k0 = pl.multiple_of(start * GRID_W, GRID_W)
        qr = qx_ref[pl.ds(q0, qlen), :]
        kw = kx_ref[pl.ds(k0, wlen), :]
        vw = vx_ref[pl.ds(k0, wlen), :]
        res = []
        for hh in range(2):
            sel = (lane_q < C_DH) if hh == 0 else (lane_q >= C_DH)
            qh = jnp.where(sel, qr, jnp.zeros_like(qr))
            s_loc = lax.dot_general(qh, kw, (((1,), (1,)), ((), ())),
                                    preferred_element_type=F32) + bias_ref[pat, hh]
            s_ctx = lax.dot_general(qh, kc, (((1,), (1,)), ((), ())), preferred_element_type=F32)
            if bounded:
                p_loc = jnp.exp2(s_loc)
                p_ctx = jnp.exp2(s_ctx - shift_ref[:, 0:1])
            else:
                m = jnp.maximum(jnp.max(s_loc, axis=-1, keepdims=True),
                                jnp.max(s_ctx, axis=-1, keepdims=True))
                p_loc = jnp.exp2(s_loc - m)
                p_ctx = jnp.exp2(s_ctx - m)
            l = jnp.sum(p_loc, axis=-1, keepdims=True) + jnp.sum(p_ctx, axis=-1, keepdims=True)
            o = (jnp.dot(p_loc.astype(BF16), vw, preferred_element_type=F32)
                 + jnp.dot(p_ctx.astype(BF16), vc, preferred_element_type=F32))
            res.append(o * (1.0 / l))
        ox_ref[pl.ds(q0, qlen), :] = jnp.where(lane_q < C_DH, res[0], res[1]).astype(BF16)
        return carry

    @pl.when(bounded_ref[0] != 0)
    def _():
        lax.fori_loop(0, n_groups, functools.partial(group_body, bounded=True), 0, unroll=2)

    @pl.when(bounded_ref[0] == 0)
    def _():
        lax.fori_loop(0, n_groups, functools.partial(group_body, bounded=False), 0)


def _na_attn(lay, bounded, q, k, v, bias, shift, need_ctx):
    width = q.shape[-1]
    hp = C_HEADS // 2
    lat = pl.BlockSpec((lay.s, LANES), lambda i, p, f: (i, p))
    ctx = pl.BlockSpec((lay.l, LANES), lambda i, p, f: (lay.ctx_block(i), p))
    out_specs = [pl.BlockSpec((lay.s, LANES), lambda i, p, f: (i, p))]
    out_shape = [jax.ShapeDtypeStruct((lay.n_lat, width), BF16)]
    if need_ctx:
        out_specs.append(pl.BlockSpec((lay.l, LANES), lambda i, p, f: (i, p)))
        out_shape.append(jax.ShapeDtypeStruct((lay.b * lay.l, width), BF16))
    return pl.pallas_call(
        functools.partial(_na_attn_kernel, rows_n=lay.s // GRID_W),
        grid_spec=pltpu.PrefetchScalarGridSpec(
            num_scalar_prefetch=1,
            grid=(lay.b, hp),
            in_specs=[lat, ctx, lat, ctx, lat, ctx,
                      pl.BlockSpec((3, 2) + bias.shape[2:], lambda i, p, f: (0, p, 0, 0)),
                      pl.BlockSpec(shift.shape, lambda i, p, f: (0, 0))],
            out_specs=out_specs,
        ),
        out_shape=out_shape,
        compiler_params=_params(("arbitrary", "arbitrary")),
        name="na_attn",
    )(bounded, q, q, k, k, v, v, bias, shift)


def _proj_router_kernel(*refs, widths):
    nm = len(widths)
    w_ref, h_ref, mod_ref, g_ref, wr_ref, h1_ref, xn_ref, route_ref = refs[nm:]
    h1 = _mixed_residual(refs[:nm], widths, w_ref, h_ref, mod_ref)
    h1_ref[...] = h1
    xn = _norm_mod(h1, g_ref, mod_ref, 3)
    xn_ref[...] = xn
    logits = jnp.dot(xn, wr_ref[...], precision=lax.Precision.HIGHEST, preferred_element_type=F32)
    lane = lax.broadcasted_iota(jnp.int32, logits.shape, 1).astype(F32)
    lg = jnp.where(lane < N_EXPERTS, logits, -jnp.inf)
    m1 = jnp.max(lg, axis=-1, keepdims=True)
    i1 = jnp.min(jnp.where(lg == m1, lane, float(LANES)), axis=-1, keepdims=True)
    lg2 = jnp.where(lane == i1, -jnp.inf, lg)
    m2 = jnp.max(lg2, axis=-1, keepdims=True)
    i2 = jnp.min(jnp.where(lg2 == m2, lane, float(LANES)), axis=-1, keepdims=True)
    e2 = jnp.exp(m2 - m1)
    g1 = 1.0 / (1.0 + e2)
    g2 = e2 * g1
    first_low = i1 < i2
    vals = (jnp.minimum(i1, i2), jnp.maximum(i1, i2), jnp.where(first_low, g1, g2), jnp.where(first_low, g2, g1))
    route = jnp.zeros_like(lane)
    for idx, val in enumerate(vals):
        route = jnp.where(lane == float(idx), val, route)
    route_ref[...] = route


def _proj_router(lay, mixes, w_out, h, mod, g, wr, n_tiles):
    d = h.shape[-1]
    rows = n_tiles * lay.tm
    widths = tuple(m.shape[-1] for m in mixes)
    return pl.pallas_call(
        functools.partial(_proj_router_kernel, widths=widths),
        grid=(n_tiles,),
        in_specs=_mix_specs(lay, mixes, w_out, d) + [_resident(g), _resident(wr)],
        out_specs=[lay.tile(d), lay.tile(d), lay.tile(LANES)],
        out_shape=[jax.ShapeDtypeStruct((rows, d), F32), jax.ShapeDtypeStruct((rows, d), F32),
                   jax.ShapeDtypeStruct((rows, LANES), F32)],
        compiler_params=_params(("arbitrary",)),
        name="proj_router",
    )(*mixes, w_out, h, mod, g, wr)


def _moe_plan(route, n_tok):
    n_pairs = len(_PAIR_LO)
    n_tiles = n_tok // MOE_TM + n_pairs
    rows = n_tiles * MOE_TM
    lo = route[:, 0].astype(jnp.int32)
    hi = route[:, 1].astype(jnp.int32)
    pid = (lo * (2 * N_EXPERTS - 1 - lo)) // 2 + (hi - lo - 1)
    onehot = (pid[:, None] == jnp.arange(n_pairs, dtype=jnp.int32)[None, :]).astype(jnp.int32)
    csum = jnp.cumsum(onehot, axis=0)
    rank = jnp.sum(csum * onehot, axis=1) - 1
    counts = csum[-1]
    padded = ((counts + MOE_TM - 1) // MOE_TM) * MOE_TM
    gend = jnp.cumsum(padded)
    gstart = gend - padded
    dest = jnp.sum(gstart[None, :] * onehot, axis=1) + rank
    src = jnp.full((rows,), -1, jnp.int32).at[dest].set(jnp.arange(n_tok, dtype=jnp.int32))
    valid = src >= 0
    src_tok = jnp.where(valid, src, 0)
    n_valid = jnp.sum(valid.reshape(n_tiles, MOE_TM).astype(jnp.int32), axis=1)
    gates = jnp.where(valid[:, None], route[src_tok, 2:4], 0.0)
    gates = jnp.concatenate([gates, jnp.zeros((rows, LANES - 2), F32)], axis=1)
    n_used = gend[-1] // MOE_TM
    tile_row = jnp.minimum(jnp.arange(n_tiles, dtype=jnp.int32), n_used - 1) * MOE_TM
    group = jnp.minimum(jnp.sum((gend[None, :] <= tile_row[:, None]).astype(jnp.int32), axis=1), n_pairs - 1)
    ea = jnp.asarray(_PAIR_LO, jnp.int32)[group]
    eb = jnp.asarray(_PAIR_HI, jnp.int32)[group]
    return ea, eb, src_tok, n_valid, n_used.astype(jnp.int32).reshape(1), gates


def _moe_kernel(ea_ref, eb_ref, src_ref, nvalid_ref, nused_ref,
                x_hbm, gates_ref, wga_ref, wua_ref, wda_ref, wgb_ref, wub_ref, wdb_ref,
                y_hbm, xbuf, ybuf, gsem, ssem):
    i = pl.program_id(0)
    n_steps = pl.num_programs(0)
    n_used = nused_ref[0]
    slot = i % 2

    def gather_copy(tile, slot_, r):
        return pltpu.make_async_copy(x_hbm.at[pl.ds(src_ref[tile * MOE_TM + r], 1)],
                                     xbuf.at[slot_, pl.ds(r, 1)], gsem.at[slot_])

    def scatter_copy(tile, slot_, r):
        return pltpu.make_async_copy(ybuf.at[slot_, pl.ds(r, 1)],
                                     y_hbm.at[pl.ds(src_ref[tile * MOE_TM + r], 1)], ssem.at[slot_])

    def start_gather(tile, slot_):
        for r in range(MOE_TM):
            gather_copy(tile, slot_, r).start()

    def wait_gather(slot_):
        pltpu.make_async_copy(x_hbm.at[pl.ds(0, MOE_TM)], xbuf.at[slot_], gsem.at[slot_]).wait()

    def start_scatter(tile, slot_):
        full = nvalid_ref[tile] == MOE_TM

        @pl.when(full)
        def _():
            for r in range(MOE_TM):
                scatter_copy(tile, slot_, r).start()

        @pl.when(jnp.logical_not(full))
        def _():
            def body(r, carry):
                scatter_copy(tile, slot_, r).start()
                return carry
            lax.fori_loop(0, nvalid_ref[tile], body, 0)

    def wait_scatter(tile, slot_):
        full = nvalid_ref[tile] == MOE_TM

        @pl.when(full)
        def _():
            pltpu.make_async_copy(ybuf.at[slot_], y_hbm.at[pl.ds(0, MOE_TM)], ssem.at[slot_]).wait()

        @pl.when(jnp.logical_not(full))
        def _():
            def body(r, carry):
                scatter_copy(tile, slot_, r).wait()
                return carry
            lax.fori_loop(0, nvalid_ref[tile], body, 0)

    @pl.when((i >= 2) & (i - 2 < n_used))
    def _():
        wait_scatter(i - 2, slot)

    @pl.when(i < n_used)
    def _():
        @pl.when(i == 0)
        def _():
            start_gather(0, 0)

        @pl.when(i + 1 < n_used)
        def _():
            start_gather(i + 1, 1 - slot)

        wait_gather(slot)
        x = xbuf[slot].astype(BF16)

        def expert(wg_ref, wu_ref, wd_ref):
            gt = jnp.dot(x, wg_ref[...], preferred_element_type=F32)
            up = jnp.dot(x, wu_ref[...], preferred_element_type=F32)
            a = (_silu(gt) * up).astype(BF16)
            return jnp.dot(a, wd_ref[...], preferred_element_type=F32)

        gates = gates_ref[...]
        ybuf[slot] = (gates[:, 0:1] * expert(wga_ref, wua_ref, wda_ref)
                      + gates[:, 1:2] * expert(wgb_ref, wub_ref, wdb_ref))
        start_scatter(i, slot)

    @pl.when(i == n_steps - 1)
    def _():
        for back in (2, 1):
            @pl.when(n_steps - back < n_used)
            def _():
                wait_scatter(n_steps - back, (n_steps - back) % 2)


def _moe(xn, route, wg, wu, wd):
    n_tok, d = xn.shape
    _, _, f = wg.shape
    ea, eb, src_tok, n_valid, n_used, gates = _moe_plan(route, n_tok)
    n_tiles = ea.shape[0]
    wspec = lambda shape, which: pl.BlockSpec(
        (None,) + shape, lambda i, ea_, eb_, s_, d_, u_: ((ea_, eb_)[which][i], 0, 0))
    return pl.pallas_call(
        _moe_kernel,
        grid_spec=pltpu.PrefetchScalarGridSpec(
            num_scalar_prefetch=5,
            grid=(n_tiles,),
            in_specs=[
                pl.BlockSpec(memory_space=pl.ANY),
                pl.BlockSpec((MOE_TM, LANES), lambda i, *_: (i, 0)),
                wspec((d, f), 0), wspec((d, f), 0), wspec((f, d), 0),
                wspec((d, f), 1), wspec((d, f), 1), wspec((f, d), 1),
            ],
            out_specs=pl.BlockSpec(memory_space=pl.ANY),
            scratch_shapes=[
                pltpu.VMEM((2, MOE_TM, d), F32),
                pltpu.VMEM((2, MOE_TM, d), F32),
                pltpu.SemaphoreType.DMA((2,)),
                pltpu.SemaphoreType.DMA((2,)),
            ],
        ),
        out_shape=jax.ShapeDtypeStruct((n_tok, d), F32),
        compiler_params=_params(("arbitrary",)),
        name="moe_experts",
    )(ea, eb, src_tok, n_valid, n_used, xn, gates, wg, wu, wd, wg, wu, wd)


def _resid_kernel(h_ref, y_ref, mod_ref, o_ref):
    o_ref[...] = h_ref[...] + mod_ref[5:6, :] * y_ref[...]


def _resid(lay, h, y, mod):
    n, d = h.shape
    return pl.pallas_call(
        _resid_kernel,
        grid=(n // lay.tm,),
        in_specs=[lay.tile(d), lay.tile(d), lay.mod_spec(d)],
        out_specs=lay.tile(d),
        out_shape=jax.ShapeDtypeStruct((n, d), F32),
        compiler_params=_params(("arbitrary",)),
        name="moe_residual",
    )(h, y, mod)


def _rope_tables(s, ctx_rows):
    t = jnp.arange(s)
    rows = (t // GRID_W).astype(F32)
    cols = (t % GRID_W).astype(F32)
    n_pairs = A_ROPE // 2
    per_axis = n_pairs // 2
    inv = ROPE_THETA ** (-jnp.arange(per_axis, dtype=F32) / per_axis)
    ang = jnp.concatenate([rows[:, None] * inv, cols[:, None] * inv], axis=-1)
    cos = jnp.repeat(jnp.cos(ang), 2, axis=-1)
    sin = jnp.repeat(jnp.sin(ang), 2, axis=-1)
    sign = jnp.tile(jnp.array([-1.0, 1.0], F32), n_pairs)
    pad = HEAD_PAD - A_QK
    t1 = jnp.concatenate([jnp.ones((s, A_NOPE), F32), cos, jnp.zeros((s, pad), F32)], axis=-1)
    t2 = jnp.concatenate([jnp.zeros((s, A_NOPE), F32), sin * sign, jnp.zeros((s, pad), F32)], axis=-1)
    c1 = jnp.concatenate([jnp.ones((ctx_rows, A_QK), F32), jnp.zeros((ctx_rows, pad), F32)], axis=-1)
    c2 = jnp.zeros((ctx_rows, HEAD_PAD), F32)
    return jnp.concatenate([t1, c1], axis=0), jnp.concatenate([t2, c2], axis=0)


_PAIR_SWAP = np.arange(A_ROPE) ^ 1


def _rope_lane_vec(g_tail, swapped):
    gt = g_tail[_PAIR_SWAP] if swapped else g_tail
    return jnp.concatenate([jnp.zeros((A_NOPE,), F32), gt, jnp.zeros((HEAD_PAD - A_QK,), F32)])[None]


def _mla_weights(w_in, w_uq, w_ukv, g_q, g_k):
    d = w_in.shape[0]
    q_end = A_Q_RANK
    kv_end = q_end + A_KV_RANK
    r_end = kv_end + A_ROPE
    z = lambda n: jnp.zeros((d, n), w_in.dtype)
    kr = w_in[:, kv_end:r_end]
    pad = HEAD_PAD - A_QK
    win = jnp.concatenate([w_in[:, :kv_end], z(A_NOPE), kr, z(pad), z(A_NOPE), kr[:, _PAIR_SWAP], z(pad),
                           w_in[:, r_end:]], axis=-1).astype(BF16)
    wq = w_uq.reshape(A_Q_RANK, A_HEADS, A_QK)
    zq = lambda n: jnp.zeros((A_Q_RANK, A_HEADS, n), wq.dtype)
    wq_main = jnp.concatenate([wq, zq(pad)], axis=-1)
    wq_swap = jnp.concatenate([zq(A_NOPE), wq[:, :, A_NOPE:][:, :, _PAIR_SWAP], zq(pad)], axis=-1)
    wq_ext = jnp.concatenate([wq_main.reshape(A_Q_RANK, -1), wq_swap.reshape(A_Q_RANK, -1)], axis=-1).astype(BF16)
    wkv = w_ukv.reshape(A_KV_RANK, A_HEADS, A_NOPE + A_V)
    zkv = jnp.zeros((A_KV_RANK, A_HEADS, HEAD_PAD - A_NOPE), wkv.dtype)
    wk = jnp.concatenate([wkv[:, :, :A_NOPE], zkv], axis=-1)
    wv = jnp.concatenate([wkv[:, :, A_NOPE:], zkv], axis=-1)
    wkv_ext = jnp.concatenate([wk.reshape(A_KV_RANK, -1), wv.reshape(A_KV_RANK, -1)], axis=-1).astype(BF16)

    bound = math.sqrt(A_QK) * jnp.max(jnp.abs(g_q)) * jnp.max(jnp.abs(g_k))
    bounded = bound <= MLA_SAFE_BOUND
    shift = jnp.where(bounded, -bound * LOG2E, 0.0)
    unit = lambda lane_idx: jnp.zeros((1, HEAD_PAD), F32).at[0, lane_idx].set(1.0)
    gq = jnp.concatenate([jnp.concatenate([g_q, jnp.zeros((pad,), F32)])[None], _rope_lane_vec(g_q[A_NOPE:], True),
                          unit(A_QK) * shift], axis=0)
    gk = jnp.concatenate([jnp.concatenate([g_k[:A_NOPE], jnp.zeros((HEAD_PAD - A_NOPE,), F32)])[None],
                          _rope_lane_vec(g_k[A_NOPE:], False), _rope_lane_vec(g_k[A_NOPE:], True),
                          unit(A_QK), unit(A_V)], axis=0)
    return win, wq_ext, wkv_ext, gq, gk, bounded.astype(jnp.int32).reshape(1)


def _dft_mats(n, norm):
    k = jnp.arange(n, dtype=jnp.int32)
    w = GRID_W if n % GRID_W == 0 and n > GRID_W else 1
    hi = jnp.arange(n // w, dtype=jnp.int32) * w
    lo = jnp.arange(w, dtype=jnp.int32)
    ang_hi = ((k[:, None] * hi[None, :]) % n).astype(F32) * (2.0 * math.pi / n)
    ang_lo = ((k[:, None] * lo[None, :]) % n).astype(F32) * (2.0 * math.pi / n)
    ch, sh = jnp.cos(ang_hi)[:, :, None], jnp.sin(ang_hi)[:, :, None]
    cl, sl = jnp.cos(ang_lo)[:, None, :], jnp.sin(ang_lo)[:, None, :]
    cos = ((ch * cl - sh * sl) * norm).astype(BF16).reshape(n, n)
    sin = ((sh * cl + ch * sl) * norm).astype(BF16).reshape(n, n)
    return cos, sin


def _channel_dft():
    c = np.arange(F_CH)
    ang = 2.0 * np.pi * ((c[:, None] * c[None, :]) % F_CH) / F_CH
    eye = np.eye(F_GROUPS)
    cb = np.kron(eye, np.cos(ang)) / math.sqrt(F_CH)
    sb = np.kron(eye, np.sin(ang)) / math.sqrt(F_CH)
    return jnp.asarray(np.concatenate([cb, sb], axis=1), BF16)


def _na_group_rows(rows_n):
    assert rows_n % NA_GROUP == 0 and rows_n >= NA_WINDOW + NA_GROUP
    n_groups = rows_n // NA_GROUP
    n_dr = 2 * WIN_R - 1
    idx = np.full((3, NA_GROUP, NA_WINDOW), n_dr, np.int32)
    seen = {}
    for g in range(n_groups):
        gs = int(np.clip(g * NA_GROUP - WIN_R // 2, 0, rows_n - NA_WINDOW))
        pat = 0 if g == 0 else (2 if g == n_groups - 1 else 1)
        cur = np.full((NA_GROUP, NA_WINDOW), n_dr, np.int32)
        for qi in range(NA_GROUP):
            r = g * NA_GROUP + qi
            start = int(np.clip(r - WIN_R // 2, 0, rows_n - WIN_R))
            assert gs <= start and start + WIN_R <= gs + NA_WINDOW
            for a in range(start, start + WIN_R):
                cur[qi, a - gs] = a - r + (WIN_R - 1)
        assert pat not in seen or np.array_equal(seen[pat], cur)
        seen[pat] = cur
        idx[pat] = cur
    return idx


def _na_score_bound(g_q, g_k, rpb):
    qk = math.sqrt(C_DH) * jnp.max(jnp.abs(g_q)) * jnp.max(jnp.abs(g_k))
    bound = qk + jnp.maximum(jnp.max(rpb), 0.0)
    lowest = -qk + jnp.minimum(jnp.min(rpb), 0.0)
    bounded = (bound - lowest) <= 2.0 * MLA_SAFE_BOUND
    shift = jnp.where(bounded, bound, 0.0).astype(F32)
    return bounded.astype(jnp.int32).reshape(1), shift


def _na_bias(rpb, rows_n, shift):
    h, n_dr, n_dc = rpb.shape
    qc = np.arange(GRID_W)
    cs = np.clip(qc - WIN_C // 2, 0, GRID_W - WIN_C)
    kcol = np.arange(GRID_W)
    valid = (kcol[None, :] >= cs[:, None]) & (kcol[None, :] < cs[:, None] + WIN_C)
    dc = kcol[None, :] - qc[:, None] + (WIN_C - 1)
    onehot = (valid[:, :, None] & (dc[:, :, None] == np.arange(n_dc)[None, None, :])).astype(np.float32)
    col = jnp.einsum('hdc,qkc->hdqk', rpb.astype(F32), jnp.asarray(onehot), precision=lax.Precision.HIGHEST)
    col = jnp.where(jnp.asarray(valid)[None, None], (col - shift) * LOG2E, NEG_BIG)
    col = jnp.concatenate([col, jnp.full((h, 1, GRID_W, GRID_W), NEG_BIG, F32)], axis=1)
    idx = _na_group_rows(rows_n)
    blocks = [[jnp.stack([col[:, int(idx[p, qi, a])] for a in range(NA_WINDOW)], axis=2)
               for qi in range(NA_GROUP)] for p in range(3)]
    tab = jnp.stack([jnp.stack(bq, axis=1) for bq in blocks], axis=0)
    return tab.reshape(3, h, NA_GROUP * GRID_W, NA_WINDOW * GRID_W)


def kernel(x, c, ctx, c_ctx, w_mod, b_mod, norm_g, a_w_in, a_g_cq, a_g_ckv, a_w_uq, a_w_ukv, a_g_q, a_g_k, a_w_out,
           f_w_gate, f_w_up, f_w_down, c_w_in, c_g_q, c_g_k, c_rpb, c_w_out, m_w_router, m_w_gate, m_w_up, m_w_down):
    b, s, d = x.shape
    l = ctx.shape[1]
    depth = w_mod.shape[0]
    lay = _make_layout(b, s, l)

    mod_rows = 16
    cin = jnp.concatenate([c, c_ctx[None], jnp.zeros((mod_rows - b - 1, d), F32)], axis=0)
    mod_all = _modulation(cin, w_mod, b_mod).reshape(depth, mod_rows, 6, d)

    t1, t2 = _rope_tables(s, lay.tm)
    cn, sn = _dft_mats(s, 1.0 / math.sqrt(s))
    cc, sc = _dft_mats(l, 1.0 / math.sqrt(l))
    cb = _channel_dft()

    h = jnp.concatenate([x.reshape(b * s, d), ctx.reshape(b * l, d)], axis=0)
    for i in range(depth):
        j = i // 2
        need_ctx = i < depth - 1
        mod = mod_all[i]
        g1 = norm_g[i, 0][None]
        g2 = norm_g[i, 1][None]
        if i % 2 == 0:
            win, wq_ext, wkv_ext, gq, gk, bounded = _mla_weights(a_w_in[j], a_w_uq[j], a_w_ukv[j],
                                                                 a_g_q[j], a_g_k[j])
            q, k, v, fcs = _mla_front(lay, h, mod, g1, win, a_g_cq[j][None], a_g_ckv[j][None], wq_ext, wkv_ext,
                                      t1, t2, gq, gk, cb)
            att = _mla_attn(lay, bounded, q, k, v)
            yf = _dft(lay, fcs, cn, sn, cc, sc)
            h = _proj_ffn(lay, [att, yf], a_w_out[j].astype(BF16), h, mod, g2,
                          f_w_gate[j].astype(BF16), f_w_up[j].astype(BF16), f_w_down[j].astype(BF16))
        else:
            gq2 = jnp.tile(c_g_q[j], 2)[None]
            gk2 = jnp.tile(c_g_k[j], 2)[None]
            q, k, v = _na_front(lay, h, mod, g1, c_w_in[j].astype(BF16), gq2, gk2)
            na_bounded, na_shift = _na_score_bound(c_g_q[j], c_g_k[j], c_rpb[j])
            bias = _na_bias(c_rpb[j], s // GRID_W, na_shift)
            outs = _na_attn(lay, na_bounded, q, k, v, bias, jnp.full((1, LANES), na_shift * LOG2E, F32), need_ctx)
            att = jnp.concatenate(outs, axis=0) if need_ctx else outs[0]
            wr = jnp.concatenate([m_w_router[j], jnp.zeros((d, LANES - N_EXPERTS), F32)], axis=-1)
            n_tiles = lay.all_tiles if need_ctx else lay.lat_tiles
            h, xn2, route = _proj_router(lay, [att], c_w_out[j].astype(BF16), h, mod, g2, wr, n_tiles)
            y = _moe(xn2, route, m_w_gate[j].astype(BF16), m_w_up[j].astype(BF16), m_w_down[j].astype(BF16))
            h = _resid(lay, h, y, mod)
    return h[:lay.n_lat].reshape(b, s, d)
```

```python
import functools
import math
from typing import NamedTuple

import jax
import jax.numpy as jnp
import numpy as np
from jax import lax
from jax.experimental import pallas as pl
from jax.experimental.pallas import tpu as pltpu

F32 = jnp.float32
BF16 = jnp.bfloat16

GRID_W = 64
A_HEADS = 12
A_NOPE = 64
A_ROPE = 32
A_QK = A_NOPE + A_ROPE
A_V = 64
A_Q_RANK = 256
A_KV_RANK = 128
F_GROUPS = 4
F_CH = 64
F_WIDTH = F_GROUPS * F_CH
C_HEADS = 16
C_DH = 64
WIN_R = 8
WIN_C = 16
N_EXPERTS = 8
ROPE_THETA = 10000.0
EPS = 1e-6

LANES = 128
HEAD_PAD = 128
TM = 512
TQ = 256
FFN_CHUNK = 1408
MOE_TM = 256
_PAIR_LO = tuple(lo for lo in range(N_EXPERTS) for hi in range(lo + 1, N_EXPERTS))
_PAIR_HI = tuple(hi for lo in range(N_EXPERTS) for hi in range(lo + 1, N_EXPERTS))
LOG2E = math.log2(math.e)
MLA_SAFE_BOUND = 40.0
MLA_KCHUNK = 512
MLA_HPS = 6
NA_GROUP = 4
NA_WINDOW = NA_GROUP + WIN_R
VMEM_LIMIT = 52 * 1024 * 1024
NEG_BIG = -1e30


def _params(sem, vmem=VMEM_LIMIT):
    return pltpu.CompilerParams(dimension_semantics=sem, vmem_limit_bytes=vmem)


def _rms(x, g, n=None):
    n = x.shape[-1] if n is None else n
    ss = jnp.sum(x * x, axis=-1, keepdims=True)
    return x * lax.rsqrt(ss * (1.0 / n) + EPS) * g


def _silu(x):
    return x * (1.0 / (1.0 + jnp.exp(-x)))


class _Layout(NamedTuple):
    b: int
    s: int
    l: int
    tm: int

    @property
    def n_lat(self):
        return self.b * self.s

    @property
    def n_tok(self):
        return self.b * (self.s + self.l)

    @property
    def lat_tiles(self):
        return self.n_lat // self.tm

    @property
    def all_tiles(self):
        return self.n_tok // self.tm

    def tile(self, width):
        return pl.BlockSpec((self.tm, width), lambda t: (t, 0))

    def mod_spec(self, d):
        per_batch = self.s // self.tm
        return pl.BlockSpec((None, 6, d), lambda t: (jnp.where(t < self.lat_tiles, t // per_batch, self.b), 0, 0))

    def ctx_block(self, i):
        return self.n_lat // self.l + i


def _make_layout(b, s, l):
    tm = TM if (s % TM == 0 and (b * l) % TM == 0) else l
    assert l == TQ and s % tm == 0 and (b * l) % tm == 0 and s % GRID_W == 0
    return _Layout(b, s, l, tm)


def _resident(a):
    return pl.BlockSpec(a.shape, lambda *_: (0,) * a.ndim, pipeline_mode=pl.Buffered(1))


def _norm_mod(x, g_ref, mod_ref, row):
    xn = _rms(x, g_ref[...])
    return xn * (1.0 + mod_ref[row + 1:row + 2, :]) + mod_ref[row:row + 1, :]


def _mod_kernel(c_ref, w_ref, b_ref, o_ref):
    s = _silu(c_ref[...])
    o_ref[...] = jnp.dot(s, w_ref[...], precision=lax.Precision.HIGHEST,
                         preferred_element_type=F32) + b_ref[...]


def _modulation(cin, w_mod, b_mod):
    depth, d, n = w_mod.shape
    tn = 1536
    rows = cin.shape[0]
    return pl.pallas_call(
        _mod_kernel,
        grid=(depth, n // tn),
        in_specs=[
            pl.BlockSpec((rows, d), lambda l, j: (0, 0)),
            pl.BlockSpec((None, d, tn), lambda l, j: (l, 0, j)),
            pl.BlockSpec((None, 1, tn), lambda l, j: (l, 0, j)),
        ],
        out_specs=pl.BlockSpec((None, rows, tn), lambda l, j: (l, 0, j)),
        out_shape=jax.ShapeDtypeStruct((depth, rows, n), F32),
        compiler_params=_params(("arbitrary", "arbitrary")),
        name="modulation",
    )(cin, w_mod, b_mod.reshape(depth, 1, n))


def _mla_front_kernel(h_ref, mod_ref, g_ref, win_ref, gcq_ref, gckv_ref, wq_ref, wkv_ref,
                      t1_ref, t2_ref, gq_ref, gk_ref, cb_ref,
                      q_ref, k_ref, v_ref, fcs_ref):
    xn = _norm_mod(h_ref[...], g_ref, mod_ref, 0)
    px = jnp.dot(xn.astype(BF16), win_ref[...], preferred_element_type=F32)
    t1 = t1_ref[...]
    t2 = t2_ref[...]
    hw = A_HEADS * HEAD_PAD

    qn = _rms(px[:, 0:A_Q_RANK], gcq_ref[...]).astype(BF16)
    qq = jnp.dot(qn, wq_ref[...], preferred_element_type=F32)
    aq = t1 * gq_ref[0:1, :]
    bq = t2 * gq_ref[1:2, :]
    q_shift = gq_ref[2:3, :]
    scale = A_QK ** -0.5 * LOG2E
    for h in range(A_HEADS):
        qm = qq[:, h * HEAD_PAD:(h + 1) * HEAD_PAD]
        qs = qq[:, hw + h * HEAD_PAD:hw + (h + 1) * HEAD_PAD]
        ss = jnp.sum(qm * qm, axis=-1, keepdims=True)
        inv = lax.rsqrt(ss * (1.0 / A_QK) + EPS) * scale
        q_ref[:, h * HEAD_PAD:(h + 1) * HEAD_PAD] = ((qm * aq + qs * bq) * inv + q_shift).astype(BF16)

    kvn = _rms(px[:, A_Q_RANK:A_Q_RANK + A_KV_RANK], gckv_ref[...]).astype(BF16)
    kv = jnp.dot(kvn, wkv_ref[...], preferred_element_type=F32)
    krm = px[:, 384:512]
    krs = px[:, 512:640]
    tail = krm * (t1 * gk_ref[1:2, :]) + krs * (t2 * gk_ref[2:3, :])
    ssr = jnp.sum(krm * krm, axis=-1, keepdims=True)
    gkn = gk_ref[0:1, :]
    k_one = gk_ref[3:4, :]
    v_one = gk_ref[4:5, :]
    for h in range(A_HEADS):
        km = kv[:, h * HEAD_PAD:(h + 1) * HEAD_PAD]
        ss = jnp.sum(km * km, axis=-1, keepdims=True) + ssr
        inv = lax.rsqrt(ss * (1.0 / A_QK) + EPS)
        k_ref[:, h * HEAD_PAD:(h + 1) * HEAD_PAD] = ((km * gkn + tail) * inv + k_one).astype(BF16)
        v_ref[:, h * HEAD_PAD:(h + 1) * HEAD_PAD] = (kv[:, hw + h * HEAD_PAD:hw + (h + 1) * HEAD_PAD]
                                                     + v_one).astype(BF16)

    f = px[:, 640:896].astype(BF16)
    fcs_ref[...] = jnp.dot(f, cb_ref[...], preferred_element_type=F32).astype(BF16)


def _mla_front(lay, h, mod, g, win, gcq, gckv, wq, wkv, t1, t2, gq, gk, cb):
    n, d = h.shape
    hw = A_HEADS * HEAD_PAD
    per_batch = lay.s // lay.tm
    rope = pl.BlockSpec((lay.tm, LANES), lambda t: (jnp.where(t < lay.lat_tiles, t % per_batch, per_batch), 0))
    return pl.pallas_call(
        _mla_front_kernel,
        grid=(lay.all_tiles,),
        in_specs=[
            lay.tile(d), lay.mod_spec(d),
            _resident(g), _resident(win), _resident(gcq), _resident(gckv), _resident(wq), _resident(wkv),
            rope, rope,
            _resident(gq), _resident(gk), _resident(cb),
        ],
        out_specs=[lay.tile(hw), lay.tile(hw), lay.tile(hw), lay.tile(2 * F_WIDTH)],
        out_shape=[
            jax.ShapeDtypeStruct((n, hw), BF16),
            jax.ShapeDtypeStruct((n, hw), BF16),
            jax.ShapeDtypeStruct((n, hw), BF16),
            jax.ShapeDtypeStruct((n, 2 * F_WIDTH), BF16),
        ],
        compiler_params=_params(("arbitrary",)),
        name="mla_front",
    )(h, mod, g, win, gcq, gckv, wq, wkv, t1, t2, gq, gk, cb)


def _softmax_pv(q, k, v):
    s = lax.dot_general(q, k, (((1,), (1,)), ((), ())), preferred_element_type=F32)
    m = jnp.max(s, axis=-1, keepdims=True)
    p = jnp.exp2(s - m)
    l = jnp.sum(p, axis=-1, keepdims=True)
    o = jnp.dot(p.astype(BF16), v, preferred_element_type=F32)
    return o * (1.0 / l)


def _mla_attn_kernel(bounded_ref, q_ref, kx_ref, kc_ref, vx_ref, vc_ref, o_ref, m_ref, acc_ref):
    t = pl.program_id(2)
    ck = MLA_KCHUNK
    n_chunks = kx_ref.shape[0] // ck

    def head(ref, rows=slice(None)):
        return lambda hh: ref[rows, hh * HEAD_PAD:(hh + 1) * HEAD_PAD]

    def scores(hh, k):
        return lax.dot_general(head(q_ref)(hh), k, (((1,), (1,)), ((), ())), preferred_element_type=F32)

    def bounded(with_latent):
        for hh in range(MLA_HPS):
            p = jnp.exp2(scores(hh, head(kc_ref)(hh))).astype(BF16)
            acc = jnp.dot(p, head(vc_ref)(hh), preferred_element_type=F32)
            if with_latent:
                p = jnp.exp2(scores(hh, head(kx_ref)(hh))).astype(BF16)
                acc = acc + jnp.dot(p, head(vx_ref)(hh), preferred_element_type=F32)
            acc_ref[hh] = acc

    def online(with_latent):
        for hh in range(MLA_HPS):
            s = scores(hh, head(kc_ref)(hh))
            m = jnp.max(s, axis=-1, keepdims=True)
            m_ref[hh] = jnp.broadcast_to(m, m_ref.shape[1:])
            acc_ref[hh] = jnp.dot(jnp.exp2(s - m).astype(BF16), head(vc_ref)(hh), preferred_element_type=F32)

        def body(c, carry):
            rows = pl.ds(pl.multiple_of(c * ck, ck), ck)
            for hh in range(MLA_HPS):
                s = scores(hh, head(kx_ref, rows)(hh))
                m_old = m_ref[hh]
                m_new = jnp.maximum(m_old, jnp.max(s, axis=-1, keepdims=True))
                p = jnp.exp2(s - jnp.tile(m_new, (1, ck // LANES))).astype(BF16)
                acc_ref[hh] = (jnp.exp2(m_old - m_new) * acc_ref[hh]
                               + jnp.dot(p, head(vx_ref, rows)(hh), preferred_element_type=F32))
                m_ref[hh] = m_new
            return carry

        if with_latent:
            lax.fori_loop(0, n_chunks, body, 0)

    is_ctx = t == pl.num_programs(2) - 1
    fast = bounded_ref[0] != 0

    @pl.when(fast & is_ctx)
    def _():
        bounded(False)

    @pl.when(fast & jnp.logical_not(is_ctx))
    def _():
        bounded(True)

    @pl.when(jnp.logical_not(fast) & is_ctx)
    def _():
        online(False)

    @pl.when(jnp.logical_not(fast) & jnp.logical_not(is_ctx))
    def _():
        online(True)

    lane = lax.broadcasted_iota(jnp.int32, (o_ref.shape[0], LANES), 1)
    for hp in range(MLA_HPS // 2):
        outs = []
        for hh in (2 * hp, 2 * hp + 1):
            acc = acc_ref[hh]
            outs.append(acc * (1.0 / acc[:, A_V:A_V + 1]))
        o_ref[:, hp * LANES:(hp + 1) * LANES] = jnp.where(
            lane < A_V, outs[0], pltpu.roll(outs[1], A_V, axis=1)).astype(BF16)


def _query_block(lay, i, t):
    nq = lay.s // TQ
    return jnp.where(t < nq, i * nq + t, lay.ctx_block(i))


def _mla_attn(lay, bounded, q, k, v):
    hp = A_HEADS // MLA_HPS
    assert lay.s % MLA_KCHUNK == 0
    wq = MLA_HPS * HEAD_PAD
    lat = pl.BlockSpec((lay.s, wq), lambda i, p, t, f: (i, p))
    ctx = pl.BlockSpec((lay.l, wq), lambda i, p, t, f: (lay.ctx_block(i), p))
    return pl.pallas_call(
        _mla_attn_kernel,
        grid_spec=pltpu.PrefetchScalarGridSpec(
            num_scalar_prefetch=1,
            grid=(lay.b, hp, lay.s // TQ + 1),
            in_specs=[
                pl.BlockSpec((TQ, wq), lambda i, p, t, f: (_query_block(lay, i, t), p)),
                lat, ctx, lat, ctx,
            ],
            out_specs=pl.BlockSpec((TQ, MLA_HPS * A_V), lambda i, p, t, f: (_query_block(lay, i, t), p)),
            scratch_shapes=[
                pltpu.VMEM((MLA_HPS, TQ, LANES), F32),
                pltpu.VMEM((MLA_HPS, TQ, HEAD_PAD), F32),
            ],
        ),
        out_shape=jax.ShapeDtypeStruct((lay.n_tok, A_HEADS * A_V), BF16),
        compiler_params=_params(("arbitrary", "arbitrary", "arbitrary")),
        name="mla_attn",
    )(bounded, q, k, k, v, v)


def _dft_kernel(cn_ref, sn_ref, cc_ref, sc_ref, fx_ref, fc_ref, y_ref):
    is_ctx = pl.program_id(1) == pl.num_programs(1) - 1

    def run(c_ref, s_ref, f_ref):
        y = (jnp.dot(c_ref[...], f_ref[:, 0:F_WIDTH], preferred_element_type=F32)
             - jnp.dot(s_ref[...], f_ref[:, F_WIDTH:2 * F_WIDTH], preferred_element_type=F32))
        y_ref[...] = y.astype(BF16)

    @pl.when(is_ctx)
    def _():
        run(cc_ref, sc_ref, fc_ref)

    @pl.when(jnp.logical_not(is_ctx))
    def _():
        run(cn_ref, sn_ref, fx_ref)


def _dft(lay, fcs, cn, sn, cc, sc):
    nq = lay.s // TQ
    table = pl.BlockSpec((TQ, lay.s), lambda i, t: (jnp.minimum(t, nq - 1), 0))
    return pl.pallas_call(
        _dft_kernel,
        grid=(lay.b, nq + 1),
        in_specs=[
            table, table, _resident(cc), _resident(sc),
            pl.BlockSpec((lay.s, 2 * F_WIDTH), lambda i, t: (i, 0)),
            pl.BlockSpec((lay.l, 2 * F_WIDTH), lambda i, t: (lay.ctx_block(i), 0)),
        ],
        out_specs=pl.BlockSpec((TQ, F_WIDTH), lambda i, t: (_query_block(lay, i, t), 0)),
        out_shape=jax.ShapeDtypeStruct((lay.n_tok, F_WIDTH), BF16),
        compiler_params=_params(("arbitrary", "arbitrary")),
        name="fourier_dft",
    )(cn, sn, cc, sc, fcs, fcs)


def _mixed_residual(mix_refs, widths, w_ref, h_ref, mod_ref):
    acc = None
    off = 0
    for m_ref, wd in zip(mix_refs, widths):
        part = jnp.dot(m_ref[...], w_ref[off:off + wd, :], preferred_element_type=F32)
        acc = part if acc is None else acc + part
        off += wd
    return h_ref[...] + mod_ref[2:3, :] * acc


def _mix_specs(lay, mixes, w, d):
    return [lay.tile(m.shape[-1]) for m in mixes] + [_resident(w), lay.tile(d), lay.mod_spec(d)]


def _proj_ffn_kernel(*refs, widths):
    nm = len(widths)
    w_ref, h_ref, mod_ref, g_ref, wg_ref, wu_ref, wd_ref, o_ref = refs[nm:]
    h1 = _mixed_residual(refs[:nm], widths, w_ref, h_ref, mod_ref)
    xn = _norm_mod(h1, g_ref, mod_ref, 3).astype(BF16)
    ff = wg_ref.shape[1]
    y = None
    for c0 in range(0, ff, FFN_CHUNK):
        c1 = min(c0 + FFN_CHUNK, ff)
        gt = jnp.dot(xn, wg_ref[:, c0:c1], preferred_element_type=F32)
        up = jnp.dot(xn, wu_ref[:, c0:c1], preferred_element_type=F32)
        a = (_silu(gt) * up).astype(BF16)
        part = jnp.dot(a, wd_ref[c0:c1, :], preferred_element_type=F32)
        y = part if y is None else y + part
    o_ref[...] = h1 + mod_ref[5:6, :] * y


def _proj_ffn(lay, mixes, w_out, h, mod, g, wg, wu, wd):
    n, d = h.shape
    widths = tuple(m.shape[-1] for m in mixes)
    return pl.pallas_call(
        functools.partial(_proj_ffn_kernel, widths=widths),
        grid=(lay.all_tiles,),
        in_specs=_mix_specs(lay, mixes, w_out, d) + [_resident(g), _resident(wg), _resident(wu), _resident(wd)],
        out_specs=lay.tile(d),
        out_shape=jax.ShapeDtypeStruct((n, d), F32),
        compiler_params=_params(("arbitrary",)),
        name="proj_ffn",
    )(*mixes, w_out, h, mod, g, wg, wu, wd)


def _na_front_kernel(h_ref, mod_ref, g_ref, win_ref, gq_ref, gk_ref, q_ref, k_ref, v_ref):
    xn = _norm_mod(h_ref[...], g_ref, mod_ref, 0)
    px = jnp.dot(xn.astype(BF16), win_ref[...], preferred_element_type=F32)
    width = C_HEADS * C_DH
    lane = lax.broadcasted_iota(jnp.int32, (px.shape[0], LANES), 1)
    low = lane < C_DH
    scale = C_DH ** -0.5 * LOG2E

    def norm_pairs(base, g_ref_, out_ref, mult):
        gg = g_ref_[...]
        for j in range(width // LANES):
            x = px[:, base + j * LANES:base + (j + 1) * LANES]
            x2 = x * x
            s_all = jnp.sum(x2, axis=-1, keepdims=True)
            s_lo = jnp.sum(jnp.where(low, x2, 0.0), axis=-1, keepdims=True)
            ss = jnp.where(low, s_lo, s_all - s_lo)
            inv = lax.rsqrt(ss * (1.0 / C_DH) + EPS)
            if mult != 1.0:
                inv = inv * mult
            out_ref[:, j * LANES:(j + 1) * LANES] = (x * inv * gg).astype(BF16)

    norm_pairs(0, gq_ref, q_ref, scale)
    norm_pairs(width, gk_ref, k_ref, 1.0)
    v_ref[...] = px[:, 2 * width:].astype(BF16)


def _na_front(lay, h, mod, g, win, gq, gk):
    n, d = h.shape
    width = C_HEADS * C_DH
    return pl.pallas_call(
        _na_front_kernel,
        grid=(lay.all_tiles,),
        in_specs=[lay.tile(d), lay.mod_spec(d), _resident(g), _resident(win), _resident(gq), _resident(gk)],
        out_specs=[lay.tile(width)] * 3,
        out_shape=[jax.ShapeDtypeStruct((n, width), BF16)] * 3,
        compiler_params=_params(("arbitrary",)),
        name="na_front",
    )(h, mod, g, win, gq, gk)


def _na_attn_kernel(bounded_ref, qx_ref, qc_ref, kx_ref, kc_ref, vx_ref, vc_ref, planes_ref, shift_ref,
                    *refs, rows_n):
    out_refs, bias_ref = refs[:-1], refs[-1]
    ox_ref = out_refs[0]

    @pl.when(pl.program_id(1) == 0)
    def _():
        idx = _na_group_rows(rows_n)
        for p in range(idx.shape[0]):
            for hh in range(2):
                for qi in range(NA_GROUP):
                    for a in range(NA_WINDOW):
                        half = (a % 2) * GRID_W
                        bias_ref[p, hh, qi * GRID_W:(qi + 1) * GRID_W, a * GRID_W:(a + 1) * GRID_W] = (
                            planes_ref[hh, int(idx[p, qi, a]), :, half:half + GRID_W])

    qlen = NA_GROUP * GRID_W
    wlen = NA_WINDOW * GRID_W
    n_groups = rows_n // NA_GROUP
    lane_q = lax.broadcasted_iota(jnp.int32, (qlen, LANES), 1)
    kc = kc_ref[...]
    vc = vc_ref[...]

    if len(out_refs) > 1:
        qc = qc_ref[...]
        lane_c = lax.broadcasted_iota(jnp.int32, qc.shape, 1)
        outs = []
        for hh in range(2):
            sel = (lane_c < C_DH) if hh == 0 else (lane_c >= C_DH)
            outs.append(_softmax_pv(jnp.where(sel, qc, jnp.zeros_like(qc)), kc, vc))
        out_refs[1][...] = jnp.where(lane_c < C_DH, outs[0], outs[1]).astype(BF16)

    def group_body(g, carry, bounded):
        start = jnp.clip(g * NA_GROUP - WIN_R // 2, 0, rows_n - NA_WINDOW)
        pat = jnp.where(g == 0, 0, jnp.where(g == n_groups - 1, 2, 1))
        q0 = pl.multiple_of(g * qlen, qlen)
        k0 = pl.multiple_of(start * GRID_W, GRID_W)
        qr = qx_ref[pl.ds(q0, qlen), :]
        kw = kx_ref[pl.ds(k0, wlen), :]
        vw = vx_ref[pl.ds(k0, wlen), :]
        res = []
        for hh in range(2):
            sel = (lane_q < C_DH) if hh == 0 else (lane_q >= C_DH)
            qh = jnp.where(sel, qr, jnp.zeros_like(qr))
            s_loc = lax.dot_general(qh, kw, (((1,), (1,)), ((), ())),
                                    preferred_element_type=F32) + bias_ref[pat, hh]
            s_ctx = lax.dot_general(qh, kc, (((1,), (1,)), ((), ())), preferred_element_type=F32)
            if bounded:
                p_loc = jnp.exp2(s_loc)
                p_ctx = jnp.exp2(s_ctx - shift_ref[:, 0:1])
            else:
                m = jnp.maximum(jnp.max(s_loc, axis=-1, keepdims=True),
                                jnp.max(s_ctx, axis=-1, keepdims=True))
                p_loc = jnp.exp2(s_loc - m)
                p_ctx = jnp.exp2(s_ctx - m)
            l = jnp.sum(p_loc, axis=-1, keepdims=True) + jnp.sum(p_ctx, axis=-1, keepdims=True)
            o = (jnp.dot(p_loc.astype(BF16), vw, preferred_element_type=F32)
                 + jnp.dot(p_ctx.astype(BF16), vc, preferred_element_type=F32))
            res.append(o * (1.0 / l))
        ox_ref[pl.ds(q0, qlen), :] = jnp.where(lane_q < C_DH, res[0], res[1]).astype(BF16)
        return carry

    @pl.when(bounded_ref[0] != 0)
    def _():
        lax.fori_loop(0, n_groups, functools.partial(group_body, bounded=True), 0, unroll=2)

    @pl.when(bounded_ref[0] == 0)
    def _():
        lax.fori_loop(0, n_groups, functools.partial(group_body, bounded=False), 0)


def _na_attn(lay, bounded, q, k, v, planes, shift, need_ctx):
    width = q.shape[-1]
    hp = C_HEADS // 2
    lat = pl.BlockSpec((lay.s, LANES), lambda p, i, f: (i, p))
    ctx = pl.BlockSpec((lay.l, LANES), lambda p, i, f: (lay.ctx_block(i), p))
    out_specs = [pl.BlockSpec((lay.s, LANES), lambda p, i, f: (i, p))]
    out_shape = [jax.ShapeDtypeStruct((lay.n_lat, width), BF16)]
    if need_ctx:
        out_specs.append(pl.BlockSpec((lay.l, LANES), lambda p, i, f: (i, p)))
        out_shape.append(jax.ShapeDtypeStruct((lay.b * lay.l, width), BF16))
    n_pat = _na_group_rows(lay.s // GRID_W).shape[0]
    return pl.pallas_call(
        functools.partial(_na_attn_kernel, rows_n=lay.s // GRID_W),
        grid_spec=pltpu.PrefetchScalarGridSpec(
            num_scalar_prefetch=1,
            grid=(hp, lay.b),
            in_specs=[lat, ctx, lat, ctx, lat, ctx,
                      pl.BlockSpec((2,) + planes.shape[1:], lambda p, i, f: (p, 0, 0, 0)),
                      pl.BlockSpec(shift.shape, lambda p, i, f: (0, 0))],
            out_specs=out_specs,
            scratch_shapes=[pltpu.VMEM((n_pat, 2, NA_GROUP * GRID_W, NA_WINDOW * GRID_W), F32)],
        ),
        out_shape=out_shape,
        compiler_params=_params(("arbitrary", "arbitrary")),
        name="na_attn",
    )(bounded, q, q, k, k, v, v, planes, shift)


def _proj_router_kernel(*refs, widths):
    nm = len(widths)
    w_ref, h_ref, mod_ref, g_ref, wr_ref, h1_ref, xr_ref, route_ref = refs[nm:]
    h1 = _mixed_residual(refs[:nm], widths, w_ref, h_ref, mod_ref)
    h1_ref[...] = h1
    xn = _norm_mod(h1, g_ref, mod_ref, 3)
    d = xn.shape[-1]
    xr_ref[:, 0:d] = xn
    logits = jnp.dot(xn, wr_ref[...], precision=lax.Precision.HIGHEST, preferred_element_type=F32)
    lane = lax.broadcasted_iota(jnp.int32, logits.shape, 1).astype(F32)
    lg = jnp.where(lane < N_EXPERTS, logits, -jnp.inf)
    m1 = jnp.max(lg, axis=-1, keepdims=True)
    i1 = jnp.min(jnp.where(lg == m1, lane, float(LANES)), axis=-1, keepdims=True)
    lg2 = jnp.where(lane == i1, -jnp.inf, lg)
    m2 = jnp.max(lg2, axis=-1, keepdims=True)
    i2 = jnp.min(jnp.where(lg2 == m2, lane, float(LANES)), axis=-1, keepdims=True)
    e2 = jnp.exp(m2 - m1)
    g1 = 1.0 / (1.0 + e2)
    g2 = e2 * g1
    first_low = i1 < i2
    vals = (jnp.minimum(i1, i2), jnp.maximum(i1, i2), jnp.where(first_low, g1, g2), jnp.where(first_low, g2, g1))
    route = jnp.zeros_like(lane)
    for idx, val in enumerate(vals):
        route = jnp.where(lane == float(idx), val, route)
    route_ref[...] = route
    xr_ref[:, d:d + LANES] = route


def _proj_router(lay, mixes, w_out, h, mod, g, wr, n_tiles):
    d = h.shape[-1]
    rows = n_tiles * lay.tm
    widths = tuple(m.shape[-1] for m in mixes)
    return pl.pallas_call(
        functools.partial(_proj_router_kernel, widths=widths),
        grid=(n_tiles,),
        in_specs=_mix_specs(lay, mixes, w_out, d) + [_resident(g), _resident(wr)],
        out_specs=[lay.tile(d), lay.tile(d + LANES), lay.tile(LANES)],
        out_shape=[jax.ShapeDtypeStruct((rows, d), F32), jax.ShapeDtypeStruct((rows, d + LANES), F32),
                   jax.ShapeDtypeStruct((rows, LANES), F32)],
        compiler_params=_params(("arbitrary",)),
        name="proj_router",
    )(*mixes, w_out, h, mod, g, wr)


def _moe_plan(route, n_tok):
    n_pairs = len(_PAIR_LO)
    n_tiles = n_tok // MOE_TM + n_pairs
    rows = n_tiles * MOE_TM
    lo = route[:, 0].astype(jnp.int32)
    hi = route[:, 1].astype(jnp.int32)
    pid = (lo * (2 * N_EXPERTS - 1 - lo)) // 2 + (hi - lo - 1)
    onehot = (pid[:, None] == jnp.arange(n_pairs, dtype=jnp.int32)[None, :]).astype(jnp.int32)
    csum = jnp.cumsum(onehot, axis=0)
    rank = jnp.sum(csum * onehot, axis=1) - 1
    counts = csum[-1]
    padded = ((counts + MOE_TM - 1) // MOE_TM) * MOE_TM
    gend = jnp.cumsum(padded)
    gstart = gend - padded
    dest = jnp.sum(gstart[None, :] * onehot, axis=1) + rank
    src = jnp.full((rows,), -1, jnp.int32).at[dest].set(jnp.arange(n_tok, dtype=jnp.int32), unique_indices=True)
    valid = src >= 0
    src_tok = jnp.where(valid, src, 0)
    n_valid = jnp.sum(valid.reshape(n_tiles, MOE_TM).astype(jnp.int32), axis=1)
    n_used = gend[-1] // MOE_TM
    tile_row = jnp.minimum(jnp.arange(n_tiles, dtype=jnp.int32), n_used - 1) * MOE_TM
    group = jnp.minimum(jnp.sum((gend[None, :] <= tile_row[:, None]).astype(jnp.int32), axis=1), n_pairs - 1)
    ea = jnp.asarray(_PAIR_LO, jnp.int32)[group]
    eb = jnp.asarray(_PAIR_HI, jnp.int32)[group]
    return ea, eb, src_tok, n_valid, n_used.astype(jnp.int32).reshape(1)


def _moe_kernel(ea_ref, eb_ref, src_ref, nvalid_ref, nused_ref,
                x_hbm, wga_ref, wua_ref, wda_ref, wgb_ref, wub_ref, wdb_ref,
                y_hbm, xbuf, ybuf, gsem, ssem):
    i = pl.program_id(0)
    n_steps = pl.num_programs(0)
    n_used = nused_ref[0]
    slot = i % 2

    def gather_copy(tile, slot_, r):
        return pltpu.make_async_copy(x_hbm.at[pl.ds(src_ref[tile * MOE_TM + r], 1)],
                                     xbuf.at[slot_, pl.ds(r, 1)], gsem.at[slot_])

    def scatter_copy(tile, slot_, r):
        return pltpu.make_async_copy(ybuf.at[slot_, pl.ds(r, 1)],
                                     y_hbm.at[pl.ds(src_ref[tile * MOE_TM + r], 1)], ssem.at[slot_])

    def start_gather(tile, slot_):
        for r in range(MOE_TM):
            gather_copy(tile, slot_, r).start()

    def wait_gather(slot_):
        pltpu.make_async_copy(x_hbm.at[pl.ds(0, MOE_TM)], xbuf.at[slot_], gsem.at[slot_]).wait()

    def start_scatter(tile, slot_):
        full = nvalid_ref[tile] == MOE_TM

        @pl.when(full)
        def _():
            for r in range(MOE_TM):
                scatter_copy(tile, slot_, r).start()

        @pl.when(jnp.logical_not(full))
        def _():
            def body(r, carry):
                scatter_copy(tile, slot_, r).start()
                return carry
            lax.fori_loop(0, nvalid_ref[tile], body, 0)

    def wait_scatter(tile, slot_):
        full = nvalid_ref[tile] == MOE_TM

        @pl.when(full)
        def _():
            pltpu.make_async_copy(ybuf.at[slot_], y_hbm.at[pl.ds(0, MOE_TM)], ssem.at[slot_]).wait()

        @pl.when(jnp.logical_not(full))
        def _():
            def body(r, carry):
                scatter_copy(tile, slot_, r).wait()
                return carry
            lax.fori_loop(0, nvalid_ref[tile], body, 0)

    @pl.when((i >= 2) & (i - 2 < n_used))
    def _():
        wait_scatter(i - 2, slot)

    @pl.when(i < n_used)
    def _():
        @pl.when(i == 0)
        def _():
            start_gather(0, 0)

        @pl.when(i + 1 < n_used)
        def _():
            start_gather(i + 1, 1 - slot)

        wait_gather(slot)
        d = ybuf.shape[-1]
        x = xbuf[slot, :, 0:d].astype(BF16)
        gates = xbuf[slot, :, d:d + LANES]

        def expert(wg_ref, wu_ref, wd_ref):
            gt = jnp.dot(x, wg_ref[...], preferred_element_type=F32)
            up = jnp.dot(x, wu_ref[...], preferred_element_type=F32)
            a = (_silu(gt) * up).astype(BF16)
            return jnp.dot(a, wd_ref[...], preferred_element_type=F32)

        ybuf[slot] = (gates[:, 2:3] * expert(wga_ref, wua_ref, wda_ref)
                      + gates[:, 3:4] * expert(wgb_ref, wub_ref, wdb_ref))
        start_scatter(i, slot)

    @pl.when(i == n_steps - 1)
    def _():
        for back in (2, 1):
            @pl.when(n_steps - back < n_used)
            def _():
                wait_scatter(n_steps - back, (n_steps - back) % 2)


def _moe(xr, route, wg, wu, wd):
    n_tok, dr = xr.shape
    _, d, f = wg.shape
    assert dr == d + LANES
    ea, eb, src_tok, n_valid, n_used = _moe_plan(route, n_tok)
    n_tiles = ea.shape[0]
    wspec = lambda shape, which: pl.BlockSpec(
        (None,) + shape, lambda i, ea_, eb_, s_, d_, u_: ((ea_, eb_)[which][i], 0, 0))
    return pl.pallas_call(
        _moe_kernel,
        grid_spec=pltpu.PrefetchScalarGridSpec(
            num_scalar_prefetch=5,
            grid=(n_tiles,),
            in_specs=[
                pl.BlockSpec(memory_space=pl.ANY),
                wspec((d, f), 0), wspec((d, f), 0), wspec((f, d), 0),
                wspec((d, f), 1), wspec((d, f), 1), wspec((f, d), 1),
            ],
            out_specs=pl.BlockSpec(memory_space=pl.ANY),
            scratch_shapes=[
                pltpu.VMEM((2, MOE_TM, dr), F32),
                pltpu.VMEM((2, MOE_TM, d), F32),
                pltpu.SemaphoreType.DMA((2,)),
                pltpu.SemaphoreType.DMA((2,)),
            ],
        ),
        out_shape=jax.ShapeDtypeStruct((n_tok, d), F32),
        compiler_params=_params(("arbitrary",)),
        name="moe_experts",
    )(ea, eb, src_tok, n_valid, n_used, xr, wg, wu, wd, wg, wu, wd)


def _resid_kernel(h_ref, y_ref, mod_ref, o_ref):
    o_ref[...] = h_ref[...] + mod_ref[5:6, :] * y_ref[...]


def _resid(lay, h, y, mod):
    n, d = h.shape
    return pl.pallas_call(
        _resid_kernel,
        grid=(n // lay.tm,),
        in_specs=[lay.tile(d), lay.tile(d), lay.mod_spec(d)],
        out_specs=lay.tile(d),
        out_shape=jax.ShapeDtypeStruct((n, d), F32),
        compiler_params=_params(("arbitrary",)),
        name="moe_residual",
    )(h, y, mod)


def _rope_tables(s, ctx_rows):
    t = jnp.arange(s)
    rows = (t // GRID_W).astype(F32)
    cols = (t % GRID_W).astype(F32)
    n_pairs = A_ROPE // 2
    per_axis = n_pairs // 2
    inv = ROPE_THETA ** (-jnp.arange(per_axis, dtype=F32) / per_axis)
    ang = jnp.concatenate([rows[:, None] * inv, cols[:, None] * inv], axis=-1)
    cos = jnp.repeat(jnp.cos(ang), 2, axis=-1)
    sin = jnp.repeat(jnp.sin(ang), 2, axis=-1)
    sign = jnp.tile(jnp.array([-1.0, 1.0], F32), n_pairs)
    pad = HEAD_PAD - A_QK
    t1 = jnp.concatenate([jnp.ones((s, A_NOPE), F32), cos, jnp.zeros((s, pad), F32)], axis=-1)
    t2 = jnp.concatenate([jnp.zeros((s, A_NOPE), F32), sin * sign, jnp.zeros((s, pad), F32)], axis=-1)
    c1 = jnp.concatenate([jnp.ones((ctx_rows, A_QK), F32), jnp.zeros((ctx_rows, pad), F32)], axis=-1)
    c2 = jnp.zeros((ctx_rows, HEAD_PAD), F32)
    return jnp.concatenate([t1, c1], axis=0), jnp.concatenate([t2, c2], axis=0)


_PAIR_SWAP = np.arange(A_ROPE) ^ 1


def _rope_lane_vec(g_tail, swapped):
    gt = g_tail[_PAIR_SWAP] if swapped else g_tail
    return jnp.concatenate([jnp.zeros((A_NOPE,), F32), gt, jnp.zeros((HEAD_PAD - A_QK,), F32)])[None]


def _mla_weights(w_in, w_uq, w_ukv, g_q, g_k):
    d = w_in.shape[0]
    q_end = A_Q_RANK
    kv_end = q_end + A_KV_RANK
    r_end = kv_end + A_ROPE
    z = lambda n: jnp.zeros((d, n), w_in.dtype)
    kr = w_in[:, kv_end:r_end]
    pad = HEAD_PAD - A_QK
    win = jnp.concatenate([w_in[:, :kv_end], z(A_NOPE), kr, z(pad), z(A_NOPE), kr[:, _PAIR_SWAP], z(pad),
                           w_in[:, r_end:]], axis=-1).astype(BF16)
    wq = w_uq.reshape(A_Q_RANK, A_HEADS, A_QK)
    zq = lambda n: jnp.zeros((A_Q_RANK, A_HEADS, n), wq.dtype)
    wq_main = jnp.concatenate([wq, zq(pad)], axis=-1)
    wq_swap = jnp.concatenate([zq(A_NOPE), wq[:, :, A_NOPE:][:, :, _PAIR_SWAP], zq(pad)], axis=-1)
    wq_ext = jnp.concatenate([wq_main.reshape(A_Q_RANK, -1), wq_swap.reshape(A_Q_RANK, -1)], axis=-1).astype(BF16)
    wkv = w_ukv.reshape(A_KV_RANK, A_HEADS, A_NOPE + A_V)
    zkv = jnp.zeros((A_KV_RANK, A_HEADS, HEAD_PAD - A_NOPE), wkv.dtype)
    wk = jnp.concatenate([wkv[:, :, :A_NOPE], zkv], axis=-1)
    wv = jnp.concatenate([wkv[:, :, A_NOPE:], zkv], axis=-1)
    wkv_ext = jnp.concatenate([wk.reshape(A_KV_RANK, -1), wv.reshape(A_KV_RANK, -1)], axis=-1).astype(BF16)

    bound = math.sqrt(A_QK) * jnp.max(jnp.abs(g_q)) * jnp.max(jnp.abs(g_k))
    bounded = bound <= MLA_SAFE_BOUND
    shift = jnp.where(bounded, -bound * LOG2E, 0.0)
    unit = lambda lane_idx: jnp.zeros((1, HEAD_PAD), F32).at[0, lane_idx].set(1.0)
    gq = jnp.concatenate([jnp.concatenate([g_q, jnp.zeros((pad,), F32)])[None], _rope_lane_vec(g_q[A_NOPE:], True),
                          unit(A_QK) * shift], axis=0)
    gk = jnp.concatenate([jnp.concatenate([g_k[:A_NOPE], jnp.zeros((HEAD_PAD - A_NOPE,), F32)])[None],
                          _rope_lane_vec(g_k[A_NOPE:], False), _rope_lane_vec(g_k[A_NOPE:], True),
                          unit(A_QK), unit(A_V)], axis=0)
    return win, wq_ext, wkv_ext, gq, gk, bounded.astype(jnp.int32).reshape(1)


def _dft_mats(n, norm):
    k = jnp.arange(n, dtype=jnp.int32)
    w = GRID_W if n % GRID_W == 0 and n > GRID_W else 1
    hi = jnp.arange(n // w, dtype=jnp.int32) * w
    lo = jnp.arange(w, dtype=jnp.int32)
    ang_hi = ((k[:, None] * hi[None, :]) % n).astype(F32) * (2.0 * math.pi / n)
    ang_lo = ((k[:, None] * lo[None, :]) % n).astype(F32) * (2.0 * math.pi / n)
    ch, sh = (jnp.repeat(f(ang_hi), w, axis=1) for f in (jnp.cos, jnp.sin))
    cl, sl = (jnp.tile(f(ang_lo), (1, n // w)) for f in (jnp.cos, jnp.sin))
    cos = ((ch * cl - sh * sl) * norm).astype(BF16)
    sin = ((sh * cl + ch * sl) * norm).astype(BF16)
    return cos, sin


def _channel_dft():
    c = np.arange(F_CH)
    ang = 2.0 * np.pi * ((c[:, None] * c[None, :]) % F_CH) / F_CH
    eye = np.eye(F_GROUPS)
    cb = np.kron(eye, np.cos(ang)) / math.sqrt(F_CH)
    sb = np.kron(eye, np.sin(ang)) / math.sqrt(F_CH)
    return jnp.asarray(np.concatenate([cb, sb], axis=1), BF16)


def _na_group_rows(rows_n):
    assert rows_n % NA_GROUP == 0 and rows_n >= NA_WINDOW + NA_GROUP
    n_groups = rows_n // NA_GROUP
    n_dr = 2 * WIN_R - 1
    idx = np.full((3, NA_GROUP, NA_WINDOW), n_dr, np.int32)
    seen = {}
    for g in range(n_groups):
        gs = int(np.clip(g * NA_GROUP - WIN_R // 2, 0, rows_n - NA_WINDOW))
        pat = 0 if g == 0 else (2 if g == n_groups - 1 else 1)
        cur = np.full((NA_GROUP, NA_WINDOW), n_dr, np.int32)
        for qi in range(NA_GROUP):
            r = g * NA_GROUP + qi
            start = int(np.clip(r - WIN_R // 2, 0, rows_n - WIN_R))
            assert gs <= start and start + WIN_R <= gs + NA_WINDOW
            for a in range(start, start + WIN_R):
                cur[qi, a - gs] = a - r + (WIN_R - 1)
        assert pat not in seen or np.array_equal(seen[pat], cur)
        seen[pat] = cur
        idx[pat] = cur
    return idx


def _na_score_bound(g_q, g_k, rpb):
    qk = math.sqrt(C_DH) * jnp.max(jnp.abs(g_q)) * jnp.max(jnp.abs(g_k))
    bound = qk + jnp.maximum(jnp.max(rpb), 0.0)
    lowest = -qk + jnp.minimum(jnp.min(rpb), 0.0)
    bounded = (bound - lowest) <= 2.0 * MLA_SAFE_BOUND
    shift = jnp.where(bounded, bound, 0.0).astype(F32)
    return bounded.astype(jnp.int32).reshape(1), shift


def _na_bias(rpb, shift):
    h, n_dr, n_dc = rpb.shape
    qc = np.arange(GRID_W)
    cs = np.clip(qc - WIN_C // 2, 0, GRID_W - WIN_C)
    kcol = np.arange(GRID_W)
    valid = (kcol[None, :] >= cs[:, None]) & (kcol[None, :] < cs[:, None] + WIN_C)
    dc = kcol[None, :] - qc[:, None] + (WIN_C - 1)
    onehot = (valid[:, :, None] & (dc[:, :, None] == np.arange(n_dc)[None, None, :])).astype(np.float32)
    col = jnp.einsum('hdc,qkc->hdqk', rpb.astype(F32), jnp.asarray(onehot), precision=lax.Precision.HIGHEST)
    col = jnp.where(jnp.asarray(valid)[None, None], (col - shift) * LOG2E, NEG_BIG)
    col = jnp.concatenate([col, jnp.full((h, 1, GRID_W, GRID_W), NEG_BIG, F32)], axis=1)
    return jnp.concatenate([col, col], axis=-1)


def kernel(x, c, ctx, c_ctx, w_mod, b_mod, norm_g, a_w_in, a_g_cq, a_g_ckv, a_w_uq, a_w_ukv, a_g_q, a_g_k, a_w_out,
           f_w_gate, f_w_up, f_w_down, c_w_in, c_g_q, c_g_k, c_rpb, c_w_out, m_w_router, m_w_gate, m_w_up, m_w_down):
    b, s, d = x.shape
    l = ctx.shape[1]
    depth = w_mod.shape[0]
    lay = _make_layout(b, s, l)

    mod_rows = 16
    cin = jnp.concatenate([c, c_ctx[None], jnp.zeros((mod_rows - b - 1, d), F32)], axis=0)
    mod_all = _modulation(cin, w_mod, b_mod).reshape(depth, mod_rows, 6, d)

    t1, t2 = _rope_tables(s, lay.tm)
    cn, sn = _dft_mats(s, 1.0 / math.sqrt(s))
    cc, sc = _dft_mats(l, 1.0 / math.sqrt(l))
    cb = _channel_dft()

    h = jnp.concatenate([x.reshape(b * s, d), ctx.reshape(b * l, d)], axis=0)
    for i in range(depth):
        j = i // 2
        need_ctx = i < depth - 1
        mod = mod_all[i]
        g1 = norm_g[i, 0][None]
        g2 = norm_g[i, 1][None]
        if i % 2 == 0:
            win, wq_ext, wkv_ext, gq, gk, bounded = _mla_weights(a_w_in[j], a_w_uq[j], a_w_ukv[j],
                                                                 a_g_q[j], a_g_k[j])
            q, k, v, fcs = _mla_front(lay, h, mod, g1, win, a_g_cq[j][None], a_g_ckv[j][None], wq_ext, wkv_ext,
                                      t1, t2, gq, gk, cb)
            att = _mla_attn(lay, bounded, q, k, v)
            yf = _dft(lay, fcs, cn, sn, cc, sc)
            h = _proj_ffn(lay, [att, yf], a_w_out[j].astype(BF16), h, mod, g2,
                          f_w_gate[j].astype(BF16), f_w_up[j].astype(BF16), f_w_down[j].astype(BF16))
        else:
            gq2 = jnp.tile(c_g_q[j], 2)[None]
            gk2 = jnp.tile(c_g_k[j], 2)[None]
            q, k, v = _na_front(lay, h, mod, g1, c_w_in[j].astype(BF16), gq2, gk2)
            na_bounded, na_shift = _na_score_bound(c_g_q[j], c_g_k[j], c_rpb[j])
            bias = _na_bias(c_rpb[j], na_shift)
            outs = _na_attn(lay, na_bounded, q, k, v, bias, jnp.full((1, LANES), na_shift * LOG2E, F32), need_ctx)
            att = jnp.concatenate(outs, axis=0) if need_ctx else outs[0]
            wr = jnp.concatenate([m_w_router[j], jnp.zeros((d, LANES - N_EXPERTS), F32)], axis=-1)
            n_tiles = lay.all_tiles if need_ctx else lay.lat_tiles
            h, xn2, route = _proj_router(lay, [att], c_w_out[j].astype(BF16), h, mod, g2, wr, n_tiles)
            y = _moe(xn2, route, m_w_gate[j].astype(BF16), m_w_up[j].astype(BF16), m_w_down[j].astype(BF16))
            h = _resid(lay, h, y, mod)
    return h[:lay.n_lat].reshape(b, s, d)
```

```python
import functools
import math
from typing import NamedTuple

import jax
import jax.numpy as jnp
import numpy as np
from jax import lax
from jax.experimental import pallas as pl
from jax.experimental.pallas import tpu as pltpu

F32 = jnp.float32
BF16 = jnp.bfloat16

GRID_W = 64
A_HEADS = 12
A_NOPE = 64
A_ROPE = 32
A_QK = A_NOPE + A_ROPE
A_V = 64
A_Q_RANK = 256
A_KV_RANK = 128
F_GROUPS = 4
F_CH = 64
F_WIDTH = F_GROUPS * F_CH
C_HEADS = 16
C_DH = 64
WIN_R = 8
WIN_C = 16
N_EXPERTS = 8
ROPE_THETA = 10000.0
EPS = 1e-6

LANES = 128
HEAD_PAD = 128
TM = 512
TQ = 256
FFN_CHUNK = 1408
MOE_TM = 256
_PAIR_LO = tuple(lo for lo in range(N_EXPERTS) for hi in range(lo + 1, N_EXPERTS))
_PAIR_HI = tuple(hi for lo in range(N_EXPERTS) for hi in range(lo + 1, N_EXPERTS))
LOG2E = math.log2(math.e)
MLA_SAFE_BOUND = 40.0
MLA_KCHUNK = 512
MLA_HPS = 6
NA_GROUP = 4
NA_WINDOW = NA_GROUP + WIN_R
VMEM_LIMIT = 52 * 1024 * 1024
NEG_BIG = -1e30


def _params(sem, vmem=VMEM_LIMIT):
    return pltpu.CompilerParams(dimension_semantics=sem, vmem_limit_bytes=vmem)


def _rms(x, g, n=None):
    n = x.shape[-1] if n is None else n
    ss = jnp.sum(x * x, axis=-1, keepdims=True)
    return x * lax.rsqrt(ss * (1.0 / n) + EPS) * g


def _silu(x):
    return x * (1.0 / (1.0 + jnp.exp(-x)))


class _Layout(NamedTuple):
    b: int
    s: int
    l: int
    tm: int

    @property
    def n_lat(self):
        return self.b * self.s

    @property
    def n_tok(self):
        return self.b * (self.s + self.l)

    @property
    def lat_tiles(self):
        return self.n_lat // self.tm

    @property
    def all_tiles(self):
        return self.n_tok // self.tm

    def tile(self, width):
        return pl.BlockSpec((self.tm, width), lambda t: (t, 0))

    def mod_spec(self, d):
        per_batch = self.s // self.tm
        return pl.BlockSpec((None, 6, d), lambda t: (jnp.where(t < self.lat_tiles, t // per_batch, self.b), 0, 0))

    def ctx_block(self, i):
        return self.n_lat // self.l + i


class _Rows(NamedTuple):
    lat: jax.Array
    ctx: jax.Array
    ctx_tile0: int

    @property
    def width(self):
        return self.lat.shape[-1]


def _flat_rows(lay, a):
    return _Rows(a, a, lay.lat_tiles)


def _row_specs(lay, rows):
    tm, nl, c0 = lay.tm, lay.lat_tiles, rows.ctx_tile0
    return [pl.BlockSpec((tm, rows.width), lambda t: (jnp.minimum(t, nl - 1), 0)),
            pl.BlockSpec((tm, rows.width), lambda t: (jnp.maximum(t - nl, 0) + c0, 0))]


def _row_tile(lat_ref, ctx_ref, lat_tiles):
    return jnp.where(pl.program_id(0) < lat_tiles, lat_ref[...], ctx_ref[...])


def _make_layout(b, s, l):
    tm = TM if (s % TM == 0 and (b * l) % TM == 0) else l
    assert l == TQ and s % tm == 0 and (b * l) % tm == 0 and s % GRID_W == 0
    return _Layout(b, s, l, tm)


def _resident(a):
    return pl.BlockSpec(a.shape, lambda *_: (0,) * a.ndim, pipeline_mode=pl.Buffered(1))


def _norm_mod(x, g_ref, mod_ref, row):
    xn = _rms(x, g_ref[...])
    return xn * (1.0 + mod_ref[row + 1:row + 2, :]) + mod_ref[row:row + 1, :]


def _mod_kernel(c_ref, w_ref, b_ref, o_ref):
    s = _silu(c_ref[...])
    o_ref[...] = jnp.dot(s, w_ref[...], precision=lax.Precision.HIGHEST,
                         preferred_element_type=F32) + b_ref[...]


def _modulation(cin, w_mod, b_mod):
    depth, d, n = w_mod.shape
    tn = 1536
    rows = cin.shape[0]
    return pl.pallas_call(
        _mod_kernel,
        grid=(depth, n // tn),
        in_specs=[
            pl.BlockSpec((rows, d), lambda l, j: (0, 0)),
            pl.BlockSpec((None, d, tn), lambda l, j: (l, 0, j)),
            pl.BlockSpec((None, 1, tn), lambda l, j: (l, 0, j)),
        ],
        out_specs=pl.BlockSpec((None, rows, tn), lambda l, j: (l, 0, j)),
        out_shape=jax.ShapeDtypeStruct((depth, rows, n), F32),
        compiler_params=_params(("arbitrary", "arbitrary")),
        name="modulation",
    )(cin, w_mod, b_mod.reshape(depth, 1, n))


def _mla_front_kernel(hl_ref, hc_ref, mod_ref, g_ref, win_ref, gcq_ref, gckv_ref, wq_ref, wkv_ref,
                      t1_ref, t2_ref, gq_ref, gk_ref, cb_ref,
                      q_ref, k_ref, v_ref, fcs_ref, *, lat_tiles):
    xn = _norm_mod(_row_tile(hl_ref, hc_ref, lat_tiles), g_ref, mod_ref, 0)
    px = jnp.dot(xn.astype(BF16), win_ref[...], preferred_element_type=F32)
    t1 = t1_ref[...]
    t2 = t2_ref[...]
    hw = A_HEADS * HEAD_PAD

    qn = _rms(px[:, 0:A_Q_RANK], gcq_ref[...]).astype(BF16)
    qq = jnp.dot(qn, wq_ref[...], preferred_element_type=F32)
    aq = t1 * gq_ref[0:1, :]
    bq = t2 * gq_ref[1:2, :]
    q_shift = gq_ref[2:3, :]
    scale = A_QK ** -0.5 * LOG2E
    for h in range(A_HEADS):
        qm = qq[:, h * HEAD_PAD:(h + 1) * HEAD_PAD]
        qs = qq[:, hw + h * HEAD_PAD:hw + (h + 1) * HEAD_PAD]
        ss = jnp.sum(qm * qm, axis=-1, keepdims=True)
        inv = lax.rsqrt(ss * (1.0 / A_QK) + EPS) * scale
        q_ref[:, h * HEAD_PAD:(h + 1) * HEAD_PAD] = ((qm * aq + qs * bq) * inv + q_shift).astype(BF16)

    kvn = _rms(px[:, A_Q_RANK:A_Q_RANK + A_KV_RANK], gckv_ref[...]).astype(BF16)
    kv = jnp.dot(kvn, wkv_ref[...], preferred_element_type=F32)
    krm = px[:, 384:512]
    krs = px[:, 512:640]
    tail = krm * (t1 * gk_ref[1:2, :]) + krs * (t2 * gk_ref[2:3, :])
    ssr = jnp.sum(krm * krm, axis=-1, keepdims=True)
    gkn = gk_ref[0:1, :]
    k_one = gk_ref[3:4, :]
    v_one = gk_ref[4:5, :]
    for h in range(A_HEADS):
        km = kv[:, h * HEAD_PAD:(h + 1) * HEAD_PAD]
        ss = jnp.sum(km * km, axis=-1, keepdims=True) + ssr
        inv = lax.rsqrt(ss * (1.0 / A_QK) + EPS)
        k_ref[:, h * HEAD_PAD:(h + 1) * HEAD_PAD] = ((km * gkn + tail) * inv + k_one).astype(BF16)
        v_ref[:, h * HEAD_PAD:(h + 1) * HEAD_PAD] = (kv[:, hw + h * HEAD_PAD:hw + (h + 1) * HEAD_PAD]
                                                     + v_one).astype(BF16)

    f = px[:, 640:896].astype(BF16)
    fcs_ref[...] = jnp.dot(f, cb_ref[...], preferred_element_type=F32).astype(BF16)


def _mla_front(lay, h, mod, g, win, gcq, gckv, wq, wkv, t1, t2, gq, gk, cb):
    n, d = lay.n_tok, h.width
    hw = A_HEADS * HEAD_PAD
    per_batch = lay.s // lay.tm
    rope = pl.BlockSpec((lay.tm, LANES), lambda t: (jnp.where(t < lay.lat_tiles, t % per_batch, per_batch), 0))
    return pl.pallas_call(
        functools.partial(_mla_front_kernel, lat_tiles=lay.lat_tiles),
        grid=(lay.all_tiles,),
        in_specs=_row_specs(lay, h) + [
            lay.mod_spec(d),
            _resident(g), _resident(win), _resident(gcq), _resident(gckv), _resident(wq), _resident(wkv),
            rope, rope,
            _resident(gq), _resident(gk), _resident(cb),
        ],
        out_specs=[lay.tile(hw), lay.tile(hw), lay.tile(hw), lay.tile(2 * F_WIDTH)],
        out_shape=[
            jax.ShapeDtypeStruct((n, hw), BF16),
            jax.ShapeDtypeStruct((n, hw), BF16),
            jax.ShapeDtypeStruct((n, hw), BF16),
            jax.ShapeDtypeStruct((n, 2 * F_WIDTH), BF16),
        ],
        compiler_params=_params(("arbitrary",)),
        name="mla_front",
    )(h.lat, h.ctx, mod, g, win, gcq, gckv, wq, wkv, t1, t2, gq, gk, cb)


def _softmax_pv(q, k, v):
    s = lax.dot_general(q, k, (((1,), (1,)), ((), ())), preferred_element_type=F32)
    m = jnp.max(s, axis=-1, keepdims=True)
    p = jnp.exp2(s - m)
    l = jnp.sum(p, axis=-1, keepdims=True)
    o = jnp.dot(p.astype(BF16), v, preferred_element_type=F32)
    return o * (1.0 / l)


def _mla_attn_kernel(bounded_ref, q_ref, kx_ref, kc_ref, vx_ref, vc_ref, o_ref, m_ref, acc_ref):
    t = pl.program_id(2)
    ck = MLA_KCHUNK
    n_chunks = kx_ref.shape[0] // ck

    def head(ref, rows=slice(None)):
        return lambda hh: ref[rows, hh * HEAD_PAD:(hh + 1) * HEAD_PAD]

    def scores(hh, k):
        return lax.dot_general(head(q_ref)(hh), k, (((1,), (1,)), ((), ())), preferred_element_type=F32)

    def bounded(with_latent):
        for hh in range(MLA_HPS):
            p = jnp.exp2(scores(hh, head(kc_ref)(hh))).astype(BF16)
            acc = jnp.dot(p, head(vc_ref)(hh), preferred_element_type=F32)
            if with_latent:
                p = jnp.exp2(scores(hh, head(kx_ref)(hh))).astype(BF16)
                acc = acc + jnp.dot(p, head(vx_ref)(hh), preferred_element_type=F32)
            acc_ref[hh] = acc

    def online(with_latent):
        for hh in range(MLA_HPS):
            s = scores(hh, head(kc_ref)(hh))
            m = jnp.max(s, axis=-1, keepdims=True)
            m_ref[hh] = jnp.broadcast_to(m, m_ref.shape[1:])
            acc_ref[hh] = jnp.dot(jnp.exp2(s - m).astype(BF16), head(vc_ref)(hh), preferred_element_type=F32)

        def body(c, carry):
            rows = pl.ds(pl.multiple_of(c * ck, ck), ck)
            for hh in range(MLA_HPS):
                s = scores(hh, head(kx_ref, rows)(hh))
                m_old = m_ref[hh]
                m_new = jnp.maximum(m_old, jnp.max(s, axis=-1, keepdims=True))
                p = jnp.exp2(s - jnp.tile(m_new, (1, ck // LANES))).astype(BF16)
                acc_ref[hh] = (jnp.exp2(m_old - m_new) * acc_ref[hh]
                               + jnp.dot(p, head(vx_ref, rows)(hh), preferred_element_type=F32))
                m_ref[hh] = m_new
            return carry

        if with_latent:
            lax.fori_loop(0, n_chunks, body, 0)

    is_ctx = t == pl.num_programs(2) - 1
    fast = bounded_ref[0] != 0

    @pl.when(fast & is_ctx)
    def _():
        bounded(False)

    @pl.when(fast & jnp.logical_not(is_ctx))
    def _():
        bounded(True)

    @pl.when(jnp.logical_not(fast) & is_ctx)
    def _():
        online(False)

    @pl.when(jnp.logical_not(fast) & jnp.logical_not(is_ctx))
    def _():
        online(True)

    lane = lax.broadcasted_iota(jnp.int32, (o_ref.shape[0], LANES), 1)
    for hp in range(MLA_HPS // 2):
        outs = []
        for hh in (2 * hp, 2 * hp + 1):
            acc = acc_ref[hh]
            outs.append(acc * (1.0 / acc[:, A_V:A_V + 1]))
        o_ref[:, hp * LANES:(hp + 1) * LANES] = jnp.where(
            lane < A_V, outs[0], pltpu.roll(outs[1], A_V, axis=1)).astype(BF16)


def _query_block(lay, i, t):
    nq = lay.s // TQ
    return jnp.where(t < nq, i * nq + t, lay.ctx_block(i))


def _mla_attn(lay, bounded, q, k, v):
    hp = A_HEADS // MLA_HPS
    assert lay.s % MLA_KCHUNK == 0
    wq = MLA_HPS * HEAD_PAD
    lat = pl.BlockSpec((lay.s, wq), lambda i, p, t, f: (i, p))
    ctx = pl.BlockSpec((lay.l, wq), lambda i, p, t, f: (lay.ctx_block(i), p))
    return pl.pallas_call(
        _mla_attn_kernel,
        grid_spec=pltpu.PrefetchScalarGridSpec(
            num_scalar_prefetch=1,
            grid=(lay.b, hp, lay.s // TQ + 1),
            in_specs=[
                pl.BlockSpec((TQ, wq), lambda i, p, t, f: (_query_block(lay, i, t), p)),
                lat, ctx, lat, ctx,
            ],
            out_specs=pl.BlockSpec((TQ, MLA_HPS * A_V), lambda i, p, t, f: (_query_block(lay, i, t), p)),
            scratch_shapes=[
                pltpu.VMEM((MLA_HPS, TQ, LANES), F32),
                pltpu.VMEM((MLA_HPS, TQ, HEAD_PAD), F32),
            ],
        ),
        out_shape=jax.ShapeDtypeStruct((lay.n_tok, A_HEADS * A_V), BF16),
        compiler_params=_params(("arbitrary", "arbitrary", "arbitrary")),
        name="mla_attn",
    )(bounded, q, k, k, v, v)


def _dft_kernel(cn_ref, sn_ref, cc_ref, sc_ref, fx_ref, fc_ref, y_ref):
    is_ctx = pl.program_id(1) == pl.num_programs(1) - 1

    def run(c_ref, s_ref, f_ref):
        y = (jnp.dot(c_ref[...], f_ref[:, 0:F_WIDTH], preferred_element_type=F32)
             - jnp.dot(s_ref[...], f_ref[:, F_WIDTH:2 * F_WIDTH], preferred_element_type=F32))
        y_ref[...] = y.astype(BF16)

    @pl.when(is_ctx)
    def _():
        run(cc_ref, sc_ref, fc_ref)

    @pl.when(jnp.logical_not(is_ctx))
    def _():
        run(cn_ref, sn_ref, fx_ref)


def _dft(lay, fcs, cn, sn, cc, sc):
    nq = lay.s // TQ
    table = pl.BlockSpec((TQ, lay.s), lambda i, t: (jnp.minimum(t, nq - 1), 0))
    return pl.pallas_call(
        _dft_kernel,
        grid=(lay.b, nq + 1),
        in_specs=[
            table, table, _resident(cc), _resident(sc),
            pl.BlockSpec((lay.s, 2 * F_WIDTH), lambda i, t: (i, 0)),
            pl.BlockSpec((lay.l, 2 * F_WIDTH), lambda i, t: (lay.ctx_block(i), 0)),
        ],
        out_specs=pl.BlockSpec((TQ, F_WIDTH), lambda i, t: (_query_block(lay, i, t), 0)),
        out_shape=jax.ShapeDtypeStruct((lay.n_tok, F_WIDTH), BF16),
        compiler_params=_params(("arbitrary", "arbitrary")),
        name="fourier_dft",
    )(cn, sn, cc, sc, fcs, fcs)


def _mixed_residual(row_refs, widths, w_ref, mod_ref, lat_tiles):
    tiles = [_row_tile(row_refs[2 * i], row_refs[2 * i + 1], lat_tiles) for i in range(len(widths) + 1)]
    acc = None
    off = 0
    for m, wd in zip(tiles[:-1], widths):
        part = jnp.dot(m, w_ref[off:off + wd, :], preferred_element_type=F32)
        acc = part if acc is None else acc + part
        off += wd
    return tiles[-1] + mod_ref[2:3, :] * acc


def _mix_specs(lay, mixes, w, h):
    specs = []
    for rows in list(mixes) + [h]:
        specs += _row_specs(lay, rows)
    return specs + [_resident(w), lay.mod_spec(h.width)]


def _mix_args(mixes, w, h, mod):
    args = []
    for rows in list(mixes) + [h]:
        args += [rows.lat, rows.ctx]
    return args + [w, mod]


def _proj_ffn_kernel(*refs, widths, lat_tiles):
    nr = 2 * (len(widths) + 1)
    w_ref, mod_ref, g_ref, wg_ref, wu_ref, wd_ref, o_ref = refs[nr:]
    h1 = _mixed_residual(refs[:nr], widths, w_ref, mod_ref, lat_tiles)
    xn = _norm_mod(h1, g_ref, mod_ref, 3).astype(BF16)
    ff = wg_ref.shape[1]
    y = None
    for c0 in range(0, ff, FFN_CHUNK):
        c1 = min(c0 + FFN_CHUNK, ff)
        gt = jnp.dot(xn, wg_ref[:, c0:c1], preferred_element_type=F32)
        up = jnp.dot(xn, wu_ref[:, c0:c1], preferred_element_type=F32)
        a = (_silu(gt) * up).astype(BF16)
        part = jnp.dot(a, wd_ref[c0:c1, :], preferred_element_type=F32)
        y = part if y is None else y + part
    o_ref[...] = h1 + mod_ref[5:6, :] * y


def _proj_ffn(lay, mixes, w_out, h, mod, g, wg, wu, wd):
    d = h.width
    widths = tuple(m.width for m in mixes)
    return pl.pallas_call(
        functools.partial(_proj_ffn_kernel, widths=widths, lat_tiles=lay.lat_tiles),
        grid=(lay.all_tiles,),
        in_specs=_mix_specs(lay, mixes, w_out, h) + [_resident(g), _resident(wg), _resident(wu), _resident(wd)],
        out_specs=lay.tile(d),
        out_shape=jax.ShapeDtypeStruct((lay.n_tok, d), F32),
        compiler_params=_params(("arbitrary",)),
        name="proj_ffn",
    )(*_mix_args(mixes, w_out, h, mod), g, wg, wu, wd)


def _na_front_kernel(h_ref, mod_ref, g_ref, win_ref, gq_ref, gk_ref, q_ref, k_ref, v_ref):
    xn = _norm_mod(h_ref[...], g_ref, mod_ref, 0)
    px = jnp.dot(xn.astype(BF16), win_ref[...], preferred_element_type=F32)
    width = C_HEADS * C_DH
    lane = lax.broadcasted_iota(jnp.int32, (px.shape[0], LANES), 1)
    low = lane < C_DH
    scale = C_DH ** -0.5 * LOG2E

    def norm_pairs(base, g_ref_, out_ref, mult):
        gg = g_ref_[...]
        for j in range(width // LANES):
            x = px[:, base + j * LANES:base + (j + 1) * LANES]
            x2 = x * x
            s_all = jnp.sum(x2, axis=-1, keepdims=True)
            s_lo = jnp.sum(jnp.where(low, x2, 0.0), axis=-1, keepdims=True)
            ss = jnp.where(low, s_lo, s_all - s_lo)
            inv = lax.rsqrt(ss * (1.0 / C_DH) + EPS)
            if mult != 1.0:
                inv = inv * mult
            out_ref[:, j * LANES:(j + 1) * LANES] = (x * inv * gg).astype(BF16)

    norm_pairs(0, gq_ref, q_ref, scale)
    norm_pairs(width, gk_ref, k_ref, 1.0)
    v_ref[...] = px[:, 2 * width:].astype(BF16)


def _na_front(lay, h, mod, g, win, gq, gk):
    n, d = h.shape
    width = C_HEADS * C_DH
    return pl.pallas_call(
        _na_front_kernel,
        grid=(lay.all_tiles,),
        in_specs=[lay.tile(d), lay.mod_spec(d), _resident(g), _resident(win), _resident(gq), _resident(gk)],
        out_specs=[lay.tile(width)] * 3,
        out_shape=[jax.ShapeDtypeStruct((n, width), BF16)] * 3,
        compiler_params=_params(("arbitrary",)),
        name="na_front",
    )(h, mod, g, win, gq, gk)


def _na_attn_kernel(bounded_ref, qx_ref, qc_ref, kx_ref, kc_ref, vx_ref, vc_ref, planes_ref, shift_ref,
                    *refs, rows_n):
    out_refs, bias_ref = refs[:-1], refs[-1]
    ox_ref = out_refs[0]

    @pl.when(pl.program_id(1) == 0)
    def _():
        idx = _na_group_rows(rows_n)
        for p in range(idx.shape[0]):
            for hh in range(2):
                for qi in range(NA_GROUP):
                    for a in range(NA_WINDOW):
                        half = (a % 2) * GRID_W
                        bias_ref[p, hh, qi * GRID_W:(qi + 1) * GRID_W, a * GRID_W:(a + 1) * GRID_W] = (
                            planes_ref[hh, int(idx[p, qi, a]), :, half:half + GRID_W])

    qlen = NA_GROUP * GRID_W
    wlen = NA_WINDOW * GRID_W
    n_groups = rows_n // NA_GROUP
    lane_q = lax.broadcasted_iota(jnp.int32, (qlen, LANES), 1)
    kc = kc_ref[...]
    vc = vc_ref[...]

    if len(out_refs) > 1:
        qc = qc_ref[...]
        lane_c = lax.broadcasted_iota(jnp.int32, qc.shape, 1)
        outs = []
        for hh in range(2):
            sel = (lane_c < C_DH) if hh == 0 else (lane_c >= C_DH)
            outs.append(_softmax_pv(jnp.where(sel, qc, jnp.zeros_like(qc)), kc, vc))
        out_refs[1][...] = jnp.where(lane_c < C_DH, outs[0], outs[1]).astype(BF16)

    def group_body(g, carry, bounded):
        start = jnp.clip(g * NA_GROUP - WIN_R // 2, 0, rows_n - NA_WINDOW)
        pat = jnp.where(g == 0, 0, jnp.where(g == n_groups - 1, 2, 1))
        q0 = pl.multiple_of(g * qlen, qlen)
        k0 = pl.multiple_of(start * GRID_W, GRID_W)
        qr = qx_ref[pl.ds(q0, qlen), :]
        kw = kx_ref[pl.ds(k0, wlen), :]
        vw = vx_ref[pl.ds(k0, wlen), :]
        res = []
        for hh in range(2):
            sel = (lane_q < C_DH) if hh == 0 else (lane_q >= C_DH)
            qh = jnp.where(sel, qr, jnp.zeros_like(qr))
            s_loc = lax.dot_general(qh, kw, (((1,), (1,)), ((), ())),
                                    preferred_element_type=F32) + bias_ref[pat, hh]
            s_ctx = lax.dot_general(qh, kc, (((1,), (1,)), ((), ())), preferred_element_type=F32)
            if bounded:
                p_loc = jnp.exp2(s_loc)
                p_ctx = jnp.exp2(s_ctx - shift_ref[:, 0:1])
            else:
                m = jnp.maximum(jnp.max(s_loc, axis=-1, keepdims=True),
                                jnp.max(s_ctx, axis=-1, keepdims=True))
                p_loc = jnp.exp2(s_loc - m)
                p_ctx = jnp.exp2(s_ctx - m)
            l = jnp.sum(p_loc, axis=-1, keepdims=True) + jnp.sum(p_ctx, axis=-1, keepdims=True)
            o = (jnp.dot(p_loc.astype(BF16), vw, preferred_element_type=F32)
                 + jnp.dot(p_ctx.astype(BF16), vc, preferred_element_type=F32))
            res.append(o * (1.0 / l))
        ox_ref[pl.ds(q0, qlen), :] = jnp.where(lane_q < C_DH, res[0], res[1]).astype(BF16)
        return carry

    @pl.when(bounded_ref[0] != 0)
    def _():
        lax.fori_loop(0, n_groups, functools.partial(group_body, bounded=True), 0, unroll=2)

    @pl.when(bounded_ref[0] == 0)
    def _():
        lax.fori_loop(0, n_groups, functools.partial(group_body, bounded=False), 0)


def _na_attn(lay, bounded, q, k, v, planes, shift, need_ctx):
    width = q.shape[-1]
    hp = C_HEADS // 2
    lat = pl.BlockSpec((lay.s, LANES), lambda p, i, f: (i, p))
    ctx = pl.BlockSpec((lay.l, LANES), lambda p, i, f: (lay.ctx_block(i), p))
    out_specs = [pl.BlockSpec((lay.s, LANES), lambda p, i, f: (i, p))]
    out_shape = [jax.ShapeDtypeStruct((lay.n_lat, width), BF16)]
    if need_ctx:
        out_specs.append(pl.BlockSpec((lay.l, LANES), lambda p, i, f: (i, p)))
        out_shape.append(jax.ShapeDtypeStruct((lay.b * lay.l, width), BF16))
    n_pat = _na_group_rows(lay.s // GRID_W).shape[0]
    return pl.pallas_call(
        functools.partial(_na_attn_kernel, rows_n=lay.s // GRID_W),
        grid_spec=pltpu.PrefetchScalarGridSpec(
            num_scalar_prefetch=1,
            grid=(hp, lay.b),
            in_specs=[lat, ctx, lat, ctx, lat, ctx,
                      pl.BlockSpec((2,) + planes.shape[1:], lambda p, i, f: (p, 0, 0, 0)),
                      pl.BlockSpec(shift.shape, lambda p, i, f: (0, 0))],
            out_specs=out_specs,
            scratch_shapes=[pltpu.VMEM((n_pat, 2, NA_GROUP * GRID_W, NA_WINDOW * GRID_W), F32)],
        ),
        out_shape=out_shape,
        compiler_params=_params(("arbitrary", "arbitrary")),
        name="na_attn",
    )(bounded, q, q, k, k, v, v, planes, shift)


def _proj_router_kernel(*refs, widths, lat_tiles):
    nr = 2 * (len(widths) + 1)
    w_ref, mod_ref, g_ref, wr_ref, h1_ref, xr_ref, route_ref = refs[nr:]
    h1 = _mixed_residual(refs[:nr], widths, w_ref, mod_ref, lat_tiles)
    h1_ref[...] = h1
    xn = _norm_mod(h1, g_ref, mod_ref, 3)
    d = xn.shape[-1]
    xr_ref[:, 0:d] = xn
    wr = wr_ref[...]
    x_hi, w_hi = xn.astype(BF16), wr.astype(BF16)
    x_lo = (xn - x_hi.astype(F32)).astype(BF16)
    w_lo = (wr - w_hi.astype(F32)).astype(BF16)
    logits = (jnp.dot(x_hi, w_hi, preferred_element_type=F32) + jnp.dot(x_lo, w_hi, preferred_element_type=F32)
              + jnp.dot(x_hi, w_lo, preferred_element_type=F32))
    lane = lax.broadcasted_iota(jnp.int32, logits.shape, 1).astype(F32)
    lg = jnp.where(lane < N_EXPERTS, logits, -jnp.inf)
    m1 = jnp.max(lg, axis=-1, keepdims=True)
    i1 = jnp.min(jnp.where(lg == m1, lane, float(LANES)), axis=-1, keepdims=True)
    lg2 = jnp.where(lane == i1, -jnp.inf, lg)
    m2 = jnp.max(lg2, axis=-1, keepdims=True)
    i2 = jnp.min(jnp.where(lg2 == m2, lane, float(LANES)), axis=-1, keepdims=True)
    e2 = jnp.exp(m2 - m1)
    g1 = 1.0 / (1.0 + e2)
    g2 = e2 * g1
    first_low = i1 < i2
    vals = (jnp.minimum(i1, i2), jnp.maximum(i1, i2), jnp.where(first_low, g1, g2), jnp.where(first_low, g2, g1))
    route = jnp.zeros_like(lane)
    for idx, val in enumerate(vals):
        route = jnp.where(lane == float(idx), val, route)
    route_ref[...] = route
    xr_ref[:, d:d + LANES] = route


def _proj_router(lay, mixes, w_out, h, mod, g, wr, n_tiles):
    d = h.width
    rows = n_tiles * lay.tm
    widths = tuple(m.width for m in mixes)
    return pl.pallas_call(
        functools.partial(_proj_router_kernel, widths=widths, lat_tiles=lay.lat_tiles),
        grid=(n_tiles,),
        in_specs=_mix_specs(lay, mixes, w_out, h) + [_resident(g), _resident(wr)],
        out_specs=[lay.tile(d), lay.tile(d + LANES), lay.tile(LANES)],
        out_shape=[jax.ShapeDtypeStruct((rows, d), F32), jax.ShapeDtypeStruct((rows, d + LANES), F32),
                   jax.ShapeDtypeStruct((rows, LANES), F32)],
        compiler_params=_params(("arbitrary",)),
        name="proj_router",
    )(*_mix_args(mixes, w_out, h, mod), g, wr)


def _moe_plan(route, n_tok):
    n_pairs = len(_PAIR_LO)
    n_tiles = n_tok // MOE_TM + n_pairs
    rows = n_tiles * MOE_TM
    lo = route[:, 0].astype(jnp.int32)
    hi = route[:, 1].astype(jnp.int32)
    pid = (lo * (2 * N_EXPERTS - 1 - lo)) // 2 + (hi - lo - 1)
    onehot = (pid[:, None] == jnp.arange(n_pairs, dtype=jnp.int32)[None, :]).astype(jnp.int32)
    csum = jnp.cumsum(onehot, axis=0)
    rank = jnp.sum(csum * onehot, axis=1) - 1
    counts = csum[-1]
    padded = ((counts + MOE_TM - 1) // MOE_TM) * MOE_TM
    gend = jnp.cumsum(padded)
    gstart = gend - padded
    dest = jnp.sum(gstart[None, :] * onehot, axis=1) + rank
    src = jnp.full((rows,), -1, jnp.int32).at[dest].set(jnp.arange(n_tok, dtype=jnp.int32), unique_indices=True)
    valid = src >= 0
    src_tok = jnp.where(valid, src, 0)
    n_valid = jnp.sum(valid.reshape(n_tiles, MOE_TM).astype(jnp.int32), axis=1)
    n_used = gend[-1] // MOE_TM
    tile_row = jnp.minimum(jnp.arange(n_tiles, dtype=jnp.int32), n_used - 1) * MOE_TM
    group = jnp.minimum(jnp.sum((gend[None, :] <= tile_row[:, None]).astype(jnp.int32), axis=1), n_pairs - 1)
    ea = jnp.asarray(_PAIR_LO, jnp.int32)[group]
    eb = jnp.asarray(_PAIR_HI, jnp.int32)[group]
    return ea, eb, src_tok, n_valid, n_used.astype(jnp.int32).reshape(1)


def _moe_kernel(ea_ref, eb_ref, src_ref, nvalid_ref, nused_ref,
                x_hbm, wga_ref, wua_ref, wda_ref, wgb_ref, wub_ref, wdb_ref,
                y_hbm, xbuf, ybuf, gsem, ssem):
    i = pl.program_id(0)
    n_steps = pl.num_programs(0)
    n_used = nused_ref[0]
    slot = i % 2

    def gather_copy(tile, slot_, r):
        return pltpu.make_async_copy(x_hbm.at[pl.ds(src_ref[tile * MOE_TM + r], 1)],
                                     xbuf.at[slot_, pl.ds(r, 1)], gsem.at[slot_])

    def scatter_copy(tile, slot_, r):
        return pltpu.make_async_copy(ybuf.at[slot_, pl.ds(r, 1)],
                                     y_hbm.at[pl.ds(src_ref[tile * MOE_TM + r], 1)], ssem.at[slot_])

    def start_gather(tile, slot_):
        for r in range(MOE_TM):
            gather_copy(tile, slot_, r).start()

    def wait_gather(slot_):
        pltpu.make_async_copy(x_hbm.at[pl.ds(0, MOE_TM)], xbuf.at[slot_], gsem.at[slot_]).wait()

    def start_scatter(tile, slot_):
        full = nvalid_ref[tile] == MOE_TM

        @pl.when(full)
        def _():
            for r in range(MOE_TM):
                scatter_copy(tile, slot_, r).start()

        @pl.when(jnp.logical_not(full))
        def _():
            def body(r, carry):
                scatter_copy(tile, slot_, r).start()
                return carry
            lax.fori_loop(0, nvalid_ref[tile], body, 0)

    def wait_scatter(tile, slot_):
        full = nvalid_ref[tile] == MOE_TM

        @pl.when(full)
        def _():
            pltpu.make_async_copy(ybuf.at[slot_], y_hbm.at[pl.ds(0, MOE_TM)], ssem.at[slot_]).wait()

        @pl.when(jnp.logical_not(full))
        def _():
            def body(r, carry):
                scatter_copy(tile, slot_, r).wait()
                return carry
            lax.fori_loop(0, nvalid_ref[tile], body, 0)

    @pl.when((i >= 2) & (i - 2 < n_used))
    def _():
        wait_scatter(i - 2, slot)

    @pl.when(i < n_used)
    def _():
        @pl.when(i == 0)
        def _():
            start_gather(0, 0)

        @pl.when(i + 1 < n_used)
        def _():
            start_gather(i + 1, 1 - slot)

        wait_gather(slot)
        d = ybuf.shape[-1]
        x = xbuf[slot, :, 0:d].astype(BF16)
        gates = xbuf[slot, :, d:d + LANES]

        def expert(wg_ref, wu_ref, wd_ref):
            gt = jnp.dot(x, wg_ref[...], preferred_element_type=F32)
            up = jnp.dot(x, wu_ref[...], preferred_element_type=F32)
            a = (_silu(gt) * up).astype(BF16)
            return jnp.dot(a, wd_ref[...], preferred_element_type=F32)

        ybuf[slot] = (gates[:, 2:3] * expert(wga_ref, wua_ref, wda_ref)
                      + gates[:, 3:4] * expert(wgb_ref, wub_ref, wdb_ref))
        start_scatter(i, slot)

    @pl.when(i == n_steps - 1)
    def _():
        for back in (2, 1):
            @pl.when(n_steps - back < n_used)
            def _():
                wait_scatter(n_steps - back, (n_steps - back) % 2)


def _moe(xr, route, wg, wu, wd, layer):
    n_tok, dr = xr.shape
    _, _, d, f = wg.shape
    assert dr == d + LANES
    ea, eb, src_tok, n_valid, n_used = _moe_plan(route, n_tok)
    n_tiles = ea.shape[0]
    wspec = lambda shape, which: pl.BlockSpec(
        (None, None) + shape, lambda i, ea_, eb_, s_, d_, u_: (layer, (ea_, eb_)[which][i], 0, 0))
    return pl.pallas_call(
        _moe_kernel,
        grid_spec=pltpu.PrefetchScalarGridSpec(
            num_scalar_prefetch=5,
            grid=(n_tiles,),
            in_specs=[
                pl.BlockSpec(memory_space=pl.ANY),
                wspec((d, f), 0), wspec((d, f), 0), wspec((f, d), 0),
                wspec((d, f), 1), wspec((d, f), 1), wspec((f, d), 1),
            ],
            out_specs=pl.BlockSpec(memory_space=pl.ANY),
            scratch_shapes=[
                pltpu.VMEM((2, MOE_TM, dr), F32),
                pltpu.VMEM((2, MOE_TM, d), F32),
                pltpu.SemaphoreType.DMA((2,)),
                pltpu.SemaphoreType.DMA((2,)),
            ],
        ),
        out_shape=jax.ShapeDtypeStruct((n_tok, d), F32),
        compiler_params=_params(("arbitrary",)),
        name="moe_experts",
    )(ea, eb, src_tok, n_valid, n_used, xr, wg, wu, wd, wg, wu, wd)


def _resid_kernel(h_ref, y_ref, mod_ref, o_ref):
    o_ref[...] = h_ref[...] + mod_ref[5:6, :] * y_ref[...]


def _resid(lay, h, y, mod):
    n, d = h.shape
    return pl.pallas_call(
        _resid_kernel,
        grid=(n // lay.tm,),
        in_specs=[lay.tile(d), lay.tile(d), lay.mod_spec(d)],
        out_specs=lay.tile(d),
        out_shape=jax.ShapeDtypeStruct((n, d), F32),
        compiler_params=_params(("arbitrary",)),
        name="moe_residual",
    )(h, y, mod)


def _rope_tables(s, ctx_rows):
    t = jnp.arange(s)
    rows = (t // GRID_W).astype(F32)
    cols = (t % GRID_W).astype(F32)
    n_pairs = A_ROPE // 2
    per_axis = n_pairs // 2
    inv = ROPE_THETA ** (-jnp.arange(per_axis, dtype=F32) / per_axis)
    ang = jnp.concatenate([rows[:, None] * inv, cols[:, None] * inv], axis=-1)
    cos = jnp.repeat(jnp.cos(ang), 2, axis=-1)
    sin = jnp.repeat(jnp.sin(ang), 2, axis=-1)
    sign = jnp.tile(jnp.array([-1.0, 1.0], F32), n_pairs)
    pad = HEAD_PAD - A_QK
    t1 = jnp.concatenate([jnp.ones((s, A_NOPE), F32), cos, jnp.zeros((s, pad), F32)], axis=-1)
    t2 = jnp.concatenate([jnp.zeros((s, A_NOPE), F32), sin * sign, jnp.zeros((s, pad), F32)], axis=-1)
    c1 = jnp.concatenate([jnp.ones((ctx_rows, A_QK), F32), jnp.zeros((ctx_rows, pad), F32)], axis=-1)
    c2 = jnp.zeros((ctx_rows, HEAD_PAD), F32)
    return jnp.concatenate([t1, c1], axis=0), jnp.concatenate([t2, c2], axis=0)


_PAIR_SWAP = np.arange(A_ROPE) ^ 1


def _rope_lane_vec(g_tail, swapped):
    gt = g_tail[_PAIR_SWAP] if swapped else g_tail
    return jnp.concatenate([jnp.zeros((A_NOPE,), F32), gt, jnp.zeros((HEAD_PAD - A_QK,), F32)])[None]


def _mla_weights(w_in, w_uq, w_ukv, g_q, g_k):
    d = w_in.shape[0]
    q_end = A_Q_RANK
    kv_end = q_end + A_KV_RANK
    r_end = kv_end + A_ROPE
    z = lambda n: jnp.zeros((d, n), w_in.dtype)
    kr = w_in[:, kv_end:r_end]
    pad = HEAD_PAD - A_QK
    win = jnp.concatenate([w_in[:, :kv_end], z(A_NOPE), kr, z(pad), z(A_NOPE), kr[:, _PAIR_SWAP], z(pad),
                           w_in[:, r_end:]], axis=-1).astype(BF16)
    wq = w_uq.reshape(A_Q_RANK, A_HEADS, A_QK)
    zq = lambda n: jnp.zeros((A_Q_RANK, A_HEADS, n), wq.dtype)
    wq_main = jnp.concatenate([wq, zq(pad)], axis=-1)
    wq_swap = jnp.concatenate([zq(A_NOPE), wq[:, :, A_NOPE:][:, :, _PAIR_SWAP], zq(pad)], axis=-1)
    wq_ext = jnp.concatenate([wq_main.reshape(A_Q_RANK, -1), wq_swap.reshape(A_Q_RANK, -1)], axis=-1).astype(BF16)
    wkv = w_ukv.reshape(A_KV_RANK, A_HEADS, A_NOPE + A_V)
    zkv = jnp.zeros((A_KV_RANK, A_HEADS, HEAD_PAD - A_NOPE), wkv.dtype)
    wk = jnp.concatenate([wkv[:, :, :A_NOPE], zkv], axis=-1)
    wv = jnp.concatenate([wkv[:, :, A_NOPE:], zkv], axis=-1)
    wkv_ext = jnp.concatenate([wk.reshape(A_KV_RANK, -1), wv.reshape(A_KV_RANK, -1)], axis=-1).astype(BF16)

    bound = math.sqrt(A_QK) * jnp.max(jnp.abs(g_q)) * jnp.max(jnp.abs(g_k))
    bounded = bound <= MLA_SAFE_BOUND
    shift = jnp.where(bounded, -bound * LOG2E, 0.0)
    unit = lambda lane_idx: jnp.zeros((1, HEAD_PAD), F32).at[0, lane_idx].set(1.0)
    gq = jnp.concatenate([jnp.concatenate([g_q, jnp.zeros((pad,), F32)])[None], _rope_lane_vec(g_q[A_NOPE:], True),
                          unit(A_QK) * shift], axis=0)
    gk = jnp.concatenate([jnp.concatenate([g_k[:A_NOPE], jnp.zeros((HEAD_PAD - A_NOPE,), F32)])[None],
                          _rope_lane_vec(g_k[A_NOPE:], False), _rope_lane_vec(g_k[A_NOPE:], True),
                          unit(A_QK), unit(A_V)], axis=0)
    return win, wq_ext, wkv_ext, gq, gk, bounded.astype(jnp.int32).reshape(1)


def _dft_mats(n, norm):
    k = jnp.arange(n, dtype=jnp.int32)
    w = GRID_W if n % GRID_W == 0 and n > GRID_W else 1
    hi = jnp.arange(n // w, dtype=jnp.int32) * w
    lo = jnp.arange(w, dtype=jnp.int32)
    ang_hi = ((k[:, None] * hi[None, :]) % n).astype(F32) * (2.0 * math.pi / n)
    ang_lo = ((k[:, None] * lo[None, :]) % n).astype(F32) * (2.0 * math.pi / n)
    ch, sh = (jnp.repeat(f(ang_hi), w, axis=1) for f in (jnp.cos, jnp.sin))
    cl, sl = (jnp.tile(f(ang_lo), (1, n // w)) for f in (jnp.cos, jnp.sin))
    cos = ((ch * cl - sh * sl) * norm).astype(BF16)
    sin = ((sh * cl + ch * sl) * norm).astype(BF16)
    return cos, sin


def _channel_dft():
    c = np.arange(F_CH)
    ang = 2.0 * np.pi * ((c[:, None] * c[None, :]) % F_CH) / F_CH
    eye = np.eye(F_GROUPS)
    cb = np.kron(eye, np.cos(ang)) / math.sqrt(F_CH)
    sb = np.kron(eye, np.sin(ang)) / math.sqrt(F_CH)
    return jnp.asarray(np.concatenate([cb, sb], axis=1), BF16)


def _na_group_rows(rows_n):
    assert rows_n % NA_GROUP == 0 and rows_n >= NA_WINDOW + NA_GROUP
    n_groups = rows_n // NA_GROUP
    n_dr = 2 * WIN_R - 1
    idx = np.full((3, NA_GROUP, NA_WINDOW), n_dr, np.int32)
    seen = {}
    for g in range(n_groups):
        gs = int(np.clip(g * NA_GROUP - WIN_R // 2, 0, rows_n - NA_WINDOW))
        pat = 0 if g == 0 else (2 if g == n_groups - 1 else 1)
        cur = np.full((NA_GROUP, NA_WINDOW), n_dr, np.int32)
        for qi in range(NA_GROUP):
            r = g * NA_GROUP + qi
            start = int(np.clip(r - WIN_R // 2, 0, rows_n - WIN_R))
            assert gs <= start and start + WIN_R <= gs + NA_WINDOW
            for a in range(start, start + WIN_R):
                cur[qi, a - gs] = a - r + (WIN_R - 1)
        assert pat not in seen or np.array_equal(seen[pat], cur)
        seen[pat] = cur
        idx[pat] = cur
    return idx


def _na_score_bound(g_q, g_k, rpb):
    qk = math.sqrt(C_DH) * jnp.max(jnp.abs(g_q)) * jnp.max(jnp.abs(g_k))
    bound = qk + jnp.maximum(jnp.max(rpb), 0.0)
    lowest = -qk + jnp.minimum(jnp.min(rpb), 0.0)
    bounded = (bound - lowest) <= 2.0 * MLA_SAFE_BOUND
    shift = jnp.where(bounded, bound, 0.0).astype(F32)
    return bounded.astype(jnp.int32).reshape(1), shift


def _na_bias(rpb, shift):
    h, n_dr, n_dc = rpb.shape
    qc = np.arange(GRID_W)
    cs = np.clip(qc - WIN_C // 2, 0, GRID_W - WIN_C)
    kcol = np.arange(GRID_W)
    valid = (kcol[None, :] >= cs[:, None]) & (kcol[None, :] < cs[:, None] + WIN_C)
    dc = kcol[None, :] - qc[:, None] + (WIN_C - 1)
    onehot = (valid[:, :, None] & (dc[:, :, None] == np.arange(n_dc)[None, None, :])).astype(np.float32)
    col = jnp.einsum('hdc,qkc->hdqk', rpb.astype(F32), jnp.asarray(onehot), precision=lax.Precision.HIGHEST)
    col = jnp.where(jnp.asarray(valid)[None, None], (col - shift) * LOG2E, NEG_BIG)
    col = jnp.concatenate([col, jnp.full((h, 1, GRID_W, GRID_W), NEG_BIG, F32)], axis=1)
    return jnp.concatenate([col, col], axis=-1)


def kernel(x, c, ctx, c_ctx, w_mod, b_mod, norm_g, a_w_in, a_g_cq, a_g_ckv, a_w_uq, a_w_ukv, a_g_q, a_g_k, a_w_out,
           f_w_gate, f_w_up, f_w_down, c_w_in, c_g_q, c_g_k, c_rpb, c_w_out, m_w_router, m_w_gate, m_w_up, m_w_down):
    b, s, d = x.shape
    l = ctx.shape[1]
    depth = w_mod.shape[0]
    lay = _make_layout(b, s, l)

    mod_rows = 16
    cin = jnp.concatenate([c, c_ctx[None], jnp.zeros((mod_rows - b - 1, d), F32)], axis=0)
    mod_all = _modulation(cin, w_mod, b_mod).reshape(depth, mod_rows, 6, d)

    t1, t2 = _rope_tables(s, lay.tm)
    cn, sn = _dft_mats(s, 1.0 / math.sqrt(s))
    cc, sc = _dft_mats(l, 1.0 / math.sqrt(l))
    cb = _channel_dft()

    moe_wg, moe_wu, moe_wd = m_w_gate.astype(BF16), m_w_up.astype(BF16), m_w_down.astype(BF16)
    h = _Rows(x.reshape(b * s, d), ctx.reshape(b * l, d), 0)
    for i in range(depth):
        j = i // 2
        need_ctx = i < depth - 1
        mod = mod_all[i]
        g1 = norm_g[i, 0][None]
        g2 = norm_g[i, 1][None]
        if i % 2 == 0:
            win, wq_ext, wkv_ext, gq, gk, bounded = _mla_weights(a_w_in[j], a_w_uq[j], a_w_ukv[j],
                                                                 a_g_q[j], a_g_k[j])
            q, k, v, fcs = _mla_front(lay, h, mod, g1, win, a_g_cq[j][None], a_g_ckv[j][None], wq_ext, wkv_ext,
                                      t1, t2, gq, gk, cb)
            att = _mla_attn(lay, bounded, q, k, v)
            yf = _dft(lay, fcs, cn, sn, cc, sc)
            h = _proj_ffn(lay, [_flat_rows(lay, att), _flat_rows(lay, yf)], a_w_out[j].astype(BF16), h, mod, g2,
                          f_w_gate[j].astype(BF16), f_w_up[j].astype(BF16), f_w_down[j].astype(BF16))
            h = _flat_rows(lay, h)
        else:
            gq2 = jnp.tile(c_g_q[j], 2)[None]
            gk2 = jnp.tile(c_g_k[j], 2)[None]
            q, k, v = _na_front(lay, h.lat, mod, g1, c_w_in[j].astype(BF16), gq2, gk2)
            na_bounded, na_shift = _na_score_bound(c_g_q[j], c_g_k[j], c_rpb[j])
            bias = _na_bias(c_rpb[j], na_shift)
            outs = _na_attn(lay, na_bounded, q, k, v, bias, jnp.full((1, LANES), na_shift * LOG2E, F32), need_ctx)
            att = _Rows(outs[0], outs[1] if need_ctx else outs[0], 0)
            wr = jnp.concatenate([m_w_router[j], jnp.zeros((d, LANES - N_EXPERTS), F32)], axis=-1)
            n_tiles = lay.all_tiles if need_ctx else lay.lat_tiles
            h1, xn2, route = _proj_router(lay, [att], c_w_out[j].astype(BF16), h, mod, g2, wr, n_tiles)
            y = _moe(xn2, route, moe_wg, moe_wu, moe_wd, j)
            h = _flat_rows(lay, _resid(lay, h1, y, mod))
    return h.lat[:lay.n_lat].reshape(b, s, d)
```

```python
import functools
import math
from typing import NamedTuple

import jax
import jax.numpy as jnp
import numpy as np
from jax import lax
from jax.experimental import pallas as pl
from jax.experimental.pallas import tpu as pltpu

F32 = jnp.float32
BF16 = jnp.bfloat16

GRID_W = 64
A_HEADS = 12
A_NOPE = 64
A_ROPE = 32
A_QK = A_NOPE + A_ROPE
A_V = 64
A_Q_RANK = 256
A_KV_RANK = 128
F_GROUPS = 4
F_CH = 64
F_WIDTH = F_GROUPS * F_CH
C_HEADS = 16
C_DH = 64
WIN_R = 8
WIN_C = 16
N_EXPERTS = 8
ROPE_THETA = 10000.0
EPS = 1e-6

LANES = 128
HEAD_PAD = 128
TM = 512
TQ = 256
FFN_CHUNK = 1408
MOE_TM = 256
_PAIR_LO = tuple(lo for lo in range(N_EXPERTS) for hi in range(lo + 1, N_EXPERTS))
_PAIR_HI = tuple(hi for lo in range(N_EXPERTS) for hi in range(lo + 1, N_EXPERTS))
LOG2E = math.log2(math.e)
MLA_SAFE_BOUND = 40.0
MLA_KCHUNK = 512
MLA_HPS = 6
NA_GROUP = 4
NA_WINDOW = NA_GROUP + WIN_R
VMEM_LIMIT = 52 * 1024 * 1024
NEG_BIG = -1e30


def _params(sem, vmem=VMEM_LIMIT):
    return pltpu.CompilerParams(dimension_semantics=sem, vmem_limit_bytes=vmem)


def _rms(x, g, n=None):
    n = x.shape[-1] if n is None else n
    ss = jnp.sum(x * x, axis=-1, keepdims=True)
    return x * lax.rsqrt(ss * (1.0 / n) + EPS) * g


def _silu(x):
    return x * (1.0 / (1.0 + jnp.exp(-x)))


class _Layout(NamedTuple):
    b: int
    s: int
    l: int
    tm: int

    @property
    def n_lat(self):
        return self.b * self.s

    @property
    def n_tok(self):
        return self.b * (self.s + self.l)

    @property
    def lat_tiles(self):
        return self.n_lat // self.tm

    @property
    def all_tiles(self):
        return self.n_tok // self.tm

    def tile(self, width):
        return pl.BlockSpec((self.tm, width), lambda t: (t, 0))

    def mod_spec(self, d):
        per_batch = self.s // self.tm
        return pl.BlockSpec((None, 6, d), lambda t: (jnp.where(t < self.lat_tiles, t // per_batch, self.b), 0, 0))

    def ctx_block(self, i):
        return self.n_lat // self.l + i


class _Rows(NamedTuple):
    lat: jax.Array
    ctx: jax.Array | None = None

    @property
    def width(self):
        return self.lat.shape[-1]

    @property
    def arrays(self):
        return [self.lat] if self.ctx is None else [self.lat, self.ctx]


def _row_specs(lay, rows):
    if rows.ctx is None:
        return [lay.tile(rows.width)]
    tm, nl = lay.tm, lay.lat_tiles
    return [pl.BlockSpec((tm, rows.width), lambda t: (jnp.minimum(t, nl - 1), 0)),
            pl.BlockSpec((tm, rows.width), lambda t: (jnp.maximum(t - nl, 0), 0))]


def _row_tiles(refs, arities, lat_tiles):
    tiles, pos = [], 0
    for arity in arities:
        if arity == 1:
            tiles.append(refs[pos][...])
        else:
            tiles.append(jnp.where(pl.program_id(0) < lat_tiles, refs[pos][...], refs[pos + 1][...]))
        pos += arity
    return tiles


def _make_layout(b, s, l):
    tm = TM if (s % TM == 0 and (b * l) % TM == 0) else l
    assert l == TQ and s % tm == 0 and (b * l) % tm == 0 and s % GRID_W == 0
    return _Layout(b, s, l, tm)


def _resident(a):
    return pl.BlockSpec(a.shape, lambda *_: (0,) * a.ndim, pipeline_mode=pl.Buffered(1))


def _norm_mod(x, g_ref, mod_ref, row):
    xn = _rms(x, g_ref[...])
    return xn * (1.0 + mod_ref[row + 1:row + 2, :]) + mod_ref[row:row + 1, :]


def _mod_kernel(c_ref, w_ref, b_ref, o_ref):
    s = _silu(c_ref[...])
    o_ref[...] = jnp.dot(s, w_ref[...], precision=lax.Precision.HIGHEST,
                         preferred_element_type=F32) + b_ref[...]


def _modulation(cin, w_mod, b_mod):
    depth, d, n = w_mod.shape
    tn = 1536
    rows = cin.shape[0]
    return pl.pallas_call(
        _mod_kernel,
        grid=(depth, n // tn),
        in_specs=[
            pl.BlockSpec((rows, d), lambda l, j: (0, 0)),
            pl.BlockSpec((None, d, tn), lambda l, j: (l, 0, j)),
            pl.BlockSpec((None, 1, tn), lambda l, j: (l, 0, j)),
        ],
        out_specs=pl.BlockSpec((None, rows, tn), lambda l, j: (l, 0, j)),
        out_shape=jax.ShapeDtypeStruct((depth, rows, n), F32),
        compiler_params=_params(("arbitrary", "arbitrary")),
        name="modulation",
    )(cin, w_mod, b_mod.reshape(depth, 1, n))


def _mla_front_kernel(*refs, h_arity, lat_tiles):
    (mod_ref, g_ref, win_ref, gcq_ref, gckv_ref, wq_ref, wkv_ref, t1_ref, t2_ref, gq_ref, gk_ref, cb_ref,
     q_ref, k_ref, v_ref, fcs_ref) = refs[h_arity:]
    (h,) = _row_tiles(refs, (h_arity,), lat_tiles)
    xn = _norm_mod(h, g_ref, mod_ref, 0)
    px = jnp.dot(xn.astype(BF16), win_ref[...], preferred_element_type=F32)
    t1 = t1_ref[...]
    t2 = t2_ref[...]
    hw = A_HEADS * HEAD_PAD

    qn = _rms(px[:, 0:A_Q_RANK], gcq_ref[...]).astype(BF16)
    qq = jnp.dot(qn, wq_ref[...], preferred_element_type=F32)
    aq = t1 * gq_ref[0:1, :]
    bq = t2 * gq_ref[1:2, :]
    q_shift = gq_ref[2:3, :]
    scale = A_QK ** -0.5 * LOG2E
    for h in range(A_HEADS):
        qm = qq[:, h * HEAD_PAD:(h + 1) * HEAD_PAD]
        qs = qq[:, hw + h * HEAD_PAD:hw + (h + 1) * HEAD_PAD]
        ss = jnp.sum(qm * qm, axis=-1, keepdims=True)
        inv = lax.rsqrt(ss * (1.0 / A_QK) + EPS) * scale
        q_ref[:, h * HEAD_PAD:(h + 1) * HEAD_PAD] = ((qm * aq + qs * bq) * inv + q_shift).astype(BF16)

    kvn = _rms(px[:, A_Q_RANK:A_Q_RANK + A_KV_RANK], gckv_ref[...]).astype(BF16)
    kv = jnp.dot(kvn, wkv_ref[...], preferred_element_type=F32)
    krm = px[:, 384:512]
    krs = px[:, 512:640]
    tail = krm * (t1 * gk_ref[1:2, :]) + krs * (t2 * gk_ref[2:3, :])
    ssr = jnp.sum(krm * krm, axis=-1, keepdims=True)
    gkn = gk_ref[0:1, :]
    k_one = gk_ref[3:4, :]
    v_one = gk_ref[4:5, :]
    for h in range(A_HEADS):
        km = kv[:, h * HEAD_PAD:(h + 1) * HEAD_PAD]
        ss = jnp.sum(km * km, axis=-1, keepdims=True) + ssr
        inv = lax.rsqrt(ss * (1.0 / A_QK) + EPS)
        k_ref[:, h * HEAD_PAD:(h + 1) * HEAD_PAD] = ((km * gkn + tail) * inv + k_one).astype(BF16)
        v_ref[:, h * HEAD_PAD:(h + 1) * HEAD_PAD] = (kv[:, hw + h * HEAD_PAD:hw + (h + 1) * HEAD_PAD]
                                                     + v_one).astype(BF16)

    f = px[:, 640:896].astype(BF16)
    fcs_ref[...] = jnp.dot(f, cb_ref[...], preferred_element_type=F32).astype(BF16)


def _mla_front(lay, h, mod, g, win, gcq, gckv, wq, wkv, t1, t2, gq, gk, cb):
    n, d = lay.n_tok, h.width
    hw = A_HEADS * HEAD_PAD
    per_batch = lay.s // lay.tm
    rope = pl.BlockSpec((lay.tm, LANES), lambda t: (jnp.where(t < lay.lat_tiles, t % per_batch, per_batch), 0))
    return pl.pallas_call(
        functools.partial(_mla_front_kernel, h_arity=len(h.arrays), lat_tiles=lay.lat_tiles),
        grid=(lay.all_tiles,),
        in_specs=_row_specs(lay, h) + [
            lay.mod_spec(d),
            _resident(g), _resident(win), _resident(gcq), _resident(gckv), _resident(wq), _resident(wkv),
            rope, rope,
            _resident(gq), _resident(gk), _resident(cb),
        ],
        out_specs=[lay.tile(hw), lay.tile(hw), lay.tile(hw), lay.tile(2 * F_WIDTH)],
        out_shape=[
            jax.ShapeDtypeStruct((n, hw), BF16),
            jax.ShapeDtypeStruct((n, hw), BF16),
            jax.ShapeDtypeStruct((n, hw), BF16),
            jax.ShapeDtypeStruct((n, 2 * F_WIDTH), BF16),
        ],
        compiler_params=_params(("arbitrary",)),
        name="mla_front",
    )(*h.arrays, mod, g, win, gcq, gckv, wq, wkv, t1, t2, gq, gk, cb)


def _softmax_pv(q, k, v):
    s = lax.dot_general(q, k, (((1,), (1,)), ((), ())), preferred_element_type=F32)
    m = jnp.max(s, axis=-1, keepdims=True)
    p = jnp.exp2(s - m)
    l = jnp.sum(p, axis=-1, keepdims=True)
    o = jnp.dot(p.astype(BF16), v, preferred_element_type=F32)
    return o * (1.0 / l)


def _mla_attn_kernel(bounded_ref, q_ref, kx_ref, kc_ref, vx_ref, vc_ref, o_ref, m_ref, acc_ref):
    t = pl.program_id(2)
    ck = MLA_KCHUNK
    n_chunks = kx_ref.shape[0] // ck

    def head(ref, rows=slice(None)):
        return lambda hh: ref[rows, hh * HEAD_PAD:(hh + 1) * HEAD_PAD]

    def scores(hh, k):
        return lax.dot_general(head(q_ref)(hh), k, (((1,), (1,)), ((), ())), preferred_element_type=F32)

    def bounded(with_latent):
        for hh in range(MLA_HPS):
            p = jnp.exp2(scores(hh, head(kc_ref)(hh))).astype(BF16)
            acc = jnp.dot(p, head(vc_ref)(hh), preferred_element_type=F32)
            if with_latent:
                p = jnp.exp2(scores(hh, head(kx_ref)(hh))).astype(BF16)
                acc = acc + jnp.dot(p, head(vx_ref)(hh), preferred_element_type=F32)
            acc_ref[hh] = acc

    def online(with_latent):
        for hh in range(MLA_HPS):
            s = scores(hh, head(kc_ref)(hh))
            m = jnp.max(s, axis=-1, keepdims=True)
            m_ref[hh] = jnp.broadcast_to(m, m_ref.shape[1:])
            acc_ref[hh] = jnp.dot(jnp.exp2(s - m).astype(BF16), head(vc_ref)(hh), preferred_element_type=F32)

        def body(c, carry):
            rows = pl.ds(pl.multiple_of(c * ck, ck), ck)
            for hh in range(MLA_HPS):
                s = scores(hh, head(kx_ref, rows)(hh))
                m_old = m_ref[hh]
                m_new = jnp.maximum(m_old, jnp.max(s, axis=-1, keepdims=True))
                p = jnp.exp2(s - jnp.tile(m_new, (1, ck // LANES))).astype(BF16)
                acc_ref[hh] = (jnp.exp2(m_old - m_new) * acc_ref[hh]
                               + jnp.dot(p, head(vx_ref, rows)(hh), preferred_element_type=F32))
                m_ref[hh] = m_new
            return carry

        if with_latent:
            lax.fori_loop(0, n_chunks, body, 0)

    is_ctx = t == pl.num_programs(2) - 1
    fast = bounded_ref[0] != 0

    @pl.when(fast & is_ctx)
    def _():
        bounded(False)

    @pl.when(fast & jnp.logical_not(is_ctx))
    def _():
        bounded(True)

    @pl.when(jnp.logical_not(fast) & is_ctx)
    def _():
        online(False)

    @pl.when(jnp.logical_not(fast) & jnp.logical_not(is_ctx))
    def _():
        online(True)

    lane = lax.broadcasted_iota(jnp.int32, (o_ref.shape[0], LANES), 1)
    for hp in range(MLA_HPS // 2):
        outs = []
        for hh in (2 * hp, 2 * hp + 1):
            acc = acc_ref[hh]
            outs.append(acc * (1.0 / acc[:, A_V:A_V + 1]))
        o_ref[:, hp * LANES:(hp + 1) * LANES] = jnp.where(
            lane < A_V, outs[0], pltpu.roll(outs[1], A_V, axis=1)).astype(BF16)


def _query_block(lay, i, t):
    nq = lay.s // TQ
    return jnp.where(t < nq, i * nq + t, lay.ctx_block(i))


def _mla_attn(lay, bounded, q, k, v):
    hp = A_HEADS // MLA_HPS
    assert lay.s % MLA_KCHUNK == 0
    wq = MLA_HPS * HEAD_PAD
    lat = pl.BlockSpec((lay.s, wq), lambda i, p, t, f: (i, p))
    ctx = pl.BlockSpec((lay.l, wq), lambda i, p, t, f: (lay.ctx_block(i), p))
    return pl.pallas_call(
        _mla_attn_kernel,
        grid_spec=pltpu.PrefetchScalarGridSpec(
            num_scalar_prefetch=1,
            grid=(lay.b, hp, lay.s // TQ + 1),
            in_specs=[
                pl.BlockSpec((TQ, wq), lambda i, p, t, f: (_query_block(lay, i, t), p)),
                lat, ctx, lat, ctx,
            ],
            out_specs=pl.BlockSpec((TQ, MLA_HPS * A_V), lambda i, p, t, f: (_query_block(lay, i, t), p)),
            scratch_shapes=[
                pltpu.VMEM((MLA_HPS, TQ, LANES), F32),
                pltpu.VMEM((MLA_HPS, TQ, HEAD_PAD), F32),
            ],
        ),
        out_shape=jax.ShapeDtypeStruct((lay.n_tok, A_HEADS * A_V), BF16),
        compiler_params=_params(("arbitrary", "arbitrary", "arbitrary")),
        name="mla_attn",
    )(bounded, q, k, k, v, v)


def _dft_kernel(cn_ref, sn_ref, cc_ref, sc_ref, fx_ref, fc_ref, y_ref):
    is_ctx = pl.program_id(1) == pl.num_programs(1) - 1

    def run(c_ref, s_ref, f_ref):
        y = (jnp.dot(c_ref[...], f_ref[:, 0:F_WIDTH], preferred_element_type=F32)
             - jnp.dot(s_ref[...], f_ref[:, F_WIDTH:2 * F_WIDTH], preferred_element_type=F32))
        y_ref[...] = y.astype(BF16)

    @pl.when(is_ctx)
    def _():
        run(cc_ref, sc_ref, fc_ref)

    @pl.when(jnp.logical_not(is_ctx))
    def _():
        run(cn_ref, sn_ref, fx_ref)


def _dft(lay, fcs, cn, sn, cc, sc):
    nq = lay.s // TQ
    table = pl.BlockSpec((TQ, lay.s), lambda i, t: (jnp.minimum(t, nq - 1), 0))
    return pl.pallas_call(
        _dft_kernel,
        grid=(lay.b, nq + 1),
        in_specs=[
            table, table, _resident(cc), _resident(sc),
            pl.BlockSpec((lay.s, 2 * F_WIDTH), lambda i, t: (i, 0)),
            pl.BlockSpec((lay.l, 2 * F_WIDTH), lambda i, t: (lay.ctx_block(i), 0)),
        ],
        out_specs=pl.BlockSpec((TQ, F_WIDTH), lambda i, t: (_query_block(lay, i, t), 0)),
        out_shape=jax.ShapeDtypeStruct((lay.n_tok, F_WIDTH), BF16),
        compiler_params=_params(("arbitrary", "arbitrary")),
        name="fourier_dft",
    )(cn, sn, cc, sc, fcs, fcs)


def _mixed_residual(refs, arities, lat_tiles):
    n_row_refs = sum(arities)
    *mixes, h = _row_tiles(refs, arities, lat_tiles)
    w_ref, mod_ref = refs[n_row_refs:n_row_refs + 2]
    acc = None
    off = 0
    for m in mixes:
        wd = m.shape[-1]
        part = jnp.dot(m, w_ref[off:off + wd, :], preferred_element_type=F32)
        acc = part if acc is None else acc + part
        off += wd
    return h + mod_ref[2:3, :] * acc, mod_ref, refs[n_row_refs + 2:]


def _mix_specs(lay, mixes, w, h):
    specs = []
    for rows in list(mixes) + [h]:
        specs += _row_specs(lay, rows)
    return specs + [_resident(w), lay.mod_spec(h.width)]


def _mix_args(mixes, w, h, mod):
    args = []
    for rows in list(mixes) + [h]:
        args += rows.arrays
    return args + [w, mod]


def _mix_arities(mixes, h):
    return tuple(len(rows.arrays) for rows in list(mixes) + [h])


def _proj_ffn_kernel(*refs, arities, lat_tiles):
    h1, mod_ref, (g_ref, wg_ref, wu_ref, wd_ref, o_ref) = _mixed_residual(refs, arities, lat_tiles)
    xn = _norm_mod(h1, g_ref, mod_ref, 3).astype(BF16)
    ff = wg_ref.shape[1]
    y = None
    for c0 in range(0, ff, FFN_CHUNK):
        c1 = min(c0 + FFN_CHUNK, ff)
        gt = jnp.dot(xn, wg_ref[:, c0:c1], preferred_element_type=F32)
        up = jnp.dot(xn, wu_ref[:, c0:c1], preferred_element_type=F32)
        a = (_silu(gt) * up).astype(BF16)
        part = jnp.dot(a, wd_ref[c0:c1, :], preferred_element_type=F32)
        y = part if y is None else y + part
    o_ref[...] = h1 + mod_ref[5:6, :] * y


def _proj_ffn(lay, mixes, w_out, h, mod, g, wg, wu, wd):
    d = h.width
    return pl.pallas_call(
        functools.partial(_proj_ffn_kernel, arities=_mix_arities(mixes, h), lat_tiles=lay.lat_tiles),
        grid=(lay.all_tiles,),
        in_specs=_mix_specs(lay, mixes, w_out, h) + [_resident(g), _resident(wg), _resident(wu), _resident(wd)],
        out_specs=lay.tile(d),
        out_shape=jax.ShapeDtypeStruct((lay.n_tok, d), F32),
        compiler_params=_params(("arbitrary",)),
        name="proj_ffn",
    )(*_mix_args(mixes, w_out, h, mod), g, wg, wu, wd)


def _na_front_kernel(h_ref, mod_ref, g_ref, win_ref, gq_ref, gk_ref, q_ref, k_ref, v_ref):
    xn = _norm_mod(h_ref[...], g_ref, mod_ref, 0)
    px = jnp.dot(xn.astype(BF16), win_ref[...], preferred_element_type=F32)
    width = C_HEADS * C_DH
    lane = lax.broadcasted_iota(jnp.int32, (px.shape[0], LANES), 1)
    low = lane < C_DH
    scale = C_DH ** -0.5 * LOG2E

    def norm_pairs(base, g_ref_, out_ref, mult):
        gg = g_ref_[...]
        for j in range(width // LANES):
            x = px[:, base + j * LANES:base + (j + 1) * LANES]
            x2 = x * x
            s_all = jnp.sum(x2, axis=-1, keepdims=True)
            s_lo = jnp.sum(jnp.where(low, x2, 0.0), axis=-1, keepdims=True)
            ss = jnp.where(low, s_lo, s_all - s_lo)
            inv = lax.rsqrt(ss * (1.0 / C_DH) + EPS)
            if mult != 1.0:
                inv = inv * mult
            out_ref[:, j * LANES:(j + 1) * LANES] = (x * inv * gg).astype(BF16)

    norm_pairs(0, gq_ref, q_ref, scale)
    norm_pairs(width, gk_ref, k_ref, 1.0)
    v_ref[...] = px[:, 2 * width:].astype(BF16)


def _na_front(lay, h, mod, g, win, gq, gk):
    n, d = h.shape
    width = C_HEADS * C_DH
    return pl.pallas_call(
        _na_front_kernel,
        grid=(lay.all_tiles,),
        in_specs=[lay.tile(d), lay.mod_spec(d), _resident(g), _resident(win), _resident(gq), _resident(gk)],
        out_specs=[lay.tile(width)] * 3,
        out_shape=[jax.ShapeDtypeStruct((n, width), BF16)] * 3,
        compiler_params=_params(("arbitrary",)),
        name="na_front",
    )(h, mod, g, win, gq, gk)


def _na_attn_kernel(bounded_ref, qx_ref, qc_ref, kx_ref, kc_ref, vx_ref, vc_ref, planes_ref, shift_ref,
                    *refs, rows_n):
    out_refs, bias_ref = refs[:-1], refs[-1]
    ox_ref = out_refs[0]

    @pl.when(pl.program_id(1) == 0)
    def _():
        idx = _na_group_rows(rows_n)
        for p in range(idx.shape[0]):
            for hh in range(2):
                for qi in range(NA_GROUP):
                    for a in range(NA_WINDOW):
                        half = (a % 2) * GRID_W
                        bias_ref[p, hh, qi * GRID_W:(qi + 1) * GRID_W, a * GRID_W:(a + 1) * GRID_W] = (
                            planes_ref[hh, int(idx[p, qi, a]), :, half:half + GRID_W])

    qlen = NA_GROUP * GRID_W
    wlen = NA_WINDOW * GRID_W
    n_groups = rows_n // NA_GROUP
    lane_q = lax.broadcasted_iota(jnp.int32, (qlen, LANES), 1)
    kc = kc_ref[...]
    vc = vc_ref[...]

    if len(out_refs) > 1:
        qc = qc_ref[...]
        lane_c = lax.broadcasted_iota(jnp.int32, qc.shape, 1)
        outs = []
        for hh in range(2):
            sel = (lane_c < C_DH) if hh == 0 else (lane_c >= C_DH)
            outs.append(_softmax_pv(jnp.where(sel, qc, jnp.zeros_like(qc)), kc, vc))
        out_refs[1][...] = jnp.where(lane_c < C_DH, outs[0], outs[1]).astype(BF16)

    def group_body(g, carry, bounded):
        start = jnp.clip(g * NA_GROUP - WIN_R // 2, 0, rows_n - NA_WINDOW)
        pat = jnp.where(g == 0, 0, jnp.where(g == n_groups - 1, 2, 1))
        q0 = pl.multiple_of(g * qlen, qlen)
        k0 = pl.multiple_of(start * GRID_W, GRID_W)
        qr = qx_ref[pl.ds(q0, qlen), :]
        kw = kx_ref[pl.ds(k0, wlen), :]
        vw = vx_ref[pl.ds(k0, wlen), :]
        res = []
        for hh in range(2):
            sel = (lane_q < C_DH) if hh == 0 else (lane_q >= C_DH)
            qh = jnp.where(sel, qr, jnp.zeros_like(qr))
            s_loc = lax.dot_general(qh, kw, (((1,), (1,)), ((), ())),
                                    preferred_element_type=F32) + bias_ref[pat, hh]
            s_ctx = lax.dot_general(qh, kc, (((1,), (1,)), ((), ())), preferred_element_type=F32)
            if bounded:
                p_loc = jnp.exp2(s_loc)
                p_ctx = jnp.exp2(s_ctx - shift_ref[:, 0:1])
            else:
                m = jnp.maximum(jnp.max(s_loc, axis=-1, keepdims=True),
                                jnp.max(s_ctx, axis=-1, keepdims=True))
                p_loc = jnp.exp2(s_loc - m)
                p_ctx = jnp.exp2(s_ctx - m)
            l = jnp.sum(p_loc, axis=-1, keepdims=True) + jnp.sum(p_ctx, axis=-1, keepdims=True)
            o = (jnp.dot(p_loc.astype(BF16), vw, preferred_element_type=F32)
                 + jnp.dot(p_ctx.astype(BF16), vc, preferred_element_type=F32))
            res.append(o * (1.0 / l))
        ox_ref[pl.ds(q0, qlen), :] = jnp.where(lane_q < C_DH, res[0], res[1]).astype(BF16)
        return carry

    @pl.when(bounded_ref[0] != 0)
    def _():
        lax.fori_loop(0, n_groups, functools.partial(group_body, bounded=True), 0, unroll=4)

    @pl.when(bounded_ref[0] == 0)
    def _():
        lax.fori_loop(0, n_groups, functools.partial(group_body, bounded=False), 0)


def _na_attn(lay, bounded, q, k, v, planes, shift, need_ctx):
    width = q.shape[-1]
    hp = C_HEADS // 2
    lat = pl.BlockSpec((lay.s, LANES), lambda p, i, f: (i, p))
    ctx = pl.BlockSpec((lay.l, LANES), lambda p, i, f: (lay.ctx_block(i), p))
    out_specs = [pl.BlockSpec((lay.s, LANES), lambda p, i, f: (i, p))]
    out_shape = [jax.ShapeDtypeStruct((lay.n_lat, width), BF16)]
    if need_ctx:
        out_specs.append(pl.BlockSpec((lay.l, LANES), lambda p, i, f: (i, p)))
        out_shape.append(jax.ShapeDtypeStruct((lay.b * lay.l, width), BF16))
    n_pat = _na_group_rows(lay.s // GRID_W).shape[0]
    return pl.pallas_call(
        functools.partial(_na_attn_kernel, rows_n=lay.s // GRID_W),
        grid_spec=pltpu.PrefetchScalarGridSpec(
            num_scalar_prefetch=1,
            grid=(hp, lay.b),
            in_specs=[lat, ctx, lat, ctx, lat, ctx,
                      pl.BlockSpec((2,) + planes.shape[1:], lambda p, i, f: (p, 0, 0, 0)),
                      pl.BlockSpec(shift.shape, lambda p, i, f: (0, 0))],
            out_specs=out_specs,
            scratch_shapes=[pltpu.VMEM((n_pat, 2, NA_GROUP * GRID_W, NA_WINDOW * GRID_W), F32)],
        ),
        out_shape=out_shape,
        compiler_params=_params(("arbitrary", "arbitrary")),
        name="na_attn",
    )(bounded, q, q, k, k, v, v, planes, shift)


def _proj_router_kernel(*refs, arities, lat_tiles):
    h1, mod_ref, (g_ref, wr_ref, h1_ref, xr_ref, route_ref) = _mixed_residual(refs, arities, lat_tiles)
    h1_ref[...] = h1
    xn = _norm_mod(h1, g_ref, mod_ref, 3)
    d = xn.shape[-1]
    xr_ref[:, 0:d] = xn
    wr = wr_ref[...]
    x_hi, w_hi = xn.astype(BF16), wr.astype(BF16)
    x_lo = (xn - x_hi.astype(F32)).astype(BF16)
    w_lo = (wr - w_hi.astype(F32)).astype(BF16)
    logits = (jnp.dot(x_hi, w_hi, preferred_element_type=F32) + jnp.dot(x_lo, w_hi, preferred_element_type=F32)
              + jnp.dot(x_hi, w_lo, preferred_element_type=F32))
    lane = lax.broadcasted_iota(jnp.int32, logits.shape, 1).astype(F32)
    lg = jnp.where(lane < N_EXPERTS, logits, -jnp.inf)
    m1 = jnp.max(lg, axis=-1, keepdims=True)
    i1 = jnp.min(jnp.where(lg == m1, lane, float(LANES)), axis=-1, keepdims=True)
    lg2 = jnp.where(lane == i1, -jnp.inf, lg)
    m2 = jnp.max(lg2, axis=-1, keepdims=True)
    i2 = jnp.min(jnp.where(lg2 == m2, lane, float(LANES)), axis=-1, keepdims=True)
    e2 = jnp.exp(m2 - m1)
    g1 = 1.0 / (1.0 + e2)
    g2 = e2 * g1
    first_low = i1 < i2
    vals = (jnp.minimum(i1, i2), jnp.maximum(i1, i2), jnp.where(first_low, g1, g2), jnp.where(first_low, g2, g1))
    route = jnp.zeros_like(lane)
    for idx, val in enumerate(vals):
        route = jnp.where(lane == float(idx), val, route)
    route_ref[...] = route
    xr_ref[:, d:d + LANES] = route


def _proj_router(lay, mixes, w_out, h, mod, g, wr, n_tiles):
    d = h.width
    rows = n_tiles * lay.tm
    return pl.pallas_call(
        functools.partial(_proj_router_kernel, arities=_mix_arities(mixes, h), lat_tiles=lay.lat_tiles),
        grid=(n_tiles,),
        in_specs=_mix_specs(lay, mixes, w_out, h) + [_resident(g), _resident(wr)],
        out_specs=[lay.tile(d), lay.tile(d + LANES), lay.tile(LANES)],
        out_shape=[jax.ShapeDtypeStruct((rows, d), F32), jax.ShapeDtypeStruct((rows, d + LANES), F32),
                   jax.ShapeDtypeStruct((rows, LANES), F32)],
        compiler_params=_params(("arbitrary",)),
        name="proj_router",
    )(*_mix_args(mixes, w_out, h, mod), g, wr)


def _moe_plan(route, n_tok):
    n_pairs = len(_PAIR_LO)
    n_tiles = n_tok // MOE_TM + n_pairs
    rows = n_tiles * MOE_TM
    lo = route[:, 0].astype(jnp.int32)
    hi = route[:, 1].astype(jnp.int32)
    pid = (lo * (2 * N_EXPERTS - 1 - lo)) // 2 + (hi - lo - 1)
    onehot = (pid[:, None] == jnp.arange(n_pairs, dtype=jnp.int32)[None, :]).astype(jnp.int32)
    csum = jnp.cumsum(onehot, axis=0)
    rank = jnp.sum(csum * onehot, axis=1) - 1
    counts = csum[-1]
    padded = ((counts + MOE_TM - 1) // MOE_TM) * MOE_TM
    gend = jnp.cumsum(padded)
    gstart = gend - padded
    dest = jnp.sum(gstart[None, :] * onehot, axis=1) + rank
    src = jnp.full((rows,), -1, jnp.int32).at[dest].set(jnp.arange(n_tok, dtype=jnp.int32), unique_indices=True)
    valid = src >= 0
    src_tok = jnp.where(valid, src, 0)
    n_valid = jnp.sum(valid.reshape(n_tiles, MOE_TM).astype(jnp.int32), axis=1)
    n_used = gend[-1] // MOE_TM
    tile_row = jnp.minimum(jnp.arange(n_tiles, dtype=jnp.int32), n_used - 1) * MOE_TM
    group = jnp.minimum(jnp.sum((gend[None, :] <= tile_row[:, None]).astype(jnp.int32), axis=1), n_pairs - 1)
    ea = jnp.asarray(_PAIR_LO, jnp.int32)[group]
    eb = jnp.asarray(_PAIR_HI, jnp.int32)[group]
    return ea, eb, src_tok, n_valid, n_used.astype(jnp.int32).reshape(1)


def _moe_kernel(ea_ref, eb_ref, src_ref, nvalid_ref, nused_ref,
                x_hbm, wga_ref, wua_ref, wda_ref, wgb_ref, wub_ref, wdb_ref,
                y_hbm, xbuf, ybuf, gsem, ssem):
    i = pl.program_id(0)
    n_steps = pl.num_programs(0)
    n_used = nused_ref[0]
    slot = i % 2

    def gather_copy(tile, slot_, r):
        return pltpu.make_async_copy(x_hbm.at[pl.ds(src_ref[tile * MOE_TM + r], 1)],
                                     xbuf.at[slot_, pl.ds(r, 1)], gsem.at[slot_])

    def scatter_copy(tile, slot_, r):
        return pltpu.make_async_copy(ybuf.at[slot_, pl.ds(r, 1)],
                                     y_hbm.at[pl.ds(src_ref[tile * MOE_TM + r], 1)], ssem.at[slot_])

    def start_gather(tile, slot_):
        for r in range(MOE_TM):
            gather_copy(tile, slot_, r).start()

    def wait_gather(slot_):
        pltpu.make_async_copy(x_hbm.at[pl.ds(0, MOE_TM)], xbuf.at[slot_], gsem.at[slot_]).wait()

    def start_scatter(tile, slot_):
        full = nvalid_ref[tile] == MOE_TM

        @pl.when(full)
        def _():
            for r in range(MOE_TM):
                scatter_copy(tile, slot_, r).start()

        @pl.when(jnp.logical_not(full))
        def _():
            def body(r, carry):
                scatter_copy(tile, slot_, r).start()
                return carry
            lax.fori_loop(0, nvalid_ref[tile], body, 0)

    def wait_scatter(tile, slot_):
        full = nvalid_ref[tile] == MOE_TM

        @pl.when(full)
        def _():
            pltpu.make_async_copy(ybuf.at[slot_], y_hbm.at[pl.ds(0, MOE_TM)], ssem.at[slot_]).wait()

        @pl.when(jnp.logical_not(full))
        def _():
            def body(r, carry):
                scatter_copy(tile, slot_, r).wait()
                return carry
            lax.fori_loop(0, nvalid_ref[tile], body, 0)

    @pl.when((i >= 2) & (i - 2 < n_used))
    def _():
        wait_scatter(i - 2, slot)

    @pl.when(i < n_used)
    def _():
        @pl.when(i == 0)
        def _():
            start_gather(0, 0)

        @pl.when(i + 1 < n_used)
        def _():
            start_gather(i + 1, 1 - slot)

        wait_gather(slot)
        d = ybuf.shape[-1]
        x = xbuf[slot, :, 0:d].astype(BF16)
        gates = xbuf[slot, :, d:d + LANES]

        def expert(wg_ref, wu_ref, wd_ref):
            gt = jnp.dot(x, wg_ref[...], preferred_element_type=F32)
            up = jnp.dot(x, wu_ref[...], preferred_element_type=F32)
            a = (_silu(gt) * up).astype(BF16)
            return jnp.dot(a, wd_ref[...], preferred_element_type=F32)

        ybuf[slot] = (gates[:, 2:3] * expert(wga_ref, wua_ref, wda_ref)
                      + gates[:, 3:4] * expert(wgb_ref, wub_ref, wdb_ref))
        start_scatter(i, slot)

    @pl.when(i == n_steps - 1)
    def _():
        for back in (2, 1):
            @pl.when(n_steps - back < n_used)
            def _():
                wait_scatter(n_steps - back, (n_steps - back) % 2)


def _moe(xr, route, wg, wu, wd, layer):
    n_tok, dr = xr.shape
    _, _, d, f = wg.shape
    assert dr == d + LANES
    ea, eb, src_tok, n_valid, n_used = _moe_plan(route, n_tok)
    n_tiles = ea.shape[0]
    wspec = lambda shape, which: pl.BlockSpec(
        (None, None) + shape, lambda i, ea_, eb_, s_, d_, u_: (layer, (ea_, eb_)[which][i], 0, 0))
    return pl.pallas_call(
        _moe_kernel,
        grid_spec=pltpu.PrefetchScalarGridSpec(
            num_scalar_prefetch=5,
            grid=(n_tiles,),
            in_specs=[
                pl.BlockSpec(memory_space=pl.ANY),
                wspec((d, f), 0), wspec((d, f), 0), wspec((f, d), 0),
                wspec((d, f), 1), wspec((d, f), 1), wspec((f, d), 1),
            ],
            out_specs=pl.BlockSpec(memory_space=pl.ANY),
            scratch_shapes=[
                pltpu.VMEM((2, MOE_TM, dr), F32),
                pltpu.VMEM((2, MOE_TM, d), F32),
                pltpu.SemaphoreType.DMA((2,)),
                pltpu.SemaphoreType.DMA((2,)),
            ],
        ),
        out_shape=jax.ShapeDtypeStruct((n_tok, d), F32),
        compiler_params=_params(("arbitrary",)),
        name="moe_experts",
    )(ea, eb, src_tok, n_valid, n_used, xr, wg, wu, wd, wg, wu, wd)


def _resid_kernel(h_ref, y_ref, mod_ref, o_ref):
    o_ref[...] = h_ref[...] + mod_ref[5:6, :] * y_ref[...]


def _resid(lay, h, y, mod):
    n, d = h.shape
    return pl.pallas_call(
        _resid_kernel,
        grid=(n // lay.tm,),
        in_specs=[lay.tile(d), lay.tile(d), lay.mod_spec(d)],
        out_specs=lay.tile(d),
        out_shape=jax.ShapeDtypeStruct((n, d), F32),
        compiler_params=_params(("arbitrary",)),
        name="moe_residual",
    )(h, y, mod)


def _rope_tables(s, ctx_rows):
    t = jnp.arange(s)
    rows = (t // GRID_W).astype(F32)
    cols = (t % GRID_W).astype(F32)
    n_pairs = A_ROPE // 2
    per_axis = n_pairs // 2
    inv = ROPE_THETA ** (-jnp.arange(per_axis, dtype=F32) / per_axis)
    ang = jnp.concatenate([rows[:, None] * inv, cols[:, None] * inv], axis=-1)
    cos = jnp.repeat(jnp.cos(ang), 2, axis=-1)
    sin = jnp.repeat(jnp.sin(ang), 2, axis=-1)
    sign = jnp.tile(jnp.array([-1.0, 1.0], F32), n_pairs)
    pad = HEAD_PAD - A_QK
    t1 = jnp.concatenate([jnp.ones((s, A_NOPE), F32), cos, jnp.zeros((s, pad), F32)], axis=-1)
    t2 = jnp.concatenate([jnp.zeros((s, A_NOPE), F32), sin * sign, jnp.zeros((s, pad), F32)], axis=-1)
    c1 = jnp.concatenate([jnp.ones((ctx_rows, A_QK), F32), jnp.zeros((ctx_rows, pad), F32)], axis=-1)
    c2 = jnp.zeros((ctx_rows, HEAD_PAD), F32)
    return jnp.concatenate([t1, c1], axis=0), jnp.concatenate([t2, c2], axis=0)


_PAIR_SWAP = np.arange(A_ROPE) ^ 1


def _rope_lane_vec(g_tail, swapped):
    gt = g_tail[_PAIR_SWAP] if swapped else g_tail
    return jnp.concatenate([jnp.zeros((A_NOPE,), F32), gt, jnp.zeros((HEAD_PAD - A_QK,), F32)])[None]


def _mla_weights(w_in, w_uq, w_ukv, g_q, g_k):
    d = w_in.shape[0]
    q_end = A_Q_RANK
    kv_end = q_end + A_KV_RANK
    r_end = kv_end + A_ROPE
    z = lambda n: jnp.zeros((d, n), w_in.dtype)
    kr = w_in[:, kv_end:r_end]
    pad = HEAD_PAD - A_QK
    win = jnp.concatenate([w_in[:, :kv_end], z(A_NOPE), kr, z(pad), z(A_NOPE), kr[:, _PAIR_SWAP], z(pad),
                           w_in[:, r_end:]], axis=-1).astype(BF16)
    wq = w_uq.reshape(A_Q_RANK, A_HEADS, A_QK)
    zq = lambda n: jnp.zeros((A_Q_RANK, A_HEADS, n), wq.dtype)
    wq_main = jnp.concatenate([wq, zq(pad)], axis=-1)
    wq_swap = jnp.concatenate([zq(A_NOPE), wq[:, :, A_NOPE:][:, :, _PAIR_SWAP], zq(pad)], axis=-1)
    wq_ext = jnp.concatenate([wq_main.reshape(A_Q_RANK, -1), wq_swap.reshape(A_Q_RANK, -1)], axis=-1).astype(BF16)
    wkv = w_ukv.reshape(A_KV_RANK, A_HEADS, A_NOPE + A_V)
    zkv = jnp.zeros((A_KV_RANK, A_HEADS, HEAD_PAD - A_NOPE), wkv.dtype)
    wk = jnp.concatenate([wkv[:, :, :A_NOPE], zkv], axis=-1)
    wv = jnp.concatenate([wkv[:, :, A_NOPE:], zkv], axis=-1)
    wkv_ext = jnp.concatenate([wk.reshape(A_KV_RANK, -1), wv.reshape(A_KV_RANK, -1)], axis=-1).astype(BF16)

    bound = math.sqrt(A_QK) * jnp.max(jnp.abs(g_q)) * jnp.max(jnp.abs(g_k))
    bounded = bound <= MLA_SAFE_BOUND
    shift = jnp.where(bounded, -bound * LOG2E, 0.0)
    unit = lambda lane_idx: jnp.zeros((1, HEAD_PAD), F32).at[0, lane_idx].set(1.0)
    gq = jnp.concatenate([jnp.concatenate([g_q, jnp.zeros((pad,), F32)])[None], _rope_lane_vec(g_q[A_NOPE:], True),
                          unit(A_QK) * shift], axis=0)
    gk = jnp.concatenate([jnp.concatenate([g_k[:A_NOPE], jnp.zeros((HEAD_PAD - A_NOPE,), F32)])[None],
                          _rope_lane_vec(g_k[A_NOPE:], False), _rope_lane_vec(g_k[A_NOPE:], True),
                          unit(A_QK), unit(A_V)], axis=0)
    return win, wq_ext, wkv_ext, gq, gk, bounded.astype(jnp.int32).reshape(1)


def _dft_mats(n, norm):
    k = jnp.arange(n, dtype=jnp.int32)
    w = GRID_W if n % GRID_W == 0 and n > GRID_W else 1
    hi = jnp.arange(n // w, dtype=jnp.int32) * w
    lo = jnp.arange(w, dtype=jnp.int32)
    ang_hi = ((k[:, None] * hi[None, :]) % n).astype(F32) * (2.0 * math.pi / n)
    ang_lo = ((k[:, None] * lo[None, :]) % n).astype(F32) * (2.0 * math.pi / n)
    ch, sh = (jnp.repeat(f(ang_hi), w, axis=1) for f in (jnp.cos, jnp.sin))
    cl, sl = (jnp.tile(f(ang_lo), (1, n // w)) for f in (jnp.cos, jnp.sin))
    cos = ((ch * cl - sh * sl) * norm).astype(BF16)
    sin = ((sh * cl + ch * sl) * norm).astype(BF16)
    return cos, sin


def _channel_dft():
    c = np.arange(F_CH)
    ang = 2.0 * np.pi * ((c[:, None] * c[None, :]) % F_CH) / F_CH
    eye = np.eye(F_GROUPS)
    cb = np.kron(eye, np.cos(ang)) / math.sqrt(F_CH)
    sb = np.kron(eye, np.sin(ang)) / math.sqrt(F_CH)
    return jnp.asarray(np.concatenate([cb, sb], axis=1), BF16)


def _na_group_rows(rows_n):
    assert rows_n % NA_GROUP == 0 and rows_n >= NA_WINDOW + NA_GROUP
    n_groups = rows_n // NA_GROUP
    n_dr = 2 * WIN_R - 1
    idx = np.full((3, NA_GROUP, NA_WINDOW), n_dr, np.int32)
    seen = {}
    for g in range(n_groups):
        gs = int(np.clip(g * NA_GROUP - WIN_R // 2, 0, rows_n - NA_WINDOW))
        pat = 0 if g == 0 else (2 if g == n_groups - 1 else 1)
        cur = np.full((NA_GROUP, NA_WINDOW), n_dr, np.int32)
        for qi in range(NA_GROUP):
            r = g * NA_GROUP + qi
            start = int(np.clip(r - WIN_R // 2, 0, rows_n - WIN_R))
            assert gs <= start and start + WIN_R <= gs + NA_WINDOW
            for a in range(start, start + WIN_R):
                cur[qi, a - gs] = a - r + (WIN_R - 1)
        assert pat not in seen or np.array_equal(seen[pat], cur)
        seen[pat] = cur
        idx[pat] = cur
    return idx


def _na_score_bound(g_q, g_k, rpb):
    qk = math.sqrt(C_DH) * jnp.max(jnp.abs(g_q)) * jnp.max(jnp.abs(g_k))
    bound = qk + jnp.maximum(jnp.max(rpb), 0.0)
    lowest = -qk + jnp.minimum(jnp.min(rpb), 0.0)
    bounded = (bound - lowest) <= 2.0 * MLA_SAFE_BOUND
    shift = jnp.where(bounded, bound, 0.0).astype(F32)
    return bounded.astype(jnp.int32).reshape(1), shift


def _na_bias(rpb, shift):
    h, n_dr, n_dc = rpb.shape
    qc = np.arange(GRID_W)
    cs = np.clip(qc - WIN_C // 2, 0, GRID_W - WIN_C)
    kcol = np.arange(GRID_W)
    valid = (kcol[None, :] >= cs[:, None]) & (kcol[None, :] < cs[:, None] + WIN_C)
    dc = kcol[None, :] - qc[:, None] + (WIN_C - 1)
    onehot = (valid[:, :, None] & (dc[:, :, None] == np.arange(n_dc)[None, None, :])).astype(np.float32)
    col = jnp.einsum('hdc,qkc->hdqk', rpb.astype(F32), jnp.asarray(onehot), precision=lax.Precision.HIGHEST)
    col = jnp.where(jnp.asarray(valid)[None, None], (col - shift) * LOG2E, NEG_BIG)
    col = jnp.concatenate([col, jnp.full((h, 1, GRID_W, GRID_W), NEG_BIG, F32)], axis=1)
    return jnp.concatenate([col, col], axis=-1)


def kernel(x, c, ctx, c_ctx, w_mod, b_mod, norm_g, a_w_in, a_g_cq, a_g_ckv, a_w_uq, a_w_ukv, a_g_q, a_g_k, a_w_out,
           f_w_gate, f_w_up, f_w_down, c_w_in, c_g_q, c_g_k, c_rpb, c_w_out, m_w_router, m_w_gate, m_w_up, m_w_down):
    b, s, d = x.shape
    l = ctx.shape[1]
    depth = w_mod.shape[0]
    lay = _make_layout(b, s, l)

    mod_rows = 16
    cin = jnp.concatenate([c, c_ctx[None], jnp.zeros((mod_rows - b - 1, d), F32)], axis=0)
    mod_all = _modulation(cin, w_mod, b_mod).reshape(depth, mod_rows, 6, d)

    t1, t2 = _rope_tables(s, lay.tm)
    cn, sn = _dft_mats(s, 1.0 / math.sqrt(s))
    cc, sc = _dft_mats(l, 1.0 / math.sqrt(l))
    cb = _channel_dft()

    moe_wg, moe_wu, moe_wd = m_w_gate.astype(BF16), m_w_up.astype(BF16), m_w_down.astype(BF16)
    h = _Rows(x.reshape(b * s, d), ctx.reshape(b * l, d))
    for i in range(depth):
        j = i // 2
        need_ctx = i < depth - 1
        mod = mod_all[i]
        g1 = norm_g[i, 0][None]
        g2 = norm_g[i, 1][None]
        if i % 2 == 0:
            win, wq_ext, wkv_ext, gq, gk, bounded = _mla_weights(a_w_in[j], a_w_uq[j], a_w_ukv[j],
                                                                 a_g_q[j], a_g_k[j])
            q, k, v, fcs = _mla_front(lay, h, mod, g1, win, a_g_cq[j][None], a_g_ckv[j][None], wq_ext, wkv_ext,
                                      t1, t2, gq, gk, cb)
            att = _mla_attn(lay, bounded, q, k, v)
            yf = _dft(lay, fcs, cn, sn, cc, sc)
            h = _Rows(_proj_ffn(lay, [_Rows(att), _Rows(yf)], a_w_out[j].astype(BF16), h, mod, g2,
                                f_w_gate[j].astype(BF16), f_w_up[j].astype(BF16), f_w_down[j].astype(BF16)))
        else:
            gq2 = jnp.tile(c_g_q[j], 2)[None]
            gk2 = jnp.tile(c_g_k[j], 2)[None]
            q, k, v = _na_front(lay, h.lat, mod, g1, c_w_in[j].astype(BF16), gq2, gk2)
            na_bounded, na_shift = _na_score_bound(c_g_q[j], c_g_k[j], c_rpb[j])
            bias = _na_bias(c_rpb[j], na_shift)
            outs = _na_attn(lay, na_bounded, q, k, v, bias, jnp.full((1, LANES), na_shift * LOG2E, F32), need_ctx)
            att = _Rows(outs[0], outs[1]) if need_ctx else _Rows(outs[0])
            wr = jnp.concatenate([m_w_router[j], jnp.zeros((d, LANES - N_EXPERTS), F32)], axis=-1)
            n_tiles = lay.all_tiles if need_ctx else lay.lat_tiles
            h1, xn2, route = _proj_router(lay, [att], c_w_out[j].astype(BF16), h, mod, g2, wr, n_tiles)
            y = _moe(xn2, route, moe_wg, moe_wu, moe_wd, j)
            h = _Rows(_resid(lay, h1, y, mod))
    return h.lat[:lay.n_lat].reshape(b, s, d)
```

```python
import functools
import math
from typing import NamedTuple

import jax
import jax.numpy as jnp
import numpy as np
from jax import lax
from jax.experimental import pallas as pl
from jax.experimental.pallas import tpu as pltpu

F32 = jnp.float32
BF16 = jnp.bfloat16

GRID_W = 64
A_HEADS = 12
A_NOPE = 64
A_ROPE = 32
A_QK = A_NOPE + A_ROPE
A_V = 64
A_Q_RANK = 256
A_KV_RANK = 128
F_GROUPS = 4
F_CH = 64
F_WIDTH = F_GROUPS * F_CH
C_HEADS = 16
C_DH = 64
WIN_R = 8
WIN_C = 16
N_EXPERTS = 8
ROPE_THETA = 10000.0
EPS = 1e-6

LANES = 128
HEAD_PAD = 128
TM = 512
TQ = 256
FFN_CHUNK = 1408
MOE_TM = 256
_PAIR_LO = tuple(lo for lo in range(N_EXPERTS) for hi in range(lo + 1, N_EXPERTS))
_PAIR_HI = tuple(hi for lo in range(N_EXPERTS) for hi in range(lo + 1, N_EXPERTS))
LOG2E = math.log2(math.e)
MLA_SAFE_BOUND = 40.0
MLA_KCHUNK = 512
MLA_HPS = 6
NA_GROUP = 4
NA_WINDOW = NA_GROUP + WIN_R
VMEM_LIMIT = 52 * 1024 * 1024
NEG_BIG = -1e30


def _params(sem, vmem=VMEM_LIMIT):
    return pltpu.CompilerParams(dimension_semantics=sem, vmem_limit_bytes=vmem)


def _rms(x, g, n=None):
    n = x.shape[-1] if n is None else n
    ss = jnp.sum(x * x, axis=-1, keepdims=True)
    return x * lax.rsqrt(ss * (1.0 / n) + EPS) * g


def _silu(x):
    return x * (1.0 / (1.0 + jnp.exp(-x)))


class _Layout(NamedTuple):
    b: int
    s: int
    l: int
    tm: int

    @property
    def n_lat(self):
        return self.b * self.s

    @property
    def n_tok(self):
        return self.b * (self.s + self.l)

    @property
    def lat_tiles(self):
        return self.n_lat // self.tm

    @property
    def all_tiles(self):
        return self.n_tok // self.tm

    def tile(self, width):
        return pl.BlockSpec((self.tm, width), lambda t: (t, 0))

    def mod_spec(self, d):
        per_batch = self.s // self.tm
        return pl.BlockSpec((None, 6, d), lambda t: (jnp.where(t < self.lat_tiles, t // per_batch, self.b), 0, 0))

    def ctx_block(self, i):
        return self.n_lat // self.l + i


class _Rows(NamedTuple):
    lat: jax.Array
    ctx: jax.Array | None = None

    @property
    def width(self):
        return self.lat.shape[-1]

    @property
    def arrays(self):
        return [self.lat] if self.ctx is None else [self.lat, self.ctx]


def _row_specs(lay, rows):
    if rows.ctx is None:
        return [lay.tile(rows.width)]
    tm, nl = lay.tm, lay.lat_tiles
    return [pl.BlockSpec((tm, rows.width), lambda t: (jnp.minimum(t, nl - 1), 0)),
            pl.BlockSpec((tm, rows.width), lambda t: (jnp.maximum(t - nl, 0), 0))]


def _row_tiles(refs, arities, lat_tiles):
    tiles, pos = [], 0
    for arity in arities:
        if arity == 1:
            tiles.append(refs[pos][...])
        else:
            tiles.append(jnp.where(pl.program_id(0) < lat_tiles, refs[pos][...], refs[pos + 1][...]))
        pos += arity
    return tiles


def _make_layout(b, s, l):
    tm = TM if (s % TM == 0 and (b * l) % TM == 0) else l
    assert l == TQ and s % tm == 0 and (b * l) % tm == 0 and s % GRID_W == 0
    return _Layout(b, s, l, tm)


def _resident(a):
    return pl.BlockSpec(a.shape, lambda *_: (0,) * a.ndim, pipeline_mode=pl.Buffered(1))


def _norm_mod(x, g_ref, mod_ref, row):
    xn = _rms(x, g_ref[...])
    return xn * (1.0 + mod_ref[row + 1:row + 2, :]) + mod_ref[row:row + 1, :]


def _mod_kernel(c_ref, w_ref, b_ref, o_ref):
    s = _silu(c_ref[...])
    o_ref[...] = jnp.dot(s, w_ref[...], precision=lax.Precision.HIGHEST,
                         preferred_element_type=F32) + b_ref[...]


def _modulation(cin, w_mod, b_mod):
    depth, d, n = w_mod.shape
    tn = 1536
    rows = cin.shape[0]
    return pl.pallas_call(
        _mod_kernel,
        grid=(depth, n // tn),
        in_specs=[
            pl.BlockSpec((rows, d), lambda l, j: (0, 0)),
            pl.BlockSpec((None, d, tn), lambda l, j: (l, 0, j)),
            pl.BlockSpec((None, 1, tn), lambda l, j: (l, 0, j)),
        ],
        out_specs=pl.BlockSpec((None, rows, tn), lambda l, j: (l, 0, j)),
        out_shape=jax.ShapeDtypeStruct((depth, rows, n), F32),
        compiler_params=_params(("arbitrary", "arbitrary")),
        name="modulation",
    )(cin, w_mod, b_mod.reshape(depth, 1, n))


def _mla_front_kernel(*refs, h_arity, lat_tiles):
    (mod_ref, g_ref, win_ref, gcq_ref, gckv_ref, wq_ref, wkv_ref, t1_ref, t2_ref, gq_ref, gk_ref, cb_ref,
     q_ref, k_ref, v_ref, fcs_ref) = refs[h_arity:]
    (h,) = _row_tiles(refs, (h_arity,), lat_tiles)
    xn = _norm_mod(h, g_ref, mod_ref, 0)
    px = jnp.dot(xn.astype(BF16), win_ref[...], preferred_element_type=F32)
    t1 = t1_ref[...]
    t2 = t2_ref[...]
    hw = A_HEADS * HEAD_PAD

    qn = _rms(px[:, 0:A_Q_RANK], gcq_ref[...]).astype(BF16)
    qq = jnp.dot(qn, wq_ref[...], preferred_element_type=F32)
    aq = t1 * gq_ref[0:1, :]
    bq = t2 * gq_ref[1:2, :]
    q_shift = gq_ref[2:3, :]
    scale = A_QK ** -0.5 * LOG2E
    for h in range(A_HEADS):
        qm = qq[:, h * HEAD_PAD:(h + 1) * HEAD_PAD]
        qs = qq[:, hw + h * HEAD_PAD:hw + (h + 1) * HEAD_PAD]
        ss = jnp.sum(qm * qm, axis=-1, keepdims=True)
        inv = lax.rsqrt(ss * (1.0 / A_QK) + EPS) * scale
        q_ref[:, h * HEAD_PAD:(h + 1) * HEAD_PAD] = ((qm * aq + qs * bq) * inv + q_shift).astype(BF16)

    kvn = _rms(px[:, A_Q_RANK:A_Q_RANK + A_KV_RANK], gckv_ref[...]).astype(BF16)
    kv = jnp.dot(kvn, wkv_ref[...], preferred_element_type=F32)
    krm = px[:, 384:512]
    krs = px[:, 512:640]
    tail = krm * (t1 * gk_ref[1:2, :]) + krs * (t2 * gk_ref[2:3, :])
    ssr = jnp.sum(krm * krm, axis=-1, keepdims=True)
    gkn = gk_ref[0:1, :]
    k_one = gk_ref[3:4, :]
    v_one = gk_ref[4:5, :]
    for h in range(A_HEADS):
        km = kv[:, h * HEAD_PAD:(h + 1) * HEAD_PAD]
        ss = jnp.sum(km * km, axis=-1, keepdims=True) + ssr
        inv = lax.rsqrt(ss * (1.0 / A_QK) + EPS)
        k_ref[:, h * HEAD_PAD:(h + 1) * HEAD_PAD] = ((km * gkn + tail) * inv + k_one).astype(BF16)
        v_ref[:, h * HEAD_PAD:(h + 1) * HEAD_PAD] = (kv[:, hw + h * HEAD_PAD:hw + (h + 1) * HEAD_PAD]
                                                     + v_one).astype(BF16)

    f = px[:, 640:896].astype(BF16)
    fcs_ref[...] = jnp.dot(f, cb_ref[...], preferred_element_type=F32).astype(BF16)


def _mla_front(lay, h, mod, g, win, gcq, gckv, wq, wkv, t1, t2, gq, gk, cb):
    n, d = lay.n_tok, h.width
    hw = A_HEADS * HEAD_PAD
    per_batch = lay.s // lay.tm
    rope = pl.BlockSpec((lay.tm, LANES), lambda t: (jnp.where(t < lay.lat_tiles, t % per_batch, per_batch), 0))
    return pl.pallas_call(
        functools.partial(_mla_front_kernel, h_arity=len(h.arrays), lat_tiles=lay.lat_tiles),
        grid=(lay.all_tiles,),
        in_specs=_row_specs(lay, h) + [
            lay.mod_spec(d),
            _resident(g), _resident(win), _resident(gcq), _resident(gckv), _resident(wq), _resident(wkv),
            rope, rope,
            _resident(gq), _resident(gk), _resident(cb),
        ],
        out_specs=[lay.tile(hw), lay.tile(hw), lay.tile(hw), lay.tile(2 * F_WIDTH)],
        out_shape=[
            jax.ShapeDtypeStruct((n, hw), BF16),
            jax.ShapeDtypeStruct((n, hw), BF16),
            jax.ShapeDtypeStruct((n, hw), BF16),
            jax.ShapeDtypeStruct((n, 2 * F_WIDTH), BF16),
        ],
        compiler_params=_params(("arbitrary",)),
        name="mla_front",
    )(*h.arrays, mod, g, win, gcq, gckv, wq, wkv, t1, t2, gq, gk, cb)


def _softmax_pv(q, k, v):
    s = lax.dot_general(q, k, (((1,), (1,)), ((), ())), preferred_element_type=F32)
    m = jnp.max(s, axis=-1, keepdims=True)
    p = jnp.exp2(s - m)
    l = jnp.sum(p, axis=-1, keepdims=True)
    o = jnp.dot(p.astype(BF16), v, preferred_element_type=F32)
    return o * (1.0 / l)


def _mla_attn_kernel(bounded_ref, q_ref, kx_ref, kc_ref, vx_ref, vc_ref, o_ref, m_ref, acc_ref):
    t = pl.program_id(2)
    ck = MLA_KCHUNK
    n_chunks = kx_ref.shape[0] // ck

    def head(ref, rows=slice(None)):
        return lambda hh: ref[rows, hh * HEAD_PAD:(hh + 1) * HEAD_PAD]

    def scores(hh, k):
        return lax.dot_general(head(q_ref)(hh), k, (((1,), (1,)), ((), ())), preferred_element_type=F32)

    def bounded(with_latent):
        for hh in range(MLA_HPS):
            p = jnp.exp2(scores(hh, head(kc_ref)(hh))).astype(BF16)
            acc = jnp.dot(p, head(vc_ref)(hh), preferred_element_type=F32)
            if with_latent:
                p = jnp.exp2(scores(hh, head(kx_ref)(hh))).astype(BF16)
                acc = acc + jnp.dot(p, head(vx_ref)(hh), preferred_element_type=F32)
            acc_ref[hh] = acc

    def online(with_latent):
        for hh in range(MLA_HPS):
            s = scores(hh, head(kc_ref)(hh))
            m = jnp.max(s, axis=-1, keepdims=True)
            m_ref[hh] = jnp.broadcast_to(m, m_ref.shape[1:])
            acc_ref[hh] = jnp.dot(jnp.exp2(s - m).astype(BF16), head(vc_ref)(hh), preferred_element_type=F32)

        def body(c, carry):
            rows = pl.ds(pl.multiple_of(c * ck, ck), ck)
            for hh in range(MLA_HPS):
                s = scores(hh, head(kx_ref, rows)(hh))
                m_old = m_ref[hh]
                m_new = jnp.maximum(m_old, jnp.max(s, axis=-1, keepdims=True))
                p = jnp.exp2(s - jnp.tile(m_new, (1, ck // LANES))).astype(BF16)
                acc_ref[hh] = (jnp.exp2(m_old - m_new) * acc_ref[hh]
                               + jnp.dot(p, head(vx_ref, rows)(hh), preferred_element_type=F32))
                m_ref[hh] = m_new
            return carry

        if with_latent:
            lax.fori_loop(0, n_chunks, body, 0)

    is_ctx = t == pl.num_programs(2) - 1
    fast = bounded_ref[0] != 0

    @pl.when(fast & is_ctx)
    def _():
        bounded(False)

    @pl.when(fast & jnp.logical_not(is_ctx))
    def _():
        bounded(True)

    @pl.when(jnp.logical_not(fast) & is_ctx)
    def _():
        online(False)

    @pl.when(jnp.logical_not(fast) & jnp.logical_not(is_ctx))
    def _():
        online(True)

    lane = lax.broadcasted_iota(jnp.int32, (o_ref.shape[0], LANES), 1)
    for hp in range(MLA_HPS // 2):
        outs = []
        for hh in (2 * hp, 2 * hp + 1):
            acc = acc_ref[hh]
            outs.append(acc * (1.0 / acc[:, A_V:A_V + 1]))
        o_ref[:, hp * LANES:(hp + 1) * LANES] = jnp.where(
            lane < A_V, outs[0], pltpu.roll(outs[1], A_V, axis=1)).astype(BF16)


def _query_block(lay, i, t):
    nq = lay.s // TQ
    return jnp.where(t < nq, i * nq + t, lay.ctx_block(i))


def _mla_attn(lay, bounded, q, k, v):
    hp = A_HEADS // MLA_HPS
    assert lay.s % MLA_KCHUNK == 0
    wq = MLA_HPS * HEAD_PAD
    lat = pl.BlockSpec((lay.s, wq), lambda i, p, t, f: (i, p))
    ctx = pl.BlockSpec((lay.l, wq), lambda i, p, t, f: (lay.ctx_block(i), p))
    return pl.pallas_call(
        _mla_attn_kernel,
        grid_spec=pltpu.PrefetchScalarGridSpec(
            num_scalar_prefetch=1,
            grid=(lay.b, hp, lay.s // TQ + 1),
            in_specs=[
                pl.BlockSpec((TQ, wq), lambda i, p, t, f: (_query_block(lay, i, t), p)),
                lat, ctx, lat, ctx,
            ],
            out_specs=pl.BlockSpec((TQ, MLA_HPS * A_V), lambda i, p, t, f: (_query_block(lay, i, t), p)),
            scratch_shapes=[
                pltpu.VMEM((MLA_HPS, TQ, LANES), F32),
                pltpu.VMEM((MLA_HPS, TQ, HEAD_PAD), F32),
            ],
        ),
        out_shape=jax.ShapeDtypeStruct((lay.n_tok, A_HEADS * A_V), BF16),
        compiler_params=_params(("arbitrary", "arbitrary", "arbitrary")),
        name="mla_attn",
    )(bounded, q, k, k, v, v)


def _dft_kernel(cn_ref, sn_ref, cc_ref, sc_ref, fx_ref, fc_ref, y_ref):
    is_ctx = pl.program_id(1) == pl.num_programs(1) - 1

    def run(c_ref, s_ref, f_ref):
        y = (jnp.dot(c_ref[...], f_ref[:, 0:F_WIDTH], preferred_element_type=F32)
             - jnp.dot(s_ref[...], f_ref[:, F_WIDTH:2 * F_WIDTH], preferred_element_type=F32))
        y_ref[...] = y.astype(BF16)

    @pl.when(is_ctx)
    def _():
        run(cc_ref, sc_ref, fc_ref)

    @pl.when(jnp.logical_not(is_ctx))
    def _():
        run(cn_ref, sn_ref, fx_ref)


def _dft(lay, fcs, cn, sn, cc, sc):
    nq = lay.s // TQ
    table = pl.BlockSpec((TQ, lay.s), lambda i, t: (jnp.minimum(t, nq - 1), 0))
    return pl.pallas_call(
        _dft_kernel,
        grid=(lay.b, nq + 1),
        in_specs=[
            table, table, _resident(cc), _resident(sc),
            pl.BlockSpec((lay.s, 2 * F_WIDTH), lambda i, t: (i, 0)),
            pl.BlockSpec((lay.l, 2 * F_WIDTH), lambda i, t: (lay.ctx_block(i), 0)),
        ],
        out_specs=pl.BlockSpec((TQ, F_WIDTH), lambda i, t: (_query_block(lay, i, t), 0)),
        out_shape=jax.ShapeDtypeStruct((lay.n_tok, F_WIDTH), BF16),
        compiler_params=_params(("arbitrary", "arbitrary")),
        name="fourier_dft",
    )(cn, sn, cc, sc, fcs, fcs)


def _mixed_residual(refs, arities, lat_tiles):
    n_row_refs = sum(arities)
    *mixes, h = _row_tiles(refs, arities, lat_tiles)
    w_ref, mod_ref = refs[n_row_refs:n_row_refs + 2]
    acc = None
    off = 0
    for m in mixes:
        wd = m.shape[-1]
        part = jnp.dot(m, w_ref[off:off + wd, :], preferred_element_type=F32)
        acc = part if acc is None else acc + part
        off += wd
    return h + mod_ref[2:3, :] * acc, mod_ref, refs[n_row_refs + 2:]


def _mix_specs(lay, mixes, w, h):
    specs = []
    for rows in list(mixes) + [h]:
        specs += _row_specs(lay, rows)
    return specs + [_resident(w), lay.mod_spec(h.width)]


def _mix_args(mixes, w, h, mod):
    args = []
    for rows in list(mixes) + [h]:
        args += rows.arrays
    return args + [w, mod]


def _mix_arities(mixes, h):
    return tuple(len(rows.arrays) for rows in list(mixes) + [h])


def _proj_ffn_kernel(*refs, arities, lat_tiles):
    h1, mod_ref, (g_ref, wg_ref, wu_ref, wd_ref, o_ref) = _mixed_residual(refs, arities, lat_tiles)
    xn = _norm_mod(h1, g_ref, mod_ref, 3).astype(BF16)
    ff = wg_ref.shape[1]
    y = None
    for c0 in range(0, ff, FFN_CHUNK):
        c1 = min(c0 + FFN_CHUNK, ff)
        gt = jnp.dot(xn, wg_ref[:, c0:c1], preferred_element_type=F32)
        up = jnp.dot(xn, wu_ref[:, c0:c1], preferred_element_type=F32)
        a = (_silu(gt) * up).astype(BF16)
        part = jnp.dot(a, wd_ref[c0:c1, :], preferred_element_type=F32)
        y = part if y is None else y + part
    o_ref[...] = h1 + mod_ref[5:6, :] * y


def _proj_ffn(lay, mixes, w_out, h, mod, g, wg, wu, wd):
    d = h.width
    return pl.pallas_call(
        functools.partial(_proj_ffn_kernel, arities=_mix_arities(mixes, h), lat_tiles=lay.lat_tiles),
        grid=(lay.all_tiles,),
        in_specs=_mix_specs(lay, mixes, w_out, h) + [_resident(g), _resident(wg), _resident(wu), _resident(wd)],
        out_specs=lay.tile(d),
        out_shape=jax.ShapeDtypeStruct((lay.n_tok, d), F32),
        compiler_params=_params(("arbitrary",)),
        name="proj_ffn",
    )(*_mix_args(mixes, w_out, h, mod), g, wg, wu, wd)


def _na_front_kernel(h_ref, mod_ref, g_ref, win_ref, gq_ref, gk_ref, q_ref, k_ref, v_ref):
    xn = _norm_mod(h_ref[...], g_ref, mod_ref, 0)
    px = jnp.dot(xn.astype(BF16), win_ref[...], preferred_element_type=F32)
    width = C_HEADS * C_DH
    lane = lax.broadcasted_iota(jnp.int32, (px.shape[0], LANES), 1)
    low = lane < C_DH
    scale = C_DH ** -0.5 * LOG2E

    def norm_pairs(base, g_ref_, out_ref, mult):
        gg = g_ref_[...]
        for j in range(width // LANES):
            x = px[:, base + j * LANES:base + (j + 1) * LANES]
            x2 = x * x
            s_all = jnp.sum(x2, axis=-1, keepdims=True)
            s_lo = jnp.sum(jnp.where(low, x2, 0.0), axis=-1, keepdims=True)
            ss = jnp.where(low, s_lo, s_all - s_lo)
            inv = lax.rsqrt(ss * (1.0 / C_DH) + EPS)
            if mult != 1.0:
                inv = inv * mult
            out_ref[:, j * LANES:(j + 1) * LANES] = (x * inv * gg).astype(BF16)

    norm_pairs(0, gq_ref, q_ref, scale)
    norm_pairs(width, gk_ref, k_ref, 1.0)
    v_ref[...] = px[:, 2 * width:].astype(BF16)


def _na_front(lay, h, mod, g, win, gq, gk):
    n, d = h.shape
    width = C_HEADS * C_DH
    return pl.pallas_call(
        _na_front_kernel,
        grid=(lay.all_tiles,),
        in_specs=[lay.tile(d), lay.mod_spec(d), _resident(g), _resident(win), _resident(gq), _resident(gk)],
        out_specs=[lay.tile(width)] * 3,
        out_shape=[jax.ShapeDtypeStruct((n, width), BF16)] * 3,
        compiler_params=_params(("arbitrary",)),
        name="na_front",
    )(h, mod, g, win, gq, gk)


def _na_attn_kernel(bounded_ref, qx_ref, qc_ref, kx_ref, kc_ref, vx_ref, vc_ref, planes_ref, shift_ref,
                    *refs, rows_n):
    out_refs, bias_ref = refs[:-1], refs[-1]
    ox_ref = out_refs[0]

    @pl.when(pl.program_id(1) == 0)
    def _():
        idx = _na_group_rows(rows_n)
        for p in range(idx.shape[0]):
            for hh in range(2):
                for qi in range(NA_GROUP):
                    for a in range(NA_WINDOW):
                        half = (a % 2) * GRID_W
                        bias_ref[p, hh, qi * GRID_W:(qi + 1) * GRID_W, a * GRID_W:(a + 1) * GRID_W] = (
                            planes_ref[hh, int(idx[p, qi, a]), :, half:half + GRID_W])

    qlen = NA_GROUP * GRID_W
    wlen = NA_WINDOW * GRID_W
    n_groups = rows_n // NA_GROUP
    lane_q = lax.broadcasted_iota(jnp.int32, (qlen, LANES), 1)
    kc = kc_ref[...]
    vc = vc_ref[...]

    if len(out_refs) > 1:
        qc = qc_ref[...]
        lane_c = lax.broadcasted_iota(jnp.int32, qc.shape, 1)
        outs = []
        for hh in range(2):
            sel = (lane_c < C_DH) if hh == 0 else (lane_c >= C_DH)
            outs.append(_softmax_pv(jnp.where(sel, qc, jnp.zeros_like(qc)), kc, vc))
        out_refs[1][...] = jnp.where(lane_c < C_DH, outs[0], outs[1]).astype(BF16)

    def group_body(g, carry, bounded):
        start = jnp.clip(g * NA_GROUP - WIN_R // 2, 0, rows_n - NA_WINDOW)
        pat = jnp.where(g == 0, 0, jnp.where(g == n_groups - 1, 2, 1))
        q0 = pl.multiple_of(g * qlen, qlen)
        k0 = pl.multiple_of(start * GRID_W, GRID_W)
        qr = qx_ref[pl.ds(q0, qlen), :]
        kw = kx_ref[pl.ds(k0, wlen), :]
        vw = vx_ref[pl.ds(k0, wlen), :]
        res = []
        for hh in range(2):
            sel = (lane_q < C_DH) if hh == 0 else (lane_q >= C_DH)
            qh = jnp.where(sel, qr, jnp.zeros_like(qr))
            s_loc = lax.dot_general(qh, kw, (((1,), (1,)), ((), ())),
                                    preferred_element_type=F32) + bias_ref[pat, hh]
            s_ctx = lax.dot_general(qh, kc, (((1,), (1,)), ((), ())), preferred_element_type=F32)
            if bounded:
                p_loc = jnp.exp2(s_loc)
                p_ctx = jnp.exp2(s_ctx - shift_ref[:, 0:1])
            else:
                m = jnp.maximum(jnp.max(s_loc, axis=-1, keepdims=True),
                                jnp.max(s_ctx, axis=-1, keepdims=True))
                p_loc = jnp.exp2(s_loc - m)
                p_ctx = jnp.exp2(s_ctx - m)
            l = jnp.sum(p_loc, axis=-1, keepdims=True) + jnp.sum(p_ctx, axis=-1, keepdims=True)
            o = (jnp.dot(p_loc.astype(BF16), vw, preferred_element_type=F32)
                 + jnp.dot(p_ctx.astype(BF16), vc, preferred_element_type=F32))
            res.append(o * (1.0 / l))
        ox_ref[pl.ds(q0, qlen), :] = jnp.where(lane_q < C_DH, res[0], res[1]).astype(BF16)
        return carry

    @pl.when(bounded_ref[0] != 0)
    def _():
        lax.fori_loop(0, n_groups, functools.partial(group_body, bounded=True), 0, unroll=4)

    @pl.when(bounded_ref[0] == 0)
    def _():
        lax.fori_loop(0, n_groups, functools.partial(group_body, bounded=False), 0)


def _na_attn(lay, bounded, q, k, v, planes, shift, need_ctx):
    width = q.shape[-1]
    hp = C_HEADS // 2
    lat = pl.BlockSpec((lay.s, LANES), lambda p, i, f: (i, p))
    ctx = pl.BlockSpec((lay.l, LANES), lambda p, i, f: (lay.ctx_block(i), p))
    out_specs = [pl.BlockSpec((lay.s, LANES), lambda p, i, f: (i, p))]
    out_shape = [jax.ShapeDtypeStruct((lay.n_lat, width), BF16)]
    if need_ctx:
        out_specs.append(pl.BlockSpec((lay.l, LANES), lambda p, i, f: (i, p)))
        out_shape.append(jax.ShapeDtypeStruct((lay.b * lay.l, width), BF16))
    n_pat = _na_group_rows(lay.s // GRID_W).shape[0]
    return pl.pallas_call(
        functools.partial(_na_attn_kernel, rows_n=lay.s // GRID_W),
        grid_spec=pltpu.PrefetchScalarGridSpec(
            num_scalar_prefetch=1,
            grid=(hp, lay.b),
            in_specs=[lat, ctx, lat, ctx, lat, ctx,
                      pl.BlockSpec((2,) + planes.shape[1:], lambda p, i, f: (p, 0, 0, 0)),
                      pl.BlockSpec(shift.shape, lambda p, i, f: (0, 0))],
            out_specs=out_specs,
            scratch_shapes=[pltpu.VMEM((n_pat, 2, NA_GROUP * GRID_W, NA_WINDOW * GRID_W), F32)],
        ),
        out_shape=out_shape,
        compiler_params=_params(("arbitrary", "arbitrary")),
        name="na_attn",
    )(bounded, q, q, k, k, v, v, planes, shift)


def _proj_router_kernel(*refs, arities, lat_tiles, tiles_per_batch, ctx_mod_row):
    h1, mod_ref, (g_ref, wr_ref, xr_ref, route_ref) = _mixed_residual(refs, arities, lat_tiles)
    xn = _norm_mod(h1, g_ref, mod_ref, 3)
    d = xn.shape[-1]
    xr_ref[:, 0:d] = xn
    xr_ref[:, d + LANES:2 * d + LANES] = h1
    t = pl.program_id(0)
    mod_row = jnp.where(t < lat_tiles, t // tiles_per_batch, ctx_mod_row).astype(F32)
    wr = wr_ref[...]
    x_hi, w_hi = xn.astype(BF16), wr.astype(BF16)
    x_lo = (xn - x_hi.astype(F32)).astype(BF16)
    w_lo = (wr - w_hi.astype(F32)).astype(BF16)
    logits = (jnp.dot(x_hi, w_hi, preferred_element_type=F32) + jnp.dot(x_lo, w_hi, preferred_element_type=F32)
              + jnp.dot(x_hi, w_lo, preferred_element_type=F32))
    lane = lax.broadcasted_iota(jnp.int32, logits.shape, 1).astype(F32)
    lg = jnp.where(lane < N_EXPERTS, logits, -jnp.inf)
    m1 = jnp.max(lg, axis=-1, keepdims=True)
    i1 = jnp.min(jnp.where(lg == m1, lane, float(LANES)), axis=-1, keepdims=True)
    lg2 = jnp.where(lane == i1, -jnp.inf, lg)
    m2 = jnp.max(lg2, axis=-1, keepdims=True)
    i2 = jnp.min(jnp.where(lg2 == m2, lane, float(LANES)), axis=-1, keepdims=True)
    e2 = jnp.exp(m2 - m1)
    g1 = 1.0 / (1.0 + e2)
    g2 = e2 * g1
    first_low = i1 < i2
    vals = (jnp.minimum(i1, i2), jnp.maximum(i1, i2), jnp.where(first_low, g1, g2), jnp.where(first_low, g2, g1),
            mod_row)
    route = jnp.zeros_like(lane)
    for idx, val in enumerate(vals):
        route = jnp.where(lane == float(idx), val, route)
    route_ref[...] = route
    xr_ref[:, d:d + LANES] = route


def _proj_router(lay, mixes, w_out, h, mod, g, wr, n_tiles):
    d = h.width
    rows = n_tiles * lay.tm
    return pl.pallas_call(
        functools.partial(_proj_router_kernel, arities=_mix_arities(mixes, h), lat_tiles=lay.lat_tiles,
                          tiles_per_batch=lay.s // lay.tm, ctx_mod_row=lay.b),
        grid=(n_tiles,),
        in_specs=_mix_specs(lay, mixes, w_out, h) + [_resident(g), _resident(wr)],
        out_specs=[lay.tile(2 * d + LANES), lay.tile(LANES)],
        out_shape=[jax.ShapeDtypeStruct((rows, 2 * d + LANES), F32), jax.ShapeDtypeStruct((rows, LANES), F32)],
        compiler_params=_params(("arbitrary",)),
        name="proj_router",
    )(*_mix_args(mixes, w_out, h, mod), g, wr)


def _moe_plan(route, n_tok):
    n_pairs = len(_PAIR_LO)
    n_tiles = n_tok // MOE_TM + n_pairs
    rows = n_tiles * MOE_TM
    lo = route[:, 0].astype(jnp.int32)
    hi = route[:, 1].astype(jnp.int32)
    pid = (lo * (2 * N_EXPERTS - 1 - lo)) // 2 + (hi - lo - 1)
    onehot = (pid[:, None] == jnp.arange(n_pairs, dtype=jnp.int32)[None, :]).astype(jnp.int32)
    csum = jnp.cumsum(onehot, axis=0)
    rank = jnp.sum(csum * onehot, axis=1) - 1
    counts = csum[-1]
    padded = ((counts + MOE_TM - 1) // MOE_TM) * MOE_TM
    gend = jnp.cumsum(padded)
    gstart = gend - padded
    dest = jnp.sum(gstart[None, :] * onehot, axis=1) + rank
    src = jnp.full((rows,), -1, jnp.int32).at[dest].set(jnp.arange(n_tok, dtype=jnp.int32), unique_indices=True)
    valid = src >= 0
    src_tok = jnp.where(valid, src, 0)
    n_valid = jnp.sum(valid.reshape(n_tiles, MOE_TM).astype(jnp.int32), axis=1)
    n_used = gend[-1] // MOE_TM
    tile_row = jnp.minimum(jnp.arange(n_tiles, dtype=jnp.int32), n_used - 1) * MOE_TM
    group = jnp.minimum(jnp.sum((gend[None, :] <= tile_row[:, None]).astype(jnp.int32), axis=1), n_pairs - 1)
    ea = jnp.asarray(_PAIR_LO, jnp.int32)[group]
    eb = jnp.asarray(_PAIR_HI, jnp.int32)[group]
    return ea, eb, src_tok, n_valid, n_used.astype(jnp.int32).reshape(1)


def _moe_kernel(ea_ref, eb_ref, src_ref, nvalid_ref, nused_ref,
                x_hbm, g2_ref, wga_ref, wua_ref, wda_ref, wgb_ref, wub_ref, wdb_ref,
                y_hbm, xbuf, ybuf, gsem, ssem, *, n_mod_rows):
    i = pl.program_id(0)
    n_steps = pl.num_programs(0)
    n_used = nused_ref[0]
    slot = i % 2

    def gather_copy(tile, slot_, r):
        return pltpu.make_async_copy(x_hbm.at[pl.ds(src_ref[tile * MOE_TM + r], 1)],
                                     xbuf.at[slot_, pl.ds(r, 1)], gsem.at[slot_])

    def scatter_copy(tile, slot_, r):
        return pltpu.make_async_copy(ybuf.at[slot_, pl.ds(r, 1)],
                                     y_hbm.at[pl.ds(src_ref[tile * MOE_TM + r], 1)], ssem.at[slot_])

    def start_gather(tile, slot_):
        for r in range(MOE_TM):
            gather_copy(tile, slot_, r).start()

    def wait_gather(slot_):
        pltpu.make_async_copy(x_hbm.at[pl.ds(0, MOE_TM)], xbuf.at[slot_], gsem.at[slot_]).wait()

    def start_scatter(tile, slot_):
        full = nvalid_ref[tile] == MOE_TM

        @pl.when(full)
        def _():
            for r in range(MOE_TM):
                scatter_copy(tile, slot_, r).start()

        @pl.when(jnp.logical_not(full))
        def _():
            def body(r, carry):
                scatter_copy(tile, slot_, r).start()
                return carry
            lax.fori_loop(0, nvalid_ref[tile], body, 0)

    def wait_scatter(tile, slot_):
        full = nvalid_ref[tile] == MOE_TM

        @pl.when(full)
        def _():
            pltpu.make_async_copy(ybuf.at[slot_], y_hbm.at[pl.ds(0, MOE_TM)], ssem.at[slot_]).wait()

        @pl.when(jnp.logical_not(full))
        def _():
            def body(r, carry):
                scatter_copy(tile, slot_, r).wait()
                return carry
            lax.fori_loop(0, nvalid_ref[tile], body, 0)

    @pl.when((i >= 2) & (i - 2 < n_used))
    def _():
        wait_scatter(i - 2, slot)

    @pl.when(i < n_used)
    def _():
        @pl.when(i == 0)
        def _():
            start_gather(0, 0)

        @pl.when(i + 1 < n_used)
        def _():
            start_gather(i + 1, 1 - slot)

        wait_gather(slot)
        d = ybuf.shape[-1]
        x = xbuf[slot, :, 0:d].astype(BF16)
        gates = xbuf[slot, :, d:d + LANES]

        def expert(wg_ref, wu_ref, wd_ref):
            gt = jnp.dot(x, wg_ref[...], preferred_element_type=F32)
            up = jnp.dot(x, wu_ref[...], preferred_element_type=F32)
            a = (_silu(gt) * up).astype(BF16)
            return jnp.dot(a, wd_ref[...], preferred_element_type=F32)

        y = (gates[:, 2:3] * expert(wga_ref, wua_ref, wda_ref)
             + gates[:, 3:4] * expert(wgb_ref, wub_ref, wdb_ref))
        mod_row = gates[:, 4:5]
        gate2 = jnp.zeros_like(y)
        for r in range(n_mod_rows):
            gate2 = jnp.where(mod_row == float(r), g2_ref[r:r + 1, :], gate2)
        ybuf[slot] = xbuf[slot, :, d + LANES:2 * d + LANES] + gate2 * y
        start_scatter(i, slot)

    @pl.when(i == n_steps - 1)
    def _():
        for back in (2, 1):
            @pl.when(n_steps - back < n_used)
            def _():
                wait_scatter(n_steps - back, (n_steps - back) % 2)


def _moe(xr, route, gate2, n_mod_rows, wg, wu, wd, layer):
    n_tok, dr = xr.shape
    _, _, d, f = wg.shape
    assert dr == 2 * d + LANES
    ea, eb, src_tok, n_valid, n_used = _moe_plan(route, n_tok)
    n_tiles = ea.shape[0]
    wspec = lambda shape, which: pl.BlockSpec(
        (None, None) + shape, lambda i, ea_, eb_, s_, d_, u_: (layer, (ea_, eb_)[which][i], 0, 0))
    return pl.pallas_call(
        functools.partial(_moe_kernel, n_mod_rows=n_mod_rows),
        grid_spec=pltpu.PrefetchScalarGridSpec(
            num_scalar_prefetch=5,
            grid=(n_tiles,),
            in_specs=[
                pl.BlockSpec(memory_space=pl.ANY),
                _resident(gate2),
                wspec((d, f), 0), wspec((d, f), 0), wspec((f, d), 0),
                wspec((d, f), 1), wspec((d, f), 1), wspec((f, d), 1),
            ],
            out_specs=pl.BlockSpec(memory_space=pl.ANY),
            scratch_shapes=[
                pltpu.VMEM((2, MOE_TM, dr), F32),
                pltpu.VMEM((2, MOE_TM, d), F32),
                pltpu.SemaphoreType.DMA((2,)),
                pltpu.SemaphoreType.DMA((2,)),
            ],
        ),
        out_shape=jax.ShapeDtypeStruct((n_tok, d), F32),
        compiler_params=_params(("arbitrary",)),
        name="moe_experts",
    )(ea, eb, src_tok, n_valid, n_used, xr, gate2, wg, wu, wd, wg, wu, wd)


def _rope_tables(s, ctx_rows):
    t = jnp.arange(s)
    rows = (t // GRID_W).astype(F32)
    cols = (t % GRID_W).astype(F32)
    n_pairs = A_ROPE // 2
    per_axis = n_pairs // 2
    inv = ROPE_THETA ** (-jnp.arange(per_axis, dtype=F32) / per_axis)
    ang = jnp.concatenate([rows[:, None] * inv, cols[:, None] * inv], axis=-1)
    cos = jnp.repeat(jnp.cos(ang), 2, axis=-1)
    sin = jnp.repeat(jnp.sin(ang), 2, axis=-1)
    sign = jnp.tile(jnp.array([-1.0, 1.0], F32), n_pairs)
    pad = HEAD_PAD - A_QK
    t1 = jnp.concatenate([jnp.ones((s, A_NOPE), F32), cos, jnp.zeros((s, pad), F32)], axis=-1)
    t2 = jnp.concatenate([jnp.zeros((s, A_NOPE), F32), sin * sign, jnp.zeros((s, pad), F32)], axis=-1)
    c1 = jnp.concatenate([jnp.ones((ctx_rows, A_QK), F32), jnp.zeros((ctx_rows, pad), F32)], axis=-1)
    c2 = jnp.zeros((ctx_rows, HEAD_PAD), F32)
    return jnp.concatenate([t1, c1], axis=0), jnp.concatenate([t2, c2], axis=0)


_PAIR_SWAP = np.arange(A_ROPE) ^ 1


def _rope_lane_vec(g_tail, swapped):
    gt = g_tail[_PAIR_SWAP] if swapped else g_tail
    return jnp.concatenate([jnp.zeros((A_NOPE,), F32), gt, jnp.zeros((HEAD_PAD - A_QK,), F32)])[None]


def _mla_weights(w_in, w_uq, w_ukv, g_q, g_k):
    d = w_in.shape[0]
    q_end = A_Q_RANK
    kv_end = q_end + A_KV_RANK
    r_end = kv_end + A_ROPE
    z = lambda n: jnp.zeros((d, n), w_in.dtype)
    kr = w_in[:, kv_end:r_end]
    pad = HEAD_PAD - A_QK
    win = jnp.concatenate([w_in[:, :kv_end], z(A_NOPE), kr, z(pad), z(A_NOPE), kr[:, _PAIR_SWAP], z(pad),
                           w_in[:, r_end:]], axis=-1).astype(BF16)
    wq = w_uq.reshape(A_Q_RANK, A_HEADS, A_QK)
    zq = lambda n: jnp.zeros((A_Q_RANK, A_HEADS, n), wq.dtype)
    wq_main = jnp.concatenate([wq, zq(pad)], axis=-1)
    wq_swap = jnp.concatenate([zq(A_NOPE), wq[:, :, A_NOPE:][:, :, _PAIR_SWAP], zq(pad)], axis=-1)
    wq_ext = jnp.concatenate([wq_main.reshape(A_Q_RANK, -1), wq_swap.reshape(A_Q_RANK, -1)], axis=-1).astype(BF16)
    wkv = w_ukv.reshape(A_KV_RANK, A_HEADS, A_NOPE + A_V)
    zkv = jnp.zeros((A_KV_RANK, A_HEADS, HEAD_PAD - A_NOPE), wkv.dtype)
    wk = jnp.concatenate([wkv[:, :, :A_NOPE], zkv], axis=-1)
    wv = jnp.concatenate([wkv[:, :, A_NOPE:], zkv], axis=-1)
    wkv_ext = jnp.concatenate([wk.reshape(A_KV_RANK, -1), wv.reshape(A_KV_RANK, -1)], axis=-1).astype(BF16)

    bound = math.sqrt(A_QK) * jnp.max(jnp.abs(g_q)) * jnp.max(jnp.abs(g_k))
    bounded = bound <= MLA_SAFE_BOUND
    shift = jnp.where(bounded, -bound * LOG2E, 0.0)
    unit = lambda lane_idx: jnp.zeros((1, HEAD_PAD), F32).at[0, lane_idx].set(1.0)
    gq = jnp.concatenate([jnp.concatenate([g_q, jnp.zeros((pad,), F32)])[None], _rope_lane_vec(g_q[A_NOPE:], True),
                          unit(A_QK) * shift], axis=0)
    gk = jnp.concatenate([jnp.concatenate([g_k[:A_NOPE], jnp.zeros((HEAD_PAD - A_NOPE,), F32)])[None],
                          _rope_lane_vec(g_k[A_NOPE:], False), _rope_lane_vec(g_k[A_NOPE:], True),
                          unit(A_QK), unit(A_V)], axis=0)
    return win, wq_ext, wkv_ext, gq, gk, bounded.astype(jnp.int32).reshape(1)


def _dft_mats(n, norm):
    k = jnp.arange(n, dtype=jnp.int32)
    w = GRID_W if n % GRID_W == 0 and n > GRID_W else 1
    hi = jnp.arange(n // w, dtype=jnp.int32) * w
    lo = jnp.arange(w, dtype=jnp.int32)
    ang_hi = ((k[:, None] * hi[None, :]) % n).astype(F32) * (2.0 * math.pi / n)
    ang_lo = ((k[:, None] * lo[None, :]) % n).astype(F32) * (2.0 * math.pi / n)
    ch, sh = (jnp.repeat(f(ang_hi), w, axis=1) for f in (jnp.cos, jnp.sin))
    cl, sl = (jnp.tile(f(ang_lo), (1, n // w)) for f in (jnp.cos, jnp.sin))
    cos = ((ch * cl - sh * sl) * norm).astype(BF16)
    sin = ((sh * cl + ch * sl) * norm).astype(BF16)
    return cos, sin


def _channel_dft():
    c = np.arange(F_CH)
    ang = 2.0 * np.pi * ((c[:, None] * c[None, :]) % F_CH) / F_CH
    eye = np.eye(F_GROUPS)
    cb = np.kron(eye, np.cos(ang)) / math.sqrt(F_CH)
    sb = np.kron(eye, np.sin(ang)) / math.sqrt(F_CH)
    return jnp.asarray(np.concatenate([cb, sb], axis=1), BF16)


def _na_group_rows(rows_n):
    assert rows_n % NA_GROUP == 0 and rows_n >= NA_WINDOW + NA_GROUP
    n_groups = rows_n // NA_GROUP
    n_dr = 2 * WIN_R - 1
    idx = np.full((3, NA_GROUP, NA_WINDOW), n_dr, np.int32)
    seen = {}
    for g in range(n_groups):
        gs = int(np.clip(g * NA_GROUP - WIN_R // 2, 0, rows_n - NA_WINDOW))
        pat = 0 if g == 0 else (2 if g == n_groups - 1 else 1)
        cur = np.full((NA_GROUP, NA_WINDOW), n_dr, np.int32)
        for qi in range(NA_GROUP):
            r = g * NA_GROUP + qi
            start = int(np.clip(r - WIN_R // 2, 0, rows_n - WIN_R))
            assert gs <= start and start + WIN_R <= gs + NA_WINDOW
            for a in range(start, start + WIN_R):
                cur[qi, a - gs] = a - r + (WIN_R - 1)
        assert pat not in seen or np.array_equal(seen[pat], cur)
        seen[pat] = cur
        idx[pat] = cur
    return idx


def _na_score_bound(g_q, g_k, rpb):
    qk = math.sqrt(C_DH) * jnp.max(jnp.abs(g_q)) * jnp.max(jnp.abs(g_k))
    bound = qk + jnp.maximum(jnp.max(rpb), 0.0)
    lowest = -qk + jnp.minimum(jnp.min(rpb), 0.0)
    bounded = (bound - lowest) <= 2.0 * MLA_SAFE_BOUND
    shift = jnp.where(bounded, bound, 0.0).astype(F32)
    return bounded.astype(jnp.int32).reshape(1), shift


def _na_bias(rpb, shift):
    h, n_dr, n_dc = rpb.shape
    qc = np.arange(GRID_W)
    cs = np.clip(qc - WIN_C // 2, 0, GRID_W - WIN_C)
    kcol = np.arange(GRID_W)
    valid = (kcol[None, :] >= cs[:, None]) & (kcol[None, :] < cs[:, None] + WIN_C)
    dc = kcol[None, :] - qc[:, None] + (WIN_C - 1)
    onehot = (valid[:, :, None] & (dc[:, :, None] == np.arange(n_dc)[None, None, :])).astype(np.float32)
    col = jnp.einsum('hdc,qkc->hdqk', rpb.astype(F32), jnp.asarray(onehot), precision=lax.Precision.HIGHEST)
    col = jnp.where(jnp.asarray(valid)[None, None], (col - shift) * LOG2E, NEG_BIG)
    col = jnp.concatenate([col, jnp.full((h, 1, GRID_W, GRID_W), NEG_BIG, F32)], axis=1)
    return jnp.concatenate([col, col], axis=-1)


def kernel(x, c, ctx, c_ctx, w_mod, b_mod, norm_g, a_w_in, a_g_cq, a_g_ckv, a_w_uq, a_w_ukv, a_g_q, a_g_k, a_w_out,
           f_w_gate, f_w_up, f_w_down, c_w_in, c_g_q, c_g_k, c_rpb, c_w_out, m_w_router, m_w_gate, m_w_up, m_w_down):
    b, s, d = x.shape
    l = ctx.shape[1]
    depth = w_mod.shape[0]
    lay = _make_layout(b, s, l)

    mod_rows = 16
    cin = jnp.concatenate([c, c_ctx[None], jnp.zeros((mod_rows - b - 1, d), F32)], axis=0)
    mod_all = _modulation(cin, w_mod, b_mod).reshape(depth, mod_rows, 6, d)

    t1, t2 = _rope_tables(s, lay.tm)
    cn, sn = _dft_mats(s, 1.0 / math.sqrt(s))
    cc, sc = _dft_mats(l, 1.0 / math.sqrt(l))
    cb = _channel_dft()

    moe_wg, moe_wu, moe_wd = m_w_gate.astype(BF16), m_w_up.astype(BF16), m_w_down.astype(BF16)
    h = _Rows(x.reshape(b * s, d), ctx.reshape(b * l, d))
    for i in range(depth):
        j = i // 2
        need_ctx = i < depth - 1
        mod = mod_all[i]
        g1 = norm_g[i, 0][None]
        g2 = norm_g[i, 1][None]
        if i % 2 == 0:
            win, wq_ext, wkv_ext, gq, gk, bounded = _mla_weights(a_w_in[j], a_w_uq[j], a_w_ukv[j],
                                                                 a_g_q[j], a_g_k[j])
            q, k, v, fcs = _mla_front(lay, h, mod, g1, win, a_g_cq[j][None], a_g_ckv[j][None], wq_ext, wkv_ext,
                                      t1, t2, gq, gk, cb)
            att = _mla_attn(lay, bounded, q, k, v)
            yf = _dft(lay, fcs, cn, sn, cc, sc)
            h = _Rows(_proj_ffn(lay, [_Rows(att), _Rows(yf)], a_w_out[j].astype(BF16), h, mod, g2,
                                f_w_gate[j].astype(BF16), f_w_up[j].astype(BF16), f_w_down[j].astype(BF16)))
        else:
            gq2 = jnp.tile(c_g_q[j], 2)[None]
            gk2 = jnp.tile(c_g_k[j], 2)[None]
            q, k, v = _na_front(lay, h.lat, mod, g1, c_w_in[j].astype(BF16), gq2, gk2)
            na_bounded, na_shift = _na_score_bound(c_g_q[j], c_g_k[j], c_rpb[j])
            bias = _na_bias(c_rpb[j], na_shift)
            outs = _na_attn(lay, na_bounded, q, k, v, bias, jnp.full((1, LANES), na_shift * LOG2E, F32), need_ctx)
            att = _Rows(outs[0], outs[1]) if need_ctx else _Rows(outs[0])
            wr = jnp.concatenate([m_w_router[j], jnp.zeros((d, LANES - N_EXPERTS), F32)], axis=-1)
            n_tiles = lay.all_tiles if need_ctx else lay.lat_tiles
            xr, route = _proj_router(lay, [att], c_w_out[j].astype(BF16), h, mod, g2, wr, n_tiles)
            h = _Rows(_moe(xr, route, mod[:, 5, :], b + 1, moe_wg, moe_wu, moe_wd, j))
    return h.lat[:lay.n_lat].reshape(b, s, d)
```

```python
import functools
import math
from typing import NamedTuple

import jax
import jax.numpy as jnp
import numpy as np
from jax import lax
from jax.experimental import pallas as pl
from jax.experimental.pallas import tpu as pltpu

F32 = jnp.float32
BF16 = jnp.bfloat16

GRID_W = 64
A_HEADS = 12
A_NOPE = 64
A_ROPE = 32
A_QK = A_NOPE + A_ROPE
A_V = 64
A_Q_RANK = 256
A_KV_RANK = 128
F_GROUPS = 4
F_CH = 64
F_WIDTH = F_GROUPS * F_CH
C_HEADS = 16
C_DH = 64
WIN_R = 8
WIN_C = 16
N_EXPERTS = 8
ROPE_THETA = 10000.0
EPS = 1e-6

LANES = 128
HEAD_PAD = 128
TM = 512
TQ = 256
FFN_CHUNK = 1536
MOE_TM = 256
_PAIR_LO = tuple(lo for lo in range(N_EXPERTS) for hi in range(lo + 1, N_EXPERTS))
_PAIR_HI = tuple(hi for lo in range(N_EXPERTS) for hi in range(lo + 1, N_EXPERTS))
LOG2E = math.log2(math.e)
MLA_SAFE_BOUND = 40.0
MLA_KCHUNK = 512
MLA_HPS = 6
NA_GROUP = 4
NA_WINDOW = NA_GROUP + WIN_R
VMEM_LIMIT = 52 * 1024 * 1024
NEG_BIG = -1e30


def _params(sem, vmem=VMEM_LIMIT):
    return pltpu.CompilerParams(dimension_semantics=sem, vmem_limit_bytes=vmem)


def _rms(x, g, n=None):
    n = x.shape[-1] if n is None else n
    ss = jnp.sum(x * x, axis=-1, keepdims=True)
    return x * lax.rsqrt(ss * (1.0 / n) + EPS) * g


def _silu(x):
    return x * (1.0 / (1.0 + jnp.exp(-x)))


class _Layout(NamedTuple):
    b: int
    s: int
    l: int
    tm: int

    @property
    def n_lat(self):
        return self.b * self.s

    @property
    def n_tok(self):
        return self.b * (self.s + self.l)

    @property
    def lat_tiles(self):
        return self.n_lat // self.tm

    @property
    def all_tiles(self):
        return self.n_tok // self.tm

    def tile(self, width):
        return pl.BlockSpec((self.tm, width), lambda t: (t, 0))

    def mod_spec(self, d):
        per_batch = self.s // self.tm
        return pl.BlockSpec((None, 6, d), lambda t: (jnp.where(t < self.lat_tiles, t // per_batch, self.b), 0, 0))

    def ctx_block(self, i):
        return self.n_lat // self.l + i


class _Rows(NamedTuple):
    lat: jax.Array
    ctx: jax.Array | None = None

    @property
    def width(self):
        return self.lat.shape[-1]

    @property
    def arrays(self):
        return [self.lat] if self.ctx is None else [self.lat, self.ctx]


def _row_specs(lay, rows):
    if rows.ctx is None:
        return [lay.tile(rows.width)]
    tm, nl = lay.tm, lay.lat_tiles
    return [pl.BlockSpec((tm, rows.width), lambda t: (jnp.minimum(t, nl - 1), 0)),
            pl.BlockSpec((tm, rows.width), lambda t: (jnp.maximum(t - nl, 0), 0))]


def _row_tiles(refs, arities, lat_tiles):
    tiles, pos = [], 0
    for arity in arities:
        if arity == 1:
            tiles.append(refs[pos][...])
        else:
            tiles.append(jnp.where(pl.program_id(0) < lat_tiles, refs[pos][...], refs[pos + 1][...]))
        pos += arity
    return tiles


def _make_layout(b, s, l):
    tm = TM if (s % TM == 0 and (b * l) % TM == 0) else l
    assert l == TQ and s % tm == 0 and (b * l) % tm == 0 and s % GRID_W == 0
    return _Layout(b, s, l, tm)


def _resident(a):
    return pl.BlockSpec(a.shape, lambda *_: (0,) * a.ndim, pipeline_mode=pl.Buffered(1))


def _norm_mod(x, g_ref, mod_ref, row):
    xn = _rms(x, g_ref[...])
    return xn * (1.0 + mod_ref[row + 1:row + 2, :]) + mod_ref[row:row + 1, :]


def _mod_kernel(c_ref, w_ref, b_ref, o_ref):
    s = _silu(c_ref[...])
    o_ref[...] = jnp.dot(s, w_ref[...], precision=lax.Precision.HIGHEST,
                         preferred_element_type=F32) + b_ref[...]


def _modulation(cin, w_mod, b_mod):
    depth, d, n = w_mod.shape
    tn = 1536
    rows = cin.shape[0]
    return pl.pallas_call(
        _mod_kernel,
        grid=(depth, n // tn),
        in_specs=[
            pl.BlockSpec((rows, d), lambda l, j: (0, 0)),
            pl.BlockSpec((None, d, tn), lambda l, j: (l, 0, j)),
            pl.BlockSpec((None, 1, tn), lambda l, j: (l, 0, j)),
        ],
        out_specs=pl.BlockSpec((None, rows, tn), lambda l, j: (l, 0, j)),
        out_shape=jax.ShapeDtypeStruct((depth, rows, n), F32),
        compiler_params=_params(("arbitrary", "arbitrary")),
        name="modulation",
    )(cin, w_mod, b_mod.reshape(depth, 1, n))


def _mla_front_kernel(*refs, h_arity, lat_tiles):
    (mod_ref, g_ref, win_ref, gcq_ref, gckv_ref, wq_ref, wkv_ref, t1_ref, t2_ref, gq_ref, gk_ref, cb_ref,
     q_ref, k_ref, v_ref, fcs_ref) = refs[h_arity:]
    (h,) = _row_tiles(refs, (h_arity,), lat_tiles)
    xn = _norm_mod(h, g_ref, mod_ref, 0)
    px = jnp.dot(xn.astype(BF16), win_ref[...], preferred_element_type=F32)
    t1 = t1_ref[...]
    t2 = t2_ref[...]
    hw = A_HEADS * HEAD_PAD

    qn = _rms(px[:, 0:A_Q_RANK], gcq_ref[...]).astype(BF16)
    qq = jnp.dot(qn, wq_ref[...], preferred_element_type=F32)
    aq = t1 * gq_ref[0:1, :]
    bq = t2 * gq_ref[1:2, :]
    q_shift = gq_ref[2:3, :]
    scale = A_QK ** -0.5 * LOG2E
    for h in range(A_HEADS):
        qm = qq[:, h * HEAD_PAD:(h + 1) * HEAD_PAD]
        qs = qq[:, hw + h * HEAD_PAD:hw + (h + 1) * HEAD_PAD]
        ss = jnp.sum(qm * qm, axis=-1, keepdims=True)
        inv = lax.rsqrt(ss * (1.0 / A_QK) + EPS) * scale
        q_ref[:, h * HEAD_PAD:(h + 1) * HEAD_PAD] = ((qm * aq + qs * bq) * inv + q_shift).astype(BF16)

    kvn = _rms(px[:, A_Q_RANK:A_Q_RANK + A_KV_RANK], gckv_ref[...]).astype(BF16)
    kv = jnp.dot(kvn, wkv_ref[...], preferred_element_type=F32)
    krm = px[:, 384:512]
    krs = px[:, 512:640]
    tail = krm * (t1 * gk_ref[1:2, :]) + krs * (t2 * gk_ref[2:3, :])
    ssr = jnp.sum(krm * krm, axis=-1, keepdims=True)
    gkn = gk_ref[0:1, :]
    k_one = gk_ref[3:4, :]
    v_one = gk_ref[4:5, :]
    for h in range(A_HEADS):
        km = kv[:, h * HEAD_PAD:(h + 1) * HEAD_PAD]
        ss = jnp.sum(km * km, axis=-1, keepdims=True) + ssr
        inv = lax.rsqrt(ss * (1.0 / A_QK) + EPS)
        k_ref[:, h * HEAD_PAD:(h + 1) * HEAD_PAD] = ((km * gkn + tail) * inv + k_one).astype(BF16)
        v_ref[:, h * HEAD_PAD:(h + 1) * HEAD_PAD] = (kv[:, hw + h * HEAD_PAD:hw + (h + 1) * HEAD_PAD]
                                                     + v_one).astype(BF16)

    f = px[:, 640:896].astype(BF16)
    fcs_ref[...] = jnp.dot(f, cb_ref[...], preferred_element_type=F32).astype(BF16)


def _mla_front(lay, h, mod, g, win, gcq, gckv, wq, wkv, t1, t2, gq, gk, cb):
    n, d = lay.n_tok, h.width
    hw = A_HEADS * HEAD_PAD
    per_batch = lay.s // lay.tm
    rope = pl.BlockSpec((lay.tm, LANES), lambda t: (jnp.where(t < lay.lat_tiles, t % per_batch, per_batch), 0))
    return pl.pallas_call(
        functools.partial(_mla_front_kernel, h_arity=len(h.arrays), lat_tiles=lay.lat_tiles),
        grid=(lay.all_tiles,),
        in_specs=_row_specs(lay, h) + [
            lay.mod_spec(d),
            _resident(g), _resident(win), _resident(gcq), _resident(gckv), _resident(wq), _resident(wkv),
            rope, rope,
            _resident(gq), _resident(gk), _resident(cb),
        ],
        out_specs=[lay.tile(hw), lay.tile(hw), lay.tile(hw), lay.tile(2 * F_WIDTH)],
        out_shape=[
            jax.ShapeDtypeStruct((n, hw), BF16),
            jax.ShapeDtypeStruct((n, hw), BF16),
            jax.ShapeDtypeStruct((n, hw), BF16),
            jax.ShapeDtypeStruct((n, 2 * F_WIDTH), BF16),
        ],
        compiler_params=_params(("arbitrary",)),
        name="mla_front",
    )(*h.arrays, mod, g, win, gcq, gckv, wq, wkv, t1, t2, gq, gk, cb)


def _softmax_pv(q, k, v):
    s = lax.dot_general(q, k, (((1,), (1,)), ((), ())), preferred_element_type=F32)
    m = jnp.max(s, axis=-1, keepdims=True)
    p = jnp.exp2(s - m)
    l = jnp.sum(p, axis=-1, keepdims=True)
    o = jnp.dot(p.astype(BF16), v, preferred_element_type=F32)
    return o * (1.0 / l)


def _mla_attn_kernel(bounded_ref, q_ref, kx_ref, kc_ref, vx_ref, vc_ref, o_ref, m_ref, acc_ref):
    t = pl.program_id(2)
    ck = MLA_KCHUNK
    n_chunks = kx_ref.shape[0] // ck

    def head(ref, rows=slice(None)):
        return lambda hh: ref[rows, hh * HEAD_PAD:(hh + 1) * HEAD_PAD]

    def scores(hh, k):
        return lax.dot_general(head(q_ref)(hh), k, (((1,), (1,)), ((), ())), preferred_element_type=F32)

    def bounded(with_latent):
        for hh in range(MLA_HPS):
            p = jnp.exp2(scores(hh, head(kc_ref)(hh))).astype(BF16)
            acc = jnp.dot(p, head(vc_ref)(hh), preferred_element_type=F32)
            if with_latent:
                p = jnp.exp2(scores(hh, head(kx_ref)(hh))).astype(BF16)
                acc = acc + jnp.dot(p, head(vx_ref)(hh), preferred_element_type=F32)
            acc_ref[hh] = acc

    def online(with_latent):
        for hh in range(MLA_HPS):
            s = scores(hh, head(kc_ref)(hh))
            m = jnp.max(s, axis=-1, keepdims=True)
            m_ref[hh] = jnp.broadcast_to(m, m_ref.shape[1:])
            acc_ref[hh] = jnp.dot(jnp.exp2(s - m).astype(BF16), head(vc_ref)(hh), preferred_element_type=F32)

        def body(c, carry):
            rows = pl.ds(pl.multiple_of(c * ck, ck), ck)
            for hh in range(MLA_HPS):
                s = scores(hh, head(kx_ref, rows)(hh))
                m_old = m_ref[hh]
                m_new = jnp.maximum(m_old, jnp.max(s, axis=-1, keepdims=True))
                p = jnp.exp2(s - jnp.tile(m_new, (1, ck // LANES))).astype(BF16)
                acc_ref[hh] = (jnp.exp2(m_old - m_new) * acc_ref[hh]
                               + jnp.dot(p, head(vx_ref, rows)(hh), preferred_element_type=F32))
                m_ref[hh] = m_new
            return carry

        if with_latent:
            lax.fori_loop(0, n_chunks, body, 0)

    is_ctx = t == pl.num_programs(2) - 1
    fast = bounded_ref[0] != 0

    @pl.when(fast & is_ctx)
    def _():
        bounded(False)

    @pl.when(fast & jnp.logical_not(is_ctx))
    def _():
        bounded(True)

    @pl.when(jnp.logical_not(fast) & is_ctx)
    def _():
        online(False)

    @pl.when(jnp.logical_not(fast) & jnp.logical_not(is_ctx))
    def _():
        online(True)

    lane = lax.broadcasted_iota(jnp.int32, (o_ref.shape[0], LANES), 1)
    for hp in range(MLA_HPS // 2):
        outs = []
        for hh in (2 * hp, 2 * hp + 1):
            acc = acc_ref[hh]
            outs.append(acc * (1.0 / acc[:, A_V:A_V + 1]))
        o_ref[:, hp * LANES:(hp + 1) * LANES] = jnp.where(
            lane < A_V, outs[0], pltpu.roll(outs[1], A_V, axis=1)).astype(BF16)


def _query_block(lay, i, t):
    nq = lay.s // TQ
    return jnp.where(t < nq, i * nq + t, lay.ctx_block(i))


def _mla_attn(lay, bounded, q, k, v):
    hp = A_HEADS // MLA_HPS
    assert lay.s % MLA_KCHUNK == 0
    wq = MLA_HPS * HEAD_PAD
    lat = pl.BlockSpec((lay.s, wq), lambda i, p, t, f: (i, p))
    ctx = pl.BlockSpec((lay.l, wq), lambda i, p, t, f: (lay.ctx_block(i), p))
    return pl.pallas_call(
        _mla_attn_kernel,
        grid_spec=pltpu.PrefetchScalarGridSpec(
            num_scalar_prefetch=1,
            grid=(lay.b, hp, lay.s // TQ + 1),
            in_specs=[
                pl.BlockSpec((TQ, wq), lambda i, p, t, f: (_query_block(lay, i, t), p)),
                lat, ctx, lat, ctx,
            ],
            out_specs=pl.BlockSpec((TQ, MLA_HPS * A_V), lambda i, p, t, f: (_query_block(lay, i, t), p)),
            scratch_shapes=[
                pltpu.VMEM((MLA_HPS, TQ, LANES), F32),
                pltpu.VMEM((MLA_HPS, TQ, HEAD_PAD), F32),
            ],
        ),
        out_shape=jax.ShapeDtypeStruct((lay.n_tok, A_HEADS * A_V), BF16),
        compiler_params=_params(("arbitrary", "arbitrary", "arbitrary")),
        name="mla_attn",
    )(bounded, q, k, k, v, v)


def _dft_kernel(cn_ref, sn_ref, cc_ref, sc_ref, fx_ref, fc_ref, y_ref):
    is_ctx = pl.program_id(1) == pl.num_programs(1) - 1

    def run(c_ref, s_ref, f_ref):
        y = (jnp.dot(c_ref[...], f_ref[:, 0:F_WIDTH], preferred_element_type=F32)
             - jnp.dot(s_ref[...], f_ref[:, F_WIDTH:2 * F_WIDTH], preferred_element_type=F32))
        y_ref[...] = y.astype(BF16)

    @pl.when(is_ctx)
    def _():
        run(cc_ref, sc_ref, fc_ref)

    @pl.when(jnp.logical_not(is_ctx))
    def _():
        run(cn_ref, sn_ref, fx_ref)


def _dft(lay, fcs, cn, sn, cc, sc):
    nq = lay.s // TQ
    table = pl.BlockSpec((TQ, lay.s), lambda i, t: (jnp.minimum(t, nq - 1), 0))
    return pl.pallas_call(
        _dft_kernel,
        grid=(lay.b, nq + 1),
        in_specs=[
            table, table, _resident(cc), _resident(sc),
            pl.BlockSpec((lay.s, 2 * F_WIDTH), lambda i, t: (i, 0)),
            pl.BlockSpec((lay.l, 2 * F_WIDTH), lambda i, t: (lay.ctx_block(i), 0)),
        ],
        out_specs=pl.BlockSpec((TQ, F_WIDTH), lambda i, t: (_query_block(lay, i, t), 0)),
        out_shape=jax.ShapeDtypeStruct((lay.n_tok, F_WIDTH), BF16),
        compiler_params=_params(("arbitrary", "arbitrary")),
        name="fourier_dft",
    )(cn, sn, cc, sc, fcs, fcs)


def _mixed_residual(refs, arities, lat_tiles):
    n_row_refs = sum(arities)
    *mixes, h = _row_tiles(refs, arities, lat_tiles)
    w_ref, mod_ref = refs[n_row_refs:n_row_refs + 2]
    acc = None
    off = 0
    for m in mixes:
        wd = m.shape[-1]
        part = jnp.dot(m, w_ref[off:off + wd, :], preferred_element_type=F32)
        acc = part if acc is None else acc + part
        off += wd
    return h + mod_ref[2:3, :] * acc, mod_ref, refs[n_row_refs + 2:]


def _mix_specs(lay, mixes, w, h):
    specs = []
    for rows in list(mixes) + [h]:
        specs += _row_specs(lay, rows)
    return specs + [_resident(w), lay.mod_spec(h.width)]


def _mix_args(mixes, w, h, mod):
    args = []
    for rows in list(mixes) + [h]:
        args += rows.arrays
    return args + [w, mod]


def _mix_arities(mixes, h):
    return tuple(len(rows.arrays) for rows in list(mixes) + [h])


def _proj_ffn_kernel(*refs, arities, lat_tiles):
    h1, mod_ref, (g_ref, wg_ref, wu_ref, wd_ref, o_ref) = _mixed_residual(refs, arities, lat_tiles)
    xn = _norm_mod(h1, g_ref, mod_ref, 3).astype(BF16)
    ff = wg_ref.shape[1]
    y = None
    for c0 in range(0, ff, FFN_CHUNK):
        c1 = min(c0 + FFN_CHUNK, ff)
        gt = jnp.dot(xn, wg_ref[:, c0:c1], preferred_element_type=F32)
        up = jnp.dot(xn, wu_ref[:, c0:c1], preferred_element_type=F32)
        a = (_silu(gt) * up).astype(BF16)
        part = jnp.dot(a, wd_ref[c0:c1, :], preferred_element_type=F32)
        y = part if y is None else y + part
    o_ref[...] = h1 + mod_ref[5:6, :] * y


def _proj_ffn(lay, mixes, w_out, h, mod, g, wg, wu, wd):
    d = h.width
    return pl.pallas_call(
        functools.partial(_proj_ffn_kernel, arities=_mix_arities(mixes, h), lat_tiles=lay.lat_tiles),
        grid=(lay.all_tiles,),
        in_specs=_mix_specs(lay, mixes, w_out, h) + [_resident(g), _resident(wg), _resident(wu), _resident(wd)],
        out_specs=lay.tile(d),
        out_shape=jax.ShapeDtypeStruct((lay.n_tok, d), F32),
        compiler_params=_params(("arbitrary",)),
        name="proj_ffn",
    )(*_mix_args(mixes, w_out, h, mod), g, wg, wu, wd)


def _na_front_kernel(h_ref, mod_ref, g_ref, win_ref, gq_ref, gk_ref, q_ref, k_ref, v_ref):
    xn = _norm_mod(h_ref[...], g_ref, mod_ref, 0)
    px = jnp.dot(xn.astype(BF16), win_ref[...], preferred_element_type=F32)
    width = C_HEADS * C_DH
    lane = lax.broadcasted_iota(jnp.int32, (px.shape[0], LANES), 1)
    low = lane < C_DH
    scale = C_DH ** -0.5 * LOG2E

    def norm_pairs(base, g_ref_, out_ref, mult):
        gg = g_ref_[...]
        for j in range(width // LANES):
            x = px[:, base + j * LANES:base + (j + 1) * LANES]
            x2 = x * x
            s_all = jnp.sum(x2, axis=-1, keepdims=True)
            s_lo = jnp.sum(jnp.where(low, x2, 0.0), axis=-1, keepdims=True)
            ss = jnp.where(low, s_lo, s_all - s_lo)
            inv = lax.rsqrt(ss * (1.0 / C_DH) + EPS)
            if mult != 1.0:
                inv = inv * mult
            out_ref[:, j * LANES:(j + 1) * LANES] = (x * inv * gg).astype(BF16)

    norm_pairs(0, gq_ref, q_ref, scale)
    norm_pairs(width, gk_ref, k_ref, 1.0)
    v_ref[...] = px[:, 2 * width:].astype(BF16)


def _na_front(lay, h, mod, g, win, gq, gk):
    n, d = h.shape
    width = C_HEADS * C_DH
    return pl.pallas_call(
        _na_front_kernel,
        grid=(lay.all_tiles,),
        in_specs=[lay.tile(d), lay.mod_spec(d), _resident(g), _resident(win), _resident(gq), _resident(gk)],
        out_specs=[lay.tile(width)] * 3,
        out_shape=[jax.ShapeDtypeStruct((n, width), BF16)] * 3,
        compiler_params=_params(("arbitrary",)),
        name="na_front",
    )(h, mod, g, win, gq, gk)


def _na_attn_kernel(bounded_ref, qx_ref, qc_ref, kx_ref, kc_ref, vx_ref, vc_ref, planes_ref, shift_ref,
                    *refs, rows_n):
    out_refs, bias_ref = refs[:-1], refs[-1]
    ox_ref = out_refs[0]

    @pl.when(pl.program_id(1) == 0)
    def _():
        idx = _na_group_rows(rows_n)
        for p in range(idx.shape[0]):
            for hh in range(2):
                for qi in range(NA_GROUP):
                    for a in range(NA_WINDOW):
                        half = (a % 2) * GRID_W
                        bias_ref[p, hh, qi * GRID_W:(qi + 1) * GRID_W, a * GRID_W:(a + 1) * GRID_W] = (
                            planes_ref[hh, int(idx[p, qi, a]), :, half:half + GRID_W])

    qlen = NA_GROUP * GRID_W
    wlen = NA_WINDOW * GRID_W
    n_groups = rows_n // NA_GROUP
    lane_q = lax.broadcasted_iota(jnp.int32, (qlen, LANES), 1)
    kc = kc_ref[...]
    vc = vc_ref[...]

    if len(out_refs) > 1:
        qc = qc_ref[...]
        lane_c = lax.broadcasted_iota(jnp.int32, qc.shape, 1)
        outs = []
        for hh in range(2):
            sel = (lane_c < C_DH) if hh == 0 else (lane_c >= C_DH)
            outs.append(_softmax_pv(jnp.where(sel, qc, jnp.zeros_like(qc)), kc, vc))
        out_refs[1][...] = jnp.where(lane_c < C_DH, outs[0], outs[1]).astype(BF16)

    def group_body(g, carry, bounded):
        start = jnp.clip(g * NA_GROUP - WIN_R // 2, 0, rows_n - NA_WINDOW)
        pat = jnp.where(g == 0, 0, jnp.where(g == n_groups - 1, 2, 1))
        q0 = pl.multiple_of(g * qlen, qlen)
        k0 = pl.multiple_of(start * GRID_W, GRID_W)
        qr = qx_ref[pl.ds(q0, qlen), :]
        kw = kx_ref[pl.ds(k0, wlen), :]
        vw = vx_ref[pl.ds(k0, wlen), :]
        res = []
        for hh in range(2):
            sel = (lane_q < C_DH) if hh == 0 else (lane_q >= C_DH)
            qh = jnp.where(sel, qr, jnp.zeros_like(qr))
            s_loc = lax.dot_general(qh, kw, (((1,), (1,)), ((), ())),
                                    preferred_element_type=F32) + bias_ref[pat, hh]
            s_ctx = lax.dot_general(qh, kc, (((1,), (1,)), ((), ())), preferred_element_type=F32)
            if bounded:
                p_loc = jnp.exp2(s_loc)
                p_ctx = jnp.exp2(s_ctx - shift_ref[:, 0:1])
            else:
                m = jnp.maximum(jnp.max(s_loc, axis=-1, keepdims=True),
                                jnp.max(s_ctx, axis=-1, keepdims=True))
                p_loc = jnp.exp2(s_loc - m)
                p_ctx = jnp.exp2(s_ctx - m)
            l = jnp.sum(p_loc, axis=-1, keepdims=True) + jnp.sum(p_ctx, axis=-1, keepdims=True)
            o = (jnp.dot(p_loc.astype(BF16), vw, preferred_element_type=F32)
                 + jnp.dot(p_ctx.astype(BF16), vc, preferred_element_type=F32))
            res.append(o * (1.0 / l))
        ox_ref[pl.ds(q0, qlen), :] = jnp.where(lane_q < C_DH, res[0], res[1]).astype(BF16)
        return carry

    @pl.when(bounded_ref[0] != 0)
    def _():
        lax.fori_loop(0, n_groups, functools.partial(group_body, bounded=True), 0, unroll=4)

    @pl.when(bounded_ref[0] == 0)
    def _():
        lax.fori_loop(0, n_groups, functools.partial(group_body, bounded=False), 0)


def _na_attn(lay, bounded, q, k, v, planes, shift, need_ctx):
    width = q.shape[-1]
    hp = C_HEADS // 2
    lat = pl.BlockSpec((lay.s, LANES), lambda p, i, f: (i, p))
    ctx = pl.BlockSpec((lay.l, LANES), lambda p, i, f: (lay.ctx_block(i), p))
    out_specs = [pl.BlockSpec((lay.s, LANES), lambda p, i, f: (i, p))]
    out_shape = [jax.ShapeDtypeStruct((lay.n_lat, width), BF16)]
    if need_ctx:
        out_specs.append(pl.BlockSpec((lay.l, LANES), lambda p, i, f: (i, p)))
        out_shape.append(jax.ShapeDtypeStruct((lay.b * lay.l, width), BF16))
    n_pat = _na_group_rows(lay.s // GRID_W).shape[0]
    return pl.pallas_call(
        functools.partial(_na_attn_kernel, rows_n=lay.s // GRID_W),
        grid_spec=pltpu.PrefetchScalarGridSpec(
            num_scalar_prefetch=1,
            grid=(hp, lay.b),
            in_specs=[lat, ctx, lat, ctx, lat, ctx,
                      pl.BlockSpec((2,) + planes.shape[1:], lambda p, i, f: (p, 0, 0, 0)),
                      pl.BlockSpec(shift.shape, lambda p, i, f: (0, 0))],
            out_specs=out_specs,
            scratch_shapes=[pltpu.VMEM((n_pat, 2, NA_GROUP * GRID_W, NA_WINDOW * GRID_W), F32)],
        ),
        out_shape=out_shape,
        compiler_params=_params(("arbitrary", "arbitrary")),
        name="na_attn",
    )(bounded, q, q, k, k, v, v, planes, shift)


def _proj_router_kernel(*refs, arities, lat_tiles, tiles_per_batch, ctx_mod_row):
    h1, mod_ref, (g_ref, wr_ref, xr_ref, route_ref) = _mixed_residual(refs, arities, lat_tiles)
    xn = _norm_mod(h1, g_ref, mod_ref, 3)
    d = xn.shape[-1]
    xr_ref[:, 0:d] = xn
    xr_ref[:, d + LANES:2 * d + LANES] = h1
    t = pl.program_id(0)
    mod_row = jnp.where(t < lat_tiles, t // tiles_per_batch, ctx_mod_row).astype(F32)
    wr = wr_ref[...]
    x_hi, w_hi = xn.astype(BF16), wr.astype(BF16)
    x_lo = (xn - x_hi.astype(F32)).astype(BF16)
    w_lo = (wr - w_hi.astype(F32)).astype(BF16)
    logits = (jnp.dot(x_hi, w_hi, preferred_element_type=F32) + jnp.dot(x_lo, w_hi, preferred_element_type=F32)
              + jnp.dot(x_hi, w_lo, preferred_element_type=F32))
    lane = lax.broadcasted_iota(jnp.int32, logits.shape, 1).astype(F32)
    lg = jnp.where(lane < N_EXPERTS, logits, -jnp.inf)
    m1 = jnp.max(lg, axis=-1, keepdims=True)
    i1 = jnp.min(jnp.where(lg == m1, lane, float(LANES)), axis=-1, keepdims=True)
    lg2 = jnp.where(lane == i1, -jnp.inf, lg)
    m2 = jnp.max(lg2, axis=-1, keepdims=True)
    i2 = jnp.min(jnp.where(lg2 == m2, lane, float(LANES)), axis=-1, keepdims=True)
    e2 = jnp.exp(m2 - m1)
    g1 = 1.0 / (1.0 + e2)
    g2 = e2 * g1
    first_low = i1 < i2
    vals = (jnp.minimum(i1, i2), jnp.maximum(i1, i2), jnp.where(first_low, g1, g2), jnp.where(first_low, g2, g1),
            mod_row)
    route = jnp.zeros_like(lane)
    for idx, val in enumerate(vals):
        route = jnp.where(lane == float(idx), val, route)
    route_ref[...] = route
    xr_ref[:, d:d + LANES] = route


def _proj_router(lay, mixes, w_out, h, mod, g, wr, n_tiles):
    d = h.width
    rows = n_tiles * lay.tm
    return pl.pallas_call(
        functools.partial(_proj_router_kernel, arities=_mix_arities(mixes, h), lat_tiles=lay.lat_tiles,
                          tiles_per_batch=lay.s // lay.tm, ctx_mod_row=lay.b),
        grid=(n_tiles,),
        in_specs=_mix_specs(lay, mixes, w_out, h) + [_resident(g), _resident(wr)],
        out_specs=[lay.tile(2 * d + LANES), lay.tile(LANES)],
        out_shape=[jax.ShapeDtypeStruct((rows, 2 * d + LANES), F32), jax.ShapeDtypeStruct((rows, LANES), F32)],
        compiler_params=_params(("arbitrary",)),
        name="proj_router",
    )(*_mix_args(mixes, w_out, h, mod), g, wr)


def _moe_plan(route, n_tok):
    n_pairs = len(_PAIR_LO)
    n_tiles = n_tok // MOE_TM + n_pairs
    rows = n_tiles * MOE_TM
    lo = route[:, 0].astype(jnp.int32)
    hi = route[:, 1].astype(jnp.int32)
    pid = (lo * (2 * N_EXPERTS - 1 - lo)) // 2 + (hi - lo - 1)
    onehot = (pid[:, None] == jnp.arange(n_pairs, dtype=jnp.int32)[None, :]).astype(jnp.int32)
    csum = jnp.cumsum(onehot, axis=0)
    rank = jnp.sum(csum * onehot, axis=1) - 1
    counts = csum[-1]
    padded = ((counts + MOE_TM - 1) // MOE_TM) * MOE_TM
    gend = jnp.cumsum(padded)
    gstart = gend - padded
    dest = jnp.sum(gstart[None, :] * onehot, axis=1) + rank
    src = jnp.full((rows,), -1, jnp.int32).at[dest].set(jnp.arange(n_tok, dtype=jnp.int32), unique_indices=True)
    valid = src >= 0
    src_tok = jnp.where(valid, src, 0)
    n_valid = jnp.sum(valid.reshape(n_tiles, MOE_TM).astype(jnp.int32), axis=1)
    n_used = gend[-1] // MOE_TM
    tile_row = jnp.minimum(jnp.arange(n_tiles, dtype=jnp.int32), n_used - 1) * MOE_TM
    group = jnp.minimum(jnp.sum((gend[None, :] <= tile_row[:, None]).astype(jnp.int32), axis=1), n_pairs - 1)
    ea = jnp.asarray(_PAIR_LO, jnp.int32)[group]
    eb = jnp.asarray(_PAIR_HI, jnp.int32)[group]
    return ea, eb, src_tok, n_valid, n_used.astype(jnp.int32).reshape(1)


def _moe_kernel(ea_ref, eb_ref, src_ref, nvalid_ref, nused_ref,
                x_hbm, g2_ref, wgua_ref, wda_ref, wgub_ref, wdb_ref,
                y_hbm, xbuf, ybuf, gsem, ssem, *, n_mod_rows):
    i = pl.program_id(0)
    n_steps = pl.num_programs(0)
    n_used = nused_ref[0]
    slot = i % 2

    def gather_copy(tile, slot_, r):
        return pltpu.make_async_copy(x_hbm.at[pl.ds(src_ref[tile * MOE_TM + r], 1)],
                                     xbuf.at[slot_, pl.ds(r, 1)], gsem.at[slot_])

    def scatter_copy(tile, slot_, r):
        return pltpu.make_async_copy(ybuf.at[slot_, pl.ds(r, 1)],
                                     y_hbm.at[pl.ds(src_ref[tile * MOE_TM + r], 1)], ssem.at[slot_])

    def start_gather(tile, slot_):
        for r in range(MOE_TM):
            gather_copy(tile, slot_, r).start()

    def wait_gather(slot_):
        pltpu.make_async_copy(x_hbm.at[pl.ds(0, MOE_TM)], xbuf.at[slot_], gsem.at[slot_]).wait()

    def start_scatter(tile, slot_):
        full = nvalid_ref[tile] == MOE_TM

        @pl.when(full)
        def _():
            for r in range(MOE_TM):
                scatter_copy(tile, slot_, r).start()

        @pl.when(jnp.logical_not(full))
        def _():
            def body(r, carry):
                scatter_copy(tile, slot_, r).start()
                return carry
            lax.fori_loop(0, nvalid_ref[tile], body, 0)

    def wait_scatter(tile, slot_):
        full = nvalid_ref[tile] == MOE_TM

        @pl.when(full)
        def _():
            pltpu.make_async_copy(ybuf.at[slot_], y_hbm.at[pl.ds(0, MOE_TM)], ssem.at[slot_]).wait()

        @pl.when(jnp.logical_not(full))
        def _():
            def body(r, carry):
                scatter_copy(tile, slot_, r).wait()
                return carry
            lax.fori_loop(0, nvalid_ref[tile], body, 0)

    @pl.when((i >= 2) & (i - 2 < n_used))
    def _():
        wait_scatter(i - 2, slot)

    @pl.when(i < n_used)
    def _():
        @pl.when(i == 0)
        def _():
            start_gather(0, 0)

        @pl.when(i + 1 < n_used)
        def _():
            start_gather(i + 1, 1 - slot)

        wait_gather(slot)
        d = ybuf.shape[-1]
        x = xbuf[slot, :, 0:d].astype(BF16)
        gates = xbuf[slot, :, d:d + LANES]

        def expert(wgu_ref, wd_ref):
            f = wd_ref.shape[0]
            gu = jnp.dot(x, wgu_ref[...], preferred_element_type=F32)
            a = (_silu(gu[:, 0:f]) * gu[:, f:2 * f]).astype(BF16)
            return jnp.dot(a, wd_ref[...], preferred_element_type=F32)

        y = (gates[:, 2:3] * expert(wgua_ref, wda_ref)
             + gates[:, 3:4] * expert(wgub_ref, wdb_ref))
        mod_row = gates[:, 4:5]
        gate2 = jnp.zeros_like(y)
        for r in range(n_mod_rows):
            gate2 = jnp.where(mod_row == float(r), g2_ref[r:r + 1, :], gate2)
        ybuf[slot] = xbuf[slot, :, d + LANES:2 * d + LANES] + gate2 * y
        start_scatter(i, slot)

    @pl.when(i == n_steps - 1)
    def _():
        for back in (2, 1):
            @pl.when(n_steps - back < n_used)
            def _():
                wait_scatter(n_steps - back, (n_steps - back) % 2)


def _moe(xr, route, gate2, n_mod_rows, wgu, wd, layer):
    n_tok, dr = xr.shape
    _, _, f, d = wd.shape
    assert dr == 2 * d + LANES
    ea, eb, src_tok, n_valid, n_used = _moe_plan(route, n_tok)
    n_tiles = ea.shape[0]
    wspec = lambda shape, which: pl.BlockSpec(
        (None, None) + shape, lambda i, ea_, eb_, s_, d_, u_: (layer, (ea_, eb_)[which][i], 0, 0))
    return pl.pallas_call(
        functools.partial(_moe_kernel, n_mod_rows=n_mod_rows),
        grid_spec=pltpu.PrefetchScalarGridSpec(
            num_scalar_prefetch=5,
            grid=(n_tiles,),
            in_specs=[
                pl.BlockSpec(memory_space=pl.ANY),
                _resident(gate2),
                wspec((d, 2 * f), 0), wspec((f, d), 0),
                wspec((d, 2 * f), 1), wspec((f, d), 1),
            ],
            out_specs=pl.BlockSpec(memory_space=pl.ANY),
            scratch_shapes=[
                pltpu.VMEM((2, MOE_TM, dr), F32),
                pltpu.VMEM((2, MOE_TM, d), F32),
                pltpu.SemaphoreType.DMA((2,)),
                pltpu.SemaphoreType.DMA((2,)),
            ],
        ),
        out_shape=jax.ShapeDtypeStruct((n_tok, d), F32),
        compiler_params=_params(("arbitrary",)),
        name="moe_experts",
    )(ea, eb, src_tok, n_valid, n_used, xr, gate2, wgu, wd, wgu, wd)


def _rope_tables(s, ctx_rows):
    t = jnp.arange(s)
    rows = (t // GRID_W).astype(F32)
    cols = (t % GRID_W).astype(F32)
    n_pairs = A_ROPE // 2
    per_axis = n_pairs // 2
    inv = ROPE_THETA ** (-jnp.arange(per_axis, dtype=F32) / per_axis)
    ang = jnp.concatenate([rows[:, None] * inv, cols[:, None] * inv], axis=-1)
    cos = jnp.repeat(jnp.cos(ang), 2, axis=-1)
    sin = jnp.repeat(jnp.sin(ang), 2, axis=-1)
    sign = jnp.tile(jnp.array([-1.0, 1.0], F32), n_pairs)
    pad = HEAD_PAD - A_QK
    t1 = jnp.concatenate([jnp.ones((s, A_NOPE), F32), cos, jnp.zeros((s, pad), F32)], axis=-1)
    t2 = jnp.concatenate([jnp.zeros((s, A_NOPE), F32), sin * sign, jnp.zeros((s, pad), F32)], axis=-1)
    c1 = jnp.concatenate([jnp.ones((ctx_rows, A_QK), F32), jnp.zeros((ctx_rows, pad), F32)], axis=-1)
    c2 = jnp.zeros((ctx_rows, HEAD_PAD), F32)
    return jnp.concatenate([t1, c1], axis=0), jnp.concatenate([t2, c2], axis=0)


_PAIR_SWAP = np.arange(A_ROPE) ^ 1


def _rope_lane_vec(g_tail, swapped):
    gt = g_tail[_PAIR_SWAP] if swapped else g_tail
    return jnp.concatenate([jnp.zeros((A_NOPE,), F32), gt, jnp.zeros((HEAD_PAD - A_QK,), F32)])[None]


def _mla_weights(w_in, w_uq, w_ukv, g_q, g_k):
    d = w_in.shape[0]
    q_end = A_Q_RANK
    kv_end = q_end + A_KV_RANK
    r_end = kv_end + A_ROPE
    z = lambda n: jnp.zeros((d, n), w_in.dtype)
    kr = w_in[:, kv_end:r_end]
    pad = HEAD_PAD - A_QK
    win = jnp.concatenate([w_in[:, :kv_end], z(A_NOPE), kr, z(pad), z(A_NOPE), kr[:, _PAIR_SWAP], z(pad),
                           w_in[:, r_end:]], axis=-1).astype(BF16)
    wq = w_uq.reshape(A_Q_RANK, A_HEADS, A_QK)
    zq = lambda n: jnp.zeros((A_Q_RANK, A_HEADS, n), wq.dtype)
    wq_main = jnp.concatenate([wq, zq(pad)], axis=-1)
    wq_swap = jnp.concatenate([zq(A_NOPE), wq[:, :, A_NOPE:][:, :, _PAIR_SWAP], zq(pad)], axis=-1)
    wq_ext = jnp.concatenate([wq_main.reshape(A_Q_RANK, -1), wq_swap.reshape(A_Q_RANK, -1)], axis=-1).astype(BF16)
    wkv = w_ukv.reshape(A_KV_RANK, A_HEADS, A_NOPE + A_V)
    zkv = jnp.zeros((A_KV_RANK, A_HEADS, HEAD_PAD - A_NOPE), wkv.dtype)
    wk = jnp.concatenate([wkv[:, :, :A_NOPE], zkv], axis=-1)
    wv = jnp.concatenate([wkv[:, :, A_NOPE:], zkv], axis=-1)
    wkv_ext = jnp.concatenate([wk.reshape(A_KV_RANK, -1), wv.reshape(A_KV_RANK, -1)], axis=-1).astype(BF16)

    bound = math.sqrt(A_QK) * jnp.max(jnp.abs(g_q)) * jnp.max(jnp.abs(g_k))
    bounded = bound <= MLA_SAFE_BOUND
    shift = jnp.where(bounded, -bound * LOG2E, 0.0)
    unit = lambda lane_idx: jnp.zeros((1, HEAD_PAD), F32).at[0, lane_idx].set(1.0)
    gq = jnp.concatenate([jnp.concatenate([g_q, jnp.zeros((pad,), F32)])[None], _rope_lane_vec(g_q[A_NOPE:], True),
                          unit(A_QK) * shift], axis=0)
    gk = jnp.concatenate([jnp.concatenate([g_k[:A_NOPE], jnp.zeros((HEAD_PAD - A_NOPE,), F32)])[None],
                          _rope_lane_vec(g_k[A_NOPE:], False), _rope_lane_vec(g_k[A_NOPE:], True),
                          unit(A_QK), unit(A_V)], axis=0)
    return win, wq_ext, wkv_ext, gq, gk, bounded.astype(jnp.int32).reshape(1)


def _dft_mats(n, norm):
    k = jnp.arange(n, dtype=jnp.int32)
    w = GRID_W if n % GRID_W == 0 and n > GRID_W else 1
    hi = jnp.arange(n // w, dtype=jnp.int32) * w
    lo = jnp.arange(w, dtype=jnp.int32)
    ang_hi = ((k[:, None] * hi[None, :]) % n).astype(F32) * (2.0 * math.pi / n)
    ang_lo = ((k[:, None] * lo[None, :]) % n).astype(F32) * (2.0 * math.pi / n)
    ch, sh = (jnp.repeat(f(ang_hi), w, axis=1) for f in (jnp.cos, jnp.sin))
    cl, sl = (jnp.tile(f(ang_lo), (1, n // w)) for f in (jnp.cos, jnp.sin))
    cos = ((ch * cl - sh * sl) * norm).astype(BF16)
    sin = ((sh * cl + ch * sl) * norm).astype(BF16)
    return cos, sin


def _channel_dft():
    c = np.arange(F_CH)
    ang = 2.0 * np.pi * ((c[:, None] * c[None, :]) % F_CH) / F_CH
    eye = np.eye(F_GROUPS)
    cb = np.kron(eye, np.cos(ang)) / math.sqrt(F_CH)
    sb = np.kron(eye, np.sin(ang)) / math.sqrt(F_CH)
    return jnp.asarray(np.concatenate([cb, sb], axis=1), BF16)


def _na_group_rows(rows_n):
    assert rows_n % NA_GROUP == 0 and rows_n >= NA_WINDOW + NA_GROUP
    n_groups = rows_n // NA_GROUP
    n_dr = 2 * WIN_R - 1
    idx = np.full((3, NA_GROUP, NA_WINDOW), n_dr, np.int32)
    seen = {}
    for g in range(n_groups):
        gs = int(np.clip(g * NA_GROUP - WIN_R // 2, 0, rows_n - NA_WINDOW))
        pat = 0 if g == 0 else (2 if g == n_groups - 1 else 1)
        cur = np.full((NA_GROUP, NA_WINDOW), n_dr, np.int32)
        for qi in range(NA_GROUP):
            r = g * NA_GROUP + qi
            start = int(np.clip(r - WIN_R // 2, 0, rows_n - WIN_R))
            assert gs <= start and start + WIN_R <= gs + NA_WINDOW
            for a in range(start, start + WIN_R):
                cur[qi, a - gs] = a - r + (WIN_R - 1)
        assert pat not in seen or np.array_equal(seen[pat], cur)
        seen[pat] = cur
        idx[pat] = cur
    return idx


def _na_score_bound(g_q, g_k, rpb):
    qk = math.sqrt(C_DH) * jnp.max(jnp.abs(g_q)) * jnp.max(jnp.abs(g_k))
    bound = qk + jnp.maximum(jnp.max(rpb), 0.0)
    lowest = -qk + jnp.minimum(jnp.min(rpb), 0.0)
    bounded = (bound - lowest) <= 2.0 * MLA_SAFE_BOUND
    shift = jnp.where(bounded, bound, 0.0).astype(F32)
    return bounded.astype(jnp.int32).reshape(1), shift


def _na_bias(rpb, shift):
    h, n_dr, n_dc = rpb.shape
    qc = np.arange(GRID_W)
    cs = np.clip(qc - WIN_C // 2, 0, GRID_W - WIN_C)
    kcol = np.arange(GRID_W)
    valid = (kcol[None, :] >= cs[:, None]) & (kcol[None, :] < cs[:, None] + WIN_C)
    dc = kcol[None, :] - qc[:, None] + (WIN_C - 1)
    onehot = (valid[:, :, None] & (dc[:, :, None] == np.arange(n_dc)[None, None, :])).astype(np.float32)
    col = jnp.einsum('hdc,qkc->hdqk', rpb.astype(F32), jnp.asarray(onehot), precision=lax.Precision.HIGHEST)
    col = jnp.where(jnp.asarray(valid)[None, None], (col - shift) * LOG2E, NEG_BIG)
    col = jnp.concatenate([col, jnp.full((h, 1, GRID_W, GRID_W), NEG_BIG, F32)], axis=1)
    return jnp.concatenate([col, col], axis=-1)


def kernel(x, c, ctx, c_ctx, w_mod, b_mod, norm_g, a_w_in, a_g_cq, a_g_ckv, a_w_uq, a_w_ukv, a_g_q, a_g_k, a_w_out,
           f_w_gate, f_w_up, f_w_down, c_w_in, c_g_q, c_g_k, c_rpb, c_w_out, m_w_router, m_w_gate, m_w_up, m_w_down):
    b, s, d = x.shape
    l = ctx.shape[1]
    depth = w_mod.shape[0]
    lay = _make_layout(b, s, l)

    mod_rows = 16
    cin = jnp.concatenate([c, c_ctx[None], jnp.zeros((mod_rows - b - 1, d), F32)], axis=0)
    mod_all = _modulation(cin, w_mod, b_mod).reshape(depth, mod_rows, 6, d)

    t1, t2 = _rope_tables(s, lay.tm)
    cn, sn = _dft_mats(s, 1.0 / math.sqrt(s))
    cc, sc = _dft_mats(l, 1.0 / math.sqrt(l))
    cb = _channel_dft()

    moe_wgu = jnp.concatenate([m_w_gate.astype(BF16), m_w_up.astype(BF16)], axis=-1)
    moe_wd = m_w_down.astype(BF16)
    h = _Rows(x.reshape(b * s, d), ctx.reshape(b * l, d))
    for i in range(depth):
        j = i // 2
        need_ctx = i < depth - 1
        mod = mod_all[i]
        g1 = norm_g[i, 0][None]
        g2 = norm_g[i, 1][None]
        if i % 2 == 0:
            win, wq_ext, wkv_ext, gq, gk, bounded = _mla_weights(a_w_in[j], a_w_uq[j], a_w_ukv[j],
                                                                 a_g_q[j], a_g_k[j])
            q, k, v, fcs = _mla_front(lay, h, mod, g1, win, a_g_cq[j][None], a_g_ckv[j][None], wq_ext, wkv_ext,
                                      t1, t2, gq, gk, cb)
            att = _mla_attn(lay, bounded, q, k, v)
            yf = _dft(lay, fcs, cn, sn, cc, sc)
            h = _Rows(_proj_ffn(lay, [_Rows(att), _Rows(yf)], a_w_out[j].astype(BF16), h, mod, g2,
                                f_w_gate[j].astype(BF16), f_w_up[j].astype(BF16), f_w_down[j].astype(BF16)))
        else:
            gq2 = jnp.tile(c_g_q[j], 2)[None]
            gk2 = jnp.tile(c_g_k[j], 2)[None]
            q, k, v = _na_front(lay, h.lat, mod, g1, c_w_in[j].astype(BF16), gq2, gk2)
            na_bounded, na_shift = _na_score_bound(c_g_q[j], c_g_k[j], c_rpb[j])
            bias = _na_bias(c_rpb[j], na_shift)
            outs = _na_attn(lay, na_bounded, q, k, v, bias, jnp.full((1, LANES), na_shift * LOG2E, F32), need_ctx)
            att = _Rows(outs[0], outs[1]) if need_ctx else _Rows(outs[0])
            wr = jnp.concatenate([m_w_router[j], jnp.zeros((d, LANES - N_EXPERTS), F32)], axis=-1)
            n_tiles = lay.all_tiles if need_ctx else lay.lat_tiles
            xr, route = _proj_router(lay, [att], c_w_out[j].astype(BF16), h, mod, g2, wr, n_tiles)
            h = _Rows(_moe(xr, route, mod[:, 5, :], b + 1, moe_wgu, moe_wd, j))
    return h.lat[:lay.n_lat].reshape(b, s, d)
```

```python
import functools
import math
from typing import NamedTuple

import jax
import jax.numpy as jnp
import numpy as np
from jax import lax
from jax.experimental import pallas as pl
from jax.experimental.pallas import tpu as pltpu

F32 = jnp.float32
BF16 = jnp.bfloat16

GRID_W = 64
A_HEADS = 12
A_NOPE = 64
A_ROPE = 32
A_QK = A_NOPE + A_ROPE
A_V = 64
A_Q_RANK = 256
A_KV_RANK = 128
F_GROUPS = 4
F_CH = 64
F_WIDTH = F_GROUPS * F_CH
C_HEADS = 16
C_DH = 64
WIN_R = 8
WIN_C = 16
N_EXPERTS = 8
ROPE_THETA = 10000.0
EPS = 1e-6

LANES = 128
HEAD_PAD = 128
TM = 512
TQ = 256
FFN_CHUNK = 1536
MOE_TM = 256
_PAIR_LO = tuple(lo for lo in range(N_EXPERTS) for hi in range(lo + 1, N_EXPERTS))
_PAIR_HI = tuple(hi for lo in range(N_EXPERTS) for hi in range(lo + 1, N_EXPERTS))
LOG2E = math.log2(math.e)
MLA_SAFE_BOUND = 40.0
MLA_KCHUNK = 512
MLA_HPS = 6
NA_GROUP = 4
NA_WINDOW = NA_GROUP + WIN_R
VMEM_LIMIT = 52 * 1024 * 1024
NEG_BIG = -1e30


def _params(sem, vmem=VMEM_LIMIT):
    return pltpu.CompilerParams(dimension_semantics=sem, vmem_limit_bytes=vmem)


def _rms(x, g, n=None):
    n = x.shape[-1] if n is None else n
    ss = jnp.sum(x * x, axis=-1, keepdims=True)
    return x * lax.rsqrt(ss * (1.0 / n) + EPS) * g


def _silu(x):
    return x * (1.0 / (1.0 + jnp.exp(-x)))


class _Layout(NamedTuple):
    b: int
    s: int
    l: int
    tm: int

    @property
    def n_lat(self):
        return self.b * self.s

    @property
    def n_tok(self):
        return self.b * (self.s + self.l)

    @property
    def lat_tiles(self):
        return self.n_lat // self.tm

    @property
    def all_tiles(self):
        return self.n_tok // self.tm

    def tile(self, width):
        return pl.BlockSpec((self.tm, width), lambda t: (t, 0))

    def mod_spec(self, d):
        per_batch = self.s // self.tm
        return pl.BlockSpec((None, 6, d), lambda t: (jnp.where(t < self.lat_tiles, t // per_batch, self.b), 0, 0))

    def ctx_block(self, i):
        return self.n_lat // self.l + i


class _Rows(NamedTuple):
    lat: jax.Array
    ctx: jax.Array | None = None

    @property
    def width(self):
        return self.lat.shape[-1]

    @property
    def arrays(self):
        return [self.lat] if self.ctx is None else [self.lat, self.ctx]


def _row_specs(lay, rows):
    if rows.ctx is None:
        return [lay.tile(rows.width)]
    tm, nl = lay.tm, lay.lat_tiles
    return [pl.BlockSpec((tm, rows.width), lambda t: (jnp.minimum(t, nl - 1), 0)),
            pl.BlockSpec((tm, rows.width), lambda t: (jnp.maximum(t - nl, 0), 0))]


def _row_tiles(refs, arities, lat_tiles):
    tiles, pos = [], 0
    for arity in arities:
        if arity == 1:
            tiles.append(refs[pos][...])
        else:
            tiles.append(jnp.where(pl.program_id(0) < lat_tiles, refs[pos][...], refs[pos + 1][...]))
        pos += arity
    return tiles


def _make_layout(b, s, l):
    tm = TM if (s % TM == 0 and (b * l) % TM == 0) else l
    assert l == TQ and s % tm == 0 and (b * l) % tm == 0 and s % GRID_W == 0
    return _Layout(b, s, l, tm)


def _resident(a):
    return pl.BlockSpec(a.shape, lambda *_: (0,) * a.ndim, pipeline_mode=pl.Buffered(1))


def _norm_mod(x, g_ref, mod_ref, row):
    xn = _rms(x, g_ref[...])
    return xn * (1.0 + mod_ref[row + 1:row + 2, :]) + mod_ref[row:row + 1, :]


def _mod_kernel(c_ref, w_ref, b_ref, o_ref):
    s = _silu(c_ref[...])
    o_ref[...] = jnp.dot(s, w_ref[...], precision=lax.Precision.HIGHEST,
                         preferred_element_type=F32) + b_ref[...]


def _modulation(cin, w_mod, b_mod):
    depth, d, n = w_mod.shape
    tn = 1536
    rows = cin.shape[0]
    return pl.pallas_call(
        _mod_kernel,
        grid=(depth, n // tn),
        in_specs=[
            pl.BlockSpec((rows, d), lambda l, j: (0, 0)),
            pl.BlockSpec((None, d, tn), lambda l, j: (l, 0, j)),
            pl.BlockSpec((None, 1, tn), lambda l, j: (l, 0, j)),
        ],
        out_specs=pl.BlockSpec((None, rows, tn), lambda l, j: (l, 0, j)),
        out_shape=jax.ShapeDtypeStruct((depth, rows, n), F32),
        compiler_params=_params(("arbitrary", "arbitrary")),
        name="modulation",
    )(cin, w_mod, b_mod.reshape(depth, 1, n))


def _mla_front_kernel(*refs, h_arity, lat_tiles):
    (mod_ref, g_ref, win_ref, gcq_ref, gckv_ref, wq_ref, wkv_ref, t1_ref, t2_ref, gq_ref, gk_ref, cb_ref,
     q_ref, k_ref, v_ref, fcs_ref) = refs[h_arity:]
    (h,) = _row_tiles(refs, (h_arity,), lat_tiles)
    xn = _norm_mod(h, g_ref, mod_ref, 0)
    px = jnp.dot(xn.astype(BF16), win_ref[...], preferred_element_type=F32)
    t1 = t1_ref[...]
    t2 = t2_ref[...]
    hw = A_HEADS * HEAD_PAD

    qn = _rms(px[:, 0:A_Q_RANK], gcq_ref[...]).astype(BF16)
    qq = jnp.dot(qn, wq_ref[...], preferred_element_type=F32)
    aq = t1 * gq_ref[0:1, :]
    bq = t2 * gq_ref[1:2, :]
    q_shift = gq_ref[2:3, :]
    scale = A_QK ** -0.5 * LOG2E
    for h in range(A_HEADS):
        qm = qq[:, h * HEAD_PAD:(h + 1) * HEAD_PAD]
        qs = qq[:, hw + h * HEAD_PAD:hw + (h + 1) * HEAD_PAD]
        ss = jnp.sum(qm * qm, axis=-1, keepdims=True)
        inv = lax.rsqrt(ss * (1.0 / A_QK) + EPS) * scale
        q_ref[:, h * HEAD_PAD:(h + 1) * HEAD_PAD] = ((qm * aq + qs * bq) * inv + q_shift).astype(BF16)

    kvn = _rms(px[:, A_Q_RANK:A_Q_RANK + A_KV_RANK], gckv_ref[...]).astype(BF16)
    kv = jnp.dot(kvn, wkv_ref[...], preferred_element_type=F32)
    krm = px[:, 384:512]
    krs = px[:, 512:640]
    tail = krm * (t1 * gk_ref[1:2, :]) + krs * (t2 * gk_ref[2:3, :])
    ssr = jnp.sum(krm * krm, axis=-1, keepdims=True)
    gkn = gk_ref[0:1, :]
    k_one = gk_ref[3:4, :]
    v_one = gk_ref[4:5, :]
    for h in range(A_HEADS):
        km = kv[:, h * HEAD_PAD:(h + 1) * HEAD_PAD]
        ss = jnp.sum(km * km, axis=-1, keepdims=True) + ssr
        inv = lax.rsqrt(ss * (1.0 / A_QK) + EPS)
        k_ref[:, h * HEAD_PAD:(h + 1) * HEAD_PAD] = ((km * gkn + tail) * inv + k_one).astype(BF16)
        v_ref[:, h * HEAD_PAD:(h + 1) * HEAD_PAD] = (kv[:, hw + h * HEAD_PAD:hw + (h + 1) * HEAD_PAD]
                                                     + v_one).astype(BF16)

    f = px[:, 640:896].astype(BF16)
    fcs_ref[...] = jnp.dot(f, cb_ref[...], preferred_element_type=F32).astype(BF16)


def _mla_front(lay, h, mod, g, win, gcq, gckv, wq, wkv, t1, t2, gq, gk, cb):
    n, d = lay.n_tok, h.width
    hw = A_HEADS * HEAD_PAD
    per_batch = lay.s // lay.tm
    rope = pl.BlockSpec((lay.tm, LANES), lambda t: (jnp.where(t < lay.lat_tiles, t % per_batch, per_batch), 0))
    return pl.pallas_call(
        functools.partial(_mla_front_kernel, h_arity=len(h.arrays), lat_tiles=lay.lat_tiles),
        grid=(lay.all_tiles,),
        in_specs=_row_specs(lay, h) + [
            lay.mod_spec(d),
            _resident(g), _resident(win), _resident(gcq), _resident(gckv), _resident(wq), _resident(wkv),
            rope, rope,
            _resident(gq), _resident(gk), _resident(cb),
        ],
        out_specs=[lay.tile(hw), lay.tile(hw), lay.tile(hw), lay.tile(2 * F_WIDTH)],
        out_shape=[
            jax.ShapeDtypeStruct((n, hw), BF16),
            jax.ShapeDtypeStruct((n, hw), BF16),
            jax.ShapeDtypeStruct((n, hw), BF16),
            jax.ShapeDtypeStruct((n, 2 * F_WIDTH), BF16),
        ],
        compiler_params=_params(("arbitrary",)),
        name="mla_front",
    )(*h.arrays, mod, g, win, gcq, gckv, wq, wkv, t1, t2, gq, gk, cb)


def _softmax_pv(q, k, v):
    s = lax.dot_general(q, k, (((1,), (1,)), ((), ())), preferred_element_type=F32)
    m = jnp.max(s, axis=-1, keepdims=True)
    p = jnp.exp2(s - m)
    l = jnp.sum(p, axis=-1, keepdims=True)
    o = jnp.dot(p.astype(BF16), v, preferred_element_type=F32)
    return o * (1.0 / l)


def _mla_attn_kernel(bounded_ref, q_ref, kx_ref, kc_ref, vx_ref, vc_ref, o_ref, m_ref, acc_ref):
    t = pl.program_id(2)
    ck = MLA_KCHUNK
    n_chunks = kx_ref.shape[0] // ck

    def head(ref, rows=slice(None)):
        return lambda hh: ref[rows, hh * HEAD_PAD:(hh + 1) * HEAD_PAD]

    def scores(hh, k):
        return lax.dot_general(head(q_ref)(hh), k, (((1,), (1,)), ((), ())), preferred_element_type=F32)

    def bounded(with_latent):
        for hh in range(MLA_HPS):
            p = jnp.exp2(scores(hh, head(kc_ref)(hh))).astype(BF16)
            acc = jnp.dot(p, head(vc_ref)(hh), preferred_element_type=F32)
            if with_latent:
                p = jnp.exp2(scores(hh, head(kx_ref)(hh))).astype(BF16)
                acc = acc + jnp.dot(p, head(vx_ref)(hh), preferred_element_type=F32)
            acc_ref[hh] = acc

    def online(with_latent):
        for hh in range(MLA_HPS):
            s = scores(hh, head(kc_ref)(hh))
            m = jnp.max(s, axis=-1, keepdims=True)
            m_ref[hh] = jnp.broadcast_to(m, m_ref.shape[1:])
            acc_ref[hh] = jnp.dot(jnp.exp2(s - m).astype(BF16), head(vc_ref)(hh), preferred_element_type=F32)

        def body(c, carry):
            rows = pl.ds(pl.multiple_of(c * ck, ck), ck)
            for hh in range(MLA_HPS):
                s = scores(hh, head(kx_ref, rows)(hh))
                m_old = m_ref[hh]
                m_new = jnp.maximum(m_old, jnp.max(s, axis=-1, keepdims=True))
                p = jnp.exp2(s - jnp.tile(m_new, (1, ck // LANES))).astype(BF16)
                acc_ref[hh] = (jnp.exp2(m_old - m_new) * acc_ref[hh]
                               + jnp.dot(p, head(vx_ref, rows)(hh), preferred_element_type=F32))
                m_ref[hh] = m_new
            return carry

        if with_latent:
            lax.fori_loop(0, n_chunks, body, 0)

    is_ctx = t == pl.num_programs(2) - 1
    fast = bounded_ref[0] != 0

    @pl.when(fast & is_ctx)
    def _():
        bounded(False)

    @pl.when(fast & jnp.logical_not(is_ctx))
    def _():
        bounded(True)

    @pl.when(jnp.logical_not(fast) & is_ctx)
    def _():
        online(False)

    @pl.when(jnp.logical_not(fast) & jnp.logical_not(is_ctx))
    def _():
        online(True)

    lane = lax.broadcasted_iota(jnp.int32, (o_ref.shape[0], LANES), 1)
    for hp in range(MLA_HPS // 2):
        outs = []
        for hh in (2 * hp, 2 * hp + 1):
            acc = acc_ref[hh]
            outs.append(acc * (1.0 / acc[:, A_V:A_V + 1]))
        o_ref[:, hp * LANES:(hp + 1) * LANES] = jnp.where(
            lane < A_V, outs[0], pltpu.roll(outs[1], A_V, axis=1)).astype(BF16)


def _query_block(lay, i, t):
    nq = lay.s // TQ
    return jnp.where(t < nq, i * nq + t, lay.ctx_block(i))


def _mla_attn(lay, bounded, q, k, v):
    hp = A_HEADS // MLA_HPS
    assert lay.s % MLA_KCHUNK == 0
    wq = MLA_HPS * HEAD_PAD
    lat = pl.BlockSpec((lay.s, wq), lambda i, p, t, f: (i, p))
    ctx = pl.BlockSpec((lay.l, wq), lambda i, p, t, f: (lay.ctx_block(i), p))
    return pl.pallas_call(
        _mla_attn_kernel,
        grid_spec=pltpu.PrefetchScalarGridSpec(
            num_scalar_prefetch=1,
            grid=(lay.b, hp, lay.s // TQ + 1),
            in_specs=[
                pl.BlockSpec((TQ, wq), lambda i, p, t, f: (_query_block(lay, i, t), p)),
                lat, ctx, lat, ctx,
            ],
            out_specs=pl.BlockSpec((TQ, MLA_HPS * A_V), lambda i, p, t, f: (_query_block(lay, i, t), p)),
            scratch_shapes=[
                pltpu.VMEM((MLA_HPS, TQ, LANES), F32),
                pltpu.VMEM((MLA_HPS, TQ, HEAD_PAD), F32),
            ],
        ),
        out_shape=jax.ShapeDtypeStruct((lay.n_tok, A_HEADS * A_V), BF16),
        compiler_params=_params(("arbitrary", "arbitrary", "arbitrary")),
        name="mla_attn",
    )(bounded, q, k, k, v, v)


def _dft_kernel(cn_ref, sn_ref, cc_ref, sc_ref, fx_ref, fc_ref, y_ref):
    is_ctx = pl.program_id(1) == pl.num_programs(1) - 1

    def run(c_ref, s_ref, f_ref):
        y = (jnp.dot(c_ref[...], f_ref[:, 0:F_WIDTH], preferred_element_type=F32)
             - jnp.dot(s_ref[...], f_ref[:, F_WIDTH:2 * F_WIDTH], preferred_element_type=F32))
        y_ref[...] = y.astype(BF16)

    @pl.when(is_ctx)
    def _():
        run(cc_ref, sc_ref, fc_ref)

    @pl.when(jnp.logical_not(is_ctx))
    def _():
        run(cn_ref, sn_ref, fx_ref)


def _dft(lay, fcs, cn, sn, cc, sc):
    nq = lay.s // TQ
    table = pl.BlockSpec((TQ, lay.s), lambda i, t: (jnp.minimum(t, nq - 1), 0))
    return pl.pallas_call(
        _dft_kernel,
        grid=(lay.b, nq + 1),
        in_specs=[
            table, table, _resident(cc), _resident(sc),
            pl.BlockSpec((lay.s, 2 * F_WIDTH), lambda i, t: (i, 0)),
            pl.BlockSpec((lay.l, 2 * F_WIDTH), lambda i, t: (lay.ctx_block(i), 0)),
        ],
        out_specs=pl.BlockSpec((TQ, F_WIDTH), lambda i, t: (_query_block(lay, i, t), 0)),
        out_shape=jax.ShapeDtypeStruct((lay.n_tok, F_WIDTH), BF16),
        compiler_params=_params(("arbitrary", "arbitrary")),
        name="fourier_dft",
    )(cn, sn, cc, sc, fcs, fcs)


def _mixed_residual(refs, arities, lat_tiles):
    n_row_refs = sum(arities)
    *mixes, h = _row_tiles(refs, arities, lat_tiles)
    w_ref, mod_ref = refs[n_row_refs:n_row_refs + 2]
    acc = None
    off = 0
    for m in mixes:
        wd = m.shape[-1]
        part = jnp.dot(m, w_ref[off:off + wd, :], preferred_element_type=F32)
        acc = part if acc is None else acc + part
        off += wd
    return h + mod_ref[2:3, :] * acc, mod_ref, refs[n_row_refs + 2:]


def _mix_specs(lay, mixes, w, h):
    specs = []
    for rows in list(mixes) + [h]:
        specs += _row_specs(lay, rows)
    return specs + [_resident(w), lay.mod_spec(h.width)]


def _mix_args(mixes, w, h, mod):
    args = []
    for rows in list(mixes) + [h]:
        args += rows.arrays
    return args + [w, mod]


def _mix_arities(mixes, h):
    return tuple(len(rows.arrays) for rows in list(mixes) + [h])


def _proj_ffn_kernel(*refs, arities, lat_tiles):
    h1, mod_ref, (g_ref, wg_ref, wu_ref, wd_ref, o_ref) = _mixed_residual(refs, arities, lat_tiles)
    xn = _norm_mod(h1, g_ref, mod_ref, 3).astype(BF16)
    ff = wg_ref.shape[1]
    y = None
    for c0 in range(0, ff, FFN_CHUNK):
        c1 = min(c0 + FFN_CHUNK, ff)
        gt = jnp.dot(xn, wg_ref[:, c0:c1], preferred_element_type=F32)
        up = jnp.dot(xn, wu_ref[:, c0:c1], preferred_element_type=F32)
        a = (_silu(gt) * up).astype(BF16)
        part = jnp.dot(a, wd_ref[c0:c1, :], preferred_element_type=F32)
        y = part if y is None else y + part
    o_ref[...] = h1 + mod_ref[5:6, :] * y


def _proj_ffn(lay, mixes, w_out, h, mod, g, wg, wu, wd):
    d = h.width
    return pl.pallas_call(
        functools.partial(_proj_ffn_kernel, arities=_mix_arities(mixes, h), lat_tiles=lay.lat_tiles),
        grid=(lay.all_tiles,),
        in_specs=_mix_specs(lay, mixes, w_out, h) + [_resident(g), _resident(wg), _resident(wu), _resident(wd)],
        out_specs=lay.tile(d),
        out_shape=jax.ShapeDtypeStruct((lay.n_tok, d), F32),
        compiler_params=_params(("arbitrary",)),
        name="proj_ffn",
    )(*_mix_args(mixes, w_out, h, mod), g, wg, wu, wd)


def _na_front_kernel(h_ref, mod_ref, g_ref, win_ref, gq_ref, gk_ref, q_ref, k_ref, v_ref):
    xn = _norm_mod(h_ref[...], g_ref, mod_ref, 0)
    px = jnp.dot(xn.astype(BF16), win_ref[...], preferred_element_type=F32)
    width = C_HEADS * C_DH
    lane = lax.broadcasted_iota(jnp.int32, (px.shape[0], LANES), 1)
    low = lane < C_DH
    scale = C_DH ** -0.5 * LOG2E

    def norm_pairs(base, g_ref_, out_ref, mult):
        gg = g_ref_[...]
        for j in range(width // LANES):
            x = px[:, base + j * LANES:base + (j + 1) * LANES]
            x2 = x * x
            s_all = jnp.sum(x2, axis=-1, keepdims=True)
            s_lo = jnp.sum(jnp.where(low, x2, 0.0), axis=-1, keepdims=True)
            ss = jnp.where(low, s_lo, s_all - s_lo)
            inv = lax.rsqrt(ss * (1.0 / C_DH) + EPS)
            if mult != 1.0:
                inv = inv * mult
            out_ref[:, j * LANES:(j + 1) * LANES] = (x * inv * gg).astype(BF16)

    norm_pairs(0, gq_ref, q_ref, scale)
    norm_pairs(width, gk_ref, k_ref, 1.0)
    v_ref[...] = px[:, 2 * width:].astype(BF16)


def _na_front(lay, h, mod, g, win, gq, gk):
    n, d = h.shape
    width = C_HEADS * C_DH
    return pl.pallas_call(
        _na_front_kernel,
        grid=(lay.all_tiles,),
        in_specs=[lay.tile(d), lay.mod_spec(d), _resident(g), _resident(win), _resident(gq), _resident(gk)],
        out_specs=[lay.tile(width)] * 3,
        out_shape=[jax.ShapeDtypeStruct((n, width), BF16)] * 3,
        compiler_params=_params(("arbitrary",)),
        name="na_front",
    )(h, mod, g, win, gq, gk)


def _na_attn_kernel(bounded_ref, qx_ref, qc_ref, kx_ref, kc_ref, vx_ref, vc_ref, planes_ref, shift_ref,
                    *refs, rows_n):
    out_refs, bias_ref = refs[:-1], refs[-1]
    ox_ref = out_refs[0]

    @pl.when(pl.program_id(1) == 0)
    def _():
        idx = _na_group_rows(rows_n)
        for p in range(idx.shape[0]):
            for hh in range(2):
                for qi in range(NA_GROUP):
                    for a in range(NA_WINDOW):
                        half = (a % 2) * GRID_W
                        bias_ref[p, hh, qi * GRID_W:(qi + 1) * GRID_W, a * GRID_W:(a + 1) * GRID_W] = (
                            planes_ref[hh, int(idx[p, qi, a]), :, half:half + GRID_W])

    qlen = NA_GROUP * GRID_W
    wlen = NA_WINDOW * GRID_W
    n_groups = rows_n // NA_GROUP
    lane_q = lax.broadcasted_iota(jnp.int32, (qlen, LANES), 1)
    kc = kc_ref[...]
    vc = vc_ref[...]

    if len(out_refs) > 1:
        qc = qc_ref[...]
        lane_c = lax.broadcasted_iota(jnp.int32, qc.shape, 1)
        outs = []
        for hh in range(2):
            sel = (lane_c < C_DH) if hh == 0 else (lane_c >= C_DH)
            outs.append(_softmax_pv(jnp.where(sel, qc, jnp.zeros_like(qc)), kc, vc))
        out_refs[1][...] = jnp.where(lane_c < C_DH, outs[0], outs[1]).astype(BF16)

    def group_body(g, carry, bounded):
        start = jnp.clip(g * NA_GROUP - WIN_R // 2, 0, rows_n - NA_WINDOW)
        pat = jnp.where(g == 0, 0, jnp.where(g == n_groups - 1, 2, 1))
        q0 = pl.multiple_of(g * qlen, qlen)
        k0 = pl.multiple_of(start * GRID_W, GRID_W)
        qr = qx_ref[pl.ds(q0, qlen), :]
        kw = kx_ref[pl.ds(k0, wlen), :]
        vw = vx_ref[pl.ds(k0, wlen), :]
        res = []
        for hh in range(2):
            sel = (lane_q < C_DH) if hh == 0 else (lane_q >= C_DH)
            qh = jnp.where(sel, qr, jnp.zeros_like(qr))
            s_loc = lax.dot_general(qh, kw, (((1,), (1,)), ((), ())),
                                    preferred_element_type=F32) + bias_ref[pat, hh]
            s_ctx = lax.dot_general(qh, kc, (((1,), (1,)), ((), ())), preferred_element_type=F32)
            if bounded:
                p_loc = jnp.exp2(s_loc)
                p_ctx = jnp.exp2(s_ctx - shift_ref[:, 0:1])
            else:
                m = jnp.maximum(jnp.max(s_loc, axis=-1, keepdims=True),
                                jnp.max(s_ctx, axis=-1, keepdims=True))
                p_loc = jnp.exp2(s_loc - m)
                p_ctx = jnp.exp2(s_ctx - m)
            l = jnp.sum(p_loc, axis=-1, keepdims=True) + jnp.sum(p_ctx, axis=-1, keepdims=True)
            o = (jnp.dot(p_loc.astype(BF16), vw, preferred_element_type=F32)
                 + jnp.dot(p_ctx.astype(BF16), vc, preferred_element_type=F32))
            res.append(o * (1.0 / l))
        ox_ref[pl.ds(q0, qlen), :] = jnp.where(lane_q < C_DH, res[0], res[1]).astype(BF16)
        return carry

    @pl.when(bounded_ref[0] != 0)
    def _():
        lax.fori_loop(0, n_groups, functools.partial(group_body, bounded=True), 0, unroll=4)

    @pl.when(bounded_ref[0] == 0)
    def _():
        lax.fori_loop(0, n_groups, functools.partial(group_body, bounded=False), 0)


def _na_attn(lay, bounded, q, k, v, planes, shift, need_ctx):
    width = q.shape[-1]
    hp = C_HEADS // 2
    lat = pl.BlockSpec((lay.s, LANES), lambda p, i, f: (i, p))
    ctx = pl.BlockSpec((lay.l, LANES), lambda p, i, f: (lay.ctx_block(i), p))
    out_specs = [pl.BlockSpec((lay.s, LANES), lambda p, i, f: (i, p))]
    out_shape = [jax.ShapeDtypeStruct((lay.n_lat, width), BF16)]
    if need_ctx:
        out_specs.append(pl.BlockSpec((lay.l, LANES), lambda p, i, f: (i, p)))
        out_shape.append(jax.ShapeDtypeStruct((lay.b * lay.l, width), BF16))
    n_pat = _na_group_rows(lay.s // GRID_W).shape[0]
    return pl.pallas_call(
        functools.partial(_na_attn_kernel, rows_n=lay.s // GRID_W),
        grid_spec=pltpu.PrefetchScalarGridSpec(
            num_scalar_prefetch=1,
            grid=(hp, lay.b),
            in_specs=[lat, ctx, lat, ctx, lat, ctx,
                      pl.BlockSpec((2,) + planes.shape[1:], lambda p, i, f: (p, 0, 0, 0)),
                      pl.BlockSpec(shift.shape, lambda p, i, f: (0, 0))],
            out_specs=out_specs,
            scratch_shapes=[pltpu.VMEM((n_pat, 2, NA_GROUP * GRID_W, NA_WINDOW * GRID_W), F32)],
        ),
        out_shape=out_shape,
        compiler_params=_params(("arbitrary", "arbitrary")),
        name="na_attn",
    )(bounded, q, q, k, k, v, v, planes, shift)


def _proj_router_kernel(*refs, arities, lat_tiles, tiles_per_batch, ctx_mod_row):
    h1, mod_ref, (g_ref, wr_ref, xr_ref, route_ref) = _mixed_residual(refs, arities, lat_tiles)
    xn = _norm_mod(h1, g_ref, mod_ref, 3)
    d = xn.shape[-1]
    xr_ref[:, 0:d] = xn
    xr_ref[:, d + LANES:2 * d + LANES] = h1
    t = pl.program_id(0)
    mod_row = jnp.where(t < lat_tiles, t // tiles_per_batch, ctx_mod_row).astype(F32)
    wr = wr_ref[...]
    x_hi, w_hi = xn.astype(BF16), wr.astype(BF16)
    x_lo = (xn - x_hi.astype(F32)).astype(BF16)
    w_lo = (wr - w_hi.astype(F32)).astype(BF16)
    both = jnp.dot(x_hi, jnp.concatenate([w_hi, w_lo], axis=-1), preferred_element_type=F32)
    logits = both[:, 0:LANES] + both[:, LANES:2 * LANES] + jnp.dot(x_lo, w_hi, preferred_element_type=F32)
    lane = lax.broadcasted_iota(jnp.int32, logits.shape, 1).astype(F32)
    lg = jnp.where(lane < N_EXPERTS, logits, -jnp.inf)
    m1 = jnp.max(lg, axis=-1, keepdims=True)
    i1 = jnp.min(jnp.where(lg == m1, lane, float(LANES)), axis=-1, keepdims=True)
    lg2 = jnp.where(lane == i1, -jnp.inf, lg)
    m2 = jnp.max(lg2, axis=-1, keepdims=True)
    i2 = jnp.min(jnp.where(lg2 == m2, lane, float(LANES)), axis=-1, keepdims=True)
    e2 = jnp.exp(m2 - m1)
    g1 = 1.0 / (1.0 + e2)
    g2 = e2 * g1
    first_low = i1 < i2
    vals = (jnp.minimum(i1, i2), jnp.maximum(i1, i2), jnp.where(first_low, g1, g2), jnp.where(first_low, g2, g1),
            mod_row)
    route = jnp.zeros_like(lane)
    for idx, val in enumerate(vals):
        route = jnp.where(lane == float(idx), val, route)
    route_ref[...] = route
    xr_ref[:, d:d + LANES] = route


def _proj_router(lay, mixes, w_out, h, mod, g, wr, n_tiles):
    d = h.width
    rows = n_tiles * lay.tm
    return pl.pallas_call(
        functools.partial(_proj_router_kernel, arities=_mix_arities(mixes, h), lat_tiles=lay.lat_tiles,
                          tiles_per_batch=lay.s // lay.tm, ctx_mod_row=lay.b),
        grid=(n_tiles,),
        in_specs=_mix_specs(lay, mixes, w_out, h) + [_resident(g), _resident(wr)],
        out_specs=[lay.tile(2 * d + LANES), lay.tile(LANES)],
        out_shape=[jax.ShapeDtypeStruct((rows, 2 * d + LANES), F32), jax.ShapeDtypeStruct((rows, LANES), F32)],
        compiler_params=_params(("arbitrary",)),
        name="proj_router",
    )(*_mix_args(mixes, w_out, h, mod), g, wr)


def _moe_plan(route, n_tok):
    n_pairs = len(_PAIR_LO)
    n_tiles = n_tok // MOE_TM + n_pairs
    rows = n_tiles * MOE_TM
    lo = route[:, 0].astype(jnp.int32)
    hi = route[:, 1].astype(jnp.int32)
    pid = (lo * (2 * N_EXPERTS - 1 - lo)) // 2 + (hi - lo - 1)
    onehot = (pid[:, None] == jnp.arange(n_pairs, dtype=jnp.int32)[None, :]).astype(jnp.int32)
    csum = jnp.cumsum(onehot, axis=0)
    rank = jnp.sum(csum * onehot, axis=1) - 1
    counts = csum[-1]
    padded = ((counts + MOE_TM - 1) // MOE_TM) * MOE_TM
    gend = jnp.cumsum(padded)
    gstart = gend - padded
    dest = jnp.sum(gstart[None, :] * onehot, axis=1) + rank
    src = jnp.full((rows,), -1, jnp.int32).at[dest].set(jnp.arange(n_tok, dtype=jnp.int32), unique_indices=True)
    valid = src >= 0
    src_tok = jnp.where(valid, src, 0)
    n_valid = jnp.sum(valid.reshape(n_tiles, MOE_TM).astype(jnp.int32), axis=1)
    n_used = gend[-1] // MOE_TM
    tile_row = jnp.minimum(jnp.arange(n_tiles, dtype=jnp.int32), n_used - 1) * MOE_TM
    group = jnp.minimum(jnp.sum((gend[None, :] <= tile_row[:, None]).astype(jnp.int32), axis=1), n_pairs - 1)
    ea = jnp.asarray(_PAIR_LO, jnp.int32)[group]
    eb = jnp.asarray(_PAIR_HI, jnp.int32)[group]
    return ea, eb, src_tok, n_valid, n_used.astype(jnp.int32).reshape(1)


def _moe_kernel(ea_ref, eb_ref, src_ref, nvalid_ref, nused_ref,
                x_hbm, g2_ref, wgua_ref, wda_ref, wgub_ref, wdb_ref,
                y_hbm, xbuf, ybuf, gsem, ssem, *, n_mod_rows):
    i = pl.program_id(0)
    n_steps = pl.num_programs(0)
    n_used = nused_ref[0]
    slot = i % 2

    def gather_copy(tile, slot_, r):
        return pltpu.make_async_copy(x_hbm.at[pl.ds(src_ref[tile * MOE_TM + r], 1)],
                                     xbuf.at[slot_, pl.ds(r, 1)], gsem.at[slot_])

    def scatter_copy(tile, slot_, r):
        return pltpu.make_async_copy(ybuf.at[slot_, pl.ds(r, 1)],
                                     y_hbm.at[pl.ds(src_ref[tile * MOE_TM + r], 1)], ssem.at[slot_])

    def start_gather(tile, slot_):
        for r in range(MOE_TM):
            gather_copy(tile, slot_, r).start()

    def wait_gather(slot_):
        pltpu.make_async_copy(x_hbm.at[pl.ds(0, MOE_TM)], xbuf.at[slot_], gsem.at[slot_]).wait()

    def start_scatter(tile, slot_):
        full = nvalid_ref[tile] == MOE_TM

        @pl.when(full)
        def _():
            for r in range(MOE_TM):
                scatter_copy(tile, slot_, r).start()

        @pl.when(jnp.logical_not(full))
        def _():
            def body(r, carry):
                scatter_copy(tile, slot_, r).start()
                return carry
            lax.fori_loop(0, nvalid_ref[tile], body, 0)

    def wait_scatter(tile, slot_):
        full = nvalid_ref[tile] == MOE_TM

        @pl.when(full)
        def _():
            pltpu.make_async_copy(ybuf.at[slot_], y_hbm.at[pl.ds(0, MOE_TM)], ssem.at[slot_]).wait()

        @pl.when(jnp.logical_not(full))
        def _():
            def body(r, carry):
                scatter_copy(tile, slot_, r).wait()
                return carry
            lax.fori_loop(0, nvalid_ref[tile], body, 0)

    @pl.when((i >= 2) & (i - 2 < n_used))
    def _():
        wait_scatter(i - 2, slot)

    @pl.when(i < n_used)
    def _():
        @pl.when(i == 0)
        def _():
            start_gather(0, 0)

        @pl.when(i + 1 < n_used)
        def _():
            start_gather(i + 1, 1 - slot)

        wait_gather(slot)
        d = ybuf.shape[-1]
        x = xbuf[slot, :, 0:d].astype(BF16)
        gates = xbuf[slot, :, d:d + LANES]

        def expert(wgu_ref, wd_ref):
            f = wd_ref.shape[0]
            gu = jnp.dot(x, wgu_ref[...], preferred_element_type=F32)
            a = (_silu(gu[:, 0:f]) * gu[:, f:2 * f]).astype(BF16)
            return jnp.dot(a, wd_ref[...], preferred_element_type=F32)

        y = (gates[:, 2:3] * expert(wgua_ref, wda_ref)
             + gates[:, 3:4] * expert(wgub_ref, wdb_ref))
        mod_row = gates[:, 4:5]
        gate2 = jnp.zeros_like(y)
        for r in range(n_mod_rows):
            gate2 = jnp.where(mod_row == float(r), g2_ref[r:r + 1, :], gate2)
        ybuf[slot] = xbuf[slot, :, d + LANES:2 * d + LANES] + gate2 * y
        start_scatter(i, slot)

    @pl.when(i == n_steps - 1)
    def _():
        for back in (2, 1):
            @pl.when(n_steps - back < n_used)
            def _():
                wait_scatter(n_steps - back, (n_steps - back) % 2)


def _moe(xr, route, gate2, n_mod_rows, wgu, wd, layer):
    n_tok, dr = xr.shape
    _, _, f, d = wd.shape
    assert dr == 2 * d + LANES
    ea, eb, src_tok, n_valid, n_used = _moe_plan(route, n_tok)
    n_tiles = ea.shape[0]
    wspec = lambda shape, which: pl.BlockSpec(
        (None, None) + shape, lambda i, ea_, eb_, s_, d_, u_: (layer, (ea_, eb_)[which][i], 0, 0))
    return pl.pallas_call(
        functools.partial(_moe_kernel, n_mod_rows=n_mod_rows),
        grid_spec=pltpu.PrefetchScalarGridSpec(
            num_scalar_prefetch=5,
            grid=(n_tiles,),
            in_specs=[
                pl.BlockSpec(memory_space=pl.ANY),
                _resident(gate2),
                wspec((d, 2 * f), 0), wspec((f, d), 0),
                wspec((d, 2 * f), 1), wspec((f, d), 1),
            ],
            out_specs=pl.BlockSpec(memory_space=pl.ANY),
            scratch_shapes=[
                pltpu.VMEM((2, MOE_TM, dr), F32),
                pltpu.VMEM((2, MOE_TM, d), F32),
                pltpu.SemaphoreType.DMA((2,)),
                pltpu.SemaphoreType.DMA((2,)),
            ],
        ),
        out_shape=jax.ShapeDtypeStruct((n_tok, d), F32),
        compiler_params=_params(("arbitrary",)),
        name="moe_experts",
    )(ea, eb, src_tok, n_valid, n_used, xr, gate2, wgu, wd, wgu, wd)


def _rope_tables(s, ctx_rows):
    t = jnp.arange(s)
    rows = (t // GRID_W).astype(F32)
    cols = (t % GRID_W).astype(F32)
    n_pairs = A_ROPE // 2
    per_axis = n_pairs // 2
    inv = ROPE_THETA ** (-jnp.arange(per_axis, dtype=F32) / per_axis)
    ang = jnp.concatenate([rows[:, None] * inv, cols[:, None] * inv], axis=-1)
    cos = jnp.repeat(jnp.cos(ang), 2, axis=-1)
    sin = jnp.repeat(jnp.sin(ang), 2, axis=-1)
    sign = jnp.tile(jnp.array([-1.0, 1.0], F32), n_pairs)
    pad = HEAD_PAD - A_QK
    t1 = jnp.concatenate([jnp.ones((s, A_NOPE), F32), cos, jnp.zeros((s, pad), F32)], axis=-1)
    t2 = jnp.concatenate([jnp.zeros((s, A_NOPE), F32), sin * sign, jnp.zeros((s, pad), F32)], axis=-1)
    c1 = jnp.concatenate([jnp.ones((ctx_rows, A_QK), F32), jnp.zeros((ctx_rows, pad), F32)], axis=-1)
    c2 = jnp.zeros((ctx_rows, HEAD_PAD), F32)
    return jnp.concatenate([t1, c1], axis=0), jnp.concatenate([t2, c2], axis=0)


_PAIR_SWAP = np.arange(A_ROPE) ^ 1


def _rope_lane_vec(g_tail, swapped):
    gt = g_tail[_PAIR_SWAP] if swapped else g_tail
    return jnp.concatenate([jnp.zeros((A_NOPE,), F32), gt, jnp.zeros((HEAD_PAD - A_QK,), F32)])[None]


def _mla_weights(w_in, w_uq, w_ukv, g_q, g_k):
    d = w_in.shape[0]
    q_end = A_Q_RANK
    kv_end = q_end + A_KV_RANK
    r_end = kv_end + A_ROPE
    z = lambda n: jnp.zeros((d, n), w_in.dtype)
    kr = w_in[:, kv_end:r_end]
    pad = HEAD_PAD - A_QK
    win = jnp.concatenate([w_in[:, :kv_end], z(A_NOPE), kr, z(pad), z(A_NOPE), kr[:, _PAIR_SWAP], z(pad),
                           w_in[:, r_end:]], axis=-1).astype(BF16)
    wq = w_uq.reshape(A_Q_RANK, A_HEADS, A_QK)
    zq = lambda n: jnp.zeros((A_Q_RANK, A_HEADS, n), wq.dtype)
    wq_main = jnp.concatenate([wq, zq(pad)], axis=-1)
    wq_swap = jnp.concatenate([zq(A_NOPE), wq[:, :, A_NOPE:][:, :, _PAIR_SWAP], zq(pad)], axis=-1)
    wq_ext = jnp.concatenate([wq_main.reshape(A_Q_RANK, -1), wq_swap.reshape(A_Q_RANK, -1)], axis=-1).astype(BF16)
    wkv = w_ukv.reshape(A_KV_RANK, A_HEADS, A_NOPE + A_V)
    zkv = jnp.zeros((A_KV_RANK, A_HEADS, HEAD_PAD - A_NOPE), wkv.dtype)
    wk = jnp.concatenate([wkv[:, :, :A_NOPE], zkv], axis=-1)
    wv = jnp.concatenate([wkv[:, :, A_NOPE:], zkv], axis=-1)
    wkv_ext = jnp.concatenate([wk.reshape(A_KV_RANK, -1), wv.reshape(A_KV_RANK, -1)], axis=-1).astype(BF16)

    bound = math.sqrt(A_QK) * jnp.max(jnp.abs(g_q)) * jnp.max(jnp.abs(g_k))
    bounded = bound <= MLA_SAFE_BOUND
    shift = jnp.where(bounded, -bound * LOG2E, 0.0)
    unit = lambda lane_idx: jnp.zeros((1, HEAD_PAD), F32).at[0, lane_idx].set(1.0)
    gq = jnp.concatenate([jnp.concatenate([g_q, jnp.zeros((pad,), F32)])[None], _rope_lane_vec(g_q[A_NOPE:], True),
                          unit(A_QK) * shift], axis=0)
    gk = jnp.concatenate([jnp.concatenate([g_k[:A_NOPE], jnp.zeros((HEAD_PAD - A_NOPE,), F32)])[None],
                          _rope_lane_vec(g_k[A_NOPE:], False), _rope_lane_vec(g_k[A_NOPE:], True),
                          unit(A_QK), unit(A_V)], axis=0)
    return win, wq_ext, wkv_ext, gq, gk, bounded.astype(jnp.int32).reshape(1)


def _dft_mats(n, norm):
    k = jnp.arange(n, dtype=jnp.int32)
    w = GRID_W if n % GRID_W == 0 and n > GRID_W else 1
    hi = jnp.arange(n // w, dtype=jnp.int32) * w
    lo = jnp.arange(w, dtype=jnp.int32)
    ang_hi = ((k[:, None] * hi[None, :]) % n).astype(F32) * (2.0 * math.pi / n)
    ang_lo = ((k[:, None] * lo[None, :]) % n).astype(F32) * (2.0 * math.pi / n)
    ch, sh = (jnp.repeat(f(ang_hi), w, axis=1) for f in (jnp.cos, jnp.sin))
    cl, sl = (jnp.tile(f(ang_lo), (1, n // w)) for f in (jnp.cos, jnp.sin))
    cos = ((ch * cl - sh * sl) * norm).astype(BF16)
    sin = ((sh * cl + ch * sl) * norm).astype(BF16)
    return cos, sin


def _channel_dft():
    c = np.arange(F_CH)
    ang = 2.0 * np.pi * ((c[:, None] * c[None, :]) % F_CH) / F_CH
    eye = np.eye(F_GROUPS)
    cb = np.kron(eye, np.cos(ang)) / math.sqrt(F_CH)
    sb = np.kron(eye, np.sin(ang)) / math.sqrt(F_CH)
    return jnp.asarray(np.concatenate([cb, sb], axis=1), BF16)


def _na_group_rows(rows_n):
    assert rows_n % NA_GROUP == 0 and rows_n >= NA_WINDOW + NA_GROUP
    n_groups = rows_n // NA_GROUP
    n_dr = 2 * WIN_R - 1
    idx = np.full((3, NA_GROUP, NA_WINDOW), n_dr, np.int32)
    seen = {}
    for g in range(n_groups):
        gs = int(np.clip(g * NA_GROUP - WIN_R // 2, 0, rows_n - NA_WINDOW))
        pat = 0 if g == 0 else (2 if g == n_groups - 1 else 1)
        cur = np.full((NA_GROUP, NA_WINDOW), n_dr, np.int32)
        for qi in range(NA_GROUP):
            r = g * NA_GROUP + qi
            start = int(np.clip(r - WIN_R // 2, 0, rows_n - WIN_R))
            assert gs <= start and start + WIN_R <= gs + NA_WINDOW
            for a in range(start, start + WIN_R):
                cur[qi, a - gs] = a - r + (WIN_R - 1)
        assert pat not in seen or np.array_equal(seen[pat], cur)
        seen[pat] = cur
        idx[pat] = cur
    return idx


def _na_score_bound(g_q, g_k, rpb):
    qk = math.sqrt(C_DH) * jnp.max(jnp.abs(g_q)) * jnp.max(jnp.abs(g_k))
    bound = qk + jnp.maximum(jnp.max(rpb), 0.0)
    lowest = -qk + jnp.minimum(jnp.min(rpb), 0.0)
    bounded = (bound - lowest) <= 2.0 * MLA_SAFE_BOUND
    shift = jnp.where(bounded, bound, 0.0).astype(F32)
    return bounded.astype(jnp.int32).reshape(1), shift


def _na_bias(rpb, shift):
    h, n_dr, n_dc = rpb.shape
    qc = np.arange(GRID_W)
    cs = np.clip(qc - WIN_C // 2, 0, GRID_W - WIN_C)
    kcol = np.arange(GRID_W)
    valid = (kcol[None, :] >= cs[:, None]) & (kcol[None, :] < cs[:, None] + WIN_C)
    dc = kcol[None, :] - qc[:, None] + (WIN_C - 1)
    onehot = (valid[:, :, None] & (dc[:, :, None] == np.arange(n_dc)[None, None, :])).astype(np.float32)
    col = jnp.einsum('hdc,qkc->hdqk', rpb.astype(F32), jnp.asarray(onehot), precision=lax.Precision.HIGHEST)
    col = jnp.where(jnp.asarray(valid)[None, None], (col - shift) * LOG2E, NEG_BIG)
    col = jnp.concatenate([col, jnp.full((h, 1, GRID_W, GRID_W), NEG_BIG, F32)], axis=1)
    return jnp.concatenate([col, col], axis=-1)


def kernel(x, c, ctx, c_ctx, w_mod, b_mod, norm_g, a_w_in, a_g_cq, a_g_ckv, a_w_uq, a_w_ukv, a_g_q, a_g_k, a_w_out,
           f_w_gate, f_w_up, f_w_down, c_w_in, c_g_q, c_g_k, c_rpb, c_w_out, m_w_router, m_w_gate, m_w_up, m_w_down):
    b, s, d = x.shape
    l = ctx.shape[1]
    depth = w_mod.shape[0]
    lay = _make_layout(b, s, l)

    mod_rows = 16
    cin = jnp.concatenate([c, c_ctx[None], jnp.zeros((mod_rows - b - 1, d), F32)], axis=0)
    mod_all = _modulation(cin, w_mod, b_mod).reshape(depth, mod_rows, 6, d)

    t1, t2 = _rope_tables(s, lay.tm)
    cn, sn = _dft_mats(s, 1.0 / math.sqrt(s))
    cc, sc = _dft_mats(l, 1.0 / math.sqrt(l))
    cb = _channel_dft()

    moe_wgu = jnp.concatenate([m_w_gate.astype(BF16), m_w_up.astype(BF16)], axis=-1)
    moe_wd = m_w_down.astype(BF16)
    h = _Rows(x.reshape(b * s, d), ctx.reshape(b * l, d))
    for i in range(depth):
        j = i // 2
        need_ctx = i < depth - 1
        mod = mod_all[i]
        g1 = norm_g[i, 0][None]
        g2 = norm_g[i, 1][None]
        if i % 2 == 0:
            win, wq_ext, wkv_ext, gq, gk, bounded = _mla_weights(a_w_in[j], a_w_uq[j], a_w_ukv[j],
                                                                 a_g_q[j], a_g_k[j])
            q, k, v, fcs = _mla_front(lay, h, mod, g1, win, a_g_cq[j][None], a_g_ckv[j][None], wq_ext, wkv_ext,
                                      t1, t2, gq, gk, cb)
            att = _mla_attn(lay, bounded, q, k, v)
            yf = _dft(lay, fcs, cn, sn, cc, sc)
            h = _Rows(_proj_ffn(lay, [_Rows(att), _Rows(yf)], a_w_out[j].astype(BF16), h, mod, g2,
                                f_w_gate[j].astype(BF16), f_w_up[j].astype(BF16), f_w_down[j].astype(BF16)))
        else:
            gq2 = jnp.tile(c_g_q[j], 2)[None]
            gk2 = jnp.tile(c_g_k[j], 2)[None]
            q, k, v = _na_front(lay, h.lat, mod, g1, c_w_in[j].astype(BF16), gq2, gk2)
            na_bounded, na_shift = _na_score_bound(c_g_q[j], c_g_k[j], c_rpb[j])
            bias = _na_bias(c_rpb[j], na_shift)
            outs = _na_attn(lay, na_bounded, q, k, v, bias, jnp.full((1, LANES), na_shift * LOG2E, F32), need_ctx)
            att = _Rows(outs[0], outs[1]) if need_ctx else _Rows(outs[0])
            wr = jnp.concatenate([m_w_router[j], jnp.zeros((d, LANES - N_EXPERTS), F32)], axis=-1)
            n_tiles = lay.all_tiles if need_ctx else lay.lat_tiles
            xr, route = _proj_router(lay, [att], c_w_out[j].astype(BF16), h, mod, g2, wr, n_tiles)
            h = _Rows(_moe(xr, route, mod[:, 5, :], b + 1, moe_wgu, moe_wd, j))
    return h.lat[:lay.n_lat].reshape(b, s, d)
```

```python
import functools
import math
from typing import NamedTuple

import jax
import jax.numpy as jnp
import numpy as np
from jax import lax
from jax.experimental import pallas as pl
from jax.experimental.pallas import tpu as pltpu

F32 = jnp.float32
BF16 = jnp.bfloat16

GRID_W = 64
A_HEADS = 12
A_NOPE = 64
A_ROPE = 32
A_QK = A_NOPE + A_ROPE
A_V = 64
A_Q_RANK = 256
A_KV_RANK = 128
F_GROUPS = 4
F_CH = 64
F_WIDTH = F_GROUPS * F_CH
C_HEADS = 16
C_DH = 64
WIN_R = 8
WIN_C = 16
N_EXPERTS = 8
ROPE_THETA = 10000.0
EPS = 1e-6

LANES = 128
HEAD_PAD = 128
TM = 512
TQ = 256
FFN_CHUNK = 1536
MOE_TM = 256
_PAIR_LO = tuple(lo for lo in range(N_EXPERTS) for hi in range(lo + 1, N_EXPERTS))
_PAIR_HI = tuple(hi for lo in range(N_EXPERTS) for hi in range(lo + 1, N_EXPERTS))
LOG2E = math.log2(math.e)
MLA_SAFE_BOUND = 40.0
MLA_KCHUNK = 512
MLA_HPS = 6
NA_GROUP = 4
NA_WINDOW = NA_GROUP + WIN_R
VMEM_LIMIT = 52 * 1024 * 1024
NEG_BIG = -1e30


def _params(sem, vmem=VMEM_LIMIT):
    return pltpu.CompilerParams(dimension_semantics=sem, vmem_limit_bytes=vmem)


def _rms(x, g, n=None):
    n = x.shape[-1] if n is None else n
    ss = jnp.sum(x * x, axis=-1, keepdims=True)
    return x * lax.rsqrt(ss * (1.0 / n) + EPS) * g


def _silu(x):
    return x * (1.0 / (1.0 + jnp.exp(-x)))


class _Layout(NamedTuple):
    b: int
    s: int
    l: int
    tm: int

    @property
    def n_lat(self):
        return self.b * self.s

    @property
    def n_tok(self):
        return self.b * (self.s + self.l)

    @property
    def lat_tiles(self):
        return self.n_lat // self.tm

    @property
    def all_tiles(self):
        return self.n_tok // self.tm

    def tile(self, width):
        return pl.BlockSpec((self.tm, width), lambda t: (t, 0))

    def mod_spec(self, d):
        per_batch = self.s // self.tm
        return pl.BlockSpec((None, 6, d), lambda t: (jnp.where(t < self.lat_tiles, t // per_batch, self.b), 0, 0))

    def ctx_block(self, i):
        return self.n_lat // self.l + i


class _Rows(NamedTuple):
    lat: jax.Array
    ctx: jax.Array | None = None

    @property
    def width(self):
        return self.lat.shape[-1]

    @property
    def arrays(self):
        return [self.lat] if self.ctx is None else [self.lat, self.ctx]


def _row_specs(lay, rows):
    if rows.ctx is None:
        return [lay.tile(rows.width)]
    tm, nl = lay.tm, lay.lat_tiles
    return [pl.BlockSpec((tm, rows.width), lambda t: (jnp.minimum(t, nl - 1), 0)),
            pl.BlockSpec((tm, rows.width), lambda t: (jnp.maximum(t - nl, 0), 0))]


def _row_tiles(refs, arities, lat_tiles):
    tiles, pos = [], 0
    for arity in arities:
        if arity == 1:
            tiles.append(refs[pos][...])
        else:
            tiles.append(jnp.where(pl.program_id(0) < lat_tiles, refs[pos][...], refs[pos + 1][...]))
        pos += arity
    return tiles


def _make_layout(b, s, l):
    tm = TM if (s % TM == 0 and (b * l) % TM == 0) else l
    assert l == TQ and s % tm == 0 and (b * l) % tm == 0 and s % GRID_W == 0
    return _Layout(b, s, l, tm)


def _resident(a):
    return pl.BlockSpec(a.shape, lambda *_: (0,) * a.ndim, pipeline_mode=pl.Buffered(1))


def _norm_mod(x, g_ref, mod_ref, row):
    gain = g_ref[...] * (1.0 + mod_ref[row + 1:row + 2, :])
    ss = jnp.sum(x * x, axis=-1, keepdims=True)
    return x * lax.rsqrt(ss * (1.0 / x.shape[-1]) + EPS) * gain + mod_ref[row:row + 1, :]


def _mod_kernel(c_ref, w_ref, b_ref, o_ref):
    s = _silu(c_ref[...])
    o_ref[...] = jnp.dot(s, w_ref[...], precision=lax.Precision.HIGHEST,
                         preferred_element_type=F32) + b_ref[...]


def _modulation(cin, w_mod, b_mod):
    depth, d, n = w_mod.shape
    tn = 1536
    rows = cin.shape[0]
    return pl.pallas_call(
        _mod_kernel,
        grid=(depth, n // tn),
        in_specs=[
            pl.BlockSpec((rows, d), lambda l, j: (0, 0)),
            pl.BlockSpec((None, d, tn), lambda l, j: (l, 0, j)),
            pl.BlockSpec((None, 1, tn), lambda l, j: (l, 0, j)),
        ],
        out_specs=pl.BlockSpec((None, rows, tn), lambda l, j: (l, 0, j)),
        out_shape=jax.ShapeDtypeStruct((depth, rows, n), F32),
        compiler_params=_params(("arbitrary", "arbitrary")),
        name="modulation",
    )(cin, w_mod, b_mod.reshape(depth, 1, n))


def _mla_front_kernel(*refs, h_arity, lat_tiles):
    (mod_ref, g_ref, win_ref, gcq_ref, gckv_ref, wq_ref, wkv_ref, t1_ref, t2_ref, gq_ref, gk_ref, cb_ref,
     q_ref, k_ref, v_ref, fcs_ref) = refs[h_arity:]
    (h,) = _row_tiles(refs, (h_arity,), lat_tiles)
    xn = _norm_mod(h, g_ref, mod_ref, 0)
    px = jnp.dot(xn.astype(BF16), win_ref[...], preferred_element_type=F32)
    t1 = t1_ref[...]
    t2 = t2_ref[...]
    hw = A_HEADS * HEAD_PAD

    qn = _rms(px[:, 0:A_Q_RANK], gcq_ref[...]).astype(BF16)
    qq = jnp.dot(qn, wq_ref[...], preferred_element_type=F32)
    aq = t1 * gq_ref[0:1, :]
    bq = t2 * gq_ref[1:2, :]
    q_shift = gq_ref[2:3, :]
    scale = A_QK ** -0.5 * LOG2E
    for h in range(A_HEADS):
        qm = qq[:, h * HEAD_PAD:(h + 1) * HEAD_PAD]
        qs = qq[:, hw + h * HEAD_PAD:hw + (h + 1) * HEAD_PAD]
        ss = jnp.sum(qm * qm, axis=-1, keepdims=True)
        inv = lax.rsqrt(ss * (1.0 / A_QK) + EPS) * scale
        q_ref[:, h * HEAD_PAD:(h + 1) * HEAD_PAD] = ((qm * aq + qs * bq) * inv + q_shift).astype(BF16)

    kvn = _rms(px[:, A_Q_RANK:A_Q_RANK + A_KV_RANK], gckv_ref[...]).astype(BF16)
    kv = jnp.dot(kvn, wkv_ref[...], preferred_element_type=F32)
    krm = px[:, 384:512]
    krs = px[:, 512:640]
    tail = krm * (t1 * gk_ref[1:2, :]) + krs * (t2 * gk_ref[2:3, :])
    ssr = jnp.sum(krm * krm, axis=-1, keepdims=True)
    gkn = gk_ref[0:1, :]
    k_one = gk_ref[3:4, :]
    v_one = gk_ref[4:5, :]
    for h in range(A_HEADS):
        km = kv[:, h * HEAD_PAD:(h + 1) * HEAD_PAD]
        ss = jnp.sum(km * km, axis=-1, keepdims=True) + ssr
        inv = lax.rsqrt(ss * (1.0 / A_QK) + EPS)
        k_ref[:, h * HEAD_PAD:(h + 1) * HEAD_PAD] = ((km * gkn + tail) * inv + k_one).astype(BF16)
        v_ref[:, h * HEAD_PAD:(h + 1) * HEAD_PAD] = (kv[:, hw + h * HEAD_PAD:hw + (h + 1) * HEAD_PAD]
                                                     + v_one).astype(BF16)

    f = px[:, 640:896].astype(BF16)
    fcs_ref[...] = jnp.dot(f, cb_ref[...], preferred_element_type=F32).astype(BF16)


def _mla_front(lay, h, mod, g, win, gcq, gckv, wq, wkv, t1, t2, gq, gk, cb):
    n, d = lay.n_tok, h.width
    hw = A_HEADS * HEAD_PAD
    per_batch = lay.s // lay.tm
    rope = pl.BlockSpec((lay.tm, LANES), lambda t: (jnp.where(t < lay.lat_tiles, t % per_batch, per_batch), 0))
    return pl.pallas_call(
        functools.partial(_mla_front_kernel, h_arity=len(h.arrays), lat_tiles=lay.lat_tiles),
        grid=(lay.all_tiles,),
        in_specs=_row_specs(lay, h) + [
            lay.mod_spec(d),
            _resident(g), _resident(win), _resident(gcq), _resident(gckv), _resident(wq), _resident(wkv),
            rope, rope,
            _resident(gq), _resident(gk), _resident(cb),
        ],
        out_specs=[lay.tile(hw), lay.tile(hw), lay.tile(hw), lay.tile(2 * F_WIDTH)],
        out_shape=[
            jax.ShapeDtypeStruct((n, hw), BF16),
            jax.ShapeDtypeStruct((n, hw), BF16),
            jax.ShapeDtypeStruct((n, hw), BF16),
            jax.ShapeDtypeStruct((n, 2 * F_WIDTH), BF16),
        ],
        compiler_params=_params(("arbitrary",)),
        name="mla_front",
    )(*h.arrays, mod, g, win, gcq, gckv, wq, wkv, t1, t2, gq, gk, cb)


def _softmax_pv(q, k, v):
    s = lax.dot_general(q, k, (((1,), (1,)), ((), ())), preferred_element_type=F32)
    m = jnp.max(s, axis=-1, keepdims=True)
    p = jnp.exp2(s - m)
    l = jnp.sum(p, axis=-1, keepdims=True)
    o = jnp.dot(p.astype(BF16), v, preferred_element_type=F32)
    return o * (1.0 / l)


def _mla_attn_kernel(bounded_ref, q_ref, kx_ref, kc_ref, vx_ref, vc_ref, o_ref, m_ref, acc_ref):
    t = pl.program_id(2)
    ck = MLA_KCHUNK
    n_chunks = kx_ref.shape[0] // ck

    def head(ref, rows=slice(None)):
        return lambda hh: ref[rows, hh * HEAD_PAD:(hh + 1) * HEAD_PAD]

    def scores(hh, k):
        return lax.dot_general(head(q_ref)(hh), k, (((1,), (1,)), ((), ())), preferred_element_type=F32)

    def bounded(with_latent):
        for hh in range(MLA_HPS):
            p = jnp.exp2(scores(hh, head(kc_ref)(hh))).astype(BF16)
            acc = jnp.dot(p, head(vc_ref)(hh), preferred_element_type=F32)
            if with_latent:
                p = jnp.exp2(scores(hh, head(kx_ref)(hh))).astype(BF16)
                acc = acc + jnp.dot(p, head(vx_ref)(hh), preferred_element_type=F32)
            acc_ref[hh] = acc

    def online(with_latent):
        for hh in range(MLA_HPS):
            s = scores(hh, head(kc_ref)(hh))
            m = jnp.max(s, axis=-1, keepdims=True)
            m_ref[hh] = jnp.broadcast_to(m, m_ref.shape[1:])
            acc_ref[hh] = jnp.dot(jnp.exp2(s - m).astype(BF16), head(vc_ref)(hh), preferred_element_type=F32)

        def body(c, carry):
            rows = pl.ds(pl.multiple_of(c * ck, ck), ck)
            for hh in range(MLA_HPS):
                s = scores(hh, head(kx_ref, rows)(hh))
                m_old = m_ref[hh]
                m_new = jnp.maximum(m_old, jnp.max(s, axis=-1, keepdims=True))
                p = jnp.exp2(s - jnp.tile(m_new, (1, ck // LANES))).astype(BF16)
                acc_ref[hh] = (jnp.exp2(m_old - m_new) * acc_ref[hh]
                               + jnp.dot(p, head(vx_ref, rows)(hh), preferred_element_type=F32))
                m_ref[hh] = m_new
            return carry

        if with_latent:
            lax.fori_loop(0, n_chunks, body, 0)

    is_ctx = t == pl.num_programs(2) - 1
    fast = bounded_ref[0] != 0

    @pl.when(fast & is_ctx)
    def _():
        bounded(False)

    @pl.when(fast & jnp.logical_not(is_ctx))
    def _():
        bounded(True)

    @pl.when(jnp.logical_not(fast) & is_ctx)
    def _():
        online(False)

    @pl.when(jnp.logical_not(fast) & jnp.logical_not(is_ctx))
    def _():
        online(True)

    lane = lax.broadcasted_iota(jnp.int32, (o_ref.shape[0], LANES), 1)
    for hp in range(MLA_HPS // 2):
        outs = []
        for hh in (2 * hp, 2 * hp + 1):
            acc = acc_ref[hh]
            outs.append(acc * (1.0 / acc[:, A_V:A_V + 1]))
        o_ref[:, hp * LANES:(hp + 1) * LANES] = jnp.where(
            lane < A_V, outs[0], pltpu.roll(outs[1], A_V, axis=1)).astype(BF16)


def _query_block(lay, i, t):
    nq = lay.s // TQ
    return jnp.where(t < nq, i * nq + t, lay.ctx_block(i))


def _mla_attn(lay, bounded, q, k, v):
    hp = A_HEADS // MLA_HPS
    assert lay.s % MLA_KCHUNK == 0
    wq = MLA_HPS * HEAD_PAD
    lat = pl.BlockSpec((lay.s, wq), lambda i, p, t, f: (i, p))
    ctx = pl.BlockSpec((lay.l, wq), lambda i, p, t, f: (lay.ctx_block(i), p))
    return pl.pallas_call(
        _mla_attn_kernel,
        grid_spec=pltpu.PrefetchScalarGridSpec(
            num_scalar_prefetch=1,
            grid=(lay.b, hp, lay.s // TQ + 1),
            in_specs=[
                pl.BlockSpec((TQ, wq), lambda i, p, t, f: (_query_block(lay, i, t), p)),
                lat, ctx, lat, ctx,
            ],
            out_specs=pl.BlockSpec((TQ, MLA_HPS * A_V), lambda i, p, t, f: (_query_block(lay, i, t), p)),
            scratch_shapes=[
                pltpu.VMEM((MLA_HPS, TQ, LANES), F32),
                pltpu.VMEM((MLA_HPS, TQ, HEAD_PAD), F32),
            ],
        ),
        out_shape=jax.ShapeDtypeStruct((lay.n_tok, A_HEADS * A_V), BF16),
        compiler_params=_params(("arbitrary", "arbitrary", "arbitrary")),
        name="mla_attn",
    )(bounded, q, k, k, v, v)


def _dft_kernel(cn_ref, sn_ref, cc_ref, sc_ref, fx_ref, fc_ref, y_ref):
    is_ctx = pl.program_id(1) == pl.num_programs(1) - 1

    def run(c_ref, s_ref, f_ref):
        y = (jnp.dot(c_ref[...], f_ref[:, 0:F_WIDTH], preferred_element_type=F32)
             - jnp.dot(s_ref[...], f_ref[:, F_WIDTH:2 * F_WIDTH], preferred_element_type=F32))
        y_ref[...] = y.astype(BF16)

    @pl.when(is_ctx)
    def _():
        run(cc_ref, sc_ref, fc_ref)

    @pl.when(jnp.logical_not(is_ctx))
    def _():
        run(cn_ref, sn_ref, fx_ref)


def _dft(lay, fcs, cn, sn, cc, sc):
    nq = lay.s // TQ
    table = pl.BlockSpec((TQ, lay.s), lambda i, t: (jnp.minimum(t, nq - 1), 0))
    return pl.pallas_call(
        _dft_kernel,
        grid=(lay.b, nq + 1),
        in_specs=[
            table, table, _resident(cc), _resident(sc),
            pl.BlockSpec((lay.s, 2 * F_WIDTH), lambda i, t: (i, 0)),
            pl.BlockSpec((lay.l, 2 * F_WIDTH), lambda i, t: (lay.ctx_block(i), 0)),
        ],
        out_specs=pl.BlockSpec((TQ, F_WIDTH), lambda i, t: (_query_block(lay, i, t), 0)),
        out_shape=jax.ShapeDtypeStruct((lay.n_tok, F_WIDTH), BF16),
        compiler_params=_params(("arbitrary", "arbitrary")),
        name="fourier_dft",
    )(cn, sn, cc, sc, fcs, fcs)


def _mixed_residual(refs, arities, lat_tiles):
    n_row_refs = sum(arities)
    *mixes, h = _row_tiles(refs, arities, lat_tiles)
    w_ref, mod_ref = refs[n_row_refs:n_row_refs + 2]
    acc = None
    off = 0
    for m in mixes:
        wd = m.shape[-1]
        part = jnp.dot(m, w_ref[off:off + wd, :], preferred_element_type=F32)
        acc = part if acc is None else acc + part
        off += wd
    return h + mod_ref[2:3, :] * acc, mod_ref, refs[n_row_refs + 2:]


def _mix_specs(lay, mixes, w, h):
    specs = []
    for rows in list(mixes) + [h]:
        specs += _row_specs(lay, rows)
    return specs + [_resident(w), lay.mod_spec(h.width)]


def _mix_args(mixes, w, h, mod):
    args = []
    for rows in list(mixes) + [h]:
        args += rows.arrays
    return args + [w, mod]


def _mix_arities(mixes, h):
    return tuple(len(rows.arrays) for rows in list(mixes) + [h])


def _proj_ffn_kernel(*refs, arities, lat_tiles):
    h1, mod_ref, (g_ref, wg_ref, wu_ref, wd_ref, o_ref) = _mixed_residual(refs, arities, lat_tiles)
    xn = _norm_mod(h1, g_ref, mod_ref, 3).astype(BF16)
    ff = wg_ref.shape[1]
    y = None
    for c0 in range(0, ff, FFN_CHUNK):
        c1 = min(c0 + FFN_CHUNK, ff)
        gt = jnp.dot(xn, wg_ref[:, c0:c1], preferred_element_type=F32)
        up = jnp.dot(xn, wu_ref[:, c0:c1], preferred_element_type=F32)
        a = (_silu(gt) * up).astype(BF16)
        part = jnp.dot(a, wd_ref[c0:c1, :], preferred_element_type=F32)
        y = part if y is None else y + part
    o_ref[...] = h1 + mod_ref[5:6, :] * y


def _proj_ffn(lay, mixes, w_out, h, mod, g, wg, wu, wd):
    d = h.width
    return pl.pallas_call(
        functools.partial(_proj_ffn_kernel, arities=_mix_arities(mixes, h), lat_tiles=lay.lat_tiles),
        grid=(lay.all_tiles,),
        in_specs=_mix_specs(lay, mixes, w_out, h) + [_resident(g), _resident(wg), _resident(wu), _resident(wd)],
        out_specs=lay.tile(d),
        out_shape=jax.ShapeDtypeStruct((lay.n_tok, d), F32),
        compiler_params=_params(("arbitrary",)),
        name="proj_ffn",
    )(*_mix_args(mixes, w_out, h, mod), g, wg, wu, wd)


def _na_front_kernel(h_ref, mod_ref, g_ref, win_ref, gq_ref, gk_ref, q_ref, k_ref, v_ref):
    xn = _norm_mod(h_ref[...], g_ref, mod_ref, 0)
    px = jnp.dot(xn.astype(BF16), win_ref[...], preferred_element_type=F32)
    width = C_HEADS * C_DH
    lane = lax.broadcasted_iota(jnp.int32, (px.shape[0], LANES), 1)
    low = lane < C_DH
    scale = C_DH ** -0.5 * LOG2E

    def norm_pairs(base, g_ref_, out_ref, mult):
        gg = g_ref_[...]
        for j in range(width // LANES):
            x = px[:, base + j * LANES:base + (j + 1) * LANES]
            x2 = x * x
            s_all = jnp.sum(x2, axis=-1, keepdims=True)
            s_lo = jnp.sum(jnp.where(low, x2, 0.0), axis=-1, keepdims=True)
            ss = jnp.where(low, s_lo, s_all - s_lo)
            inv = lax.rsqrt(ss * (1.0 / C_DH) + EPS)
            if mult != 1.0:
                inv = inv * mult
            out_ref[:, j * LANES:(j + 1) * LANES] = (x * inv * gg).astype(BF16)

    norm_pairs(0, gq_ref, q_ref, scale)
    norm_pairs(width, gk_ref, k_ref, 1.0)
    v_ref[...] = px[:, 2 * width:].astype(BF16)


def _na_front(lay, h, mod, g, win, gq, gk):
    n, d = h.shape
    width = C_HEADS * C_DH
    return pl.pallas_call(
        _na_front_kernel,
        grid=(lay.all_tiles,),
        in_specs=[lay.tile(d), lay.mod_spec(d), _resident(g), _resident(win), _resident(gq), _resident(gk)],
        out_specs=[lay.tile(width)] * 3,
        out_shape=[jax.ShapeDtypeStruct((n, width), BF16)] * 3,
        compiler_params=_params(("arbitrary",)),
        name="na_front",
    )(h, mod, g, win, gq, gk)


def _na_attn_kernel(bounded_ref, qx_ref, qc_ref, kx_ref, kc_ref, vx_ref, vc_ref, planes_ref, shift_ref,
                    *refs, rows_n):
    out_refs, bias_ref = refs[:-1], refs[-1]
    ox_ref = out_refs[0]

    @pl.when(pl.program_id(1) == 0)
    def _():
        idx = _na_group_rows(rows_n)
        for p in range(idx.shape[0]):
            for hh in range(2):
                for qi in range(NA_GROUP):
                    for a in range(NA_WINDOW):
                        half = (a % 2) * GRID_W
                        bias_ref[p, hh, qi * GRID_W:(qi + 1) * GRID_W, a * GRID_W:(a + 1) * GRID_W] = (
                            planes_ref[hh, int(idx[p, qi, a]), :, half:half + GRID_W])

    qlen = NA_GROUP * GRID_W
    wlen = NA_WINDOW * GRID_W
    n_groups = rows_n // NA_GROUP
    lane_q = lax.broadcasted_iota(jnp.int32, (qlen, LANES), 1)
    kc = kc_ref[...]
    vc = vc_ref[...]

    if len(out_refs) > 1:
        qc = qc_ref[...]
        lane_c = lax.broadcasted_iota(jnp.int32, qc.shape, 1)
        outs = []
        for hh in range(2):
            sel = (lane_c < C_DH) if hh == 0 else (lane_c >= C_DH)
            outs.append(_softmax_pv(jnp.where(sel, qc, jnp.zeros_like(qc)), kc, vc))
        out_refs[1][...] = jnp.where(lane_c < C_DH, outs[0], outs[1]).astype(BF16)

    def group_body(g, carry, bounded):
        start = jnp.clip(g * NA_GROUP - WIN_R // 2, 0, rows_n - NA_WINDOW)
        pat = jnp.where(g == 0, 0, jnp.where(g == n_groups - 1, 2, 1))
        q0 = pl.multiple_of(g * qlen, qlen)
        k0 = pl.multiple_of(start * GRID_W, GRID_W)
        qr = qx_ref[pl.ds(q0, qlen), :]
        kw = kx_ref[pl.ds(k0, wlen), :]
        vw = vx_ref[pl.ds(k0, wlen), :]
        res = []
        for hh in range(2):
            sel = (lane_q < C_DH) if hh == 0 else (lane_q >= C_DH)
            qh = jnp.where(sel, qr, jnp.zeros_like(qr))
            s_loc = lax.dot_general(qh, kw, (((1,), (1,)), ((), ())),
                                    preferred_element_type=F32) + bias_ref[pat, hh]
            s_ctx = lax.dot_general(qh, kc, (((1,), (1,)), ((), ())), preferred_element_type=F32)
            if bounded:
                p_loc = jnp.exp2(s_loc)
                p_ctx = jnp.exp2(s_ctx - shift_ref[:, 0:1])
            else:
                m = jnp.maximum(jnp.max(s_loc, axis=-1, keepdims=True),
                                jnp.max(s_ctx, axis=-1, keepdims=True))
                p_loc = jnp.exp2(s_loc - m)
                p_ctx = jnp.exp2(s_ctx - m)
            l = jnp.sum(p_loc, axis=-1, keepdims=True) + jnp.sum(p_ctx, axis=-1, keepdims=True)
            o = (jnp.dot(p_loc.astype(BF16), vw, preferred_element_type=F32)
                 + jnp.dot(p_ctx.astype(BF16), vc, preferred_element_type=F32))
            res.append(o * (1.0 / l))
        ox_ref[pl.ds(q0, qlen), :] = jnp.where(lane_q < C_DH, res[0], res[1]).astype(BF16)
        return carry

    @pl.when(bounded_ref[0] != 0)
    def _():
        lax.fori_loop(0, n_groups, functools.partial(group_body, bounded=True), 0, unroll=8)

    @pl.when(bounded_ref[0] == 0)
    def _():
        lax.fori_loop(0, n_groups, functools.partial(group_body, bounded=False), 0)


def _na_attn(lay, bounded, q, k, v, planes, shift, need_ctx):
    width = q.shape[-1]
    hp = C_HEADS // 2
    lat = pl.BlockSpec((lay.s, LANES), lambda p, i, f: (i, p))
    ctx = pl.BlockSpec((lay.l, LANES), lambda p, i, f: (lay.ctx_block(i), p))
    out_specs = [pl.BlockSpec((lay.s, LANES), lambda p, i, f: (i, p))]
    out_shape = [jax.ShapeDtypeStruct((lay.n_lat, width), BF16)]
    if need_ctx:
        out_specs.append(pl.BlockSpec((lay.l, LANES), lambda p, i, f: (i, p)))
        out_shape.append(jax.ShapeDtypeStruct((lay.b * lay.l, width), BF16))
    n_pat = _na_group_rows(lay.s // GRID_W).shape[0]
    return pl.pallas_call(
        functools.partial(_na_attn_kernel, rows_n=lay.s // GRID_W),
        grid_spec=pltpu.PrefetchScalarGridSpec(
            num_scalar_prefetch=1,
            grid=(hp, lay.b),
            in_specs=[lat, ctx, lat, ctx, lat, ctx,
                      pl.BlockSpec((2,) + planes.shape[1:], lambda p, i, f: (p, 0, 0, 0)),
                      pl.BlockSpec(shift.shape, lambda p, i, f: (0, 0))],
            out_specs=out_specs,
            scratch_shapes=[pltpu.VMEM((n_pat, 2, NA_GROUP * GRID_W, NA_WINDOW * GRID_W), F32)],
        ),
        out_shape=out_shape,
        compiler_params=_params(("arbitrary", "arbitrary")),
        name="na_attn",
    )(bounded, q, q, k, k, v, v, planes, shift)


def _proj_router_kernel(*refs, arities, lat_tiles, tiles_per_batch, ctx_mod_row):
    h1, mod_ref, (g_ref, wr_ref, xr_ref, route_ref) = _mixed_residual(refs, arities, lat_tiles)
    xn = _norm_mod(h1, g_ref, mod_ref, 3)
    d = xn.shape[-1]
    xr_ref[:, 0:d] = xn
    xr_ref[:, d + LANES:2 * d + LANES] = h1
    t = pl.program_id(0)
    mod_row = jnp.where(t < lat_tiles, t // tiles_per_batch, ctx_mod_row).astype(F32)
    wr = wr_ref[...]
    x_hi, w_hi = xn.astype(BF16), wr.astype(BF16)
    x_lo = (xn - x_hi.astype(F32)).astype(BF16)
    w_lo = (wr - w_hi.astype(F32)).astype(BF16)
    both = jnp.dot(x_hi, jnp.concatenate([w_hi, w_lo], axis=-1), preferred_element_type=F32)
    logits = both[:, 0:LANES] + both[:, LANES:2 * LANES] + jnp.dot(x_lo, w_hi, preferred_element_type=F32)
    lane = lax.broadcasted_iota(jnp.int32, logits.shape, 1).astype(F32)
    lg = jnp.where(lane < N_EXPERTS, logits, -jnp.inf)
    m1 = jnp.max(lg, axis=-1, keepdims=True)
    i1 = jnp.min(jnp.where(lg == m1, lane, float(LANES)), axis=-1, keepdims=True)
    lg2 = jnp.where(lane == i1, -jnp.inf, lg)
    m2 = jnp.max(lg2, axis=-1, keepdims=True)
    i2 = jnp.min(jnp.where(lg2 == m2, lane, float(LANES)), axis=-1, keepdims=True)
    e2 = jnp.exp(m2 - m1)
    g1 = 1.0 / (1.0 + e2)
    g2 = e2 * g1
    first_low = i1 < i2
    vals = (jnp.minimum(i1, i2), jnp.maximum(i1, i2), jnp.where(first_low, g1, g2), jnp.where(first_low, g2, g1),
            mod_row)
    route = jnp.zeros_like(lane)
    for idx, val in enumerate(vals):
        route = jnp.where(lane == float(idx), val, route)
    route_ref[...] = route
    xr_ref[:, d:d + LANES] = route


def _proj_router(lay, mixes, w_out, h, mod, g, wr, n_tiles):
    d = h.width
    rows = n_tiles * lay.tm
    return pl.pallas_call(
        functools.partial(_proj_router_kernel, arities=_mix_arities(mixes, h), lat_tiles=lay.lat_tiles,
                          tiles_per_batch=lay.s // lay.tm, ctx_mod_row=lay.b),
        grid=(n_tiles,),
        in_specs=_mix_specs(lay, mixes, w_out, h) + [_resident(g), _resident(wr)],
        out_specs=[lay.tile(2 * d + LANES), lay.tile(LANES)],
        out_shape=[jax.ShapeDtypeStruct((rows, 2 * d + LANES), F32), jax.ShapeDtypeStruct((rows, LANES), F32)],
        compiler_params=_params(("arbitrary",)),
        name="proj_router",
    )(*_mix_args(mixes, w_out, h, mod), g, wr)


def _moe_plan(route, n_tok):
    n_pairs = len(_PAIR_LO)
    n_tiles = n_tok // MOE_TM + n_pairs
    rows = n_tiles * MOE_TM
    lo = route[:, 0].astype(jnp.int32)
    hi = route[:, 1].astype(jnp.int32)
    pid = (lo * (2 * N_EXPERTS - 1 - lo)) // 2 + (hi - lo - 1)
    onehot = (pid[:, None] == jnp.arange(n_pairs, dtype=jnp.int32)[None, :]).astype(jnp.int32)
    csum = jnp.cumsum(onehot, axis=0)
    rank = jnp.sum(csum * onehot, axis=1) - 1
    counts = csum[-1]
    padded = ((counts + MOE_TM - 1) // MOE_TM) * MOE_TM
    gend = jnp.cumsum(padded)
    gstart = gend - padded
    dest = jnp.sum(gstart[None, :] * onehot, axis=1) + rank
    src = jnp.full((rows,), -1, jnp.int32).at[dest].set(jnp.arange(n_tok, dtype=jnp.int32), unique_indices=True)
    valid = src >= 0
    src_tok = jnp.where(valid, src, 0)
    n_valid = jnp.sum(valid.reshape(n_tiles, MOE_TM).astype(jnp.int32), axis=1)
    n_used = gend[-1] // MOE_TM
    tile_row = jnp.minimum(jnp.arange(n_tiles, dtype=jnp.int32), n_used - 1) * MOE_TM
    group = jnp.minimum(jnp.sum((gend[None, :] <= tile_row[:, None]).astype(jnp.int32), axis=1), n_pairs - 1)
    ea = jnp.asarray(_PAIR_LO, jnp.int32)[group]
    eb = jnp.asarray(_PAIR_HI, jnp.int32)[group]
    return ea, eb, src_tok, n_valid, n_used.astype(jnp.int32).reshape(1)


def _moe_kernel(ea_ref, eb_ref, src_ref, nvalid_ref, nused_ref,
                x_hbm, g2_ref, wgua_ref, wda_ref, wgub_ref, wdb_ref,
                y_hbm, xbuf, ybuf, gsem, ssem, *, n_mod_rows):
    i = pl.program_id(0)
    n_steps = pl.num_programs(0)
    n_used = nused_ref[0]
    slot = i % 2

    def gather_copy(tile, slot_, r):
        return pltpu.make_async_copy(x_hbm.at[pl.ds(src_ref[tile * MOE_TM + r], 1)],
                                     xbuf.at[slot_, pl.ds(r, 1)], gsem.at[slot_])

    def scatter_copy(tile, slot_, r):
        return pltpu.make_async_copy(ybuf.at[slot_, pl.ds(r, 1)],
                                     y_hbm.at[pl.ds(src_ref[tile * MOE_TM + r], 1)], ssem.at[slot_])

    def start_gather(tile, slot_):
        for r in range(MOE_TM):
            gather_copy(tile, slot_, r).start()

    def wait_gather(slot_):
        pltpu.make_async_copy(x_hbm.at[pl.ds(0, MOE_TM)], xbuf.at[slot_], gsem.at[slot_]).wait()

    def start_scatter(tile, slot_):
        full = nvalid_ref[tile] == MOE_TM

        @pl.when(full)
        def _():
            for r in range(MOE_TM):
                scatter_copy(tile, slot_, r).start()

        @pl.when(jnp.logical_not(full))
        def _():
            def body(r, carry):
                scatter_copy(tile, slot_, r).start()
                return carry
            lax.fori_loop(0, nvalid_ref[tile], body, 0)

    def wait_scatter(tile, slot_):
        full = nvalid_ref[tile] == MOE_TM

        @pl.when(full)
        def _():
            pltpu.make_async_copy(ybuf.at[slot_], y_hbm.at[pl.ds(0, MOE_TM)], ssem.at[slot_]).wait()

        @pl.when(jnp.logical_not(full))
        def _():
            def body(r, carry):
                scatter_copy(tile, slot_, r).wait()
                return carry
            lax.fori_loop(0, nvalid_ref[tile], body, 0)

    @pl.when((i >= 2) & (i - 2 < n_used))
    def _():
        wait_scatter(i - 2, slot)

    @pl.when(i < n_used)
    def _():
        @pl.when(i == 0)
        def _():
            start_gather(0, 0)

        @pl.when(i + 1 < n_used)
        def _():
            start_gather(i + 1, 1 - slot)

        wait_gather(slot)
        d = ybuf.shape[-1]
        x = xbuf[slot, :, 0:d].astype(BF16)
        gates = xbuf[slot, :, d:d + LANES]

        def expert(wgu_ref, wd_ref):
            f = wd_ref.shape[0]
            gu = jnp.dot(x, wgu_ref[...], preferred_element_type=F32)
            a = (_silu(gu[:, 0:f]) * gu[:, f:2 * f]).astype(BF16)
            return jnp.dot(a, wd_ref[...], preferred_element_type=F32)

        y = (gates[:, 2:3] * expert(wgua_ref, wda_ref)
             + gates[:, 3:4] * expert(wgub_ref, wdb_ref))
        mod_row = gates[:, 4:5]
        gate2 = jnp.zeros_like(y)
        for r in range(n_mod_rows):
            gate2 = jnp.where(mod_row == float(r), g2_ref[r:r + 1, :], gate2)
        ybuf[slot] = xbuf[slot, :, d + LANES:2 * d + LANES] + gate2 * y
        start_scatter(i, slot)

    @pl.when(i == n_steps - 1)
    def _():
        for back in (2, 1):
            @pl.when(n_steps - back < n_used)
            def _():
                wait_scatter(n_steps - back, (n_steps - back) % 2)


def _moe(xr, route, gate2, n_mod_rows, wgu, wd, layer):
    n_tok, dr = xr.shape
    _, _, f, d = wd.shape
    assert dr == 2 * d + LANES
    ea, eb, src_tok, n_valid, n_used = _moe_plan(route, n_tok)
    n_tiles = ea.shape[0]
    wspec = lambda shape, which: pl.BlockSpec(
        (None, None) + shape, lambda i, ea_, eb_, s_, d_, u_: (layer, (ea_, eb_)[which][i], 0, 0))
    return pl.pallas_call(
        functools.partial(_moe_kernel, n_mod_rows=n_mod_rows),
        grid_spec=pltpu.PrefetchScalarGridSpec(
            num_scalar_prefetch=5,
            grid=(n_tiles,),
            in_specs=[
                pl.BlockSpec(memory_space=pl.ANY),
                _resident(gate2),
                wspec((d, 2 * f), 0), wspec((f, d), 0),
                wspec((d, 2 * f), 1), wspec((f, d), 1),
            ],
            out_specs=pl.BlockSpec(memory_space=pl.ANY),
            scratch_shapes=[
                pltpu.VMEM((2, MOE_TM, dr), F32),
                pltpu.VMEM((2, MOE_TM, d), F32),
                pltpu.SemaphoreType.DMA((2,)),
                pltpu.SemaphoreType.DMA((2,)),
            ],
        ),
        out_shape=jax.ShapeDtypeStruct((n_tok, d), F32),
        compiler_params=_params(("arbitrary",)),
        name="moe_experts",
    )(ea, eb, src_tok, n_valid, n_used, xr, gate2, wgu, wd, wgu, wd)


def _rope_tables(s, ctx_rows):
    t = jnp.arange(s)
    rows = (t // GRID_W).astype(F32)
    cols = (t % GRID_W).astype(F32)
    n_pairs = A_ROPE // 2
    per_axis = n_pairs // 2
    inv = ROPE_THETA ** (-jnp.arange(per_axis, dtype=F32) / per_axis)
    ang = jnp.concatenate([rows[:, None] * inv, cols[:, None] * inv], axis=-1)
    cos = jnp.repeat(jnp.cos(ang), 2, axis=-1)
    sin = jnp.repeat(jnp.sin(ang), 2, axis=-1)
    sign = jnp.tile(jnp.array([-1.0, 1.0], F32), n_pairs)
    pad = HEAD_PAD - A_QK
    t1 = jnp.concatenate([jnp.ones((s, A_NOPE), F32), cos, jnp.zeros((s, pad), F32)], axis=-1)
    t2 = jnp.concatenate([jnp.zeros((s, A_NOPE), F32), sin * sign, jnp.zeros((s, pad), F32)], axis=-1)
    c1 = jnp.concatenate([jnp.ones((ctx_rows, A_QK), F32), jnp.zeros((ctx_rows, pad), F32)], axis=-1)
    c2 = jnp.zeros((ctx_rows, HEAD_PAD), F32)
    return jnp.concatenate([t1, c1], axis=0), jnp.concatenate([t2, c2], axis=0)


_PAIR_SWAP = np.arange(A_ROPE) ^ 1


def _rope_lane_vec(g_tail, swapped):
    gt = g_tail[_PAIR_SWAP] if swapped else g_tail
    return jnp.concatenate([jnp.zeros((A_NOPE,), F32), gt, jnp.zeros((HEAD_PAD - A_QK,), F32)])[None]


def _mla_weights(w_in, w_uq, w_ukv, g_q, g_k):
    d = w_in.shape[0]
    q_end = A_Q_RANK
    kv_end = q_end + A_KV_RANK
    r_end = kv_end + A_ROPE
    z = lambda n: jnp.zeros((d, n), w_in.dtype)
    kr = w_in[:, kv_end:r_end]
    pad = HEAD_PAD - A_QK
    win = jnp.concatenate([w_in[:, :kv_end], z(A_NOPE), kr, z(pad), z(A_NOPE), kr[:, _PAIR_SWAP], z(pad),
                           w_in[:, r_end:]], axis=-1).astype(BF16)
    wq = w_uq.reshape(A_Q_RANK, A_HEADS, A_QK)
    zq = lambda n: jnp.zeros((A_Q_RANK, A_HEADS, n), wq.dtype)
    wq_main = jnp.concatenate([wq, zq(pad)], axis=-1)
    wq_swap = jnp.concatenate([zq(A_NOPE), wq[:, :, A_NOPE:][:, :, _PAIR_SWAP], zq(pad)], axis=-1)
    wq_ext = jnp.concatenate([wq_main.reshape(A_Q_RANK, -1), wq_swap.reshape(A_Q_RANK, -1)], axis=-1).astype(BF16)
    wkv = w_ukv.reshape(A_KV_RANK, A_HEADS, A_NOPE + A_V)
    zkv = jnp.zeros((A_KV_RANK, A_HEADS, HEAD_PAD - A_NOPE), wkv.dtype)
    wk = jnp.concatenate([wkv[:, :, :A_NOPE], zkv], axis=-1)
    wv = jnp.concatenate([wkv[:, :, A_NOPE:], zkv], axis=-1)
    wkv_ext = jnp.concatenate([wk.reshape(A_KV_RANK, -1), wv.reshape(A_KV_RANK, -1)], axis=-1).astype(BF16)

    bound = math.sqrt(A_QK) * jnp.max(jnp.abs(g_q)) * jnp.max(jnp.abs(g_k))
    bounded = bound <= MLA_SAFE_BOUND
    shift = jnp.where(bounded, -bound * LOG2E, 0.0)
    unit = lambda lane_idx: jnp.zeros((1, HEAD_PAD), F32).at[0, lane_idx].set(1.0)
    gq = jnp.concatenate([jnp.concatenate([g_q, jnp.zeros((pad,), F32)])[None], _rope_lane_vec(g_q[A_NOPE:], True),
                          unit(A_QK) * shift], axis=0)
    gk = jnp.concatenate([jnp.concatenate([g_k[:A_NOPE], jnp.zeros((HEAD_PAD - A_NOPE,), F32)])[None],
                          _rope_lane_vec(g_k[A_NOPE:], False), _rope_lane_vec(g_k[A_NOPE:], True),
                          unit(A_QK), unit(A_V)], axis=0)
    return win, wq_ext, wkv_ext, gq, gk, bounded.astype(jnp.int32).reshape(1)


def _dft_mats(n, norm):
    k = jnp.arange(n, dtype=jnp.int32)
    w = GRID_W if n % GRID_W == 0 and n > GRID_W else 1
    hi = jnp.arange(n // w, dtype=jnp.int32) * w
    lo = jnp.arange(w, dtype=jnp.int32)
    ang_hi = ((k[:, None] * hi[None, :]) % n).astype(F32) * (2.0 * math.pi / n)
    ang_lo = ((k[:, None] * lo[None, :]) % n).astype(F32) * (2.0 * math.pi / n)
    ch, sh = (jnp.repeat(f(ang_hi), w, axis=1) for f in (jnp.cos, jnp.sin))
    cl, sl = (jnp.tile(f(ang_lo), (1, n // w)) for f in (jnp.cos, jnp.sin))
    cos = ((ch * cl - sh * sl) * norm).astype(BF16)
    sin = ((sh * cl + ch * sl) * norm).astype(BF16)
    return cos, sin


def _channel_dft():
    c = np.arange(F_CH)
    ang = 2.0 * np.pi * ((c[:, None] * c[None, :]) % F_CH) / F_CH
    eye = np.eye(F_GROUPS)
    cb = np.kron(eye, np.cos(ang)) / math.sqrt(F_CH)
    sb = np.kron(eye, np.sin(ang)) / math.sqrt(F_CH)
    return jnp.asarray(np.concatenate([cb, sb], axis=1), BF16)


def _na_group_rows(rows_n):
    assert rows_n % NA_GROUP == 0 and rows_n >= NA_WINDOW + NA_GROUP
    n_groups = rows_n // NA_GROUP
    n_dr = 2 * WIN_R - 1
    idx = np.full((3, NA_GROUP, NA_WINDOW), n_dr, np.int32)
    seen = {}
    for g in range(n_groups):
        gs = int(np.clip(g * NA_GROUP - WIN_R // 2, 0, rows_n - NA_WINDOW))
        pat = 0 if g == 0 else (2 if g == n_groups - 1 else 1)
        cur = np.full((NA_GROUP, NA_WINDOW), n_dr, np.int32)
        for qi in range(NA_GROUP):
            r = g * NA_GROUP + qi
            start = int(np.clip(r - WIN_R // 2, 0, rows_n - WIN_R))
            assert gs <= start and start + WIN_R <= gs + NA_WINDOW
            for a in range(start, start + WIN_R):
                cur[qi, a - gs] = a - r + (WIN_R - 1)
        assert pat not in seen or np.array_equal(seen[pat], cur)
        seen[pat] = cur
        idx[pat] = cur
    return idx


def _na_score_bound(g_q, g_k, rpb):
    qk = math.sqrt(C_DH) * jnp.max(jnp.abs(g_q)) * jnp.max(jnp.abs(g_k))
    bound = qk + jnp.maximum(jnp.max(rpb), 0.0)
    lowest = -qk + jnp.minimum(jnp.min(rpb), 0.0)
    bounded = (bound - lowest) <= 2.0 * MLA_SAFE_BOUND
    shift = jnp.where(bounded, bound, 0.0).astype(F32)
    return bounded.astype(jnp.int32).reshape(1), shift


def _na_bias(rpb, shift):
    h, n_dr, n_dc = rpb.shape
    qc = np.arange(GRID_W)
    cs = np.clip(qc - WIN_C // 2, 0, GRID_W - WIN_C)
    kcol = np.arange(GRID_W)
    valid = (kcol[None, :] >= cs[:, None]) & (kcol[None, :] < cs[:, None] + WIN_C)
    dc = kcol[None, :] - qc[:, None] + (WIN_C - 1)
    onehot = (valid[:, :, None] & (dc[:, :, None] == np.arange(n_dc)[None, None, :])).astype(np.float32)
    col = jnp.einsum('hdc,qkc->hdqk', rpb.astype(F32), jnp.asarray(onehot), precision=lax.Precision.HIGHEST)
    col = jnp.where(jnp.asarray(valid)[None, None], (col - shift) * LOG2E, NEG_BIG)
    col = jnp.concatenate([col, jnp.full((h, 1, GRID_W, GRID_W), NEG_BIG, F32)], axis=1)
    return jnp.concatenate([col, col], axis=-1)


def kernel(x, c, ctx, c_ctx, w_mod, b_mod, norm_g, a_w_in, a_g_cq, a_g_ckv, a_w_uq, a_w_ukv, a_g_q, a_g_k, a_w_out,
           f_w_gate, f_w_up, f_w_down, c_w_in, c_g_q, c_g_k, c_rpb, c_w_out, m_w_router, m_w_gate, m_w_up, m_w_down):
    b, s, d = x.shape
    l = ctx.shape[1]
    depth = w_mod.shape[0]
    lay = _make_layout(b, s, l)

    mod_rows = 16
    cin = jnp.concatenate([c, c_ctx[None], jnp.zeros((mod_rows - b - 1, d), F32)], axis=0)
    mod_all = _modulation(cin, w_mod, b_mod).reshape(depth, mod_rows, 6, d)

    t1, t2 = _rope_tables(s, lay.tm)
    cn, sn = _dft_mats(s, 1.0 / math.sqrt(s))
    cc, sc = _dft_mats(l, 1.0 / math.sqrt(l))
    cb = _channel_dft()

    moe_wgu = jnp.concatenate([m_w_gate.astype(BF16), m_w_up.astype(BF16)], axis=-1)
    moe_wd = m_w_down.astype(BF16)
    h = _Rows(x.reshape(b * s, d), ctx.reshape(b * l, d))
    for i in range(depth):
        j = i // 2
        need_ctx = i < depth - 1
        mod = mod_all[i]
        g1 = norm_g[i, 0][None]
        g2 = norm_g[i, 1][None]
        if i % 2 == 0:
            win, wq_ext, wkv_ext, gq, gk, bounded = _mla_weights(a_w_in[j], a_w_uq[j], a_w_ukv[j],
                                                                 a_g_q[j], a_g_k[j])
            q, k, v, fcs = _mla_front(lay, h, mod, g1, win, a_g_cq[j][None], a_g_ckv[j][None], wq_ext, wkv_ext,
                                      t1, t2, gq, gk, cb)
            att = _mla_attn(lay, bounded, q, k, v)
            yf = _dft(lay, fcs, cn, sn, cc, sc)
            h = _Rows(_proj_ffn(lay, [_Rows(att), _Rows(yf)], a_w_out[j].astype(BF16), h, mod, g2,
                                f_w_gate[j].astype(BF16), f_w_up[j].astype(BF16), f_w_down[j].astype(BF16)))
        else:
            gq2 = jnp.tile(c_g_q[j], 2)[None]
            gk2 = jnp.tile(c_g_k[j], 2)[None]
            q, k, v = _na_front(lay, h.lat, mod, g1, c_w_in[j].astype(BF16), gq2, gk2)
            na_bounded, na_shift = _na_score_bound(c_g_q[j], c_g_k[j], c_rpb[j])
            bias = _na_bias(c_rpb[j], na_shift)
            outs = _na_attn(lay, na_bounded, q, k, v, bias, jnp.full((1, LANES), na_shift * LOG2E, F32), need_ctx)
            att = _Rows(outs[0], outs[1]) if need_ctx else _Rows(outs[0])
            wr = jnp.concatenate([m_w_router[j], jnp.zeros((d, LANES - N_EXPERTS), F32)], axis=-1)
            n_tiles = lay.all_tiles if need_ctx else lay.lat_tiles
            xr, route = _proj_router(lay, [att], c_w_out[j].astype(BF16), h, mod, g2, wr, n_tiles)
            h = _Rows(_moe(xr, route, mod[:, 5, :], b + 1, moe_wgu, moe_wd, j))
    return h.lat[:lay.n_lat].reshape(b, s, d)
```
